```python
import math
import jax, jax.numpy as jnp
from jax import lax
import numpy as np

D_MODEL = 1024
BATCH = 8
SEQ = 8192
DEPTH = 1

DSA_HEADS = D_MODEL // 128
DSA_HEAD_DIM = 64
DSA_LATENT = D_MODEL // 8
IDX_HEADS = 4
IDX_DIM = 32
TOPK_MAX = 256
DIFF_HEADS = D_MODEL // 256
DIFF_QK_DIM = 64
DIFF_V_DIM = 2 * DIFF_QK_DIM
DSA_OUT = DSA_HEADS * DSA_HEAD_DIM
DIFF_OUT = DIFF_HEADS * DIFF_V_DIM
MIX_WIDTH = DSA_OUT + DIFF_OUT
IN_SPLITS = (DSA_HEADS * DSA_HEAD_DIM,
             DSA_LATENT,
             IDX_HEADS * IDX_DIM,
             IDX_DIM,
             IDX_HEADS,
             DIFF_HEADS * 2 * DIFF_QK_DIM,
             DIFF_HEADS * 2 * DIFF_QK_DIM,
             DIFF_HEADS * DIFF_V_DIM)
IN_WIDTH = sum(IN_SPLITS)
Q_BLOCK = 128
N_GROUPS = 4
EXPERTS_PER_GROUP = 8
N_EXPERTS = N_GROUPS * EXPERTS_PER_GROUP
EXPERT_TOPK = 2
D_EXPERT = D_MODEL // 4
EPS = 1e-6

kernel_name = "hybrid_dsa_diffattn_hmoe_block"


def rms_norm(x, g):
    xf = x.astype(jnp.float32)
    y = xf * lax.rsqrt(jnp.mean(xf * xf, axis=-1, keepdims=True) + EPS)
    return (y * g.astype(jnp.float32)).astype(x.dtype)


def alibi_slopes(n):
    return jnp.asarray([2.0 ** (-8.0 * (i + 1) / n) for i in range(n)], dtype=jnp.float32)


def lambda_init(layer):
    return 0.8 - 0.6 * math.exp(-0.3 * layer)


def to_blocks(a):
    b, s = a.shape[:2]
    a = a.reshape((b, s // Q_BLOCK, Q_BLOCK) + a.shape[2:])
    return jnp.moveaxis(a, 1, 0)


def from_blocks(a):
    a = jnp.moveaxis(a, 0, 1)
    return a.reshape((a.shape[0], -1) + a.shape[3:])


def dsa_attention(q, c_kv, q_idx, k_idx, w_idx, w_uk, w_uv, slopes):
    L = q.shape[1]
    n_blocks = L // Q_BLOCK
    topk = min(TOPK_MAX, L // 4)
    key_pos = jnp.arange(L)
    q_abs = jnp.einsum('bshd,hcd->bshc', q, w_uk)
    scale = DSA_HEAD_DIM ** -0.5
    idx_scale = IDX_DIM ** -0.5

    def block(args):
        blk, qa, qi, wi = args
        q_pos = blk * Q_BLOCK + jnp.arange(Q_BLOCK)
        causal = key_pos[None, :] <= q_pos[:, None]
        rel = jax.nn.relu(jnp.einsum('bqhd,bsd->bqhs', qi, k_idx) * idx_scale)
        score = jnp.einsum('bqhs,bqh->bqs', rel, wi)
        score = jnp.where(causal[None], score, -jnp.inf)
        _, sel = lax.top_k(score, topk)
        kv_sel = jax.vmap(lambda ck, i: ck[i])(c_kv, sel)
        s = jnp.einsum('bqhc,bqkc->bqhk', qa, kv_sel).astype(jnp.float32) * scale
        dist = (q_pos[None, :, None] - sel).astype(jnp.float32)
        s = s - slopes[None, None, :, None] * dist[:, :, None, :]
        s = jnp.where((dist >= 0)[:, :, None, :], s, -jnp.inf)
        p = jax.nn.softmax(s, axis=-1).astype(kv_sel.dtype)
        o_lat = jnp.einsum('bqhk,bqkc->bqhc', p, kv_sel)
        return jnp.einsum('bqhc,hcd->bqhd', o_lat, w_uv)

    out = lax.map(block, (jnp.arange(n_blocks), to_blocks(q_abs), to_blocks(q_idx), to_blocks(w_idx)))
    return from_blocks(out)


def diff_attention(q, k, v, lam, lam_init, sub_g, slopes):
    L = q.shape[1]
    n_blocks = L // Q_BLOCK
    key_pos = jnp.arange(L)
    scale = DIFF_QK_DIM ** -0.5

    def block(args):
        blk, qb = args
        q_pos = blk * Q_BLOCK + jnp.arange(Q_BLOCK)
        dist = (q_pos[:, None] - key_pos[None, :]).astype(jnp.float32)
        s = jnp.einsum('bqhmd,bshmd->bhmqs', qb, k).astype(jnp.float32) * scale
        s = s - slopes[None, :, None, None, None] * dist
        s = jnp.where(dist >= 0, s, -jnp.inf)
        p = jax.nn.softmax(s, axis=-1)
        a = p[:, :, 0] - lam * p[:, :, 1]
        return jnp.einsum('bhqs,bshe->bqhe', a.astype(v.dtype), v)

    o = from_blocks(lax.map(block, (jnp.arange(n_blocks), to_blocks(q))))
    return rms_norm(o, sub_g) * (1.0 - lam_init)


def hier_moe(h, w_group, b_group, w_router, b_router, w_gate, w_up, w_down):
    T = h.shape[0]
    g_logits = jnp.dot(h, w_group).astype(jnp.float32) + b_group.astype(jnp.float32)
    g_prob = jax.nn.softmax(g_logits, axis=-1)
    g_sel = jnp.argmax(g_logits, axis=-1)
    p_g = jnp.take_along_axis(g_prob, g_sel[:, None], axis=-1)
    e_logits = (jnp.dot(h, w_router).astype(jnp.float32) + b_router.astype(jnp.float32))
    e_logits = e_logits.reshape(T, N_GROUPS, EXPERTS_PER_GROUP)
    e_logits = jnp.take_along_axis(e_logits, g_sel[:, None, None], axis=1)[:, 0]
    top_v, top_i = lax.top_k(e_logits, EXPERT_TOPK)
    top_w = jax.nn.softmax(top_v, axis=-1) * p_g
    eid = g_sel[:, None] * EXPERTS_PER_GROUP + top_i
    gates = jnp.sum(jax.nn.one_hot(eid, N_EXPERTS, dtype=jnp.float32) * top_w[..., None], axis=1)
    gates = gates.astype(h.dtype)
    y = jnp.zeros_like(h)
    for e in range(N_EXPERTS):
        hid = jax.nn.silu(jnp.dot(h, w_gate[e])) * jnp.dot(h, w_up[e])
        y = y + gates[:, e:e + 1] * jnp.dot(hid, w_down[e])
    return y


def setup_inputs(seed: int = 0) -> dict:
    key = jax.random.key(seed)
    ks = jax.random.split(key, 26)
    f32 = jnp.float32
    D = D_MODEL

    def nrm(k, shape, std):
        return jax.random.normal(k, shape, f32) * std

    def gain(k, shape):
        return 1.0 + 0.05 * jax.random.normal(k, shape, f32)

    return {
        "x": nrm(ks[0], (BATCH, SEQ, D), 1.0),
        "c": nrm(ks[1], (BATCH, D), 1.0),
        "w_ada": nrm(ks[2], (DEPTH, D, 6 * D), 0.5 * D ** -0.5),
        "b_ada": nrm(ks[3], (DEPTH, 6 * D), 0.02),
        "g_attn": gain(ks[4], (DEPTH, D)),
        "w_in": nrm(ks[5], (DEPTH, D, IN_WIDTH), D ** -0.5),
        "g_kv": gain(ks[6], (DEPTH, DSA_LATENT)),
        "w_uk": nrm(ks[7], (DEPTH, DSA_HEADS, DSA_LATENT, DSA_HEAD_DIM), DSA_LATENT ** -0.5),
        "w_uv": nrm(ks[8], (DEPTH, DSA_HEADS, DSA_LATENT, DSA_HEAD_DIM), DSA_LATENT ** -0.5),
        "lam_q1": nrm(ks[9], (DEPTH, DIFF_QK_DIM), 0.1),
        "lam_k1": nrm(ks[10], (DEPTH, DIFF_QK_DIM), 0.1),
        "lam_q2": nrm(ks[11], (DEPTH, DIFF_QK_DIM), 0.1),
        "lam_k2": nrm(ks[12], (DEPTH, DIFF_QK_DIM), 0.1),
        "g_sub": gain(ks[13], (DEPTH, DIFF_V_DIM)),
        "w_out": nrm(ks[14], (DEPTH, MIX_WIDTH, D), MIX_WIDTH ** -0.5),
        "g_moe": gain(ks[15], (DEPTH, D)),
        "w_group": nrm(ks[16], (DEPTH, D, N_GROUPS), D ** -0.5),
        "b_group": nrm(ks[17], (DEPTH, N_GROUPS), 0.01),
        "w_router": nrm(ks[18], (DEPTH, D, N_EXPERTS), D ** -0.5),
        "b_router": nrm(ks[19], (DEPTH, N_EXPERTS), 0.01),
        "w_gate": nrm(ks[20], (DEPTH, N_EXPERTS, D, D_EXPERT), D ** -0.5),
        "w_up": nrm(ks[21], (DEPTH, N_EXPERTS, D, D_EXPERT), D ** -0.5),
        "w_down": nrm(ks[22], (DEPTH, N_EXPERTS, D_EXPERT, D), D_EXPERT ** -0.5),
        "g_final": gain(ks[23], (D,)),
    }


def reference(x, c, w_ada, b_ada, g_attn, w_in, g_kv, w_uk, w_uv, lam_q1, lam_k1, lam_q2, lam_k2,
              g_sub, w_out, g_moe, w_group, b_group, w_router, b_router, w_gate, w_up, w_down, g_final):
    B, S, D = x.shape
    dsa_slopes = alibi_slopes(DSA_HEADS)
    diff_slopes = alibi_slopes(DIFF_HEADS)
    split_pts = list(np.cumsum(IN_SPLITS)[:-1])
    c_act = jax.nn.silu(c)
    for l in range(DEPTH):
        mod = jnp.dot(c_act, w_ada[l]) + b_ada[l]
        shift_a, scale_a, gate_a, shift_m, scale_m, gate_m = jnp.split(mod, 6, axis=-1)

        h = rms_norm(x, g_attn[l]) * (1.0 + scale_a[:, None]) + shift_a[:, None]
        proj = jnp.dot(h, w_in[l])
        dq, dlat, iq, ik, iw, fq, fk, fv = jnp.split(proj, split_pts, axis=-1)
        dq = dq.reshape(B, S, DSA_HEADS, DSA_HEAD_DIM)
        dlat = rms_norm(dlat, g_kv[l])
        iq = iq.reshape(B, S, IDX_HEADS, IDX_DIM)
        iw = iw * (IDX_HEADS ** -0.5)
        fq = fq.reshape(B, S, DIFF_HEADS, 2, DIFF_QK_DIM)
        fk = fk.reshape(B, S, DIFF_HEADS, 2, DIFF_QK_DIM)
        fv = fv.reshape(B, S, DIFF_HEADS, DIFF_V_DIM)

        o_dsa = dsa_attention(dq, dlat, iq, ik, iw, w_uk[l], w_uv[l], dsa_slopes)
        lam0 = lambda_init(l)
        lam = (jnp.exp(jnp.sum(lam_q1[l].astype(jnp.float32) * lam_k1[l].astype(jnp.float32)))
               - jnp.exp(jnp.sum(lam_q2[l].astype(jnp.float32) * lam_k2[l].astype(jnp.float32)))
               + lam0)
        o_diff = diff_attention(fq, fk, fv, lam, lam0, g_sub[l], diff_slopes)
        o = jnp.concatenate([o_dsa.reshape(B, S, DSA_OUT), o_diff.reshape(B, S, DIFF_OUT)], axis=-1)
        x = x + gate_a[:, None] * jnp.dot(o, w_out[l])

        h = rms_norm(x, g_moe[l]) * (1.0 + scale_m[:, None]) + shift_m[:, None]
        y = hier_moe(h.reshape(B * S, D), w_group[l], b_group[l], w_router[l], b_router[l],
                     w_gate[l], w_up[l], w_down[l]).reshape(B, S, D)
        x = x + gate_m[:, None] * y
    return rms_norm(x, g_final)
```

```python
import functools
import math

import jax
import jax.numpy as jnp
import numpy as np
from jax import lax
from jax.experimental import pallas as pl
from jax.experimental.pallas import tpu as pltpu

F32 = jnp.float32
BF16 = jnp.bfloat16

D_MODEL = 1024
DSA_HEADS = 8
DSA_HEAD_DIM = 64
DSA_LATENT = 128
IDX_HEADS = 4
IDX_DIM = 32
TOPK_MAX = 256
DIFF_HEADS = 4
DIFF_QK_DIM = 64
DIFF_V_DIM = 128
N_GROUPS = 4
EXPERTS_PER_GROUP = 8
N_EXPERTS = 32
D_EXPERT = 256
EPS = 1e-6

LANES = 128
POS_SPLIT = 64
NEG_BIG = -1e30
VMEM_LIMIT = 56 * 1024 * 1024

W_DQ = DSA_HEADS * DSA_HEAD_DIM
W_QABS = DSA_HEADS * DSA_LATENT
W_CKV = 2 * LANES
W_FQ = DIFF_HEADS * 2 * LANES
W_FV = DIFF_HEADS * 2 * LANES


def _alibi_slopes(n):
    return [2.0 ** (-8.0 * (i + 1) / n) for i in range(n)]


def _rms(x, g):
    return x * lax.rsqrt(jnp.mean(x * x, axis=-1, keepdims=True) + EPS) * g


def _dot_nt(a, b):
    return lax.dot_general(a, b, (((1,), (1,)), ((), ())), preferred_element_type=F32)


def _mod_kernel(c_ref, w_ref, b_ref, o_ref):
    c = c_ref[...]
    act = c * jax.nn.sigmoid(c)
    o_ref[...] = jnp.dot(act, w_ref[...], preferred_element_type=F32,
                         precision=lax.Precision.HIGHEST) + b_ref[...]


def _mod_call(c, w_ada, b_ada):
    B, D = c.shape
    n = w_ada.shape[1] // D
    return pl.pallas_call(
        _mod_kernel,
        out_shape=jax.ShapeDtypeStruct((B, n * D), F32),
        grid=(n,),
        in_specs=[pl.BlockSpec((B, D), lambda j: (0, 0)),
                  pl.BlockSpec((D, D), lambda j: (0, j)),
                  pl.BlockSpec((1, D), lambda j: (0, j))],
        out_specs=pl.BlockSpec((B, D), lambda j: (0, j)),
        compiler_params=pltpu.CompilerParams(dimension_semantics=("arbitrary",),
                                             vmem_limit_bytes=VMEM_LIMIT),
        name="mod",
    )(c, w_ada, b_ada)


def _proj_kernel(x_ref, mod_ref, g_ref, w_ref, wpair_ref, gkv_ref, fc_ref,
                 qabs_ref, ckv_ref, iq_ref, ik_ref, iw_ref, fq_ref, fk_ref, fv_ref, *, tm):
    x = x_ref[...]
    shift = mod_ref[0:1, :]
    scale = mod_ref[1:2, :]
    h = _rms(x, g_ref[...]) * (1.0 + scale) + shift
    hb = h.astype(BF16)

    pos = pl.program_id(1) * tm + lax.broadcasted_iota(jnp.int32, (tm, 1), 0)
    pa = (pos // POS_SPLIT).astype(F32)
    pb = (pos % POS_SPLIT).astype(F32)

    def feats(row, width, off):
        c1 = fc_ref[row:row + 1, off:off + width]
        c2 = fc_ref[row + 1:row + 2, off:off + width]
        c3 = fc_ref[row + 2:row + 3, off:off + width]
        return c1 + c2 * pa + c3 * pb

    def proj(off, width):
        return jnp.dot(hb, w_ref[:, off:off + width], preferred_element_type=F32)

    off = 0
    dq = proj(off, W_DQ).astype(BF16)
    off += W_DQ
    for p in range(DSA_HEADS // 2):
        qa = jnp.dot(dq[:, p * LANES:(p + 1) * LANES], wpair_ref[p], preferred_element_type=F32)
        qabs_ref[:, p * 2 * LANES:(p + 1) * 2 * LANES] = qa.astype(BF16)
    dlat = proj(off, LANES)
    off += LANES
    ckv_ref[:, 0:LANES] = _rms(dlat, gkv_ref[...]).astype(BF16)
    ckv_ref[:, LANES:2 * LANES] = feats(0, LANES, 0).astype(BF16)
    iq_ref[...] = proj(off, LANES).astype(BF16)
    off += LANES
    ik_ref[...] = proj(off, LANES).astype(BF16)
    off += LANES
    iw_ref[...] = proj(off, LANES) * (IDX_HEADS ** -0.5 * IDX_DIM ** -0.5)
    off += LANES
    fq_ref[...] = (proj(off, W_FQ) + feats(3, W_FQ, 0)).astype(BF16)
    off += W_FQ
    fk_ref[...] = (proj(off, W_FQ) + feats(6, W_FQ, 0)).astype(BF16)
    off += W_FQ
    fv_ref[...] = (proj(off, W_FV) + feats(9, W_FV, 0)).astype(BF16)


def _feature_consts():
    fc = np.zeros((16, W_FQ), np.float32)
    fc[1, 0] = 1.0
    fc[2, 1] = 1.0
    fc[0, 2] = 1.0
    fc[0, 3] = 1.0
    slopes = _alibi_slopes(DIFF_HEADS)
    for hh in range(DIFF_HEADS):
        for m in range(2):
            base = (hh * 2 + m) * LANES + DIFF_QK_DIM
            fc[3, base + 0] = POS_SPLIT * slopes[hh]
            fc[3, base + 1] = slopes[hh]
            fc[4, base + 2] = -POS_SPLIT * slopes[hh]
            fc[5, base + 3] = -slopes[hh]
            fc[7, base + 0] = 1.0
            fc[8, base + 1] = 1.0
            fc[6, base + 2] = 1.0
            fc[6, base + 3] = 1.0
        base = hh * 2 * LANES + DIFF_V_DIM
        fc[10, base + 0] = 1.0
        fc[11, base + 1] = 1.0
        fc[9, base + 2] = 1.0
        fc[9, base + 3] = 1.0
    return jnp.asarray(fc)


def _proj_weights(w_in, w_uk):
    D = w_in.shape[0]
    pts = np.cumsum([W_DQ, DSA_LATENT, IDX_HEADS * IDX_DIM, IDX_DIM, IDX_HEADS,
                     DIFF_HEADS * 2 * DIFF_QK_DIM, DIFF_HEADS * 2 * DIFF_QK_DIM])
    dq, dlat, iq, ik, iw, fq, fk, fv = jnp.split(w_in, list(pts), axis=1)
    ik4 = jnp.tile(ik, (1, IDX_HEADS))
    iwp = jnp.pad(iw, ((0, 0), (0, LANES - IDX_HEADS)))
    qk_scale = DIFF_QK_DIM ** -0.5

    def pad_maps(w, s):
        w = w.reshape(D, DIFF_HEADS * 2, DIFF_QK_DIM) * s
        return jnp.pad(w, ((0, 0), (0, 0), (0, LANES - DIFF_QK_DIM))).reshape(D, W_FQ)

    fve = jnp.pad(fv.reshape(D, DIFF_HEADS, DIFF_V_DIM), ((0, 0), (0, 0), (0, LANES))).reshape(D, W_FV)
    w_cat = jnp.concatenate([dq, dlat, iq, ik4, iwp, pad_maps(fq, qk_scale), pad_maps(fk, 1.0), fve],
                            axis=1).astype(BF16)
    ukt = jnp.swapaxes(w_uk, 1, 2) * (DSA_HEAD_DIM ** -0.5)
    z = jnp.zeros_like(ukt[0])
    pairs = [jnp.concatenate([jnp.concatenate([ukt[2 * p], z], axis=1),
                              jnp.concatenate([z, ukt[2 * p + 1]], axis=1)], axis=0)
             for p in range(DSA_HEADS // 2)]
    return w_cat, jnp.stack(pairs).astype(BF16)


def _proj_call(x, mod3, g_attn, w_cat, w_pair, g_kv, fc, tm):
    B, S, D = x.shape
    wtot = w_cat.shape[1]
    kern = functools.partial(_proj_kernel, tm=tm)
    tok = lambda w: pl.BlockSpec((None, tm, w), lambda b, i: (b, i, 0))
    full2 = lambda a: pl.BlockSpec(a.shape, lambda b, i: (0,) * a.ndim)
    outs = [(W_QABS, BF16), (W_CKV, BF16), (LANES, BF16), (LANES, BF16), (LANES, F32),
            (W_FQ, BF16), (W_FQ, BF16), (W_FV, BF16)]
    return pl.pallas_call(
        kern,
        out_shape=[jax.ShapeDtypeStruct((B, S, w), dt) for w, dt in outs],
        grid=(B, S // tm),
        in_specs=[tok(D),
                  pl.BlockSpec((None, 6, D), lambda b, i: (b, 0, 0)),
                  full2(g_attn), full2(w_cat), full2(w_pair), full2(g_kv), full2(fc)],
        out_specs=[tok(w) for w, _ in outs],
        compiler_params=pltpu.CompilerParams(dimension_semantics=("arbitrary", "arbitrary"),
                                             vmem_limit_bytes=VMEM_LIMIT),
        name="proj",
    )(x, mod3, g_attn, w_cat, w_pair, g_kv, fc)


def _dsa_kernel(qabs_ref, iq_ref, iw_ref, ik_ref, ckv_ref, ustrict_ref, wuv_ref, o_ref,
                sc_ref, qs_ref, acc_ref, m_ref, tie_ref, *, tq, tk, topk):
    qi = pl.program_id(1)
    nkb = (qi * tq) // tk + 1
    kf = float(topk)
    slopes = _alibi_slopes(DSA_HEADS)

    row = lax.broadcasted_iota(jnp.int32, (tq, 1), 0)
    q_pos = qi * tq + row
    lane = lax.broadcasted_iota(jnp.int32, (tq, LANES), 1)

    iq = iq_ref[...]
    zero = jnp.zeros_like(iq)
    qstack = jnp.concatenate(
        [jnp.where((lane // IDX_DIM) == hh, iq, zero) for hh in range(IDX_HEADS)], axis=0)
    iw = iw_ref[...]
    wcol = [iw[:, hh:hh + 1] for hh in range(IDX_HEADS)]
    key_off = lax.broadcasted_iota(jnp.int32, (tq, tk), 1)

    def score_body(kb, carry):
        kblk = ik_ref[pl.ds(pl.multiple_of(kb * tk, tk), tk), :]
        a = _dot_nt(qstack, kblk)
        sc = jnp.maximum(a[0:tq], 0.0) * wcol[0]
        for hh in range(1, IDX_HEADS):
            sc = sc + jnp.maximum(a[hh * tq:(hh + 1) * tq], 0.0) * wcol[hh]
        sc = jnp.where(key_off + kb * tk <= q_pos, sc, jnp.nan)
        sc_ref[kb] = sc
        return carry

    lax.fori_loop(0, nkb, score_body, 0)

    def count(pred):
        def body(kb, acc):
            blk = sc_ref[kb]
            for j in range(tk // LANES):
                acc = acc + jnp.where(pred(blk[:, j * LANES:(j + 1) * LANES]), 1.0, 0.0)
            return acc
        acc = lax.fori_loop(0, nkb, body, jnp.zeros((tq, LANES), F32))
        return jnp.sum(acc, axis=-1, keepdims=True)

    def decode(u):
        bits = jnp.where(u < 0, u ^ jnp.int32(-2 ** 31), ~u)
        return lax.bitcast_convert_type(bits, F32)

    def bisect_body(i, u):
        cand = u | lax.shift_left(jnp.int32(1), 31 - i)
        theta = jnp.broadcast_to(decode(cand), (tq, LANES))
        c = count(lambda v: v >= theta)
        return jnp.where(c >= kf, cand, u)

    u = lax.fori_loop(0, 32, bisect_body, jnp.zeros((tq, 1), jnp.int32))
    tau = decode(u)
    tau = jnp.where(tau != tau, -jnp.inf, tau)
    tau_b = jnp.broadcast_to(tau, (tq, LANES))
    c_gt = count(lambda v: v > tau_b)
    c_ge = count(lambda v: v >= tau_b)
    quota = kf - c_gt
    need_ties = jnp.max(c_ge) > kf

    @pl.when(need_ties)
    def _():
        tie_ref[...] = jnp.zeros_like(tie_ref)
        tau_k = jnp.broadcast_to(tau, (tq, tk))

        def tie_body(kb, carry):
            sc = sc_ref[kb]
            eq = sc == tau_k
            eqf = jnp.where(eq, 1.0, 0.0)
            rank = jnp.dot(eqf.astype(BF16), ustrict_ref[...], preferred_element_type=F32) + tie_ref[...]
            sc_ref[kb] = jnp.where(eq & (rank >= quota), jnp.nan, sc)
            tie_ref[...] = tie_ref[...] + jnp.sum(eqf, axis=-1, keepdims=True)
            return carry

        lax.fori_loop(0, nkb, tie_body, 0)

    qa = qabs_ref[...]
    a_t = (q_pos // POS_SPLIT).astype(F32)
    b_t = (q_pos % POS_SPLIT).astype(F32)
    for hh in range(DSA_HEADS):
        s_h = slopes[hh]
        f = jnp.where(lane == 0, POS_SPLIT * s_h,
                      jnp.where(lane == 1, s_h,
                                jnp.where(lane == 2, -POS_SPLIT * s_h * a_t,
                                          jnp.where(lane == 3, -s_h * b_t, 0.0))))
        qs_ref[hh * tq:(hh + 1) * tq, 0:LANES] = qa[:, hh * LANES:(hh + 1) * LANES]
        qs_ref[hh * tq:(hh + 1) * tq, LANES:2 * LANES] = f.astype(BF16)
    acc_ref[...] = jnp.zeros_like(acc_ref)
    m_ref[...] = jnp.full_like(m_ref, NEG_BIG)
    tau_k = jnp.broadcast_to(tau, (tq, tk))

    def attn_body(kb, carry):
        kx = ckv_ref[pl.ds(pl.multiple_of(kb * tk, tk), tk), :]
        s = _dot_nt(qs_ref[...], kx)
        keep = sc_ref[kb] >= tau_k
        for hh in range(DSA_HEADS):
            r0, r1 = hh * tq, (hh + 1) * tq
            sh = jnp.where(keep, s[r0:r1], NEG_BIG)
            m_old = m_ref[r0:r1]
            m_new = jnp.maximum(m_old, jnp.max(sh, axis=-1, keepdims=True))
            p = jnp.exp(sh - m_new)
            alpha = jnp.exp(m_old - m_new)
            pv = jnp.dot(p.astype(BF16), kx, preferred_element_type=F32)
            acc_ref[r0:r1] = alpha * acc_ref[r0:r1] + pv
            m_ref[r0:r1] = m_new
        return carry

    lax.fori_loop(0, nkb, attn_body, 0)

    ones_col = LANES + 2
    for p in range(DSA_HEADS // 2):
        lat = []
        for hh in (2 * p, 2 * p + 1):
            a = acc_ref[hh * tq:(hh + 1) * tq]
            lat.append(a[:, 0:LANES] / a[:, ones_col:ones_col + 1])
        pair = jnp.concatenate(lat, axis=1).astype(BF16)
        o_ref[:, p * LANES:(p + 1) * LANES] = jnp.dot(
            pair, wuv_ref[p], preferred_element_type=F32).astype(BF16)


def _dsa_call(qabs, iq, iw, ik4, ckv, ustrict, wuv_pair, tq, tk, topk):
    B, S, _ = qabs.shape
    kern = functools.partial(_dsa_kernel, tq=tq, tk=tk, topk=topk)
    blk = lambda w: pl.BlockSpec((None, tq, w), lambda b, i: (b, i, 0))
    per_b = lambda w: pl.BlockSpec((None, S, w), lambda b, i: (b, 0, 0))
    full = lambda a: pl.BlockSpec(a.shape, lambda b, i: (0,) * a.ndim)
    return pl.pallas_call(
        kern,
        out_shape=jax.ShapeDtypeStruct((B, S, W_DQ), BF16),
        grid=(B, S // tq),
        in_specs=[blk(W_QABS), blk(LANES), blk(LANES), per_b(LANES), per_b(W_CKV),
                  full(ustrict), full(wuv_pair)],
        out_specs=blk(W_DQ),
        scratch_shapes=[pltpu.VMEM((S // tk, tq, tk), F32),
                        pltpu.VMEM((DSA_HEADS * tq, 2 * LANES), BF16),
                        pltpu.VMEM((DSA_HEADS * tq, 2 * LANES), F32),
                        pltpu.VMEM((DSA_HEADS * tq, 1), F32),
                        pltpu.VMEM((tq, 1), F32)],
        compiler_params=pltpu.CompilerParams(dimension_semantics=("arbitrary", "arbitrary"),
                                             vmem_limit_bytes=VMEM_LIMIT),
        name="dsa",
    )(qabs, iq, iw, ik4, ckv, ustrict, wuv_pair)


def _diff_kernel(fq_ref, fk_ref, fv_ref, lam_ref, gsub_ref, o_ref, acc_ref, m_ref, *, tq, tk, lam0):
    qi = pl.program_id(2)
    nkb = (qi * tq) // tk + (tq + tk - 1) // tk
    q = fq_ref[...]
    q_pos = qi * tq + lax.broadcasted_iota(jnp.int32, (tq, 1), 0)
    key_off = lax.broadcasted_iota(jnp.int32, (tq, tk), 1)
    acc_ref[...] = jnp.zeros_like(acc_ref)
    m_ref[...] = jnp.full_like(m_ref, NEG_BIG)

    def body(kb, carry):
        start = pl.multiple_of(kb * tk, tk)
        kx = fk_ref[pl.ds(start, tk), :]
        vx = fv_ref[pl.ds(start, tk), :]
        causal = key_off + kb * tk <= q_pos
        for m in range(2):
            s = _dot_nt(q[:, m * LANES:(m + 1) * LANES], kx[:, m * LANES:(m + 1) * LANES])
            s = jnp.where(causal, s, NEG_BIG)
            m_old = m_ref[m]
            m_new = jnp.maximum(m_old, jnp.max(s, axis=-1, keepdims=True))
            p = jnp.exp(s - m_new)
            alpha = jnp.exp(m_old - m_new)
            acc_ref[m] = alpha * acc_ref[m] + jnp.dot(p.astype(BF16), vx, preferred_element_type=F32)
            m_ref[m] = m_new
        return carry

    lax.fori_loop(0, nkb, body, 0)

    lv = lam_ref[...]
    lam = (jnp.exp(jnp.sum(lv[0:1] * lv[1:2], axis=-1, keepdims=True))
           - jnp.exp(jnp.sum(lv[2:3] * lv[3:4], axis=-1, keepdims=True)) + lam0)
    ones_col = DIFF_V_DIM + 2
    a1 = acc_ref[0]
    a2 = acc_ref[1]
    o = (a1[:, 0:DIFF_V_DIM] / a1[:, ones_col:ones_col + 1]
         - lam * (a2[:, 0:DIFF_V_DIM] / a2[:, ones_col:ones_col + 1]))
    o_ref[...] = (_rms(o, gsub_ref[...]) * (1.0 - lam0)).astype(BF16)


def _diff_call(fq, fk, fv, lam_vecs, g_sub, tq, tk, lam0):
    B, S, _ = fq.shape
    kern = functools.partial(_diff_kernel, tq=tq, tk=tk, lam0=lam0)
    full = lambda a: pl.BlockSpec(a.shape, lambda b, h, i: (0,) * a.ndim)
    return pl.pallas_call(
        kern,
        out_shape=jax.ShapeDtypeStruct((B, S, DIFF_HEADS * DIFF_V_DIM), BF16),
        grid=(B, DIFF_HEADS, S // tq),
        in_specs=[pl.BlockSpec((None, tq, 2 * LANES), lambda b, h, i: (b, i, h)),
                  pl.BlockSpec((None, S, 2 * LANES), lambda b, h, i: (b, 0, h)),
                  pl.BlockSpec((None, S, 2 * LANES), lambda b, h, i: (b, 0, h)),
                  full(lam_vecs), full(g_sub)],
        out_specs=pl.BlockSpec((None, tq, DIFF_V_DIM), lambda b, h, i: (b, i, h)),
        scratch_shapes=[pltpu.VMEM((2, tq, 2 * LANES), F32),
                        pltpu.VMEM((2, tq, 1), F32)],
        compiler_params=pltpu.CompilerParams(
            dimension_semantics=("arbitrary", "arbitrary", "arbitrary"),
            vmem_limit_bytes=VMEM_LIMIT),
        name="diff",
    )(fq, fk, fv, lam_vecs, g_sub)


def _route_kernel(x_ref, od_ref, of_ref, mod_ref, wout_ref, g_ref, wr_hi_ref, wr_lo_ref, br_ref,
                  x1_ref, h_ref, gates_ref, *, tm):
    attn = (jnp.dot(od_ref[...], wout_ref[0:W_DQ, :], preferred_element_type=F32)
            + jnp.dot(of_ref[...], wout_ref[W_DQ:, :], preferred_element_type=F32))
    x1 = x_ref[...] + mod_ref[2:3, :] * attn
    x1_ref[...] = x1
    h = _rms(x1, g_ref[...]) * (1.0 + mod_ref[4:5, :]) + mod_ref[3:4, :]
    h_hi = h.astype(BF16)
    h_ref[...] = h_hi
    h_lo = (h - h_hi.astype(F32)).astype(BF16)
    w_hi = wr_hi_ref[...]
    logits = (jnp.dot(h_hi, w_hi, preferred_element_type=F32)
              + jnp.dot(h_lo, w_hi, preferred_element_type=F32)
              + jnp.dot(h_hi, wr_lo_ref[...], preferred_element_type=F32)) + br_ref[...]

    lane = lax.broadcasted_iota(jnp.int32, (tm, LANES), 1)
    big = jnp.int32(4 * LANES)
    neg = -jnp.inf
    is_g = (lane >= N_EXPERTS) & (lane < N_EXPERTS + N_GROUPS)
    gl = jnp.where(is_g, logits, neg)
    gmax = jnp.max(gl, axis=-1, keepdims=True)
    g_lane = jnp.min(jnp.where(gl == gmax, lane, big), axis=-1, keepdims=True)
    g_sel = g_lane - N_EXPERTS
    p_g = 1.0 / jnp.sum(jnp.where(is_g, jnp.exp(gl - gmax), 0.0), axis=-1, keepdims=True)
    in_grp = (lane < N_EXPERTS) & ((lane // EXPERTS_PER_GROUP) == g_sel)
    el = jnp.where(in_grp, logits, neg)
    v1 = jnp.max(el, axis=-1, keepdims=True)
    i1 = jnp.min(jnp.where(el == v1, lane, big), axis=-1, keepdims=True)
    el2 = jnp.where(lane == i1, neg, el)
    v2 = jnp.max(el2, axis=-1, keepdims=True)
    i2 = jnp.min(jnp.where(el2 == v2, lane, big), axis=-1, keepdims=True)
    e = jnp.exp(v2 - v1)
    w1 = 1.0 / (1.0 + e)
    w2 = e * w1
    gates_ref[...] = jnp.where(lane == i1, w1 * p_g, jnp.where(lane == i2, w2 * p_g, 0.0))


def _route_call(x, o_dsa, o_diff, mod3, w_out, g_moe, wr_hi, wr_lo, b_r, tm):
    B, S, D = x.shape
    kern = functools.partial(_route_kernel, tm=tm)
    tok = lambda w: pl.BlockSpec((None, tm, w), lambda b, i: (b, i, 0))
    full = lambda a: pl.BlockSpec(a.shape, lambda b, i: (0,) * a.ndim)
    return pl.pallas_call(
        kern,
        out_shape=[jax.ShapeDtypeStruct((B, S, D), F32), jax.ShapeDtypeStruct((B, S, D), BF16),
                   jax.ShapeDtypeStruct((B, S, LANES), F32)],
        grid=(B, S // tm),
        in_specs=[tok(D), tok(W_DQ), tok(DIFF_HEADS * DIFF_V_DIM),
                  pl.BlockSpec((None, 6, D), lambda b, i: (b, 0, 0)),
                  full(w_out), full(g_moe), full(wr_hi), full(wr_lo), full(b_r)],
        out_specs=[tok(D), tok(D), tok(LANES)],
        compiler_params=pltpu.CompilerParams(dimension_semantics=("arbitrary", "arbitrary"),
                                             vmem_limit_bytes=VMEM_LIMIT),
        name="route",
    )(x, o_dsa, o_diff, mod3, w_out, g_moe, wr_hi, wr_lo, b_r)


def _moe_kernel(h_ref, gates_ref, x1_ref, mod_ref, wg_ref, wu_ref, wd_ref, gf_ref, o_ref, acc_ref,
                *, tm, ec):
    j = pl.program_id(2)
    nc = pl.num_programs(2)

    @pl.when(j == 0)
    def _():
        acc_ref[...] = jnp.zeros_like(acc_ref)

    h = h_ref[...]
    hg = jnp.dot(h, wg_ref[...], preferred_element_type=F32)
    hu = jnp.dot(h, wu_ref[...], preferred_element_type=F32)
    hid = hg * jax.nn.sigmoid(hg) * hu
    gates = gates_ref[...]
    lane = lax.broadcasted_iota(jnp.int32, (tm, LANES), 1)
    parts = []
    for e in range(ec):
        gcol = jnp.sum(jnp.where(lane == j * ec + e, gates, 0.0), axis=-1, keepdims=True)
        parts.append((hid[:, e * D_EXPERT:(e + 1) * D_EXPERT] * gcol).astype(BF16))
    hs = jnp.concatenate(parts, axis=1)
    acc_ref[...] += jnp.dot(hs, wd_ref[...], preferred_element_type=F32)

    @pl.when(j == nc - 1)
    def _():
        x2 = x1_ref[...] + mod_ref[5:6, :] * acc_ref[...]
        o_ref[...] = _rms(x2, gf_ref[...])


def _moe_call(h, gates, x1, mod3, wg, wu, wd, g_final, tm, ec):
    B, S, D = x1.shape
    kern = functools.partial(_moe_kernel, tm=tm, ec=ec)
    tok = lambda w: pl.BlockSpec((None, tm, w), lambda b, i, j: (b, i, 0))
    return pl.pallas_call(
        kern,
        out_shape=jax.ShapeDtypeStruct((B, S, D), F32),
        grid=(B, S // tm, N_EXPERTS // ec),
        in_specs=[tok(D), tok(LANES), tok(D),
                  pl.BlockSpec((None, 6, D), lambda b, i, j: (b, 0, 0)),
                  pl.BlockSpec((D, ec * D_EXPERT), lambda b, i, j: (0, j)),
                  pl.BlockSpec((D, ec * D_EXPERT), lambda b, i, j: (0, j)),
                  pl.BlockSpec((ec * D_EXPERT, D), lambda b, i, j: (j, 0)),
                  pl.BlockSpec((1, D), lambda b, i, j: (0, 0))],
        out_specs=tok(D),
        scratch_shapes=[pltpu.VMEM((tm, D), F32)],
        compiler_params=pltpu.CompilerParams(
            dimension_semantics=("arbitrary", "arbitrary", "arbitrary"),
            vmem_limit_bytes=VMEM_LIMIT),
        name="moe",
    )(h, gates, x1, mod3, wg, wu, wd, g_final)


def _tile(n, pref):
    t = min(n, pref)
    assert n % t == 0, (n, t)
    return t


def kernel(x, c, w_ada, b_ada, g_attn, w_in, g_kv, w_uk, w_uv, lam_q1, lam_k1, lam_q2, lam_k2,
           g_sub, w_out, g_moe, w_group, b_group, w_router, b_router, w_gate, w_up, w_down, g_final):
    B, S, D = x.shape
    assert D == D_MODEL and w_ada.shape[0] == 1
    assert S % 512 == 0 and S <= POS_SPLIT * 256
    topk = min(TOPK_MAX, S // 4)
    l = 0
    lam0 = 0.8 - 0.6 * math.exp(-0.3 * l)

    mod3 = _mod_call(c, w_ada[l], b_ada[l].reshape(1, -1)).reshape(B, 6, D)

    w_cat, w_pair = _proj_weights(w_in[l], w_uk[l])
    qabs, ckv, iq, ik4, iw, fq, fk, fv = _proj_call(
        x, mod3, g_attn[l].reshape(1, D), w_cat, w_pair, g_kv[l].reshape(1, -1),
        _feature_consts(), _tile(S, 512))

    tk_dsa = 512
    ustrict = jnp.asarray(np.triu(np.ones((tk_dsa, tk_dsa), np.float32), 1), BF16)
    z = jnp.zeros_like(w_uv[l, 0])
    wuv_pair = jnp.stack([
        jnp.concatenate([jnp.concatenate([w_uv[l, 2 * p], z], axis=1),
                         jnp.concatenate([z, w_uv[l, 2 * p + 1]], axis=1)], axis=0)
        for p in range(DSA_HEADS // 2)]).astype(BF16)
    o_dsa = _dsa_call(qabs, iq, iw, ik4, ckv, ustrict, wuv_pair, 128, tk_dsa, topk)

    lam_vecs = jnp.concatenate([lam_q1[l][None], lam_k1[l][None], lam_q2[l][None], lam_k2[l][None]],
                               axis=0).astype(F32)
    o_diff = _diff_call(fq, fk, fv, lam_vecs, g_sub[l].reshape(1, -1), 512, 512, lam0)

    wr = jnp.pad(jnp.concatenate([w_router[l], w_group[l]], axis=1),
                 ((0, 0), (0, LANES - N_EXPERTS - N_GROUPS)))
    wr_hi = wr.astype(BF16)
    wr_lo = (wr - wr_hi.astype(F32)).astype(BF16)
    b_r = jnp.pad(jnp.concatenate([b_router[l], b_group[l]]), (0, LANES - N_EXPERTS - N_GROUPS))
    x1, h2, gates = _route_call(x, o_dsa, o_diff, mod3, w_out[l].astype(BF16),
                                g_moe[l].reshape(1, D), wr_hi, wr_lo, b_r.reshape(1, LANES),
                                _tile(S, 512))

    wg = jnp.swapaxes(w_gate[l], 0, 1).reshape(D, N_EXPERTS * D_EXPERT).astype(BF16)
    wu = jnp.swapaxes(w_up[l], 0, 1).reshape(D, N_EXPERTS * D_EXPERT).astype(BF16)
    wd = w_down[l].reshape(N_EXPERTS * D_EXPERT, D).astype(BF16)
    return _moe_call(h2, gates, x1, mod3, wg, wu, wd, g_final.reshape(1, D), _tile(S, 512), 4)
```

```python
import functools
import math

import jax
import jax.numpy as jnp
import numpy as np
from jax import lax
from jax.experimental import pallas as pl
from jax.experimental.pallas import tpu as pltpu

F32 = jnp.float32
BF16 = jnp.bfloat16

D_MODEL = 1024
DSA_HEADS = 8
DSA_HEAD_DIM = 64
DSA_LATENT = 128
IDX_HEADS = 4
IDX_DIM = 32
TOPK_MAX = 256
DIFF_HEADS = 4
DIFF_QK_DIM = 64
DIFF_V_DIM = 128
N_GROUPS = 4
EXPERTS_PER_GROUP = 8
N_EXPERTS = 32
D_EXPERT = 256
EPS = 1e-6

LANES = 128
BF16_ROWS = 16
POS_SPLIT = 64
NEG_BIG = -1e30
VMEM_LIMIT = 56 * 1024 * 1024
KEY_TILE = 512
VALUE_BISECT_STEPS = 24
UNCHECKED_BISECT_STEPS = 16
COUNT_CHAINS = 4
ATTN_CHAINS = 1

W_DQ = DSA_HEADS * DSA_HEAD_DIM
W_QABS = DSA_HEADS * DSA_LATENT
W_CKV = 2 * LANES
VT_ROWS = DSA_LATENT + BF16_ROWS
ONES_ROW = DSA_LATENT + 2
W_FQ = DIFF_HEADS * 2 * LANES
W_DIFF_QK = DIFF_HEADS * 2 * DIFF_QK_DIM

T_DQ = 0
T_DLAT = T_DQ + W_DQ
T_IQ = T_DLAT + DSA_LATENT
T_IW = T_IQ + IDX_HEADS * IDX_DIM
T_FQ = T_IW + BF16_ROWS
T_FV = T_FQ + W_DIFF_QK
T_ROWS = T_FV + DIFF_HEADS * DIFF_V_DIM
N_IK = 0
N_DLAT = N_IK + LANES
N_FK = N_DLAT + DSA_LATENT
N_COLS = N_FK + W_FQ


def _alibi_slopes(n):
    return [2.0 ** (-8.0 * (i + 1) / n) for i in range(n)]


def _rms(x, g):
    return x * lax.rsqrt(jnp.mean(x * x, axis=-1, keepdims=True) + EPS) * g


def _dot(a, b):
    return jnp.dot(a, b, preferred_element_type=F32)


def _mod_kernel(c_ref, w_ref, b_ref, o_ref):
    c = c_ref[...]
    act = c * jax.nn.sigmoid(c)
    o_ref[...] = jnp.dot(act, w_ref[...], preferred_element_type=F32,
                         precision=lax.Precision.HIGHEST) + b_ref[...]


def _mod_call(c, w_ada, b_ada):
    B, D = c.shape
    n = w_ada.shape[1] // D
    return pl.pallas_call(
        _mod_kernel,
        out_shape=jax.ShapeDtypeStruct((B, n * D), F32),
        grid=(n,),
        in_specs=[pl.BlockSpec((B, D), lambda j: (0, 0)),
                  pl.BlockSpec((D, D), lambda j: (0, j)),
                  pl.BlockSpec((1, D), lambda j: (0, j))],
        out_specs=pl.BlockSpec((B, D), lambda j: (0, j)),
        compiler_params=pltpu.CompilerParams(dimension_semantics=("arbitrary",),
                                             vmem_limit_bytes=VMEM_LIMIT),
        name="mod",
    )(c, w_ada, b_ada)


def _key_feature_rows(pos_row, n_rows):
    r = lax.broadcasted_iota(jnp.int32, (n_rows, pos_row.shape[1]), 0)
    pa = (pos_row // POS_SPLIT).astype(F32)
    pb = (pos_row % POS_SPLIT).astype(F32)
    return jnp.where(r == 0, pa, jnp.where(r == 1, pb, jnp.where(r < 4, 1.0, 0.0)))


def _query_feature_rows(pos_row, n_rows, slope):
    r = lax.broadcasted_iota(jnp.int32, (n_rows, pos_row.shape[1]), 0)
    pa = (pos_row // POS_SPLIT).astype(F32)
    pb = (pos_row % POS_SPLIT).astype(F32)
    return jnp.where(r == 0, POS_SPLIT * slope,
                     jnp.where(r == 1, slope,
                               jnp.where(r == 2, -POS_SPLIT * slope * pa,
                                         jnp.where(r == 3, -slope * pb, 0.0))))


def _proj_kernel(x_ref, mod_ref, g_ref, wn_ref, wt_ref, wpair_ref, gkv_ref, gkvc_ref, fc_ref,
                 qabsT_ref, iqT_ref, iwT_ref, ik_ref, ckv_ref, ckvT_ref, fqT_ref, fk_ref, fvT_ref,
                 *, tm):
    x = x_ref[...]
    h = _rms(x, g_ref[...]) * (1.0 + mod_ref[1:2, :]) + mod_ref[0:1, :]
    hb = h.astype(BF16)
    hT = h.T.astype(BF16)

    base = pl.program_id(1) * tm
    pos_c = base + lax.broadcasted_iota(jnp.int32, (tm, 1), 0)
    pos_r = base + lax.broadcasted_iota(jnp.int32, (1, tm), 1)
    pa = (pos_c // POS_SPLIT).astype(F32)
    pb = (pos_c % POS_SPLIT).astype(F32)

    def feats(row, width):
        return (fc_ref[row:row + 1, 0:width] + fc_ref[row + 1:row + 2, 0:width] * pa
                + fc_ref[row + 2:row + 3, 0:width] * pb)

    ik_ref[...] = _dot(hb, wn_ref[:, N_IK:N_IK + LANES]).astype(BF16)
    dlat = _dot(hb, wn_ref[:, N_DLAT:N_DLAT + DSA_LATENT])
    ckv_ref[:, 0:LANES] = _rms(dlat, gkv_ref[...]).astype(BF16)
    ckv_ref[:, LANES:2 * LANES] = feats(0, LANES).astype(BF16)
    fk_ref[...] = (_dot(hb, wn_ref[:, N_FK:N_FK + W_FQ]) + feats(3, W_FQ)).astype(BF16)

    dqT = _dot(wt_ref[T_DQ:T_DQ + W_DQ, :], hT).astype(BF16)
    for p in range(DSA_HEADS // 2):
        qa = _dot(wpair_ref[p], dqT[p * LANES:(p + 1) * LANES, :])
        qabsT_ref[p * 2 * LANES:(p + 1) * 2 * LANES, :] = qa.astype(BF16)
    dlatT = _dot(wt_ref[T_DLAT:T_DLAT + DSA_LATENT, :], hT)
    inv = lax.rsqrt(jnp.mean(dlatT * dlatT, axis=0, keepdims=True) + EPS)
    kfeat = _key_feature_rows(pos_r, BF16_ROWS).astype(BF16)
    ckvT_ref[0:DSA_LATENT, :] = (dlatT * inv * gkvc_ref[...]).astype(BF16)
    ckvT_ref[DSA_LATENT:VT_ROWS, :] = kfeat
    iqT_ref[...] = _dot(wt_ref[T_IQ:T_IQ + IDX_HEADS * IDX_DIM, :], hT).astype(BF16)
    iwT = _dot(wt_ref[T_IW:T_IW + BF16_ROWS, :], hT)
    iwT_ref[...] = iwT[0:8, :] * (IDX_HEADS ** -0.5 * IDX_DIM ** -0.5)
    fqT = _dot(wt_ref[T_FQ:T_FQ + W_DIFF_QK, :], hT)
    slopes = _alibi_slopes(DIFF_HEADS)
    zpad = jnp.zeros((LANES - DIFF_QK_DIM - BF16_ROWS, tm), BF16)
    for s in range(DIFF_HEADS * 2):
        r0 = s * LANES
        fqT_ref[r0:r0 + DIFF_QK_DIM, :] = fqT[s * DIFF_QK_DIM:(s + 1) * DIFF_QK_DIM, :].astype(BF16)
        fqT_ref[r0 + DIFF_QK_DIM:r0 + DIFF_QK_DIM + BF16_ROWS, :] = _query_feature_rows(
            pos_r, BF16_ROWS, slopes[s // 2]).astype(BF16)
        fqT_ref[r0 + DIFF_QK_DIM + BF16_ROWS:r0 + LANES, :] = zpad
    fvT = _dot(wt_ref[T_FV:T_FV + DIFF_HEADS * DIFF_V_DIM, :], hT)
    for hh in range(DIFF_HEADS):
        fvT_ref[hh, 0:DIFF_V_DIM, :] = fvT[hh * DIFF_V_DIM:(hh + 1) * DIFF_V_DIM, :].astype(BF16)
        fvT_ref[hh, DIFF_V_DIM:VT_ROWS, :] = kfeat


def _feature_consts():
    fc = np.zeros((8, W_FQ), np.float32)
    fc[1, 0] = 1.0
    fc[2, 1] = 1.0
    fc[0, 2] = 1.0
    fc[0, 3] = 1.0
    for s in range(DIFF_HEADS * 2):
        base = s * LANES + DIFF_QK_DIM
        fc[4, base + 0] = 1.0
        fc[5, base + 1] = 1.0
        fc[3, base + 2] = 1.0
        fc[3, base + 3] = 1.0
    return jnp.asarray(fc)


def _proj_weights(w_in, w_uk):
    D = w_in.shape[0]
    pts = np.cumsum([W_DQ, DSA_LATENT, IDX_HEADS * IDX_DIM, IDX_DIM, IDX_HEADS,
                     W_DIFF_QK, W_DIFF_QK])
    dq, dlat, iq, ik, iw, fq, fk, fv = jnp.split(w_in, list(pts), axis=1)
    ik4 = jnp.tile(ik, (1, IDX_HEADS))
    fke = jnp.pad(fk.reshape(D, DIFF_HEADS * 2, DIFF_QK_DIM),
                  ((0, 0), (0, 0), (0, LANES - DIFF_QK_DIM))).reshape(D, W_FQ)
    wn = jnp.concatenate([ik4, dlat, fke], axis=1).astype(BF16)
    iwp = jnp.pad(iw, ((0, 0), (0, BF16_ROWS - IDX_HEADS)))
    wt = jnp.concatenate([dq, dlat, iq, iwp, fq * (DIFF_QK_DIM ** -0.5), fv], axis=1).T.astype(BF16)
    uk = w_uk * (DSA_HEAD_DIM ** -0.5)
    z = jnp.zeros_like(uk[0])
    pairs = [jnp.concatenate([jnp.concatenate([uk[2 * p], z], axis=1),
                              jnp.concatenate([z, uk[2 * p + 1]], axis=1)], axis=0)
             for p in range(DSA_HEADS // 2)]
    return wn, wt, jnp.stack(pairs).astype(BF16)


def _proj_call(x, mod3, g_attn, wn, wt, w_pair, g_kv, fc, tm):
    B, S, D = x.shape
    nt = S // tm
    kern = functools.partial(_proj_kernel, tm=tm)
    tok = lambda w: pl.BlockSpec((None, tm, w), lambda b, i: (b, i, 0))
    tokT = lambda r: pl.BlockSpec((None, r, tm), lambda b, i: (b, 0, i))
    full = lambda a: pl.BlockSpec(a.shape, lambda b, i: (0,) * a.ndim)
    g_kv_col = g_kv.reshape(-1, 1)
    out_shape = [jax.ShapeDtypeStruct((B, W_QABS, S), BF16),
                 jax.ShapeDtypeStruct((B, LANES, S), BF16),
                 jax.ShapeDtypeStruct((B, 8, S), F32),
                 jax.ShapeDtypeStruct((B, S, LANES), BF16),
                 jax.ShapeDtypeStruct((B, S, W_CKV), BF16),
                 jax.ShapeDtypeStruct((B, nt, VT_ROWS, tm), BF16),
                 jax.ShapeDtypeStruct((B, W_FQ, S), BF16),
                 jax.ShapeDtypeStruct((B, S, W_FQ), BF16),
                 jax.ShapeDtypeStruct((B, DIFF_HEADS, nt, VT_ROWS, tm), BF16)]
    out_specs = [tokT(W_QABS), tokT(LANES), tokT(8), tok(LANES), tok(W_CKV),
                 pl.BlockSpec((None, None, VT_ROWS, tm), lambda b, i: (b, i, 0, 0)),
                 tokT(W_FQ), tok(W_FQ),
                 pl.BlockSpec((None, DIFF_HEADS, None, VT_ROWS, tm), lambda b, i: (b, 0, i, 0, 0))]
    return pl.pallas_call(
        kern,
        out_shape=out_shape,
        grid=(B, nt),
        in_specs=[tok(D),
                  pl.BlockSpec((None, 6, D), lambda b, i: (b, 0, 0)),
                  full(g_attn), full(wn), full(wt), full(w_pair), full(g_kv), full(g_kv_col),
                  full(fc)],
        out_specs=out_specs,
        compiler_params=pltpu.CompilerParams(dimension_semantics=("arbitrary", "arbitrary"),
                                             vmem_limit_bytes=VMEM_LIMIT),
        name="proj",
    )(x, mod3, g_attn, wn, wt, w_pair, g_kv, g_kv_col, fc)


def _float_code(x):
    b = lax.bitcast_convert_type(x, jnp.int32)
    return b ^ (lax.shift_right_arithmetic(b, 31) & jnp.int32(0x7FFFFFFF))


def _float_decode(c):
    b = c ^ (lax.shift_right_arithmetic(c, 31) & jnp.int32(0x7FFFFFFF))
    return lax.bitcast_convert_type(b, F32)


def _dsa_kernel(qabsT_ref, iqT_ref, iwT_ref, ik_ref, ckv_ref, ckvT_ref, lstrict_ref, wuvT_ref, o_ref,
                sc_ref, qst_ref, qt_ref, acc_ref, m_ref, *, tq, tk, topk):
    qi = pl.program_id(1)
    nkb = (qi * tq) // tk + 1
    kf = float(topk)
    slopes = _alibi_slopes(DSA_HEADS)
    q_pos = qi * tq + lax.broadcasted_iota(jnp.int32, (1, tq), 1)

    iqT = iqT_ref[...]
    rowi = lax.broadcasted_iota(jnp.int32, (LANES, tq), 0)
    for hh in range(IDX_HEADS):
        qst_ref[:, hh * tq:(hh + 1) * tq] = jnp.where((rowi // IDX_DIM) == hh, iqT, jnp.zeros_like(iqT))
    iw = iwT_ref[...]
    wrow = [iw[hh:hh + 1, :] for hh in range(IDX_HEADS)]

    def scores(kb):
        kblk = ik_ref[pl.ds(pl.multiple_of(kb * tk, tk), tk), :]
        a = _dot(kblk, qst_ref[...])
        sc = jnp.maximum(a[:, 0:tq], 0.0) * wrow[0]
        for hh in range(1, IDX_HEADS):
            sc = sc + jnp.maximum(a[:, hh * tq:(hh + 1) * tq], 0.0) * wrow[hh]
        return sc

    def score_body(kb, carry):
        mn, mx = carry
        sc = scores(kb)
        sc_ref[kb] = sc
        return (jnp.minimum(mn, jnp.min(sc, axis=0, keepdims=True)),
                jnp.maximum(mx, jnp.max(sc, axis=0, keepdims=True)))

    mn, mx = lax.fori_loop(0, nkb - 1, score_body,
                           (jnp.full((1, tq), jnp.inf, F32), jnp.full((1, tq), -jnp.inf, F32)))
    last = nkb - 1
    sc = scores(last)
    causal = lax.broadcasted_iota(jnp.int32, (tk, tq), 0) + last * tk <= q_pos
    sc_ref[last] = jnp.where(causal, sc, jnp.nan)
    mn = jnp.minimum(mn, jnp.min(jnp.where(causal, sc, jnp.inf), axis=0, keepdims=True))
    mx = jnp.maximum(mx, jnp.max(jnp.where(causal, sc, -jnp.inf), axis=0, keepdims=True))

    def count(pred):
        def body(kb, acc):
            parts = [None] * COUNT_CHAINS
            for r in range(tk // 8):
                v = jnp.where(pred(sc_ref[kb, r * 8:(r + 1) * 8, :]), 1.0, 0.0)
                c = r % COUNT_CHAINS
                parts[c] = v if parts[c] is None else parts[c] + v
            return acc + ((parts[0] + parts[1]) + (parts[2] + parts[3]))
        acc = lax.fori_loop(0, nkb, body, jnp.zeros((8, tq), F32))
        return jnp.sum(acc, axis=0, keepdims=True)

    n_causal = (q_pos + 1).astype(F32)
    done0 = n_causal <= kf

    def bisect(st, value_mid):
        lo, hi, c_lo, theta, done = st
        th8 = jnp.broadcast_to(theta, (8, tq))
        c = count(lambda v: v >= th8)
        ge = c >= kf
        live = done == 0.0
        lo = jnp.where(live & ge, theta, lo)
        c_lo = jnp.where(live & ge, c, c_lo)
        hi = jnp.where(live & jnp.logical_not(ge), theta, hi)
        if value_mid:
            nxt = 0.5 * lo + 0.5 * hi
        else:
            cl, ch = _float_code(lo), _float_code(hi)
            nxt = _float_decode((cl & ch) + lax.shift_right_arithmetic(cl ^ ch, 1))
        inside = (nxt > lo) & (nxt < hi)
        done = jnp.where((c_lo == kf) | jnp.logical_not(inside), 1.0, done)
        return lo, hi, c_lo, nxt, done

    def checked(value_mid, max_steps):
        def cond(c):
            return jnp.logical_and(c[1] > 0.0, c[2] < max_steps)

        def body(c):
            st = bisect(c[0], value_mid)
            return st, jnp.sum(1.0 - st[4]), c[2] + 1
        return cond, body

    st = (mn, mx, n_causal, mx, jnp.where(done0, 1.0, 0.0))
    st = lax.fori_loop(0, UNCHECKED_BISECT_STEPS, lambda i, s: bisect(s, True), st)
    cond, body = checked(True, VALUE_BISECT_STEPS - UNCHECKED_BISECT_STEPS)
    st, active, _ = lax.while_loop(cond, body, (st, jnp.sum(1.0 - st[4]), jnp.int32(0)))
    cond, body = checked(False, 34)
    st, _, _ = lax.while_loop(cond, body, (st, active, jnp.int32(0)))
    tau, c_ge = st[0], st[2]

    @pl.when(jnp.max(c_ge) > kf)
    def _():
        tau8 = jnp.broadcast_to(tau, (8, tq))
        quota = kf - count(lambda v: v > tau8)

        def tie_body(kb, seen):
            s = sc_ref[kb]
            eq = s == tau
            eqf = jnp.where(eq, 1.0, 0.0)
            rank = _dot(lstrict_ref[...], eqf.astype(BF16)) + seen
            sc_ref[kb] = jnp.where(eq & (rank >= quota), jnp.nan, s)
            return seen + jnp.sum(eqf, axis=0, keepdims=True)

        lax.fori_loop(0, nkb, tie_body, jnp.zeros((1, tq), F32))

    for hh in range(DSA_HEADS):
        qt_ref[0:DSA_LATENT, hh * tq:(hh + 1) * tq] = qabsT_ref[hh * DSA_LATENT:(hh + 1) * DSA_LATENT, :]
        qt_ref[DSA_LATENT:2 * DSA_LATENT, hh * tq:(hh + 1) * tq] = _query_feature_rows(
            q_pos, DSA_LATENT, slopes[hh]).astype(BF16)
    acc_ref[...] = jnp.zeros_like(acc_ref)
    m_ref[...] = jnp.full_like(m_ref, NEG_BIG)

    def attn_body(kb, carry):
        kx = ckv_ref[pl.ds(pl.multiple_of(kb * tk, tk), tk), :]
        vT = ckvT_ref[kb]
        keep = sc_ref[kb] >= tau
        qt = qt_ref[...]
        m_all = m_ref[...]
        m_out, alphas, pvs = [], [], []
        for ch in range(ATTN_CHAINS):
            hpc = DSA_HEADS // ATTN_CHAINS
            l0 = ch * hpc * tq
            s = _dot(kx, qt[:, l0:l0 + hpc * tq])
            ps = []
            for j in range(hpc):
                c0 = l0 + j * tq
                sh = jnp.where(keep, s[:, j * tq:(j + 1) * tq], NEG_BIG)
                m_old = m_all[:, c0:c0 + tq]
                m_new = jnp.maximum(m_old, jnp.max(sh, axis=0, keepdims=True))
                ps.append(jnp.exp(sh - m_new).astype(BF16))
                alphas.append(jnp.exp(m_old - m_new))
                m_out.append(m_new)
            pvs.append(_dot(vT, jnp.concatenate(ps, axis=1)))
        m_ref[...] = jnp.concatenate(m_out, axis=1)
        acc_ref[...] = jnp.concatenate(alphas, axis=1) * acc_ref[...] + jnp.concatenate(pvs, axis=1)
        return carry

    lax.fori_loop(0, nkb, attn_body, 0)

    acc = acc_ref[...]
    lat = acc[0:DSA_LATENT, :] / acc[ONES_ROW:ONES_ROW + 1, :]
    for p in range(DSA_HEADS // 2):
        pair = lat[:, 2 * p * tq:(2 * p + 2) * tq]
        pair = jnp.concatenate([pair[:, 0:tq], pair[:, tq:2 * tq]], axis=0).astype(BF16)
        o_ref[:, p * LANES:(p + 1) * LANES] = _dot(wuvT_ref[p], pair).T.astype(BF16)


def _dsa_call(qabsT, iqT, iwT, ik4, ckv, ckvT, lstrict, wuvT_pair, tq, tk, topk):
    B, _, S = qabsT.shape
    nkb = S // tk
    kern = functools.partial(_dsa_kernel, tq=tq, tk=tk, topk=topk)
    blkT = lambda r: pl.BlockSpec((None, r, tq), lambda b, i: (b, 0, i))
    per_b = lambda w: pl.BlockSpec((None, S, w), lambda b, i: (b, 0, 0))
    full = lambda a: pl.BlockSpec(a.shape, lambda b, i: (0,) * a.ndim)
    return pl.pallas_call(
        kern,
        out_shape=jax.ShapeDtypeStruct((B, S, W_DQ), BF16),
        grid=(B, S // tq),
        in_specs=[blkT(W_QABS), blkT(LANES), blkT(8), per_b(LANES), per_b(W_CKV),
                  pl.BlockSpec((None, nkb, VT_ROWS, tk), lambda b, i: (b, 0, 0, 0)),
                  full(lstrict), full(wuvT_pair)],
        out_specs=pl.BlockSpec((None, tq, W_DQ), lambda b, i: (b, i, 0)),
        scratch_shapes=[pltpu.VMEM((nkb, tk, tq), F32),
                        pltpu.VMEM((LANES, IDX_HEADS * tq), BF16),
                        pltpu.VMEM((2 * DSA_LATENT, DSA_HEADS * tq), BF16),
                        pltpu.VMEM((VT_ROWS, DSA_HEADS * tq), F32),
                        pltpu.VMEM((1, DSA_HEADS * tq), F32)],
        compiler_params=pltpu.CompilerParams(dimension_semantics=("arbitrary", "arbitrary"),
                                             vmem_limit_bytes=VMEM_LIMIT),
        name="dsa",
    )(qabsT, iqT, iwT, ik4, ckv, ckvT, lstrict, wuvT_pair)


def _diff_kernel(fqT_ref, fk_ref, fvT_ref, lam_ref, gsub_ref, o_ref, acc_ref, m_ref, *, tq, tk, lam0):
    qi = pl.program_id(2)
    q_pos = qi * tq + lax.broadcasted_iota(jnp.int32, (1, tq), 1)
    acc_ref[...] = jnp.zeros_like(acc_ref)
    m_ref[...] = jnp.full_like(m_ref, NEG_BIG)

    def block(kb, masked):
        kx = fk_ref[pl.ds(pl.multiple_of(kb * tk, tk), tk), :]
        vT = fvT_ref[kb]
        qT = fqT_ref[...]
        m_all = m_ref[...]
        new = []
        for m in range(2):
            s = _dot(kx[:, m * LANES:(m + 1) * LANES], qT[m * LANES:(m + 1) * LANES, :])
            if masked:
                causal = lax.broadcasted_iota(jnp.int32, (tk, tq), 0) + kb * tk <= q_pos
                s = jnp.where(causal, s, NEG_BIG)
            m_old = m_all[m]
            m_new = jnp.maximum(m_old, jnp.max(s, axis=0, keepdims=True))
            p = jnp.exp(s - m_new).astype(BF16)
            new.append((m_new, jnp.exp(m_old - m_new), _dot(vT, p)))
        for m in range(2):
            m_ref[m] = new[m][0]
            acc_ref[m] = new[m][1] * acc_ref[m] + new[m][2]

    def body(kb, carry):
        block(kb, False)
        return carry

    n_full = (qi * tq) // tk
    lax.fori_loop(0, n_full, body, 0)
    for d in range((tq + tk - 1) // tk):
        block(n_full + d, True)

    lv = lam_ref[...]
    lam = (jnp.exp(jnp.sum(lv[0:1] * lv[1:2], axis=-1, keepdims=True))
           - jnp.exp(jnp.sum(lv[2:3] * lv[3:4], axis=-1, keepdims=True)) + lam0)
    a1 = acc_ref[0]
    a2 = acc_ref[1]
    o = (a1[0:DIFF_V_DIM, :] / a1[ONES_ROW:ONES_ROW + 1, :]
         - lam * (a2[0:DIFF_V_DIM, :] / a2[ONES_ROW:ONES_ROW + 1, :]))
    o = o * lax.rsqrt(jnp.mean(o * o, axis=0, keepdims=True) + EPS) * gsub_ref[...] * (1.0 - lam0)
    o_ref[...] = o.T.astype(BF16)


def _diff_call(fqT, fk, fvT, lam_vecs, g_sub_col, tq, tk, lam0):
    B, S, _ = fk.shape
    nkb = S // tk
    kern = functools.partial(_diff_kernel, tq=tq, tk=tk, lam0=lam0)
    full = lambda a: pl.BlockSpec(a.shape, lambda b, h, i: (0,) * a.ndim)
    return pl.pallas_call(
        kern,
        out_shape=jax.ShapeDtypeStruct((B, S, DIFF_HEADS * DIFF_V_DIM), BF16),
        grid=(B, DIFF_HEADS, S // tq),
        in_specs=[pl.BlockSpec((None, 2 * LANES, tq), lambda b, h, i: (b, h, i)),
                  pl.BlockSpec((None, S, 2 * LANES), lambda b, h, i: (b, 0, h)),
                  pl.BlockSpec((None, None, nkb, VT_ROWS, tk), lambda b, h, i: (b, h, 0, 0, 0)),
                  full(lam_vecs), full(g_sub_col)],
        out_specs=pl.BlockSpec((None, tq, DIFF_V_DIM), lambda b, h, i: (b, i, h)),
        scratch_shapes=[pltpu.VMEM((2, VT_ROWS, tq), F32),
                        pltpu.VMEM((2, 1, tq), F32)],
        compiler_params=pltpu.CompilerParams(
            dimension_semantics=("arbitrary", "arbitrary", "arbitrary"),
            vmem_limit_bytes=VMEM_LIMIT),
        name="diff",
    )(fqT, fk, fvT, lam_vecs, g_sub_col)


def _route_kernel(x_ref, od_ref, of_ref, mod_ref, wout_ref, g_ref, wr_hi_ref, wr_lo_ref, br_ref,
                  x1_ref, h_ref, gates_ref, *, tm):
    attn = (_dot(od_ref[...], wout_ref[0:W_DQ, :]) + _dot(of_ref[...], wout_ref[W_DQ:, :]))
    x1 = x_ref[...] + mod_ref[2:3, :] * attn
    x1_ref[...] = x1
    h = _rms(x1, g_ref[...]) * (1.0 + mod_ref[4:5, :]) + mod_ref[3:4, :]
    h_hi = h.astype(BF16)
    h_ref[...] = h_hi
    h_lo = (h - h_hi.astype(F32)).astype(BF16)
    w_hi = wr_hi_ref[...]
    logits = (_dot(h_hi, w_hi) + _dot(h_lo, w_hi) + _dot(h_hi, wr_lo_ref[...])) + br_ref[...]

    lane = lax.broadcasted_iota(jnp.int32, (tm, LANES), 1)
    big = jnp.int32(4 * LANES)
    neg = -jnp.inf
    is_g = (lane >= N_EXPERTS) & (lane < N_EXPERTS + N_GROUPS)
    gl = jnp.where(is_g, logits, neg)
    gmax = jnp.max(gl, axis=-1, keepdims=True)
    g_lane = jnp.min(jnp.where(gl == gmax, lane, big), axis=-1, keepdims=True)
    g_sel = g_lane - N_EXPERTS
    p_g = 1.0 / jnp.sum(jnp.where(is_g, jnp.exp(gl - gmax), 0.0), axis=-1, keepdims=True)
    in_grp = (lane < N_EXPERTS) & ((lane // EXPERTS_PER_GROUP) == g_sel)
    el = jnp.where(in_grp, logits, neg)
    v1 = jnp.max(el, axis=-1, keepdims=True)
    i1 = jnp.min(jnp.where(el == v1, lane, big), axis=-1, keepdims=True)
    el2 = jnp.where(lane == i1, neg, el)
    v2 = jnp.max(el2, axis=-1, keepdims=True)
    i2 = jnp.min(jnp.where(el2 == v2, lane, big), axis=-1, keepdims=True)
    e = jnp.exp(v2 - v1)
    w1 = 1.0 / (1.0 + e)
    w2 = e * w1
    gates_ref[...] = jnp.where(lane == i1, w1 * p_g, jnp.where(lane == i2, w2 * p_g, 0.0))


def _route_call(x, o_dsa, o_diff, mod3, w_out, g_moe, wr_hi, wr_lo, b_r, tm):
    B, S, D = x.shape
    kern = functools.partial(_route_kernel, tm=tm)
    tok = lambda w: pl.BlockSpec((None, tm, w), lambda b, i: (b, i, 0))
    full = lambda a: pl.BlockSpec(a.shape, lambda b, i: (0,) * a.ndim)
    return pl.pallas_call(
        kern,
        out_shape=[jax.ShapeDtypeStruct((B, S, D), F32), jax.ShapeDtypeStruct((B, S, D), BF16),
                   jax.ShapeDtypeStruct((B, S, LANES), F32)],
        grid=(B, S // tm),
        in_specs=[tok(D), tok(W_DQ), tok(DIFF_HEADS * DIFF_V_DIM),
                  pl.BlockSpec((None, 6, D), lambda b, i: (b, 0, 0)),
                  full(w_out), full(g_moe), full(wr_hi), full(wr_lo), full(b_r)],
        out_specs=[tok(D), tok(D), tok(LANES)],
        compiler_params=pltpu.CompilerParams(dimension_semantics=("arbitrary", "arbitrary"),
                                             vmem_limit_bytes=VMEM_LIMIT),
        name="route",
    )(x, o_dsa, o_diff, mod3, w_out, g_moe, wr_hi, wr_lo, b_r)


def _moe_kernel(h_ref, gates_ref, x1_ref, mod_ref, wg_ref, wu_ref, wd_ref, gf_ref, o_ref, acc_ref,
                *, tm, ec):
    j = pl.program_id(2)
    nc = pl.num_programs(2)

    @pl.when(j == 0)
    def _():
        acc_ref[...] = jnp.zeros_like(acc_ref)

    h = h_ref[...]
    hg = _dot(h, wg_ref[...])
    hu = _dot(h, wu_ref[...])
    hid = hg * jax.nn.sigmoid(hg) * hu
    gates = gates_ref[...]
    lane = lax.broadcasted_iota(jnp.int32, (tm, LANES), 1)
    parts = []
    for e in range(ec):
        gcol = jnp.sum(jnp.where(lane == j * ec + e, gates, 0.0), axis=-1, keepdims=True)
        parts.append((hid[:, e * D_EXPERT:(e + 1) * D_EXPERT] * gcol).astype(BF16))
    hs = jnp.concatenate(parts, axis=1)
    acc_ref[...] += _dot(hs, wd_ref[...])

    @pl.when(j == nc - 1)
    def _():
        x2 = x1_ref[...] + mod_ref[5:6, :] * acc_ref[...]
        o_ref[...] = _rms(x2, gf_ref[...])


def _moe_call(h, gates, x1, mod3, wg, wu, wd, g_final, tm, ec):
    B, S, D = x1.shape
    kern = functools.partial(_moe_kernel, tm=tm, ec=ec)
    tok = lambda w: pl.BlockSpec((None, tm, w), lambda b, i, j: (b, i, 0))
    return pl.pallas_call(
        kern,
        out_shape=jax.ShapeDtypeStruct((B, S, D), F32),
        grid=(B, S // tm, N_EXPERTS // ec),
        in_specs=[tok(D), tok(LANES), tok(D),
                  pl.BlockSpec((None, 6, D), lambda b, i, j: (b, 0, 0)),
                  pl.BlockSpec((D, ec * D_EXPERT), lambda b, i, j: (0, j)),
                  pl.BlockSpec((D, ec * D_EXPERT), lambda b, i, j: (0, j)),
                  pl.BlockSpec((ec * D_EXPERT, D), lambda b, i, j: (j, 0)),
                  pl.BlockSpec((1, D), lambda b, i, j: (0, 0))],
        out_specs=tok(D),
        scratch_shapes=[pltpu.VMEM((tm, D), F32)],
        compiler_params=pltpu.CompilerParams(
            dimension_semantics=("arbitrary", "arbitrary", "arbitrary"),
            vmem_limit_bytes=VMEM_LIMIT),
        name="moe",
    )(h, gates, x1, mod3, wg, wu, wd, g_final)


def _tile(n, pref):
    t = min(n, pref)
    assert n % t == 0, (n, t)
    return t


def kernel(x, c, w_ada, b_ada, g_attn, w_in, g_kv, w_uk, w_uv, lam_q1, lam_k1, lam_q2, lam_k2,
           g_sub, w_out, g_moe, w_group, b_group, w_router, b_router, w_gate, w_up, w_down, g_final):
    B, S, D = x.shape
    assert D == D_MODEL and w_ada.shape[0] == 1
    assert S % KEY_TILE == 0 and S <= POS_SPLIT * 256
    topk = min(TOPK_MAX, S // 4)
    l = 0
    lam0 = 0.8 - 0.6 * math.exp(-0.3 * l)

    mod3 = _mod_call(c, w_ada[l], b_ada[l].reshape(1, -1)).reshape(B, 6, D)

    wn, wt, w_pair = _proj_weights(w_in[l], w_uk[l])
    qabsT, iqT, iwT, ik4, ckv, ckvT, fqT, fk, fvT = _proj_call(
        x, mod3, g_attn[l].reshape(1, D), wn, wt, w_pair, g_kv[l].reshape(1, -1),
        _feature_consts(), KEY_TILE)

    lstrict = jnp.asarray(np.tril(np.ones((KEY_TILE, KEY_TILE), np.float32), -1), BF16)
    uvT = jnp.swapaxes(w_uv[l], 1, 2)
    z = jnp.zeros_like(uvT[0])
    wuvT_pair = jnp.stack([
        jnp.concatenate([jnp.concatenate([uvT[2 * p], z], axis=1),
                         jnp.concatenate([z, uvT[2 * p + 1]], axis=1)], axis=0)
        for p in range(DSA_HEADS // 2)]).astype(BF16)
    o_dsa = _dsa_call(qabsT, iqT, iwT, ik4, ckv, ckvT, lstrict, wuvT_pair, LANES, KEY_TILE, topk)

    lam_vecs = jnp.concatenate([lam_q1[l][None], lam_k1[l][None], lam_q2[l][None], lam_k2[l][None]],
                               axis=0).astype(F32)
    o_diff = _diff_call(fqT, fk, fvT, lam_vecs, g_sub[l].reshape(-1, 1), KEY_TILE, KEY_TILE, lam0)

    wr = jnp.pad(jnp.concatenate([w_router[l], w_group[l]], axis=1),
                 ((0, 0), (0, LANES - N_EXPERTS - N_GROUPS)))
    wr_hi = wr.astype(BF16)
    wr_lo = (wr - wr_hi.astype(F32)).astype(BF16)
    b_r = jnp.pad(jnp.concatenate([b_router[l], b_group[l]]), (0, LANES - N_EXPERTS - N_GROUPS))
    x1, h2, gates = _route_call(x, o_dsa, o_diff, mod3, w_out[l].astype(BF16),
                                g_moe[l].reshape(1, D), wr_hi, wr_lo, b_r.reshape(1, LANES),
                                _tile(S, 512))

    wg = jnp.swapaxes(w_gate[l], 0, 1).reshape(D, N_EXPERTS * D_EXPERT).astype(BF16)
    wu = jnp.swapaxes(w_up[l], 0, 1).reshape(D, N_EXPERTS * D_EXPERT).astype(BF16)
    wd = w_down[l].reshape(N_EXPERTS * D_EXPERT, D).astype(BF16)
    return _moe_call(h2, gates, x1, mod3, wg, wu, wd, g_final.reshape(1, D), _tile(S, 512), 4)
```

```python
import functools
import math

import jax
import jax.numpy as jnp
import numpy as np
from jax import lax
from jax.experimental import pallas as pl
from jax.experimental.pallas import tpu as pltpu

F32 = jnp.float32
BF16 = jnp.bfloat16

D_MODEL = 1024
DSA_HEADS = 8
DSA_HEAD_DIM = 64
DSA_LATENT = 128
IDX_HEADS = 4
IDX_DIM = 32
TOPK_MAX = 256
DIFF_HEADS = 4
DIFF_QK_DIM = 64
DIFF_V_DIM = 128
N_GROUPS = 4
EXPERTS_PER_GROUP = 8
N_EXPERTS = 32
D_EXPERT = 256
EPS = 1e-6

LANES = 128
BF16_ROWS = 16
POS_SPLIT = 64
NEG_BIG = -1e30
VMEM_LIMIT = 56 * 1024 * 1024
KEY_TILE = 512
VALUE_BISECT_STEPS = 24
UNCHECKED_BISECT_STEPS = 16
COUNT_CHAINS = 4
ATTN_CHAINS = 1

W_DQ = DSA_HEADS * DSA_HEAD_DIM
W_QABS = DSA_HEADS * DSA_LATENT
W_CKV = 2 * LANES
VT_ROWS = DSA_LATENT + BF16_ROWS
ONES_ROW = DSA_LATENT + 2
N_FEATS = 5
KMAX_LANE_CKV = 8
L_MIN = 1e-30
W_FQ = DIFF_HEADS * 2 * LANES
W_DIFF_QK = DIFF_HEADS * 2 * DIFF_QK_DIM

T_DQ = 0
T_DLAT = T_DQ + W_DQ
T_IQ = T_DLAT + DSA_LATENT
T_IW = T_IQ + IDX_HEADS * IDX_DIM
T_FQ = T_IW + BF16_ROWS
T_FV = T_FQ + W_DIFF_QK
T_ROWS = T_FV + DIFF_HEADS * DIFF_V_DIM
N_IK = 0
N_DLAT = N_IK + LANES
N_FK = N_DLAT + DSA_LATENT
N_COLS = N_FK + W_FQ


def _alibi_slopes(n):
    return [2.0 ** (-8.0 * (i + 1) / n) for i in range(n)]


def _rms(x, g):
    return x * lax.rsqrt(jnp.mean(x * x, axis=-1, keepdims=True) + EPS) * g


def _dot(a, b):
    return jnp.dot(a, b, preferred_element_type=F32)


def _mod_kernel(c_ref, w_ref, b_ref, o_ref):
    c = c_ref[...]
    act = c * jax.nn.sigmoid(c)
    o_ref[...] = jnp.dot(act, w_ref[...], preferred_element_type=F32,
                         precision=lax.Precision.HIGHEST) + b_ref[...]


def _mod_call(c, w_ada, b_ada):
    B, D = c.shape
    n = w_ada.shape[1] // D
    return pl.pallas_call(
        _mod_kernel,
        out_shape=jax.ShapeDtypeStruct((B, n * D), F32),
        grid=(n,),
        in_specs=[pl.BlockSpec((B, D), lambda j: (0, 0)),
                  pl.BlockSpec((D, D), lambda j: (0, j)),
                  pl.BlockSpec((1, D), lambda j: (0, j))],
        out_specs=pl.BlockSpec((B, D), lambda j: (0, j)),
        compiler_params=pltpu.CompilerParams(dimension_semantics=("arbitrary",),
                                             vmem_limit_bytes=VMEM_LIMIT),
        name="mod",
    )(c, w_ada, b_ada)


def _key_feature_rows(pos_row, n_rows):
    r = lax.broadcasted_iota(jnp.int32, (n_rows, pos_row.shape[1]), 0)
    pa = (pos_row // POS_SPLIT).astype(F32)
    pb = (pos_row % POS_SPLIT).astype(F32)
    return jnp.where(r == 0, pa, jnp.where(r == 1, pb, jnp.where(r < N_FEATS, 1.0, 0.0)))


def _query_feature_rows(pos_row, n_rows, slope, shift):
    r = lax.broadcasted_iota(jnp.int32, (n_rows, pos_row.shape[1]), 0)
    pa = (pos_row // POS_SPLIT).astype(F32)
    pb = (pos_row % POS_SPLIT).astype(F32)
    return jnp.where(r == 0, POS_SPLIT * slope,
                     jnp.where(r == 1, slope,
                               jnp.where(r == 2, -POS_SPLIT * slope * pa,
                                         jnp.where(r == 3, -slope * pb,
                                                   jnp.where(r == 4, -shift, 0.0)))))


def _shift_bound(qn2, kmax2, extra):
    b = jnp.sqrt(qn2 * kmax2) + extra
    return b + jnp.abs(b) * (2.0 ** -6) + 2.0 ** -20


def _tile_max(kmax_ref, last_tile, lane_idx):
    x = kmax_ref[...]
    t = lax.broadcasted_iota(jnp.int32, x.shape, 0)
    ln = lax.broadcasted_iota(jnp.int32, x.shape, 2)
    x = jnp.where((t <= last_tile) & (ln == lane_idx), x, 0.0)
    return jnp.max(jnp.max(x, axis=0), axis=1, keepdims=True)[0:1, :]


def _proj_kernel(x_ref, mod_ref, g_ref, wn_ref, wt_ref, wpair_ref, gkv_ref, gkvc_ref, fc_ref, sel_ref,
                 qabsT_ref, iqT_ref, iwT_ref, ik_ref, ckv_ref, ckvT_ref, fqT_ref, fk_ref, fvT_ref,
                 kmax_ref, *, tm):
    x = x_ref[...]
    h = _rms(x, g_ref[...]) * (1.0 + mod_ref[1:2, :]) + mod_ref[0:1, :]
    hb = h.astype(BF16)
    hT = h.T.astype(BF16)

    base = pl.program_id(1) * tm
    pos_c = base + lax.broadcasted_iota(jnp.int32, (tm, 1), 0)
    pos_r = base + lax.broadcasted_iota(jnp.int32, (1, tm), 1)
    pa = (pos_c // POS_SPLIT).astype(F32)
    pb = (pos_c % POS_SPLIT).astype(F32)

    def feats(row, width):
        return (fc_ref[row:row + 1, 0:width] + fc_ref[row + 1:row + 2, 0:width] * pa
                + fc_ref[row + 2:row + 3, 0:width] * pb)

    ik_ref[...] = _dot(hb, wn_ref[:, N_IK:N_IK + LANES]).astype(BF16)
    dlat = _dot(hb, wn_ref[:, N_DLAT:N_DLAT + DSA_LATENT])
    ckv_b = _rms(dlat, gkv_ref[...]).astype(BF16)
    ckv_ref[:, 0:LANES] = ckv_b
    ckv_ref[:, LANES:2 * LANES] = feats(0, LANES).astype(BF16)
    fk = _dot(hb, wn_ref[:, N_FK:N_FK + W_FQ])
    fk_ref[...] = (fk + feats(3, W_FQ)).astype(BF16)
    sq = jnp.concatenate([fk.astype(BF16).astype(F32), ckv_b.astype(F32)], axis=1)
    sq_up = (sq * sq * (1.0 + 2.0 ** -7)).astype(BF16)
    kmax_ref[...] = jnp.broadcast_to(jnp.max(_dot(sq_up, sel_ref[...]), axis=0, keepdims=True),
                                     (8, LANES))

    dqT = _dot(wt_ref[T_DQ:T_DQ + W_DQ, :], hT).astype(BF16)
    for p in range(DSA_HEADS // 2):
        qa = _dot(wpair_ref[p], dqT[p * LANES:(p + 1) * LANES, :])
        qabsT_ref[p * 2 * LANES:(p + 1) * 2 * LANES, :] = qa.astype(BF16)
    dlatT = _dot(wt_ref[T_DLAT:T_DLAT + DSA_LATENT, :], hT)
    inv = lax.rsqrt(jnp.mean(dlatT * dlatT, axis=0, keepdims=True) + EPS)
    kfeat = _key_feature_rows(pos_r, BF16_ROWS).astype(BF16)
    ckvT_ref[0:DSA_LATENT, :] = (dlatT * inv * gkvc_ref[...]).astype(BF16)
    ckvT_ref[DSA_LATENT:VT_ROWS, :] = kfeat
    iqT_ref[...] = _dot(wt_ref[T_IQ:T_IQ + IDX_HEADS * IDX_DIM, :], hT).astype(BF16)
    iwT = _dot(wt_ref[T_IW:T_IW + BF16_ROWS, :], hT)
    iwT_ref[...] = iwT[0:8, :] * (IDX_HEADS ** -0.5 * IDX_DIM ** -0.5)
    fqT = _dot(wt_ref[T_FQ:T_FQ + W_DIFF_QK, :], hT)
    zpad = jnp.zeros((LANES - DIFF_QK_DIM, tm), BF16)
    for s in range(DIFF_HEADS * 2):
        r0 = s * LANES
        fqT_ref[r0:r0 + DIFF_QK_DIM, :] = fqT[s * DIFF_QK_DIM:(s + 1) * DIFF_QK_DIM, :].astype(BF16)
        fqT_ref[r0 + DIFF_QK_DIM:r0 + LANES, :] = zpad
    fvT = _dot(wt_ref[T_FV:T_FV + DIFF_HEADS * DIFF_V_DIM, :], hT)
    for hh in range(DIFF_HEADS):
        fvT_ref[hh, 0:DIFF_V_DIM, :] = fvT[hh * DIFF_V_DIM:(hh + 1) * DIFF_V_DIM, :].astype(BF16)
        fvT_ref[hh, DIFF_V_DIM:VT_ROWS, :] = kfeat


def _feature_consts():
    fc = np.zeros((8, W_FQ), np.float32)
    fc[1, 0] = 1.0
    fc[2, 1] = 1.0
    fc[0, 2:N_FEATS] = 1.0
    for s in range(DIFF_HEADS * 2):
        base = s * LANES + DIFF_QK_DIM
        fc[4, base + 0] = 1.0
        fc[5, base + 1] = 1.0
        fc[3, base + 2:base + N_FEATS] = 1.0
    return jnp.asarray(fc)


def _norm_selector():
    sel = np.zeros((W_FQ + DSA_LATENT, LANES), np.float32)
    for s in range(DIFF_HEADS * 2):
        sel[s * LANES:s * LANES + DIFF_QK_DIM, s] = 1.0
    sel[W_FQ:, KMAX_LANE_CKV] = 1.0
    return jnp.asarray(sel, BF16)


def _proj_weights(w_in, w_uk):
    D = w_in.shape[0]
    pts = np.cumsum([W_DQ, DSA_LATENT, IDX_HEADS * IDX_DIM, IDX_DIM, IDX_HEADS,
                     W_DIFF_QK, W_DIFF_QK])
    dq, dlat, iq, ik, iw, fq, fk, fv = jnp.split(w_in, list(pts), axis=1)
    ik4 = jnp.tile(ik, (1, IDX_HEADS))
    fke = jnp.pad(fk.reshape(D, DIFF_HEADS * 2, DIFF_QK_DIM),
                  ((0, 0), (0, 0), (0, LANES - DIFF_QK_DIM))).reshape(D, W_FQ)
    wn = jnp.concatenate([ik4, dlat, fke], axis=1).astype(BF16)
    iwp = jnp.pad(iw, ((0, 0), (0, BF16_ROWS - IDX_HEADS)))
    wt = jnp.concatenate([dq, dlat, iq, iwp, fq * (DIFF_QK_DIM ** -0.5), fv], axis=1).T.astype(BF16)
    uk = w_uk * (DSA_HEAD_DIM ** -0.5)
    z = jnp.zeros_like(uk[0])
    pairs = [jnp.concatenate([jnp.concatenate([uk[2 * p], z], axis=1),
                              jnp.concatenate([z, uk[2 * p + 1]], axis=1)], axis=0)
             for p in range(DSA_HEADS // 2)]
    return wn, wt, jnp.stack(pairs).astype(BF16)


def _proj_call(x, mod3, g_attn, wn, wt, w_pair, g_kv, tm):
    B, S, D = x.shape
    nt = S // tm
    fc = _feature_consts()
    sel = _norm_selector()
    kern = functools.partial(_proj_kernel, tm=tm)
    tok = lambda w: pl.BlockSpec((None, tm, w), lambda b, i: (b, i, 0))
    tokT = lambda r: pl.BlockSpec((None, r, tm), lambda b, i: (b, 0, i))
    full = lambda a: pl.BlockSpec(a.shape, lambda b, i: (0,) * a.ndim)
    g_kv_col = g_kv.reshape(-1, 1)
    out_shape = [jax.ShapeDtypeStruct((B, W_QABS, S), BF16),
                 jax.ShapeDtypeStruct((B, LANES, S), BF16),
                 jax.ShapeDtypeStruct((B, 8, S), F32),
                 jax.ShapeDtypeStruct((B, S, LANES), BF16),
                 jax.ShapeDtypeStruct((B, S, W_CKV), BF16),
                 jax.ShapeDtypeStruct((B, nt, VT_ROWS, tm), BF16),
                 jax.ShapeDtypeStruct((B, W_FQ, S), BF16),
                 jax.ShapeDtypeStruct((B, S, W_FQ), BF16),
                 jax.ShapeDtypeStruct((B, DIFF_HEADS, nt, VT_ROWS, tm), BF16),
                 jax.ShapeDtypeStruct((B, nt, 8, LANES), F32)]
    out_specs = [tokT(W_QABS), tokT(LANES), tokT(8), tok(LANES), tok(W_CKV),
                 pl.BlockSpec((None, None, VT_ROWS, tm), lambda b, i: (b, i, 0, 0)),
                 tokT(W_FQ), tok(W_FQ),
                 pl.BlockSpec((None, DIFF_HEADS, None, VT_ROWS, tm), lambda b, i: (b, 0, i, 0, 0)),
                 pl.BlockSpec((None, None, 8, LANES), lambda b, i: (b, i, 0, 0))]
    return pl.pallas_call(
        kern,
        out_shape=out_shape,
        grid=(B, nt),
        in_specs=[tok(D),
                  pl.BlockSpec((None, 6, D), lambda b, i: (b, 0, 0)),
                  full(g_attn), full(wn), full(wt), full(w_pair), full(g_kv), full(g_kv_col),
                  full(fc), full(sel)],
        out_specs=out_specs,
        compiler_params=pltpu.CompilerParams(dimension_semantics=("arbitrary", "arbitrary"),
                                             vmem_limit_bytes=VMEM_LIMIT),
        name="proj",
    )(x, mod3, g_attn, wn, wt, w_pair, g_kv, g_kv_col, fc, sel)


def _float_code(x):
    b = lax.bitcast_convert_type(x, jnp.int32)
    return b ^ (lax.shift_right_arithmetic(b, 31) & jnp.int32(0x7FFFFFFF))


def _float_decode(c):
    b = c ^ (lax.shift_right_arithmetic(c, 31) & jnp.int32(0x7FFFFFFF))
    return lax.bitcast_convert_type(b, F32)


def _dsa_kernel(qabsT_ref, iqT_ref, iwT_ref, ik_ref, ckv_ref, ckvT_ref, kmax_ref, lstrict_ref, wuvT_ref,
                o_ref, sc_ref, qst_ref, qt_ref, acc_ref, m_ref, *, tq, tk, topk):
    qi = pl.program_id(1)
    nkb = (qi * tq) // tk + 1
    kf = float(topk)
    slopes = _alibi_slopes(DSA_HEADS)
    q_pos = qi * tq + lax.broadcasted_iota(jnp.int32, (1, tq), 1)

    iqT = iqT_ref[...]
    rowi = lax.broadcasted_iota(jnp.int32, (LANES, tq), 0)
    for hh in range(IDX_HEADS):
        qst_ref[:, hh * tq:(hh + 1) * tq] = jnp.where((rowi // IDX_DIM) == hh, iqT, jnp.zeros_like(iqT))
    iw = iwT_ref[...]
    wrow = [iw[hh:hh + 1, :] for hh in range(IDX_HEADS)]

    def scores(kb):
        kblk = ik_ref[pl.ds(pl.multiple_of(kb * tk, tk), tk), :]
        a = _dot(kblk, qst_ref[...])
        sc = jnp.maximum(a[:, 0:tq], 0.0) * wrow[0]
        for hh in range(1, IDX_HEADS):
            sc = sc + jnp.maximum(a[:, hh * tq:(hh + 1) * tq], 0.0) * wrow[hh]
        return sc

    def score_body(kb, carry):
        mn, mx = carry
        sc = scores(kb)
        sc_ref[kb] = sc
        return (jnp.minimum(mn, jnp.min(sc, axis=0, keepdims=True)),
                jnp.maximum(mx, jnp.max(sc, axis=0, keepdims=True)))

    mn, mx = lax.fori_loop(0, nkb - 1, score_body,
                           (jnp.full((1, tq), jnp.inf, F32), jnp.full((1, tq), -jnp.inf, F32)))
    last = nkb - 1
    sc = scores(last)
    causal = lax.broadcasted_iota(jnp.int32, (tk, tq), 0) + last * tk <= q_pos
    sc_ref[last] = jnp.where(causal, sc, jnp.nan)
    mn = jnp.minimum(mn, jnp.min(jnp.where(causal, sc, jnp.inf), axis=0, keepdims=True))
    mx = jnp.maximum(mx, jnp.max(jnp.where(causal, sc, -jnp.inf), axis=0, keepdims=True))

    def count(pred):
        def body(kb, acc):
            parts = [None] * COUNT_CHAINS
            for r in range(tk // 8):
                v = jnp.where(pred(sc_ref[kb, r * 8:(r + 1) * 8, :]), 1.0, 0.0)
                c = r % COUNT_CHAINS
                parts[c] = v if parts[c] is None else parts[c] + v
            return acc + ((parts[0] + parts[1]) + (parts[2] + parts[3]))
        acc = lax.fori_loop(0, nkb, body, jnp.zeros((8, tq), F32))
        return jnp.sum(acc, axis=0, keepdims=True)

    n_causal = (q_pos + 1).astype(F32)
    done0 = n_causal <= kf

    def bisect(st, value_mid):
        lo, hi, c_lo, theta, done = st
        th8 = jnp.broadcast_to(theta, (8, tq))
        c = count(lambda v: v >= th8)
        ge = c >= kf
        live = done == 0.0
        lo = jnp.where(live & ge, theta, lo)
        c_lo = jnp.where(live & ge, c, c_lo)
        hi = jnp.where(live & jnp.logical_not(ge), theta, hi)
        if value_mid:
            nxt = 0.5 * lo + 0.5 * hi
        else:
            cl, ch = _float_code(lo), _float_code(hi)
            nxt = _float_decode((cl & ch) + lax.shift_right_arithmetic(cl ^ ch, 1))
        inside = (nxt > lo) & (nxt < hi)
        done = jnp.where((c_lo == kf) | jnp.logical_not(inside), 1.0, done)
        return lo, hi, c_lo, nxt, done

    def checked(value_mid, max_steps):
        def cond(c):
            return jnp.logical_and(c[1] > 0.0, c[2] < max_steps)

        def body(c):
            st = bisect(c[0], value_mid)
            return st, jnp.sum(1.0 - st[4]), c[2] + 1
        return cond, body

    st = (mn, mx, n_causal, mx, jnp.where(done0, 1.0, 0.0))
    st = lax.fori_loop(0, UNCHECKED_BISECT_STEPS, lambda i, s: bisect(s, True), st)
    cond, body = checked(True, VALUE_BISECT_STEPS - UNCHECKED_BISECT_STEPS)
    st, active, _ = lax.while_loop(cond, body, (st, jnp.sum(1.0 - st[4]), jnp.int32(0)))
    cond, body = checked(False, 34)
    st, _, _ = lax.while_loop(cond, body, (st, active, jnp.int32(0)))
    tau, c_ge = st[0], st[2]

    @pl.when(jnp.max(c_ge) > kf)
    def _():
        tau8 = jnp.broadcast_to(tau, (8, tq))
        quota = kf - count(lambda v: v > tau8)

        def tie_body(kb, seen):
            s = sc_ref[kb]
            eq = s == tau
            eqf = jnp.where(eq, 1.0, 0.0)
            rank = _dot(lstrict_ref[...], eqf.astype(BF16)) + seen
            sc_ref[kb] = jnp.where(eq & (rank >= quota), jnp.nan, s)
            return seen + jnp.sum(eqf, axis=0, keepdims=True)

        lax.fori_loop(0, nkb, tie_body, jnp.zeros((1, tq), F32))

    tau8 = jnp.broadcast_to(tau, (8, tq))
    sub = lax.broadcasted_iota(jnp.int32, (8, tq), 0).astype(F32)

    def nearest_body(kb, acc):
        parts = [None] * COUNT_CHAINS
        for r in range(tk // 8):
            idx = sub + (kb * tk + r * 8).astype(F32)
            v = jnp.where(sc_ref[kb, r * 8:(r + 1) * 8, :] >= tau8, idx, -1.0)
            c = r % COUNT_CHAINS
            parts[c] = v if parts[c] is None else jnp.maximum(parts[c], v)
        return jnp.maximum(acc, jnp.maximum(jnp.maximum(parts[0], parts[1]),
                                            jnp.maximum(parts[2], parts[3])))

    last_sel = jnp.max(lax.fori_loop(0, nkb, nearest_body, jnp.full((8, tq), -1.0, F32)),
                       axis=0, keepdims=True)
    d_min = q_pos.astype(F32) - last_sel
    kmax2 = _tile_max(kmax_ref, nkb - 1, KMAX_LANE_CKV)
    for hh in range(DSA_HEADS):
        qh = qabsT_ref[hh * DSA_LATENT:(hh + 1) * DSA_LATENT, :]
        qf = qh.astype(F32)
        shift = _shift_bound(jnp.sum(qf * qf, axis=0, keepdims=True), kmax2, -slopes[hh] * d_min)
        qt_ref[0:DSA_LATENT, hh * tq:(hh + 1) * tq] = qh
        qt_ref[DSA_LATENT:2 * DSA_LATENT, hh * tq:(hh + 1) * tq] = _query_feature_rows(
            q_pos, DSA_LATENT, slopes[hh], shift).astype(BF16)
    acc_ref[...] = jnp.zeros_like(acc_ref)

    def fast_body(kb, carry):
        kx = ckv_ref[pl.ds(pl.multiple_of(kb * tk, tk), tk), :]
        s = _dot(kx, qt_ref[...])
        keep = sc_ref[kb] >= tau
        ps = [jnp.exp(jnp.where(keep, s[:, hh * tq:(hh + 1) * tq], NEG_BIG)).astype(BF16)
              for hh in range(DSA_HEADS)]
        acc_ref[...] += _dot(ckvT_ref[kb], jnp.concatenate(ps, axis=1))
        return carry

    lax.fori_loop(0, nkb, fast_body, 0)

    def attn_body(kb, carry):
        kx = ckv_ref[pl.ds(pl.multiple_of(kb * tk, tk), tk), :]
        vT = ckvT_ref[kb]
        keep = sc_ref[kb] >= tau
        qt = qt_ref[...]
        m_all = m_ref[...]
        m_out, alphas, pvs = [], [], []
        for ch in range(ATTN_CHAINS):
            hpc = DSA_HEADS // ATTN_CHAINS
            l0 = ch * hpc * tq
            s = _dot(kx, qt[:, l0:l0 + hpc * tq])
            ps = []
            for j in range(hpc):
                c0 = l0 + j * tq
                sh = jnp.where(keep, s[:, j * tq:(j + 1) * tq], NEG_BIG)
                m_old = m_all[:, c0:c0 + tq]
                m_new = jnp.maximum(m_old, jnp.max(sh, axis=0, keepdims=True))
                ps.append(jnp.exp(sh - m_new).astype(BF16))
                alphas.append(jnp.exp(m_old - m_new))
                m_out.append(m_new)
            pvs.append(_dot(vT, jnp.concatenate(ps, axis=1)))
        m_ref[...] = jnp.concatenate(m_out, axis=1)
        acc_ref[...] = jnp.concatenate(alphas, axis=1) * acc_ref[...] + jnp.concatenate(pvs, axis=1)
        return carry

    denom_ok = jnp.min(acc_ref[ONES_ROW:ONES_ROW + 1, :]) > L_MIN

    @pl.when(jnp.logical_not(denom_ok))
    def _():
        acc_ref[...] = jnp.zeros_like(acc_ref)
        m_ref[...] = jnp.full_like(m_ref, NEG_BIG)
        lax.fori_loop(0, nkb, attn_body, 0)

    acc = acc_ref[...]
    lat = acc[0:DSA_LATENT, :] / acc[ONES_ROW:ONES_ROW + 1, :]
    for p in range(DSA_HEADS // 2):
        pair = lat[:, 2 * p * tq:(2 * p + 2) * tq]
        pair = jnp.concatenate([pair[:, 0:tq], pair[:, tq:2 * tq]], axis=0).astype(BF16)
        o_ref[:, p * LANES:(p + 1) * LANES] = _dot(wuvT_ref[p], pair).T.astype(BF16)


def _dsa_call(qabsT, iqT, iwT, ik4, ckv, ckvT, kmax, lstrict, wuvT_pair, tq, tk, topk):
    B, _, S = qabsT.shape
    nkb = S // tk
    kern = functools.partial(_dsa_kernel, tq=tq, tk=tk, topk=topk)
    blkT = lambda r: pl.BlockSpec((None, r, tq), lambda b, i: (b, 0, i))
    per_b = lambda w: pl.BlockSpec((None, S, w), lambda b, i: (b, 0, 0))
    full = lambda a: pl.BlockSpec(a.shape, lambda b, i: (0,) * a.ndim)
    return pl.pallas_call(
        kern,
        out_shape=jax.ShapeDtypeStruct((B, S, W_DQ), BF16),
        grid=(B, S // tq),
        in_specs=[blkT(W_QABS), blkT(LANES), blkT(8), per_b(LANES), per_b(W_CKV),
                  pl.BlockSpec((None, nkb, VT_ROWS, tk), lambda b, i: (b, 0, 0, 0)),
                  pl.BlockSpec((None, nkb, 8, LANES), lambda b, i: (b, 0, 0, 0)),
                  full(lstrict), full(wuvT_pair)],
        out_specs=pl.BlockSpec((None, tq, W_DQ), lambda b, i: (b, i, 0)),
        scratch_shapes=[pltpu.VMEM((nkb, tk, tq), F32),
                        pltpu.VMEM((LANES, IDX_HEADS * tq), BF16),
                        pltpu.VMEM((2 * DSA_LATENT, DSA_HEADS * tq), BF16),
                        pltpu.VMEM((VT_ROWS, DSA_HEADS * tq), F32),
                        pltpu.VMEM((1, DSA_HEADS * tq), F32)],
        compiler_params=pltpu.CompilerParams(dimension_semantics=("arbitrary", "arbitrary"),
                                             vmem_limit_bytes=VMEM_LIMIT),
        name="dsa",
    )(qabsT, iqT, iwT, ik4, ckv, ckvT, kmax, lstrict, wuvT_pair)


def _diff_kernel(fqT_ref, fk_ref, fvT_ref, kmax_ref, lam_ref, gsub_ref, o_ref, qt_ref, acc_ref, m_ref,
                 *, tq, tk, lam0):
    hd = pl.program_id(1)
    qi = pl.program_id(2)
    q_pos = qi * tq + lax.broadcasted_iota(jnp.int32, (1, tq), 1)
    n_full = (qi * tq) // tk
    n_diag = (tq + tk - 1) // tk
    slopes = _alibi_slopes(DIFF_HEADS)
    slope = jnp.float32(slopes[DIFF_HEADS - 1])
    for i in range(DIFF_HEADS - 1):
        slope = jnp.where(hd == i, slopes[i], slope)

    for m in range(2):
        r0 = m * LANES
        qh = fqT_ref[r0:r0 + DIFF_QK_DIM, :]
        qf = qh.astype(F32)
        shift = _shift_bound(jnp.sum(qf * qf, axis=0, keepdims=True),
                             _tile_max(kmax_ref, n_full + n_diag - 1, 2 * hd + m), 0.0)
        qt_ref[r0:r0 + DIFF_QK_DIM, :] = qh
        qt_ref[r0 + DIFF_QK_DIM:r0 + DIFF_QK_DIM + BF16_ROWS, :] = _query_feature_rows(
            q_pos, BF16_ROWS, slope, shift).astype(BF16)
        qt_ref[r0 + DIFF_QK_DIM + BF16_ROWS:r0 + LANES, :] = jnp.zeros(
            (LANES - DIFF_QK_DIM - BF16_ROWS, tq), BF16)
    acc_ref[...] = jnp.zeros_like(acc_ref)

    def fast_block(kb, masked):
        kx = fk_ref[pl.ds(pl.multiple_of(kb * tk, tk), tk), :]
        vT = fvT_ref[kb]
        for m in range(2):
            s = _dot(kx[:, m * LANES:(m + 1) * LANES], qt_ref[m * LANES:(m + 1) * LANES, :])
            if masked:
                causal = lax.broadcasted_iota(jnp.int32, (tk, tq), 0) + kb * tk <= q_pos
                s = jnp.where(causal, s, NEG_BIG)
            acc_ref[m] += _dot(vT, jnp.exp(s).astype(BF16))

    def fast_body(kb, carry):
        fast_block(kb, False)
        return carry

    lax.fori_loop(0, n_full, fast_body, 0)
    for d in range(n_diag):
        fast_block(n_full + d, True)

    def block(kb, masked):
        kx = fk_ref[pl.ds(pl.multiple_of(kb * tk, tk), tk), :]
        vT = fvT_ref[kb]
        qT = qt_ref[...]
        m_all = m_ref[...]
        new = []
        for m in range(2):
            s = _dot(kx[:, m * LANES:(m + 1) * LANES], qT[m * LANES:(m + 1) * LANES, :])
            if masked:
                causal = lax.broadcasted_iota(jnp.int32, (tk, tq), 0) + kb * tk <= q_pos
                s = jnp.where(causal, s, NEG_BIG)
            m_old = m_all[m]
            m_new = jnp.maximum(m_old, jnp.max(s, axis=0, keepdims=True))
            p = jnp.exp(s - m_new).astype(BF16)
            new.append((m_new, jnp.exp(m_old - m_new), _dot(vT, p)))
        for m in range(2):
            m_ref[m] = new[m][0]
            acc_ref[m] = new[m][1] * acc_ref[m] + new[m][2]

    def body(kb, carry):
        block(kb, False)
        return carry

    denom_ok = jnp.min(acc_ref[:, ONES_ROW:ONES_ROW + 1, :]) > L_MIN

    @pl.when(jnp.logical_not(denom_ok))
    def _():
        acc_ref[...] = jnp.zeros_like(acc_ref)
        m_ref[...] = jnp.full_like(m_ref, NEG_BIG)
        lax.fori_loop(0, n_full, body, 0)
        for d in range(n_diag):
            block(n_full + d, True)

    lv = lam_ref[...]
    lam = (jnp.exp(jnp.sum(lv[0:1] * lv[1:2], axis=-1, keepdims=True))
           - jnp.exp(jnp.sum(lv[2:3] * lv[3:4], axis=-1, keepdims=True)) + lam0)
    a1 = acc_ref[0]
    a2 = acc_ref[1]
    o = (a1[0:DIFF_V_DIM, :] / a1[ONES_ROW:ONES_ROW + 1, :]
         - lam * (a2[0:DIFF_V_DIM, :] / a2[ONES_ROW:ONES_ROW + 1, :]))
    o = o * lax.rsqrt(jnp.mean(o * o, axis=0, keepdims=True) + EPS) * gsub_ref[...] * (1.0 - lam0)
    o_ref[...] = o.T.astype(BF16)


def _diff_call(fqT, fk, fvT, kmax, lam_vecs, g_sub_col, tq, tk, lam0):
    B, S, _ = fk.shape
    nkb = S // tk
    kern = functools.partial(_diff_kernel, tq=tq, tk=tk, lam0=lam0)
    full = lambda a: pl.BlockSpec(a.shape, lambda b, h, i: (0,) * a.ndim)
    return pl.pallas_call(
        kern,
        out_shape=jax.ShapeDtypeStruct((B, S, DIFF_HEADS * DIFF_V_DIM), BF16),
        grid=(B, DIFF_HEADS, S // tq),
        in_specs=[pl.BlockSpec((None, 2 * LANES, tq), lambda b, h, i: (b, h, i)),
                  pl.BlockSpec((None, S, 2 * LANES), lambda b, h, i: (b, 0, h)),
                  pl.BlockSpec((None, None, nkb, VT_ROWS, tk), lambda b, h, i: (b, h, 0, 0, 0)),
                  pl.BlockSpec((None, nkb, 8, LANES), lambda b, h, i: (b, 0, 0, 0)),
                  full(lam_vecs), full(g_sub_col)],
        out_specs=pl.BlockSpec((None, tq, DIFF_V_DIM), lambda b, h, i: (b, i, h)),
        scratch_shapes=[pltpu.VMEM((2 * LANES, tq), BF16),
                        pltpu.VMEM((2, VT_ROWS, tq), F32),
                        pltpu.VMEM((2, 1, tq), F32)],
        compiler_params=pltpu.CompilerParams(
            dimension_semantics=("arbitrary", "arbitrary", "arbitrary"),
            vmem_limit_bytes=VMEM_LIMIT),
        name="diff",
    )(fqT, fk, fvT, kmax, lam_vecs, g_sub_col)


def _route_kernel(x_ref, od_ref, of_ref, mod_ref, wout_ref, g_ref, wr_hi_ref, wr_lo_ref, br_ref,
                  x1_ref, h_ref, gates_ref, *, tm):
    attn = (_dot(od_ref[...], wout_ref[0:W_DQ, :]) + _dot(of_ref[...], wout_ref[W_DQ:, :]))
    x1 = x_ref[...] + mod_ref[2:3, :] * attn
    x1_ref[...] = x1
    h = _rms(x1, g_ref[...]) * (1.0 + mod_ref[4:5, :]) + mod_ref[3:4, :]
    h_hi = h.astype(BF16)
    h_ref[...] = h_hi
    h_lo = (h - h_hi.astype(F32)).astype(BF16)
    w_hi = wr_hi_ref[...]
    logits = (_dot(h_hi, w_hi) + _dot(h_lo, w_hi) + _dot(h_hi, wr_lo_ref[...])) + br_ref[...]

    lane = lax.broadcasted_iota(jnp.int32, (tm, LANES), 1)
    big = jnp.int32(4 * LANES)
    neg = -jnp.inf
    is_g = (lane >= N_EXPERTS) & (lane < N_EXPERTS + N_GROUPS)
    gl = jnp.where(is_g, logits, neg)
    gmax = jnp.max(gl, axis=-1, keepdims=True)
    g_lane = jnp.min(jnp.where(gl == gmax, lane, big), axis=-1, keepdims=True)
    g_sel = g_lane - N_EXPERTS
    p_g = 1.0 / jnp.sum(jnp.where(is_g, jnp.exp(gl - gmax), 0.0), axis=-1, keepdims=True)
    in_grp = (lane < N_EXPERTS) & ((lane // EXPERTS_PER_GROUP) == g_sel)
    el = jnp.where(in_grp, logits, neg)
    v1 = jnp.max(el, axis=-1, keepdims=True)
    i1 = jnp.min(jnp.where(el == v1, lane, big), axis=-1, keepdims=True)
    el2 = jnp.where(lane == i1, neg, el)
    v2 = jnp.max(el2, axis=-1, keepdims=True)
    i2 = jnp.min(jnp.where(el2 == v2, lane, big), axis=-1, keepdims=True)
    e = jnp.exp(v2 - v1)
    w1 = 1.0 / (1.0 + e)
    w2 = e * w1
    gates_ref[...] = jnp.where(lane == i1, w1 * p_g, jnp.where(lane == i2, w2 * p_g, 0.0))


def _route_call(x, o_dsa, o_diff, mod3, w_out, g_moe, wr_hi, wr_lo, b_r, tm):
    B, S, D = x.shape
    kern = functools.partial(_route_kernel, tm=tm)
    tok = lambda w: pl.BlockSpec((None, tm, w), lambda b, i: (b, i, 0))
    full = lambda a: pl.BlockSpec(a.shape, lambda b, i: (0,) * a.ndim)
    return pl.pallas_call(
        kern,
        out_shape=[jax.ShapeDtypeStruct((B, S, D), F32), jax.ShapeDtypeStruct((B, S, D), BF16),
                   jax.ShapeDtypeStruct((B, S, LANES), F32)],
        grid=(B, S // tm),
        in_specs=[tok(D), tok(W_DQ), tok(DIFF_HEADS * DIFF_V_DIM),
                  pl.BlockSpec((None, 6, D), lambda b, i: (b, 0, 0)),
                  full(w_out), full(g_moe), full(wr_hi), full(wr_lo), full(b_r)],
        out_specs=[tok(D), tok(D), tok(LANES)],
        compiler_params=pltpu.CompilerParams(dimension_semantics=("arbitrary", "arbitrary"),
                                             vmem_limit_bytes=VMEM_LIMIT),
        name="route",
    )(x, o_dsa, o_diff, mod3, w_out, g_moe, wr_hi, wr_lo, b_r)


def _moe_kernel(h_ref, gates_ref, x1_ref, mod_ref, wg_ref, wu_ref, wd_ref, gf_ref, o_ref, acc_ref,
                *, tm, ec):
    j = pl.program_id(2)
    nc = pl.num_programs(2)

    @pl.when(j == 0)
    def _():
        acc_ref[...] = jnp.zeros_like(acc_ref)

    h = h_ref[...]
    hg = _dot(h, wg_ref[...])
    hu = _dot(h, wu_ref[...])
    hid = hg * jax.nn.sigmoid(hg) * hu
    gates = gates_ref[...]
    lane = lax.broadcasted_iota(jnp.int32, (tm, LANES), 1)
    parts = []
    for e in range(ec):
        gcol = jnp.sum(jnp.where(lane == j * ec + e, gates, 0.0), axis=-1, keepdims=True)
        parts.append((hid[:, e * D_EXPERT:(e + 1) * D_EXPERT] * gcol).astype(BF16))
    hs = jnp.concatenate(parts, axis=1)
    acc_ref[...] += _dot(hs, wd_ref[...])

    @pl.when(j == nc - 1)
    def _():
        x2 = x1_ref[...] + mod_ref[5:6, :] * acc_ref[...]
        o_ref[...] = _rms(x2, gf_ref[...])


def _moe_call(h, gates, x1, mod3, wg, wu, wd, g_final, tm, ec):
    B, S, D = x1.shape
    kern = functools.partial(_moe_kernel, tm=tm, ec=ec)
    tok = lambda w: pl.BlockSpec((None, tm, w), lambda b, i, j: (b, i, 0))
    return pl.pallas_call(
        kern,
        out_shape=jax.ShapeDtypeStruct((B, S, D), F32),
        grid=(B, S // tm, N_EXPERTS // ec),
        in_specs=[tok(D), tok(LANES), tok(D),
                  pl.BlockSpec((None, 6, D), lambda b, i, j: (b, 0, 0)),
                  pl.BlockSpec((D, ec * D_EXPERT), lambda b, i, j: (0, j)),
                  pl.BlockSpec((D, ec * D_EXPERT), lambda b, i, j: (0, j)),
                  pl.BlockSpec((ec * D_EXPERT, D), lambda b, i, j: (j, 0)),
                  pl.BlockSpec((1, D), lambda b, i, j: (0, 0))],
        out_specs=tok(D),
        scratch_shapes=[pltpu.VMEM((tm, D), F32)],
        compiler_params=pltpu.CompilerParams(
            dimension_semantics=("arbitrary", "arbitrary", "arbitrary"),
            vmem_limit_bytes=VMEM_LIMIT),
        name="moe",
    )(h, gates, x1, mod3, wg, wu, wd, g_final)


def _tile(n, pref):
    t = min(n, pref)
    assert n % t == 0, (n, t)
    return t


def kernel(x, c, w_ada, b_ada, g_attn, w_in, g_kv, w_uk, w_uv, lam_q1, lam_k1, lam_q2, lam_k2,
           g_sub, w_out, g_moe, w_group, b_group, w_router, b_router, w_gate, w_up, w_down, g_final):
    B, S, D = x.shape
    assert D == D_MODEL and w_ada.shape[0] == 1
    assert S % KEY_TILE == 0 and S <= POS_SPLIT * 256
    topk = min(TOPK_MAX, S // 4)
    l = 0
    lam0 = 0.8 - 0.6 * math.exp(-0.3 * l)

    mod3 = _mod_call(c, w_ada[l], b_ada[l].reshape(1, -1)).reshape(B, 6, D)

    wn, wt, w_pair = _proj_weights(w_in[l], w_uk[l])
    qabsT, iqT, iwT, ik4, ckv, ckvT, fqT, fk, fvT, kmax = _proj_call(
        x, mod3, g_attn[l].reshape(1, D), wn, wt, w_pair, g_kv[l].reshape(1, -1), KEY_TILE)

    lstrict = jnp.asarray(np.tril(np.ones((KEY_TILE, KEY_TILE), np.float32), -1), BF16)
    uvT = jnp.swapaxes(w_uv[l], 1, 2)
    z = jnp.zeros_like(uvT[0])
    wuvT_pair = jnp.stack([
        jnp.concatenate([jnp.concatenate([uvT[2 * p], z], axis=1),
                         jnp.concatenate([z, uvT[2 * p + 1]], axis=1)], axis=0)
        for p in range(DSA_HEADS // 2)]).astype(BF16)
    o_dsa = _dsa_call(qabsT, iqT, iwT, ik4, ckv, ckvT, kmax, lstrict, wuvT_pair, LANES, KEY_TILE, topk)

    lam_vecs = jnp.concatenate([lam_q1[l][None], lam_k1[l][None], lam_q2[l][None], lam_k2[l][None]],
                               axis=0).astype(F32)
    o_diff = _diff_call(fqT, fk, fvT, kmax, lam_vecs, g_sub[l].reshape(-1, 1), KEY_TILE, KEY_TILE, lam0)

    wr = jnp.pad(jnp.concatenate([w_router[l], w_group[l]], axis=1),
                 ((0, 0), (0, LANES - N_EXPERTS - N_GROUPS)))
    wr_hi = wr.astype(BF16)
    wr_lo = (wr - wr_hi.astype(F32)).astype(BF16)
    b_r = jnp.pad(jnp.concatenate([b_router[l], b_group[l]]), (0, LANES - N_EXPERTS - N_GROUPS))
    x1, h2, gates = _route_call(x, o_dsa, o_diff, mod3, w_out[l].astype(BF16),
                                g_moe[l].reshape(1, D), wr_hi, wr_lo, b_r.reshape(1, LANES),
                                _tile(S, 512))

    wg = jnp.swapaxes(w_gate[l], 0, 1).reshape(D, N_EXPERTS * D_EXPERT).astype(BF16)
    wu = jnp.swapaxes(w_up[l], 0, 1).reshape(D, N_EXPERTS * D_EXPERT).astype(BF16)
    wd = w_down[l].reshape(N_EXPERTS * D_EXPERT, D).astype(BF16)
    return _moe_call(h2, gates, x1, mod3, wg, wu, wd, g_final.reshape(1, D), _tile(S, 512), 4)
```

```python
import functools
import math

import jax
import jax.numpy as jnp
import numpy as np
from jax import lax
from jax.experimental import pallas as pl
from jax.experimental.pallas import tpu as pltpu

F32 = jnp.float32
BF16 = jnp.bfloat16

D_MODEL = 1024
DSA_HEADS = 8
DSA_HEAD_DIM = 64
DSA_LATENT = 128
IDX_HEADS = 4
IDX_DIM = 32
TOPK_MAX = 256
DIFF_HEADS = 4
DIFF_QK_DIM = 64
DIFF_V_DIM = 128
N_GROUPS = 4
EXPERTS_PER_GROUP = 8
N_EXPERTS = 32
D_EXPERT = 256
EPS = 1e-6

LANES = 128
BF16_ROWS = 16
POS_SPLIT = 64
NEG_BIG = -1e30
VMEM_LIMIT = 56 * 1024 * 1024
KEY_TILE = 512
VALUE_BISECT_STEPS = 24
UNCHECKED_BISECT_STEPS = 16
COUNT_CHAINS = 4
ATTN_CHAINS = 1

W_DQ = DSA_HEADS * DSA_HEAD_DIM
W_QABS = DSA_HEADS * DSA_LATENT
W_CKV = 2 * LANES
VT_ROWS = DSA_LATENT + BF16_ROWS
ONES_ROW = DSA_LATENT + 2
N_FEATS = 5
KMAX_LANE_CKV = 8
L_MIN = 1e-30
W_FQ = DIFF_HEADS * 2 * LANES
W_DIFF_QK = DIFF_HEADS * 2 * DIFF_QK_DIM

T_DQ = 0
T_DLAT = T_DQ + W_DQ
T_IQ = T_DLAT + DSA_LATENT
T_IW = T_IQ + IDX_HEADS * IDX_DIM
T_FQ = T_IW + BF16_ROWS
T_FV = T_FQ + W_DIFF_QK
T_ROWS = T_FV + DIFF_HEADS * DIFF_V_DIM
N_IK = 0
N_DLAT = N_IK + LANES
N_FK = N_DLAT + DSA_LATENT
N_COLS = N_FK + W_FQ


def _alibi_slopes(n):
    return [2.0 ** (-8.0 * (i + 1) / n) for i in range(n)]


def _rms(x, g):
    return x * lax.rsqrt(jnp.mean(x * x, axis=-1, keepdims=True) + EPS) * g


def _dot(a, b):
    return jnp.dot(a, b, preferred_element_type=F32)


def _mod_kernel(c_ref, w_ref, b_ref, o_ref):
    c = c_ref[...]
    act = c * jax.nn.sigmoid(c)
    o_ref[...] = jnp.dot(act, w_ref[...], preferred_element_type=F32,
                         precision=lax.Precision.HIGHEST) + b_ref[...]


def _mod_call(c, w_ada, b_ada):
    B, D = c.shape
    n = w_ada.shape[1] // D
    return pl.pallas_call(
        _mod_kernel,
        out_shape=jax.ShapeDtypeStruct((B, n * D), F32),
        grid=(n,),
        in_specs=[pl.BlockSpec((B, D), lambda j: (0, 0)),
                  pl.BlockSpec((D, D), lambda j: (0, j)),
                  pl.BlockSpec((1, D), lambda j: (0, j))],
        out_specs=pl.BlockSpec((B, D), lambda j: (0, j)),
        compiler_params=pltpu.CompilerParams(dimension_semantics=("arbitrary",),
                                             vmem_limit_bytes=VMEM_LIMIT),
        name="mod",
    )(c, w_ada, b_ada)


def _key_feature_rows(pos_row, n_rows):
    r = lax.broadcasted_iota(jnp.int32, (n_rows, pos_row.shape[1]), 0)
    pa = (pos_row // POS_SPLIT).astype(F32)
    pb = (pos_row % POS_SPLIT).astype(F32)
    return jnp.where(r == 0, pa, jnp.where(r == 1, pb, jnp.where(r < N_FEATS, 1.0, 0.0)))


def _query_feature_rows(pos_row, n_rows, slope, shift):
    r = lax.broadcasted_iota(jnp.int32, (n_rows, pos_row.shape[1]), 0)
    pa = (pos_row // POS_SPLIT).astype(F32)
    pb = (pos_row % POS_SPLIT).astype(F32)
    return jnp.where(r == 0, POS_SPLIT * slope,
                     jnp.where(r == 1, slope,
                               jnp.where(r == 2, -POS_SPLIT * slope * pa,
                                         jnp.where(r == 3, -slope * pb,
                                                   jnp.where(r == 4, -shift, 0.0)))))


def _shift_bound(qn2, kmax2, extra):
    b = jnp.sqrt(qn2 * kmax2) + extra
    return b + jnp.abs(b) * (2.0 ** -6) + 2.0 ** -20


def _tile_max(kmax_ref, last_tile, lane_idx):
    x = kmax_ref[...]
    t = lax.broadcasted_iota(jnp.int32, x.shape, 0)
    ln = lax.broadcasted_iota(jnp.int32, x.shape, 2)
    x = jnp.where((t <= last_tile) & (ln == lane_idx), x, 0.0)
    return jnp.max(jnp.max(x, axis=0), axis=1, keepdims=True)[0:1, :]


def _proj_kernel(x_ref, mod_ref, g_ref, wn_ref, wt_ref, wpair_ref, gkv_ref, gkvc_ref, fc_ref, sel_ref,
                 qabsT_ref, iqT_ref, iwT_ref, ik_ref, ckv_ref, ckvT_ref, fqT_ref, fk_ref, fvT_ref,
                 kmax_ref, *, tm):
    x = x_ref[...]
    h = _rms(x, g_ref[...]) * (1.0 + mod_ref[1:2, :]) + mod_ref[0:1, :]
    hb = h.astype(BF16)
    hT = h.T.astype(BF16)

    base = pl.program_id(1) * tm
    pos_c = base + lax.broadcasted_iota(jnp.int32, (tm, 1), 0)
    pos_r = base + lax.broadcasted_iota(jnp.int32, (1, tm), 1)
    pa = (pos_c // POS_SPLIT).astype(F32)
    pb = (pos_c % POS_SPLIT).astype(F32)

    def feats(row, width):
        return (fc_ref[row:row + 1, 0:width] + fc_ref[row + 1:row + 2, 0:width] * pa
                + fc_ref[row + 2:row + 3, 0:width] * pb)

    ik_ref[...] = _dot(hb, wn_ref[:, N_IK:N_IK + LANES]).astype(BF16)
    dlat = _dot(hb, wn_ref[:, N_DLAT:N_DLAT + DSA_LATENT])
    ckv_b = _rms(dlat, gkv_ref[...]).astype(BF16)
    ckv_ref[:, 0:LANES] = ckv_b
    ckv_ref[:, LANES:2 * LANES] = feats(0, LANES).astype(BF16)
    fk = _dot(hb, wn_ref[:, N_FK:N_FK + W_FQ])
    fk_ref[...] = (fk + feats(3, W_FQ)).astype(BF16)
    sq = jnp.concatenate([fk.astype(BF16).astype(F32), ckv_b.astype(F32)], axis=1)
    sq_up = (sq * sq * (1.0 + 2.0 ** -7)).astype(BF16)
    kmax_ref[...] = jnp.broadcast_to(jnp.max(_dot(sq_up, sel_ref[...]), axis=0, keepdims=True),
                                     (8, LANES))

    dqT = _dot(wt_ref[T_DQ:T_DQ + W_DQ, :], hT).astype(BF16)
    for p in range(DSA_HEADS // 2):
        qa = _dot(wpair_ref[p], dqT[p * LANES:(p + 1) * LANES, :])
        qabsT_ref[p * 2 * LANES:(p + 1) * 2 * LANES, :] = qa.astype(BF16)
    dlatT = _dot(wt_ref[T_DLAT:T_DLAT + DSA_LATENT, :], hT)
    inv = lax.rsqrt(jnp.mean(dlatT * dlatT, axis=0, keepdims=True) + EPS)
    kfeat = _key_feature_rows(pos_r, BF16_ROWS).astype(BF16)
    ckvT_ref[0:DSA_LATENT, :] = (dlatT * inv * gkvc_ref[...]).astype(BF16)
    ckvT_ref[DSA_LATENT:VT_ROWS, :] = kfeat
    iqT_ref[...] = _dot(wt_ref[T_IQ:T_IQ + IDX_HEADS * IDX_DIM, :], hT).astype(BF16)
    iwT = _dot(wt_ref[T_IW:T_IW + BF16_ROWS, :], hT)
    iwT_ref[...] = iwT[0:8, :] * (IDX_HEADS ** -0.5 * IDX_DIM ** -0.5)
    fqT = _dot(wt_ref[T_FQ:T_FQ + W_DIFF_QK, :], hT)
    zpad = jnp.zeros((LANES - DIFF_QK_DIM, tm), BF16)
    for s in range(DIFF_HEADS * 2):
        r0 = s * LANES
        fqT_ref[r0:r0 + DIFF_QK_DIM, :] = fqT[s * DIFF_QK_DIM:(s + 1) * DIFF_QK_DIM, :].astype(BF16)
        fqT_ref[r0 + DIFF_QK_DIM:r0 + LANES, :] = zpad
    fvT = _dot(wt_ref[T_FV:T_FV + DIFF_HEADS * DIFF_V_DIM, :], hT)
    for hh in range(DIFF_HEADS):
        fvT_ref[hh, 0:DIFF_V_DIM, :] = fvT[hh * DIFF_V_DIM:(hh + 1) * DIFF_V_DIM, :].astype(BF16)
        fvT_ref[hh, DIFF_V_DIM:VT_ROWS, :] = kfeat


def _feature_consts():
    fc = np.zeros((8, W_FQ), np.float32)
    fc[1, 0] = 1.0
    fc[2, 1] = 1.0
    fc[0, 2:N_FEATS] = 1.0
    for s in range(DIFF_HEADS * 2):
        base = s * LANES + DIFF_QK_DIM
        fc[4, base + 0] = 1.0
        fc[5, base + 1] = 1.0
        fc[3, base + 2:base + N_FEATS] = 1.0
    return jnp.asarray(fc)


def _norm_selector():
    sel = np.zeros((W_FQ + DSA_LATENT, LANES), np.float32)
    for s in range(DIFF_HEADS * 2):
        sel[s * LANES:s * LANES + DIFF_QK_DIM, s] = 1.0
    sel[W_FQ:, KMAX_LANE_CKV] = 1.0
    return jnp.asarray(sel, BF16)


def _proj_weights(w_in, w_uk):
    D = w_in.shape[0]
    pts = np.cumsum([W_DQ, DSA_LATENT, IDX_HEADS * IDX_DIM, IDX_DIM, IDX_HEADS,
                     W_DIFF_QK, W_DIFF_QK])
    dq, dlat, iq, ik, iw, fq, fk, fv = jnp.split(w_in, list(pts), axis=1)
    ik4 = jnp.tile(ik, (1, IDX_HEADS))
    fke = jnp.pad(fk.reshape(D, DIFF_HEADS * 2, DIFF_QK_DIM),
                  ((0, 0), (0, 0), (0, LANES - DIFF_QK_DIM))).reshape(D, W_FQ)
    wn = jnp.concatenate([ik4, dlat, fke], axis=1).astype(BF16)
    iwp = jnp.pad(iw, ((0, 0), (0, BF16_ROWS - IDX_HEADS)))
    wt = jnp.concatenate([dq, dlat, iq, iwp, fq * (DIFF_QK_DIM ** -0.5), fv], axis=1).T.astype(BF16)
    uk = w_uk * (DSA_HEAD_DIM ** -0.5)
    z = jnp.zeros_like(uk[0])
    pairs = [jnp.concatenate([jnp.concatenate([uk[2 * p], z], axis=1),
                              jnp.concatenate([z, uk[2 * p + 1]], axis=1)], axis=0)
             for p in range(DSA_HEADS // 2)]
    return wn, wt, jnp.stack(pairs).astype(BF16)


def _proj_call(x, mod3, g_attn, wn, wt, w_pair, g_kv, tm):
    B, S, D = x.shape
    nt = S // tm
    fc = _feature_consts()
    sel = _norm_selector()
    kern = functools.partial(_proj_kernel, tm=tm)
    tok = lambda w: pl.BlockSpec((None, tm, w), lambda b, i: (b, i, 0))
    tokT = lambda r: pl.BlockSpec((None, r, tm), lambda b, i: (b, 0, i))
    full = lambda a: pl.BlockSpec(a.shape, lambda b, i: (0,) * a.ndim)
    g_kv_col = g_kv.reshape(-1, 1)
    out_shape = [jax.ShapeDtypeStruct((B, W_QABS, S), BF16),
                 jax.ShapeDtypeStruct((B, LANES, S), BF16),
                 jax.ShapeDtypeStruct((B, 8, S), F32),
                 jax.ShapeDtypeStruct((B, S, LANES), BF16),
                 jax.ShapeDtypeStruct((B, S, W_CKV), BF16),
                 jax.ShapeDtypeStruct((B, nt, VT_ROWS, tm), BF16),
                 jax.ShapeDtypeStruct((B, W_FQ, S), BF16),
                 jax.ShapeDtypeStruct((B, S, W_FQ), BF16),
                 jax.ShapeDtypeStruct((B, DIFF_HEADS, nt, VT_ROWS, tm), BF16),
                 jax.ShapeDtypeStruct((B, nt, 8, LANES), F32)]
    out_specs = [tokT(W_QABS), tokT(LANES), tokT(8), tok(LANES), tok(W_CKV),
                 pl.BlockSpec((None, None, VT_ROWS, tm), lambda b, i: (b, i, 0, 0)),
                 tokT(W_FQ), tok(W_FQ),
                 pl.BlockSpec((None, DIFF_HEADS, None, VT_ROWS, tm), lambda b, i: (b, 0, i, 0, 0)),
                 pl.BlockSpec((None, None, 8, LANES), lambda b, i: (b, i, 0, 0))]
    return pl.pallas_call(
        kern,
        out_shape=out_shape,
        grid=(B, nt),
        in_specs=[tok(D),
                  pl.BlockSpec((None, 6, D), lambda b, i: (b, 0, 0)),
                  full(g_attn), full(wn), full(wt), full(w_pair), full(g_kv), full(g_kv_col),
                  full(fc), full(sel)],
        out_specs=out_specs,
        compiler_params=pltpu.CompilerParams(dimension_semantics=("arbitrary", "arbitrary"),
                                             vmem_limit_bytes=VMEM_LIMIT),
        name="proj",
    )(x, mod3, g_attn, wn, wt, w_pair, g_kv, g_kv_col, fc, sel)


def _float_code(x):
    b = lax.bitcast_convert_type(x, jnp.int32)
    return b ^ (lax.shift_right_arithmetic(b, 31) & jnp.int32(0x7FFFFFFF))


def _float_decode(c):
    b = c ^ (lax.shift_right_arithmetic(c, 31) & jnp.int32(0x7FFFFFFF))
    return lax.bitcast_convert_type(b, F32)


def _dsa_kernel(qabsT_ref, iqT_ref, iwT_ref, ik_ref, ckv_ref, ckvT_ref, kmax_ref, lstrict_ref, wuvT_ref,
                o_ref, sc_ref, qst_ref, qt_ref, acc_ref, m_ref, *, tq, tk, topk):
    qi = pl.program_id(1)
    nkb = (qi * tq) // tk + 1
    kf = float(topk)
    slopes = _alibi_slopes(DSA_HEADS)
    q_pos = qi * tq + lax.broadcasted_iota(jnp.int32, (1, tq), 1)

    iqT = iqT_ref[...]
    rowi = lax.broadcasted_iota(jnp.int32, (LANES, tq), 0)
    for hh in range(IDX_HEADS):
        qst_ref[:, hh * tq:(hh + 1) * tq] = jnp.where((rowi // IDX_DIM) == hh, iqT, jnp.zeros_like(iqT))
    iw = iwT_ref[...]
    wrow = [iw[hh:hh + 1, :] for hh in range(IDX_HEADS)]

    def scores(kb):
        kblk = ik_ref[pl.ds(pl.multiple_of(kb * tk, tk), tk), :]
        a = _dot(kblk, qst_ref[...])
        sc = jnp.maximum(a[:, 0:tq], 0.0) * wrow[0]
        for hh in range(1, IDX_HEADS):
            sc = sc + jnp.maximum(a[:, hh * tq:(hh + 1) * tq], 0.0) * wrow[hh]
        return sc

    def score_body(kb, carry):
        mn, mx = carry
        sc = scores(kb)
        sc_ref[kb] = sc
        return (jnp.minimum(mn, jnp.min(sc, axis=0, keepdims=True)),
                jnp.maximum(mx, jnp.max(sc, axis=0, keepdims=True)))

    mn, mx = lax.fori_loop(0, nkb - 1, score_body,
                           (jnp.full((1, tq), jnp.inf, F32), jnp.full((1, tq), -jnp.inf, F32)))
    last = nkb - 1
    sc = scores(last)
    causal = lax.broadcasted_iota(jnp.int32, (tk, tq), 0) + last * tk <= q_pos
    sc_ref[last] = jnp.where(causal, sc, jnp.nan)
    mn = jnp.minimum(mn, jnp.min(jnp.where(causal, sc, jnp.inf), axis=0, keepdims=True))
    mx = jnp.maximum(mx, jnp.max(jnp.where(causal, sc, -jnp.inf), axis=0, keepdims=True))

    def count(pred):
        def body(kb, acc):
            parts = [None] * COUNT_CHAINS
            for r in range(tk // 8):
                v = jnp.where(pred(sc_ref[kb, r * 8:(r + 1) * 8, :]), 1.0, 0.0)
                c = r % COUNT_CHAINS
                parts[c] = v if parts[c] is None else parts[c] + v
            return acc + ((parts[0] + parts[1]) + (parts[2] + parts[3]))
        acc = lax.fori_loop(0, nkb, body, jnp.zeros((8, tq), F32))
        return jnp.sum(acc, axis=0, keepdims=True)

    n_causal = (q_pos + 1).astype(F32)
    done0 = n_causal <= kf

    def bisect(st, value_mid):
        lo, hi, c_lo, theta, done = st
        th8 = jnp.broadcast_to(theta, (8, tq))
        c = count(lambda v: v >= th8)
        ge = c >= kf
        live = done == 0.0
        lo = jnp.where(live & ge, theta, lo)
        c_lo = jnp.where(live & ge, c, c_lo)
        hi = jnp.where(live & jnp.logical_not(ge), theta, hi)
        if value_mid:
            nxt = 0.5 * lo + 0.5 * hi
        else:
            cl, ch = _float_code(lo), _float_code(hi)
            nxt = _float_decode((cl & ch) + lax.shift_right_arithmetic(cl ^ ch, 1))
        inside = (nxt > lo) & (nxt < hi)
        done = jnp.where((c_lo == kf) | jnp.logical_not(inside), 1.0, done)
        return lo, hi, c_lo, nxt, done

    def checked(value_mid, max_steps):
        def cond(c):
            return jnp.logical_and(c[1] > 0.0, c[2] < max_steps)

        def body(c):
            st = bisect(bisect(c[0], value_mid), value_mid)
            return st, jnp.sum(1.0 - st[4]), c[2] + 2
        return cond, body

    zero8 = jnp.zeros((8, tq), F32)
    c_ge0 = count(lambda v: v >= zero8)
    c_gt0 = count(lambda v: v > zero8)
    live0 = jnp.logical_not(done0)
    at0 = live0 & (c_ge0 >= kf)
    tie0 = at0 & (c_gt0 < kf)
    below0 = live0 & (c_ge0 < kf)
    lo0 = jnp.where(at0, 0.0, mn)
    hi0 = jnp.where(below0, 0.0, mx)
    st = (lo0, hi0, jnp.where(at0, c_ge0, n_causal),
          jnp.where(below0, 0.5 * lo0 + 0.5 * hi0, mx), jnp.where(done0 | tie0, 1.0, 0.0))
    st = lax.fori_loop(0, UNCHECKED_BISECT_STEPS, lambda i, s: bisect(s, True), st)
    cond, body = checked(True, VALUE_BISECT_STEPS - UNCHECKED_BISECT_STEPS)
    st, active, _ = lax.while_loop(cond, body, (st, jnp.sum(1.0 - st[4]), jnp.int32(0)))
    cond, body = checked(False, 34)
    st, _, _ = lax.while_loop(cond, body, (st, active, jnp.int32(0)))
    tau, c_ge = st[0], st[2]

    @pl.when(jnp.max(c_ge) > kf)
    def _():
        tau8 = jnp.broadcast_to(tau, (8, tq))
        quota = kf - count(lambda v: v > tau8)

        def tie_body(kb, seen):
            s = sc_ref[kb]
            eq = s == tau
            eqf = jnp.where(eq, 1.0, 0.0)
            rank = _dot(lstrict_ref[...], eqf.astype(BF16)) + seen
            sc_ref[kb] = jnp.where(eq & (rank >= quota), jnp.nan, s)
            return seen + jnp.sum(eqf, axis=0, keepdims=True)

        lax.fori_loop(0, nkb, tie_body, jnp.zeros((1, tq), F32))

    tau8 = jnp.broadcast_to(tau, (8, tq))
    sub = lax.broadcasted_iota(jnp.int32, (8, tq), 0).astype(F32)

    def nearest_body(kb, acc):
        parts = [None] * COUNT_CHAINS
        for r in range(tk // 8):
            idx = sub + (kb * tk + r * 8).astype(F32)
            v = jnp.where(sc_ref[kb, r * 8:(r + 1) * 8, :] >= tau8, idx, -1.0)
            c = r % COUNT_CHAINS
            parts[c] = v if parts[c] is None else jnp.maximum(parts[c], v)
        return jnp.maximum(acc, jnp.maximum(jnp.maximum(parts[0], parts[1]),
                                            jnp.maximum(parts[2], parts[3])))

    last_sel = jnp.max(lax.fori_loop(0, nkb, nearest_body, jnp.full((8, tq), -1.0, F32)),
                       axis=0, keepdims=True)
    d_min = q_pos.astype(F32) - last_sel
    kmax2 = _tile_max(kmax_ref, nkb - 1, KMAX_LANE_CKV)
    for hh in range(DSA_HEADS):
        qh = qabsT_ref[hh * DSA_LATENT:(hh + 1) * DSA_LATENT, :]
        qf = qh.astype(F32)
        shift = _shift_bound(jnp.sum(qf * qf, axis=0, keepdims=True), kmax2, -slopes[hh] * d_min)
        qt_ref[0:DSA_LATENT, hh * tq:(hh + 1) * tq] = qh
        qt_ref[DSA_LATENT:VT_ROWS, hh * tq:(hh + 1) * tq] = _query_feature_rows(
            q_pos, BF16_ROWS, slopes[hh], shift).astype(BF16)
    qt_ref[VT_ROWS:2 * DSA_LATENT, :] = jnp.zeros((2 * DSA_LATENT - VT_ROWS, DSA_HEADS * tq), BF16)
    acc_ref[...] = jnp.zeros_like(acc_ref)

    def fast_body(kb, carry):
        kx = ckv_ref[pl.ds(pl.multiple_of(kb * tk, tk), tk), :]
        s = _dot(kx, qt_ref[...])
        keep = sc_ref[kb] >= tau
        ps = [jnp.exp(jnp.where(keep, s[:, hh * tq:(hh + 1) * tq], NEG_BIG)).astype(BF16)
              for hh in range(DSA_HEADS)]
        acc_ref[...] += _dot(ckvT_ref[kb], jnp.concatenate(ps, axis=1))
        return carry

    lax.fori_loop(0, nkb, fast_body, 0)

    def attn_body(kb, carry):
        kx = ckv_ref[pl.ds(pl.multiple_of(kb * tk, tk), tk), :]
        vT = ckvT_ref[kb]
        keep = sc_ref[kb] >= tau
        qt = qt_ref[...]
        m_all = m_ref[...]
        m_out, alphas, pvs = [], [], []
        for ch in range(ATTN_CHAINS):
            hpc = DSA_HEADS // ATTN_CHAINS
            l0 = ch * hpc * tq
            s = _dot(kx, qt[:, l0:l0 + hpc * tq])
            ps = []
            for j in range(hpc):
                c0 = l0 + j * tq
                sh = jnp.where(keep, s[:, j * tq:(j + 1) * tq], NEG_BIG)
                m_old = m_all[:, c0:c0 + tq]
                m_new = jnp.maximum(m_old, jnp.max(sh, axis=0, keepdims=True))
                ps.append(jnp.exp(sh - m_new).astype(BF16))
                alphas.append(jnp.exp(m_old - m_new))
                m_out.append(m_new)
            pvs.append(_dot(vT, jnp.concatenate(ps, axis=1)))
        m_ref[...] = jnp.concatenate(m_out, axis=1)
        acc_ref[...] = jnp.concatenate(alphas, axis=1) * acc_ref[...] + jnp.concatenate(pvs, axis=1)
        return carry

    denom_ok = jnp.min(acc_ref[ONES_ROW:ONES_ROW + 1, :]) > L_MIN

    @pl.when(jnp.logical_not(denom_ok))
    def _():
        acc_ref[...] = jnp.zeros_like(acc_ref)
        m_ref[...] = jnp.full_like(m_ref, NEG_BIG)
        lax.fori_loop(0, nkb, attn_body, 0)

    acc = acc_ref[...]
    lat = acc[0:DSA_LATENT, :] / acc[ONES_ROW:ONES_ROW + 1, :]
    for p in range(DSA_HEADS // 2):
        pair = lat[:, 2 * p * tq:(2 * p + 2) * tq]
        pair = jnp.concatenate([pair[:, 0:tq], pair[:, tq:2 * tq]], axis=0).astype(BF16)
        o_ref[:, p * LANES:(p + 1) * LANES] = _dot(wuvT_ref[p], pair).T.astype(BF16)


def _dsa_call(qabsT, iqT, iwT, ik4, ckv, ckvT, kmax, lstrict, wuvT_pair, tq, tk, topk):
    B, _, S = qabsT.shape
    nkb = S // tk
    kern = functools.partial(_dsa_kernel, tq=tq, tk=tk, topk=topk)
    blkT = lambda r: pl.BlockSpec((None, r, tq), lambda b, i: (b, 0, i))
    per_b = lambda w: pl.BlockSpec((None, S, w), lambda b, i: (b, 0, 0))
    full = lambda a: pl.BlockSpec(a.shape, lambda b, i: (0,) * a.ndim)
    return pl.pallas_call(
        kern,
        out_shape=jax.ShapeDtypeStruct((B, S, W_DQ), BF16),
        grid=(B, S // tq),
        in_specs=[blkT(W_QABS), blkT(LANES), blkT(8), per_b(LANES), per_b(W_CKV),
                  pl.BlockSpec((None, nkb, VT_ROWS, tk), lambda b, i: (b, 0, 0, 0)),
                  pl.BlockSpec((None, nkb, 8, LANES), lambda b, i: (b, 0, 0, 0)),
                  full(lstrict), full(wuvT_pair)],
        out_specs=pl.BlockSpec((None, tq, W_DQ), lambda b, i: (b, i, 0)),
        scratch_shapes=[pltpu.VMEM((nkb, tk, tq), F32),
                        pltpu.VMEM((LANES, IDX_HEADS * tq), BF16),
                        pltpu.VMEM((2 * DSA_LATENT, DSA_HEADS * tq), BF16),
                        pltpu.VMEM((VT_ROWS, DSA_HEADS * tq), F32),
                        pltpu.VMEM((1, DSA_HEADS * tq), F32)],
        compiler_params=pltpu.CompilerParams(dimension_semantics=("arbitrary", "arbitrary"),
                                             vmem_limit_bytes=VMEM_LIMIT),
        name="dsa",
    )(qabsT, iqT, iwT, ik4, ckv, ckvT, kmax, lstrict, wuvT_pair)


def _diff_kernel(fqT_ref, fk_ref, fvT_ref, kmax_ref, lam_ref, gsub_ref, o_ref, qt_ref, acc_ref, m_ref,
                 *, tq, tk, lam0):
    hd = pl.program_id(1)
    qi = pl.program_id(2)
    q_pos = qi * tq + lax.broadcasted_iota(jnp.int32, (1, tq), 1)
    n_full = (qi * tq) // tk
    n_diag = (tq + tk - 1) // tk
    slopes = _alibi_slopes(DIFF_HEADS)
    slope = jnp.float32(slopes[DIFF_HEADS - 1])
    for i in range(DIFF_HEADS - 1):
        slope = jnp.where(hd == i, slopes[i], slope)

    for m in range(2):
        r0 = m * LANES
        qh = fqT_ref[r0:r0 + DIFF_QK_DIM, :]
        qf = qh.astype(F32)
        shift = _shift_bound(jnp.sum(qf * qf, axis=0, keepdims=True),
                             _tile_max(kmax_ref, n_full + n_diag - 1, 2 * hd + m), 0.0)
        qt_ref[r0:r0 + DIFF_QK_DIM, :] = qh
        qt_ref[r0 + DIFF_QK_DIM:r0 + DIFF_QK_DIM + BF16_ROWS, :] = _query_feature_rows(
            q_pos, BF16_ROWS, slope, shift).astype(BF16)
        qt_ref[r0 + DIFF_QK_DIM + BF16_ROWS:r0 + LANES, :] = jnp.zeros(
            (LANES - DIFF_QK_DIM - BF16_ROWS, tq), BF16)
    acc_ref[...] = jnp.zeros_like(acc_ref)

    def fast_block(kb, masked):
        kx = fk_ref[pl.ds(pl.multiple_of(kb * tk, tk), tk), :]
        vT = fvT_ref[kb]
        for m in range(2):
            s = _dot(kx[:, m * LANES:(m + 1) * LANES], qt_ref[m * LANES:(m + 1) * LANES, :])
            if masked:
                causal = lax.broadcasted_iota(jnp.int32, (tk, tq), 0) + kb * tk <= q_pos
                s = jnp.where(causal, s, NEG_BIG)
            acc_ref[m] += _dot(vT, jnp.exp(s).astype(BF16))

    def fast_body(kb, carry):
        fast_block(kb, False)
        return carry

    lax.fori_loop(0, n_full, fast_body, 0)
    for d in range(n_diag):
        fast_block(n_full + d, True)

    def block(kb, masked):
        kx = fk_ref[pl.ds(pl.multiple_of(kb * tk, tk), tk), :]
        vT = fvT_ref[kb]
        qT = qt_ref[...]
        m_all = m_ref[...]
        new = []
        for m in range(2):
            s = _dot(kx[:, m * LANES:(m + 1) * LANES], qT[m * LANES:(m + 1) * LANES, :])
            if masked:
                causal = lax.broadcasted_iota(jnp.int32, (tk, tq), 0) + kb * tk <= q_pos
                s = jnp.where(causal, s, NEG_BIG)
            m_old = m_all[m]
            m_new = jnp.maximum(m_old, jnp.max(s, axis=0, keepdims=True))
            p = jnp.exp(s - m_new).astype(BF16)
            new.append((m_new, jnp.exp(m_old - m_new), _dot(vT, p)))
        for m in range(2):
            m_ref[m] = new[m][0]
            acc_ref[m] = new[m][1] * acc_ref[m] + new[m][2]

    def body(kb, carry):
        block(kb, False)
        return carry

    denom_ok = jnp.min(acc_ref[:, ONES_ROW:ONES_ROW + 1, :]) > L_MIN

    @pl.when(jnp.logical_not(denom_ok))
    def _():
        acc_ref[...] = jnp.zeros_like(acc_ref)
        m_ref[...] = jnp.full_like(m_ref, NEG_BIG)
        lax.fori_loop(0, n_full, body, 0)
        for d in range(n_diag):
            block(n_full + d, True)

    lv = lam_ref[...]
    lam = (jnp.exp(jnp.sum(lv[0:1] * lv[1:2], axis=-1, keepdims=True))
           - jnp.exp(jnp.sum(lv[2:3] * lv[3:4], axis=-1, keepdims=True)) + lam0)
    a1 = acc_ref[0]
    a2 = acc_ref[1]
    o = (a1[0:DIFF_V_DIM, :] / a1[ONES_ROW:ONES_ROW + 1, :]
         - lam * (a2[0:DIFF_V_DIM, :] / a2[ONES_ROW:ONES_ROW + 1, :]))
    o = o * lax.rsqrt(jnp.mean(o * o, axis=0, keepdims=True) + EPS) * gsub_ref[...] * (1.0 - lam0)
    o_ref[...] = o.T.astype(BF16)


def _diff_call(fqT, fk, fvT, kmax, lam_vecs, g_sub_col, tq, tk, lam0):
    B, S, _ = fk.shape
    nkb = S // tk
    kern = functools.partial(_diff_kernel, tq=tq, tk=tk, lam0=lam0)
    full = lambda a: pl.BlockSpec(a.shape, lambda b, h, i: (0,) * a.ndim)
    return pl.pallas_call(
        kern,
        out_shape=jax.ShapeDtypeStruct((B, S, DIFF_HEADS * DIFF_V_DIM), BF16),
        grid=(B, DIFF_HEADS, S // tq),
        in_specs=[pl.BlockSpec((None, 2 * LANES, tq), lambda b, h, i: (b, h, i)),
                  pl.BlockSpec((None, S, 2 * LANES), lambda b, h, i: (b, 0, h)),
                  pl.BlockSpec((None, None, nkb, VT_ROWS, tk), lambda b, h, i: (b, h, 0, 0, 0)),
                  pl.BlockSpec((None, nkb, 8, LANES), lambda b, h, i: (b, 0, 0, 0)),
                  full(lam_vecs), full(g_sub_col)],
        out_specs=pl.BlockSpec((None, tq, DIFF_V_DIM), lambda b, h, i: (b, i, h)),
        scratch_shapes=[pltpu.VMEM((2 * LANES, tq), BF16),
                        pltpu.VMEM((2, VT_ROWS, tq), F32),
                        pltpu.VMEM((2, 1, tq), F32)],
        compiler_params=pltpu.CompilerParams(
            dimension_semantics=("arbitrary", "arbitrary", "arbitrary"),
            vmem_limit_bytes=VMEM_LIMIT),
        name="diff",
    )(fqT, fk, fvT, kmax, lam_vecs, g_sub_col)


def _route_kernel(x_ref, od_ref, of_ref, mod_ref, wout_ref, g_ref, wr_hi_ref, wr_lo_ref, br_ref,
                  x1_ref, h_ref, gates_ref, *, tm):
    attn = (_dot(od_ref[...], wout_ref[0:W_DQ, :]) + _dot(of_ref[...], wout_ref[W_DQ:, :]))
    x1 = x_ref[...] + mod_ref[2:3, :] * attn
    x1_ref[...] = x1
    h = _rms(x1, g_ref[...]) * (1.0 + mod_ref[4:5, :]) + mod_ref[3:4, :]
    h_hi = h.astype(BF16)
    h_ref[...] = h_hi
    h_lo = (h - h_hi.astype(F32)).astype(BF16)
    w_hi = wr_hi_ref[...]
    logits = (_dot(h_hi, w_hi) + _dot(h_lo, w_hi) + _dot(h_hi, wr_lo_ref[...])) + br_ref[...]

    lane = lax.broadcasted_iota(jnp.int32, (tm, LANES), 1)
    big = jnp.int32(4 * LANES)
    neg = -jnp.inf
    is_g = (lane >= N_EXPERTS) & (lane < N_EXPERTS + N_GROUPS)
    gl = jnp.where(is_g, logits, neg)
    gmax = jnp.max(gl, axis=-1, keepdims=True)
    g_lane = jnp.min(jnp.where(gl == gmax, lane, big), axis=-1, keepdims=True)
    g_sel = g_lane - N_EXPERTS
    p_g = 1.0 / jnp.sum(jnp.where(is_g, jnp.exp(gl - gmax), 0.0), axis=-1, keepdims=True)
    in_grp = (lane < N_EXPERTS) & ((lane // EXPERTS_PER_GROUP) == g_sel)
    el = jnp.where(in_grp, logits, neg)
    v1 = jnp.max(el, axis=-1, keepdims=True)
    i1 = jnp.min(jnp.where(el == v1, lane, big), axis=-1, keepdims=True)
    el2 = jnp.where(lane == i1, neg, el)
    v2 = jnp.max(el2, axis=-1, keepdims=True)
    i2 = jnp.min(jnp.where(el2 == v2, lane, big), axis=-1, keepdims=True)
    e = jnp.exp(v2 - v1)
    w1 = 1.0 / (1.0 + e)
    w2 = e * w1
    gates_ref[...] = jnp.where(lane == i1, w1 * p_g, jnp.where(lane == i2, w2 * p_g, 0.0))


def _route_call(x, o_dsa, o_diff, mod3, w_out, g_moe, wr_hi, wr_lo, b_r, tm):
    B, S, D = x.shape
    kern = functools.partial(_route_kernel, tm=tm)
    tok = lambda w: pl.BlockSpec((None, tm, w), lambda b, i: (b, i, 0))
    full = lambda a: pl.BlockSpec(a.shape, lambda b, i: (0,) * a.ndim)
    return pl.pallas_call(
        kern,
        out_shape=[jax.ShapeDtypeStruct((B, S, D), F32), jax.ShapeDtypeStruct((B, S, D), BF16),
                   jax.ShapeDtypeStruct((B, S, LANES), F32)],
        grid=(B, S // tm),
        in_specs=[tok(D), tok(W_DQ), tok(DIFF_HEADS * DIFF_V_DIM),
                  pl.BlockSpec((None, 6, D), lambda b, i: (b, 0, 0)),
                  full(w_out), full(g_moe), full(wr_hi), full(wr_lo), full(b_r)],
        out_specs=[tok(D), tok(D), tok(LANES)],
        compiler_params=pltpu.CompilerParams(dimension_semantics=("arbitrary", "arbitrary"),
                                             vmem_limit_bytes=VMEM_LIMIT),
        name="route",
    )(x, o_dsa, o_diff, mod3, w_out, g_moe, wr_hi, wr_lo, b_r)


def _moe_kernel(h_ref, gates_ref, x1_ref, mod_ref, wg_ref, wu_ref, wd_ref, gf_ref, o_ref, acc_ref,
                *, tm, ec):
    j = pl.program_id(2)
    nc = pl.num_programs(2)

    @pl.when(j == 0)
    def _():
        acc_ref[...] = jnp.zeros_like(acc_ref)

    h = h_ref[...]
    hg = _dot(h, wg_ref[...])
    hu = _dot(h, wu_ref[...])
    hid = hg * jax.nn.sigmoid(hg) * hu
    gates = gates_ref[...]
    lane = lax.broadcasted_iota(jnp.int32, (tm, LANES), 1)
    parts = []
    for e in range(ec):
        gcol = jnp.sum(jnp.where(lane == j * ec + e, gates, 0.0), axis=-1, keepdims=True)
        parts.append((hid[:, e * D_EXPERT:(e + 1) * D_EXPERT] * gcol).astype(BF16))
    hs = jnp.concatenate(parts, axis=1)
    acc_ref[...] += _dot(hs, wd_ref[...])

    @pl.when(j == nc - 1)
    def _():
        x2 = x1_ref[...] + mod_ref[5:6, :] * acc_ref[...]
        o_ref[...] = _rms(x2, gf_ref[...])


def _moe_call(h, gates, x1, mod3, wg, wu, wd, g_final, tm, ec):
    B, S, D = x1.shape
    kern = functools.partial(_moe_kernel, tm=tm, ec=ec)
    tok = lambda w: pl.BlockSpec((None, tm, w), lambda b, i, j: (b, i, 0))
    return pl.pallas_call(
        kern,
        out_shape=jax.ShapeDtypeStruct((B, S, D), F32),
        grid=(B, S // tm, N_EXPERTS // ec),
        in_specs=[tok(D), tok(LANES), tok(D),
                  pl.BlockSpec((None, 6, D), lambda b, i, j: (b, 0, 0)),
                  pl.BlockSpec((D, ec * D_EXPERT), lambda b, i, j: (0, j)),
                  pl.BlockSpec((D, ec * D_EXPERT), lambda b, i, j: (0, j)),
                  pl.BlockSpec((ec * D_EXPERT, D), lambda b, i, j: (j, 0)),
                  pl.BlockSpec((1, D), lambda b, i, j: (0, 0))],
        out_specs=tok(D),
        scratch_shapes=[pltpu.VMEM((tm, D), F32)],
        compiler_params=pltpu.CompilerParams(
            dimension_semantics=("arbitrary", "arbitrary", "arbitrary"),
            vmem_limit_bytes=VMEM_LIMIT),
        name="moe",
    )(h, gates, x1, mod3, wg, wu, wd, g_final)


def _tile(n, pref):
    t = min(n, pref)
    assert n % t == 0, (n, t)
    return t


def kernel(x, c, w_ada, b_ada, g_attn, w_in, g_kv, w_uk, w_uv, lam_q1, lam_k1, lam_q2, lam_k2,
           g_sub, w_out, g_moe, w_group, b_group, w_router, b_router, w_gate, w_up, w_down, g_final):
    B, S, D = x.shape
    assert D == D_MODEL and w_ada.shape[0] == 1
    assert S % KEY_TILE == 0 and S <= POS_SPLIT * 256
    topk = min(TOPK_MAX, S // 4)
    l = 0
    lam0 = 0.8 - 0.6 * math.exp(-0.3 * l)

    mod3 = _mod_call(c, w_ada[l], b_ada[l].reshape(1, -1)).reshape(B, 6, D)

    wn, wt, w_pair = _proj_weights(w_in[l], w_uk[l])
    qabsT, iqT, iwT, ik4, ckv, ckvT, fqT, fk, fvT, kmax = _proj_call(
        x, mod3, g_attn[l].reshape(1, D), wn, wt, w_pair, g_kv[l].reshape(1, -1), KEY_TILE)

    lstrict = jnp.asarray(np.tril(np.ones((KEY_TILE, KEY_TILE), np.float32), -1), BF16)
    uvT = jnp.swapaxes(w_uv[l], 1, 2)
    z = jnp.zeros_like(uvT[0])
    wuvT_pair = jnp.stack([
        jnp.concatenate([jnp.concatenate([uvT[2 * p], z], axis=1),
                         jnp.concatenate([z, uvT[2 * p + 1]], axis=1)], axis=0)
        for p in range(DSA_HEADS // 2)]).astype(BF16)
    o_dsa = _dsa_call(qabsT, iqT, iwT, ik4, ckv, ckvT, kmax, lstrict, wuvT_pair, LANES, KEY_TILE, topk)

    lam_vecs = jnp.concatenate([lam_q1[l][None], lam_k1[l][None], lam_q2[l][None], lam_k2[l][None]],
                               axis=0).astype(F32)
    o_diff = _diff_call(fqT, fk, fvT, kmax, lam_vecs, g_sub[l].reshape(-1, 1), KEY_TILE, KEY_TILE, lam0)

    wr = jnp.pad(jnp.concatenate([w_router[l], w_group[l]], axis=1),
                 ((0, 0), (0, LANES - N_EXPERTS - N_GROUPS)))
    wr_hi = wr.astype(BF16)
    wr_lo = (wr - wr_hi.astype(F32)).astype(BF16)
    b_r = jnp.pad(jnp.concatenate([b_router[l], b_group[l]]), (0, LANES - N_EXPERTS - N_GROUPS))
    x1, h2, gates = _route_call(x, o_dsa, o_diff, mod3, w_out[l].astype(BF16),
                                g_moe[l].reshape(1, D), wr_hi, wr_lo, b_r.reshape(1, LANES),
                                _tile(S, 512))

    wg = jnp.swapaxes(w_gate[l], 0, 1).reshape(D, N_EXPERTS * D_EXPERT).astype(BF16)
    wu = jnp.swapaxes(w_up[l], 0, 1).reshape(D, N_EXPERTS * D_EXPERT).astype(BF16)
    wd = w_down[l].reshape(N_EXPERTS * D_EXPERT, D).astype(BF16)
    return _moe_call(h2, gates, x1, mod3, wg, wu, wd, g_final.reshape(1, D), _tile(S, 512), 4)
```

```python
import functools
import math

import jax
import jax.numpy as jnp
import numpy as np
from jax import lax
from jax.experimental import pallas as pl
from jax.experimental.pallas import tpu as pltpu

F32 = jnp.float32
BF16 = jnp.bfloat16

D_MODEL = 1024
DSA_HEADS = 8
DSA_HEAD_DIM = 64
DSA_LATENT = 128
IDX_HEADS = 4
IDX_DIM = 32
TOPK_MAX = 256
DIFF_HEADS = 4
DIFF_QK_DIM = 64
DIFF_V_DIM = 128
N_GROUPS = 4
EXPERTS_PER_GROUP = 8
N_EXPERTS = 32
D_EXPERT = 256
EPS = 1e-6

LANES = 128
BF16_ROWS = 16
POS_SPLIT = 64
NEG_BIG = -1e30
VMEM_LIMIT = 56 * 1024 * 1024
KEY_TILE = 512
VALUE_BISECT_STEPS = 24
UNCHECKED_BISECT_STEPS = 16
COUNT_CHAINS = 4
ATTN_CHAINS = 1

W_DQ = DSA_HEADS * DSA_HEAD_DIM
W_QABS = DSA_HEADS * DSA_LATENT
W_CKV = 2 * LANES
VT_ROWS = DSA_LATENT + BF16_ROWS
ONES_ROW = DSA_LATENT + 2
N_FEATS = 5
KMAX_LANE_CKV = 8
L_MIN = 1e-30
W_FQ = DIFF_HEADS * 2 * LANES
W_DIFF_QK = DIFF_HEADS * 2 * DIFF_QK_DIM

T_DQ = 0
T_DLAT = T_DQ + W_DQ
T_IQ = T_DLAT + DSA_LATENT
T_IW = T_IQ + IDX_HEADS * IDX_DIM
T_FQ = T_IW + BF16_ROWS
T_FV = T_FQ + W_DIFF_QK
T_ROWS = T_FV + DIFF_HEADS * DIFF_V_DIM
N_IK = 0
N_DLAT = N_IK + LANES
N_FK = N_DLAT + DSA_LATENT
N_COLS = N_FK + W_FQ


def _alibi_slopes(n):
    return [2.0 ** (-8.0 * (i + 1) / n) for i in range(n)]


def _rms(x, g):
    return x * lax.rsqrt(jnp.mean(x * x, axis=-1, keepdims=True) + EPS) * g


def _dot(a, b):
    return jnp.dot(a, b, preferred_element_type=F32)


def _mod_kernel(c_ref, w_ref, b_ref, o_ref):
    c = c_ref[...]
    act = c * jax.nn.sigmoid(c)
    o_ref[...] = jnp.dot(act, w_ref[...], preferred_element_type=F32,
                         precision=lax.Precision.HIGHEST) + b_ref[...]


def _mod_call(c, w_ada, b_ada):
    B, D = c.shape
    n = w_ada.shape[1] // D
    return pl.pallas_call(
        _mod_kernel,
        out_shape=jax.ShapeDtypeStruct((B, n * D), F32),
        grid=(n,),
        in_specs=[pl.BlockSpec((B, D), lambda j: (0, 0)),
                  pl.BlockSpec((D, D), lambda j: (0, j)),
                  pl.BlockSpec((1, D), lambda j: (0, j))],
        out_specs=pl.BlockSpec((B, D), lambda j: (0, j)),
        compiler_params=pltpu.CompilerParams(dimension_semantics=("arbitrary",),
                                             vmem_limit_bytes=VMEM_LIMIT),
        name="mod",
    )(c, w_ada, b_ada)


def _key_feature_rows(pos_row, n_rows):
    r = lax.broadcasted_iota(jnp.int32, (n_rows, pos_row.shape[1]), 0)
    pa = (pos_row // POS_SPLIT).astype(F32)
    pb = (pos_row % POS_SPLIT).astype(F32)
    return jnp.where(r == 0, pa, jnp.where(r == 1, pb, jnp.where(r < N_FEATS, 1.0, 0.0)))


def _query_feature_rows(pos_row, n_rows, slope, shift):
    r = lax.broadcasted_iota(jnp.int32, (n_rows, pos_row.shape[1]), 0)
    pa = (pos_row // POS_SPLIT).astype(F32)
    pb = (pos_row % POS_SPLIT).astype(F32)
    return jnp.where(r == 0, POS_SPLIT * slope,
                     jnp.where(r == 1, slope,
                               jnp.where(r == 2, -POS_SPLIT * slope * pa,
                                         jnp.where(r == 3, -slope * pb,
                                                   jnp.where(r == 4, -shift, 0.0)))))


def _shift_bound(qn2, kmax2, extra):
    b = jnp.sqrt(qn2 * kmax2) + extra
    return b + jnp.abs(b) * (2.0 ** -6) + 2.0 ** -20


def _tile_max(kmax_ref, last_tile, lane_idx):
    x = kmax_ref[...]
    t = lax.broadcasted_iota(jnp.int32, x.shape, 0)
    ln = lax.broadcasted_iota(jnp.int32, x.shape, 2)
    x = jnp.where((t <= last_tile) & (ln == lane_idx), x, 0.0)
    return jnp.max(jnp.max(x, axis=0), axis=1, keepdims=True)[0:1, :]


def _proj_kernel(x_ref, mod_ref, g_ref, wn_ref, wt_ref, wpair_ref, gkv_ref, gkvc_ref, fc_ref, sel_ref,
                 qabsT_ref, iqT_ref, iwT_ref, ik_ref, ckv_ref, ckvT_ref, fqT_ref, fk_ref, fvT_ref,
                 kmax_ref, *, tm):
    x = x_ref[...]
    h = _rms(x, g_ref[...]) * (1.0 + mod_ref[1:2, :]) + mod_ref[0:1, :]
    hb = h.astype(BF16)
    hT = h.T.astype(BF16)

    base = pl.program_id(1) * tm
    pos_c = base + lax.broadcasted_iota(jnp.int32, (tm, 1), 0)
    pos_r = base + lax.broadcasted_iota(jnp.int32, (1, tm), 1)
    pa = (pos_c // POS_SPLIT).astype(F32)
    pb = (pos_c % POS_SPLIT).astype(F32)

    def feats(row, width):
        return (fc_ref[row:row + 1, 0:width] + fc_ref[row + 1:row + 2, 0:width] * pa
                + fc_ref[row + 2:row + 3, 0:width] * pb)

    ik_ref[...] = _dot(hb, wn_ref[:, N_IK:N_IK + LANES]).astype(BF16)
    dlat = _dot(hb, wn_ref[:, N_DLAT:N_DLAT + DSA_LATENT])
    ckv_b = _rms(dlat, gkv_ref[...]).astype(BF16)
    ckv_ref[:, 0:LANES] = ckv_b
    ckv_ref[:, LANES:2 * LANES] = feats(0, LANES).astype(BF16)
    fk = _dot(hb, wn_ref[:, N_FK:N_FK + W_FQ])
    fk_ref[...] = (fk + feats(3, W_FQ)).astype(BF16)
    sq = jnp.concatenate([fk.astype(BF16).astype(F32), ckv_b.astype(F32)], axis=1)
    sq_up = (sq * sq * (1.0 + 2.0 ** -7)).astype(BF16)
    kmax_ref[...] = jnp.broadcast_to(jnp.max(_dot(sq_up, sel_ref[...]), axis=0, keepdims=True),
                                     (8, LANES))

    dqT = _dot(wt_ref[T_DQ:T_DQ + W_DQ, :], hT).astype(BF16)
    for p in range(DSA_HEADS // 2):
        qa = _dot(wpair_ref[p], dqT[p * LANES:(p + 1) * LANES, :])
        qabsT_ref[p * 2 * LANES:(p + 1) * 2 * LANES, :] = qa.astype(BF16)
    dlatT = _dot(wt_ref[T_DLAT:T_DLAT + DSA_LATENT, :], hT)
    inv = lax.rsqrt(jnp.mean(dlatT * dlatT, axis=0, keepdims=True) + EPS)
    kfeat = _key_feature_rows(pos_r, BF16_ROWS).astype(BF16)
    ckvT_ref[0:DSA_LATENT, :] = (dlatT * inv * gkvc_ref[...]).astype(BF16)
    ckvT_ref[DSA_LATENT:VT_ROWS, :] = kfeat
    iqT_ref[...] = _dot(wt_ref[T_IQ:T_IQ + IDX_HEADS * IDX_DIM, :], hT).astype(BF16)
    iwT = _dot(wt_ref[T_IW:T_IW + BF16_ROWS, :], hT)
    iwT_ref[...] = iwT[0:8, :] * (IDX_HEADS ** -0.5 * IDX_DIM ** -0.5)
    fqT = _dot(wt_ref[T_FQ:T_FQ + W_DIFF_QK, :], hT)
    zpad = jnp.zeros((LANES - DIFF_QK_DIM, tm), BF16)
    for s in range(DIFF_HEADS * 2):
        r0 = s * LANES
        fqT_ref[r0:r0 + DIFF_QK_DIM, :] = fqT[s * DIFF_QK_DIM:(s + 1) * DIFF_QK_DIM, :].astype(BF16)
        fqT_ref[r0 + DIFF_QK_DIM:r0 + LANES, :] = zpad
    fvT = _dot(wt_ref[T_FV:T_FV + DIFF_HEADS * DIFF_V_DIM, :], hT)
    for hh in range(DIFF_HEADS):
        fvT_ref[hh, 0:DIFF_V_DIM, :] = fvT[hh * DIFF_V_DIM:(hh + 1) * DIFF_V_DIM, :].astype(BF16)
        fvT_ref[hh, DIFF_V_DIM:VT_ROWS, :] = kfeat


def _feature_consts():
    fc = np.zeros((8, W_FQ), np.float32)
    fc[1, 0] = 1.0
    fc[2, 1] = 1.0
    fc[0, 2:N_FEATS] = 1.0
    for s in range(DIFF_HEADS * 2):
        base = s * LANES + DIFF_QK_DIM
        fc[4, base + 0] = 1.0
        fc[5, base + 1] = 1.0
        fc[3, base + 2:base + N_FEATS] = 1.0
    return jnp.asarray(fc)


def _norm_selector():
    sel = np.zeros((W_FQ + DSA_LATENT, LANES), np.float32)
    for s in range(DIFF_HEADS * 2):
        sel[s * LANES:s * LANES + DIFF_QK_DIM, s] = 1.0
    sel[W_FQ:, KMAX_LANE_CKV] = 1.0
    return jnp.asarray(sel, BF16)


def _proj_weights(w_in, w_uk):
    D = w_in.shape[0]
    pts = np.cumsum([W_DQ, DSA_LATENT, IDX_HEADS * IDX_DIM, IDX_DIM, IDX_HEADS,
                     W_DIFF_QK, W_DIFF_QK])
    dq, dlat, iq, ik, iw, fq, fk, fv = jnp.split(w_in, list(pts), axis=1)
    ik4 = jnp.tile(ik, (1, IDX_HEADS))
    fke = jnp.pad(fk.reshape(D, DIFF_HEADS * 2, DIFF_QK_DIM),
                  ((0, 0), (0, 0), (0, LANES - DIFF_QK_DIM))).reshape(D, W_FQ)
    wn = jnp.concatenate([ik4, dlat, fke], axis=1).astype(BF16)
    iwp = jnp.pad(iw, ((0, 0), (0, BF16_ROWS - IDX_HEADS)))
    wt = jnp.concatenate([dq, dlat, iq, iwp, fq * (DIFF_QK_DIM ** -0.5), fv], axis=1).T.astype(BF16)
    uk = w_uk * (DSA_HEAD_DIM ** -0.5)
    z = jnp.zeros_like(uk[0])
    pairs = [jnp.concatenate([jnp.concatenate([uk[2 * p], z], axis=1),
                              jnp.concatenate([z, uk[2 * p + 1]], axis=1)], axis=0)
             for p in range(DSA_HEADS // 2)]
    return wn, wt, jnp.stack(pairs).astype(BF16)


def _proj_call(x, mod3, g_attn, wn, wt, w_pair, g_kv, tm):
    B, S, D = x.shape
    nt = S // tm
    fc = _feature_consts()
    sel = _norm_selector()
    kern = functools.partial(_proj_kernel, tm=tm)
    tok = lambda w: pl.BlockSpec((None, tm, w), lambda b, i: (b, i, 0))
    tokT = lambda r: pl.BlockSpec((None, r, tm), lambda b, i: (b, 0, i))
    full = lambda a: pl.BlockSpec(a.shape, lambda b, i: (0,) * a.ndim)
    g_kv_col = g_kv.reshape(-1, 1)
    out_shape = [jax.ShapeDtypeStruct((B, W_QABS, S), BF16),
                 jax.ShapeDtypeStruct((B, LANES, S), BF16),
                 jax.ShapeDtypeStruct((B, 8, S), F32),
                 jax.ShapeDtypeStruct((B, S, LANES), BF16),
                 jax.ShapeDtypeStruct((B, S, W_CKV), BF16),
                 jax.ShapeDtypeStruct((B, nt, VT_ROWS, tm), BF16),
                 jax.ShapeDtypeStruct((B, W_FQ, S), BF16),
                 jax.ShapeDtypeStruct((B, S, W_FQ), BF16),
                 jax.ShapeDtypeStruct((B, DIFF_HEADS, nt, VT_ROWS, tm), BF16),
                 jax.ShapeDtypeStruct((B, nt, 8, LANES), F32)]
    out_specs = [tokT(W_QABS), tokT(LANES), tokT(8), tok(LANES), tok(W_CKV),
                 pl.BlockSpec((None, None, VT_ROWS, tm), lambda b, i: (b, i, 0, 0)),
                 tokT(W_FQ), tok(W_FQ),
                 pl.BlockSpec((None, DIFF_HEADS, None, VT_ROWS, tm), lambda b, i: (b, 0, i, 0, 0)),
                 pl.BlockSpec((None, None, 8, LANES), lambda b, i: (b, i, 0, 0))]
    return pl.pallas_call(
        kern,
        out_shape=out_shape,
        grid=(B, nt),
        in_specs=[tok(D),
                  pl.BlockSpec((None, 6, D), lambda b, i: (b, 0, 0)),
                  full(g_attn), full(wn), full(wt), full(w_pair), full(g_kv), full(g_kv_col),
                  full(fc), full(sel)],
        out_specs=out_specs,
        compiler_params=pltpu.CompilerParams(dimension_semantics=("arbitrary", "arbitrary"),
                                             vmem_limit_bytes=VMEM_LIMIT),
        name="proj",
    )(x, mod3, g_attn, wn, wt, w_pair, g_kv, g_kv_col, fc, sel)


def _float_code(x):
    b = lax.bitcast_convert_type(x, jnp.int32)
    return b ^ (lax.shift_right_arithmetic(b, 31) & jnp.int32(0x7FFFFFFF))


def _float_decode(c):
    b = c ^ (lax.shift_right_arithmetic(c, 31) & jnp.int32(0x7FFFFFFF))
    return lax.bitcast_convert_type(b, F32)


def _dsa_kernel(qabsT_ref, iqT_ref, iwT_ref, ik_ref, ckv_ref, ckvT_ref, kmax_ref, lstrict_ref, wuvT_ref,
                o_ref, sc_ref, qst_ref, qt_ref, acc_ref, m_ref, *, tq, tk, topk):
    qi = pl.program_id(1)
    nkb = (qi * tq) // tk + 1
    kf = float(topk)
    slopes = _alibi_slopes(DSA_HEADS)
    q_pos = qi * tq + lax.broadcasted_iota(jnp.int32, (1, tq), 1)

    iqT = iqT_ref[...]
    rowi = lax.broadcasted_iota(jnp.int32, (LANES, tq), 0)
    for hh in range(IDX_HEADS):
        qst_ref[:, hh * tq:(hh + 1) * tq] = jnp.where((rowi // IDX_DIM) == hh, iqT, jnp.zeros_like(iqT))
    iw = iwT_ref[...]
    wrow = [iw[hh:hh + 1, :] for hh in range(IDX_HEADS)]

    def scores(kb):
        kblk = ik_ref[pl.ds(pl.multiple_of(kb * tk, tk), tk), :]
        a = _dot(kblk, qst_ref[...])
        sc = jnp.maximum(a[:, 0:tq], 0.0) * wrow[0]
        for hh in range(1, IDX_HEADS):
            sc = sc + jnp.maximum(a[:, hh * tq:(hh + 1) * tq], 0.0) * wrow[hh]
        return sc

    def score_body(kb, carry):
        mn, mx = carry
        sc = scores(kb)
        sc_ref[kb] = sc
        return (jnp.minimum(mn, jnp.min(sc, axis=0, keepdims=True)),
                jnp.maximum(mx, jnp.max(sc, axis=0, keepdims=True)))

    mn, mx = lax.fori_loop(0, nkb - 1, score_body,
                           (jnp.full((1, tq), jnp.inf, F32), jnp.full((1, tq), -jnp.inf, F32)))
    last = nkb - 1
    sc = scores(last)
    causal = lax.broadcasted_iota(jnp.int32, (tk, tq), 0) + last * tk <= q_pos
    sc_ref[last] = jnp.where(causal, sc, jnp.nan)
    mn = jnp.minimum(mn, jnp.min(jnp.where(causal, sc, jnp.inf), axis=0, keepdims=True))
    mx = jnp.maximum(mx, jnp.max(jnp.where(causal, sc, -jnp.inf), axis=0, keepdims=True))

    def count(pred):
        def body(kb, acc):
            parts = [None] * COUNT_CHAINS
            for r in range(tk // 8):
                v = jnp.where(pred(sc_ref[kb, r * 8:(r + 1) * 8, :]), 1.0, 0.0)
                c = r % COUNT_CHAINS
                parts[c] = v if parts[c] is None else parts[c] + v
            return acc + ((parts[0] + parts[1]) + (parts[2] + parts[3]))
        acc = lax.fori_loop(0, nkb, body, jnp.zeros((8, tq), F32))
        return jnp.sum(acc, axis=0, keepdims=True)

    n_causal = (q_pos + 1).astype(F32)
    done0 = n_causal <= kf

    def bisect(st, value_mid):
        lo, hi, c_lo, theta, done = st
        th8 = jnp.broadcast_to(theta, (8, tq))
        c = count(lambda v: v >= th8)
        ge = c >= kf
        live = done == 0.0
        lo = jnp.where(live & ge, theta, lo)
        c_lo = jnp.where(live & ge, c, c_lo)
        hi = jnp.where(live & jnp.logical_not(ge), theta, hi)
        if value_mid:
            nxt = 0.5 * lo + 0.5 * hi
        else:
            cl, ch = _float_code(lo), _float_code(hi)
            nxt = _float_decode((cl & ch) + lax.shift_right_arithmetic(cl ^ ch, 1))
        inside = (nxt > lo) & (nxt < hi)
        done = jnp.where((c_lo == kf) | jnp.logical_not(inside), 1.0, done)
        return lo, hi, c_lo, nxt, done

    def checked(value_mid, max_steps):
        def cond(c):
            return jnp.logical_and(c[1] > 0.0, c[2] < max_steps)

        def body(c):
            st = bisect(bisect(c[0], value_mid), value_mid)
            return st, jnp.sum(1.0 - st[4]), c[2] + 2
        return cond, body

    zero8 = jnp.zeros((8, tq), F32)
    c_ge0 = count(lambda v: v >= zero8)
    c_gt0 = count(lambda v: v > zero8)
    live0 = jnp.logical_not(done0)
    at0 = live0 & (c_ge0 >= kf)
    tie0 = at0 & (c_gt0 < kf)
    below0 = live0 & (c_ge0 < kf)
    lo0 = jnp.where(at0, 0.0, mn)
    hi0 = jnp.where(below0, 0.0, mx)
    st = (lo0, hi0, jnp.where(at0, c_ge0, n_causal),
          jnp.where(below0, 0.5 * lo0 + 0.5 * hi0, mx), jnp.where(done0 | tie0, 1.0, 0.0))
    st = lax.fori_loop(0, UNCHECKED_BISECT_STEPS, lambda i, s: bisect(s, True), st)
    cond, body = checked(True, VALUE_BISECT_STEPS - UNCHECKED_BISECT_STEPS)
    st, active, _ = lax.while_loop(cond, body, (st, jnp.sum(1.0 - st[4]), jnp.int32(0)))
    cond, body = checked(False, 34)
    st, _, _ = lax.while_loop(cond, body, (st, active, jnp.int32(0)))
    tau, c_ge = st[0], st[2]

    @pl.when(jnp.max(c_ge) > kf)
    def _():
        tau8 = jnp.broadcast_to(tau, (8, tq))
        quota = kf - count(lambda v: v > tau8)

        def tie_body(kb, seen):
            s = sc_ref[kb]
            eq = s == tau
            eqf = jnp.where(eq, 1.0, 0.0)
            rank = _dot(lstrict_ref[...], eqf.astype(BF16)) + seen
            sc_ref[kb] = jnp.where(eq & (rank >= quota), jnp.nan, s)
            return seen + jnp.sum(eqf, axis=0, keepdims=True)

        lax.fori_loop(0, nkb, tie_body, jnp.zeros((1, tq), F32))

    tau8 = jnp.broadcast_to(tau, (8, tq))
    sub = lax.broadcasted_iota(jnp.int32, (8, tq), 0).astype(F32)

    def nearest_body(kb, acc):
        parts = [None] * COUNT_CHAINS
        for r in range(tk // 8):
            idx = sub + (kb * tk + r * 8).astype(F32)
            v = jnp.where(sc_ref[kb, r * 8:(r + 1) * 8, :] >= tau8, idx, -1.0)
            c = r % COUNT_CHAINS
            parts[c] = v if parts[c] is None else jnp.maximum(parts[c], v)
        return jnp.maximum(acc, jnp.maximum(jnp.maximum(parts[0], parts[1]),
                                            jnp.maximum(parts[2], parts[3])))

    last_sel = jnp.max(lax.fori_loop(0, nkb, nearest_body, jnp.full((8, tq), -1.0, F32)),
                       axis=0, keepdims=True)
    d_min = q_pos.astype(F32) - last_sel
    kmax2 = _tile_max(kmax_ref, nkb - 1, KMAX_LANE_CKV)
    for hh in range(DSA_HEADS):
        qh = qabsT_ref[hh * DSA_LATENT:(hh + 1) * DSA_LATENT, :]
        qf = qh.astype(F32)
        shift = _shift_bound(jnp.sum(qf * qf, axis=0, keepdims=True), kmax2, -slopes[hh] * d_min)
        qt_ref[0:DSA_LATENT, hh * tq:(hh + 1) * tq] = qh
        qt_ref[DSA_LATENT:VT_ROWS, hh * tq:(hh + 1) * tq] = _query_feature_rows(
            q_pos, BF16_ROWS, slopes[hh], shift).astype(BF16)
    qt_ref[VT_ROWS:2 * DSA_LATENT, :] = jnp.zeros((2 * DSA_LATENT - VT_ROWS, DSA_HEADS * tq), BF16)
    acc_ref[...] = jnp.zeros_like(acc_ref)

    def fast_pv(kb):
        kx = ckv_ref[pl.ds(pl.multiple_of(kb * tk, tk), tk), :]
        s = _dot(kx, qt_ref[...])
        keep = sc_ref[kb] >= tau
        ps = [jnp.exp(jnp.where(keep, s[:, hh * tq:(hh + 1) * tq], NEG_BIG)).astype(BF16)
              for hh in range(DSA_HEADS)]
        return _dot(ckvT_ref[kb], jnp.concatenate(ps, axis=1))

    def fast_pair(j, carry):
        acc_ref[...] += fast_pv(2 * j) + fast_pv(2 * j + 1)
        return carry

    lax.fori_loop(0, nkb // 2, fast_pair, 0)

    @pl.when(nkb % 2 == 1)
    def _():
        acc_ref[...] += fast_pv(nkb - 1)

    def attn_body(kb, carry):
        kx = ckv_ref[pl.ds(pl.multiple_of(kb * tk, tk), tk), :]
        vT = ckvT_ref[kb]
        keep = sc_ref[kb] >= tau
        qt = qt_ref[...]
        m_all = m_ref[...]
        m_out, alphas, pvs = [], [], []
        for ch in range(ATTN_CHAINS):
            hpc = DSA_HEADS // ATTN_CHAINS
            l0 = ch * hpc * tq
            s = _dot(kx, qt[:, l0:l0 + hpc * tq])
            ps = []
            for j in range(hpc):
                c0 = l0 + j * tq
                sh = jnp.where(keep, s[:, j * tq:(j + 1) * tq], NEG_BIG)
                m_old = m_all[:, c0:c0 + tq]
                m_new = jnp.maximum(m_old, jnp.max(sh, axis=0, keepdims=True))
                ps.append(jnp.exp(sh - m_new).astype(BF16))
                alphas.append(jnp.exp(m_old - m_new))
                m_out.append(m_new)
            pvs.append(_dot(vT, jnp.concatenate(ps, axis=1)))
        m_ref[...] = jnp.concatenate(m_out, axis=1)
        acc_ref[...] = jnp.concatenate(alphas, axis=1) * acc_ref[...] + jnp.concatenate(pvs, axis=1)
        return carry

    denom_ok = jnp.min(acc_ref[ONES_ROW:ONES_ROW + 1, :]) > L_MIN

    @pl.when(jnp.logical_not(denom_ok))
    def _():
        acc_ref[...] = jnp.zeros_like(acc_ref)
        m_ref[...] = jnp.full_like(m_ref, NEG_BIG)
        lax.fori_loop(0, nkb, attn_body, 0)

    acc = acc_ref[...]
    lat = acc[0:DSA_LATENT, :] / acc[ONES_ROW:ONES_ROW + 1, :]
    for p in range(DSA_HEADS // 2):
        pair = lat[:, 2 * p * tq:(2 * p + 2) * tq]
        pair = jnp.concatenate([pair[:, 0:tq], pair[:, tq:2 * tq]], axis=0).astype(BF16)
        o_ref[:, p * LANES:(p + 1) * LANES] = _dot(wuvT_ref[p], pair).T.astype(BF16)


def _dsa_call(qabsT, iqT, iwT, ik4, ckv, ckvT, kmax, lstrict, wuvT_pair, tq, tk, topk):
    B, _, S = qabsT.shape
    nkb = S // tk
    kern = functools.partial(_dsa_kernel, tq=tq, tk=tk, topk=topk)
    blkT = lambda r: pl.BlockSpec((None, r, tq), lambda b, i: (b, 0, i))
    per_b = lambda w: pl.BlockSpec((None, S, w), lambda b, i: (b, 0, 0))
    full = lambda a: pl.BlockSpec(a.shape, lambda b, i: (0,) * a.ndim)
    return pl.pallas_call(
        kern,
        out_shape=jax.ShapeDtypeStruct((B, S, W_DQ), BF16),
        grid=(B, S // tq),
        in_specs=[blkT(W_QABS), blkT(LANES), blkT(8), per_b(LANES), per_b(W_CKV),
                  pl.BlockSpec((None, nkb, VT_ROWS, tk), lambda b, i: (b, 0, 0, 0)),
                  pl.BlockSpec((None, nkb, 8, LANES), lambda b, i: (b, 0, 0, 0)),
                  full(lstrict), full(wuvT_pair)],
        out_specs=pl.BlockSpec((None, tq, W_DQ), lambda b, i: (b, i, 0)),
        scratch_shapes=[pltpu.VMEM((nkb, tk, tq), F32),
                        pltpu.VMEM((LANES, IDX_HEADS * tq), BF16),
                        pltpu.VMEM((2 * DSA_LATENT, DSA_HEADS * tq), BF16),
                        pltpu.VMEM((VT_ROWS, DSA_HEADS * tq), F32),
                        pltpu.VMEM((1, DSA_HEADS * tq), F32)],
        compiler_params=pltpu.CompilerParams(dimension_semantics=("arbitrary", "arbitrary"),
                                             vmem_limit_bytes=VMEM_LIMIT),
        name="dsa",
    )(qabsT, iqT, iwT, ik4, ckv, ckvT, kmax, lstrict, wuvT_pair)


def _diff_kernel(fqT_ref, fk_ref, fvT_ref, kmax_ref, lam_ref, gsub_ref, o_ref, qt_ref, acc_ref, m_ref,
                 *, tq, tk, lam0):
    hd = pl.program_id(1)
    qi = pl.program_id(2)
    q_pos = qi * tq + lax.broadcasted_iota(jnp.int32, (1, tq), 1)
    n_full = (qi * tq) // tk
    n_diag = (tq + tk - 1) // tk
    slopes = _alibi_slopes(DIFF_HEADS)
    slope = jnp.float32(slopes[DIFF_HEADS - 1])
    for i in range(DIFF_HEADS - 1):
        slope = jnp.where(hd == i, slopes[i], slope)

    for m in range(2):
        r0 = m * LANES
        qh = fqT_ref[r0:r0 + DIFF_QK_DIM, :]
        qf = qh.astype(F32)
        shift = _shift_bound(jnp.sum(qf * qf, axis=0, keepdims=True),
                             _tile_max(kmax_ref, n_full + n_diag - 1, 2 * hd + m), 0.0)
        qt_ref[r0:r0 + DIFF_QK_DIM, :] = qh
        qt_ref[r0 + DIFF_QK_DIM:r0 + DIFF_QK_DIM + BF16_ROWS, :] = _query_feature_rows(
            q_pos, BF16_ROWS, slope, shift).astype(BF16)
        qt_ref[r0 + DIFF_QK_DIM + BF16_ROWS:r0 + LANES, :] = jnp.zeros(
            (LANES - DIFF_QK_DIM - BF16_ROWS, tq), BF16)
    acc_ref[...] = jnp.zeros_like(acc_ref)

    def fast_pv(kb, masked):
        kx = fk_ref[pl.ds(pl.multiple_of(kb * tk, tk), tk), :]
        vT = fvT_ref[kb]
        out = []
        for m in range(2):
            s = _dot(kx[:, m * LANES:(m + 1) * LANES], qt_ref[m * LANES:(m + 1) * LANES, :])
            if masked:
                causal = lax.broadcasted_iota(jnp.int32, (tk, tq), 0) + kb * tk <= q_pos
                s = jnp.where(causal, s, NEG_BIG)
            out.append(_dot(vT, jnp.exp(s).astype(BF16)))
        return out

    def accumulate(*pvs):
        for m in range(2):
            acc_ref[m] += functools.reduce(lambda a, b: a + b, [pv[m] for pv in pvs])

    def fast_pair(j, carry):
        accumulate(fast_pv(2 * j, False), fast_pv(2 * j + 1, False))
        return carry

    lax.fori_loop(0, n_full // 2, fast_pair, 0)
    assert n_diag == 1

    @pl.when(n_full % 2 == 1)
    def _():
        accumulate(fast_pv(n_full - 1, False), fast_pv(n_full, True))

    @pl.when(n_full % 2 == 0)
    def _():
        accumulate(fast_pv(n_full, True))

    def block(kb, masked):
        kx = fk_ref[pl.ds(pl.multiple_of(kb * tk, tk), tk), :]
        vT = fvT_ref[kb]
        qT = qt_ref[...]
        m_all = m_ref[...]
        new = []
        for m in range(2):
            s = _dot(kx[:, m * LANES:(m + 1) * LANES], qT[m * LANES:(m + 1) * LANES, :])
            if masked:
                causal = lax.broadcasted_iota(jnp.int32, (tk, tq), 0) + kb * tk <= q_pos
                s = jnp.where(causal, s, NEG_BIG)
            m_old = m_all[m]
            m_new = jnp.maximum(m_old, jnp.max(s, axis=0, keepdims=True))
            p = jnp.exp(s - m_new).astype(BF16)
            new.append((m_new, jnp.exp(m_old - m_new), _dot(vT, p)))
        for m in range(2):
            m_ref[m] = new[m][0]
            acc_ref[m] = new[m][1] * acc_ref[m] + new[m][2]

    def body(kb, carry):
        block(kb, False)
        return carry

    denom_ok = jnp.min(acc_ref[:, ONES_ROW:ONES_ROW + 1, :]) > L_MIN

    @pl.when(jnp.logical_not(denom_ok))
    def _():
        acc_ref[...] = jnp.zeros_like(acc_ref)
        m_ref[...] = jnp.full_like(m_ref, NEG_BIG)
        lax.fori_loop(0, n_full, body, 0)
        for d in range(n_diag):
            block(n_full + d, True)

    lv = lam_ref[...]
    lam = (jnp.exp(jnp.sum(lv[0:1] * lv[1:2], axis=-1, keepdims=True))
           - jnp.exp(jnp.sum(lv[2:3] * lv[3:4], axis=-1, keepdims=True)) + lam0)
    a1 = acc_ref[0]
    a2 = acc_ref[1]
    o = (a1[0:DIFF_V_DIM, :] / a1[ONES_ROW:ONES_ROW + 1, :]
         - lam * (a2[0:DIFF_V_DIM, :] / a2[ONES_ROW:ONES_ROW + 1, :]))
    o = o * lax.rsqrt(jnp.mean(o * o, axis=0, keepdims=True) + EPS) * gsub_ref[...] * (1.0 - lam0)
    o_ref[...] = o.T.astype(BF16)


def _diff_call(fqT, fk, fvT, kmax, lam_vecs, g_sub_col, tq, tk, lam0):
    B, S, _ = fk.shape
    nkb = S // tk
    kern = functools.partial(_diff_kernel, tq=tq, tk=tk, lam0=lam0)
    full = lambda a: pl.BlockSpec(a.shape, lambda b, h, i: (0,) * a.ndim)
    return pl.pallas_call(
        kern,
        out_shape=jax.ShapeDtypeStruct((B, S, DIFF_HEADS * DIFF_V_DIM), BF16),
        grid=(B, DIFF_HEADS, S // tq),
        in_specs=[pl.BlockSpec((None, 2 * LANES, tq), lambda b, h, i: (b, h, i)),
                  pl.BlockSpec((None, S, 2 * LANES), lambda b, h, i: (b, 0, h)),
                  pl.BlockSpec((None, None, nkb, VT_ROWS, tk), lambda b, h, i: (b, h, 0, 0, 0)),
                  pl.BlockSpec((None, nkb, 8, LANES), lambda b, h, i: (b, 0, 0, 0)),
                  full(lam_vecs), full(g_sub_col)],
        out_specs=pl.BlockSpec((None, tq, DIFF_V_DIM), lambda b, h, i: (b, i, h)),
        scratch_shapes=[pltpu.VMEM((2 * LANES, tq), BF16),
                        pltpu.VMEM((2, VT_ROWS, tq), F32),
                        pltpu.VMEM((2, 1, tq), F32)],
        compiler_params=pltpu.CompilerParams(
            dimension_semantics=("arbitrary", "arbitrary", "arbitrary"),
            vmem_limit_bytes=VMEM_LIMIT),
        name="diff",
    )(fqT, fk, fvT, kmax, lam_vecs, g_sub_col)


def _route_kernel(x_ref, od_ref, of_ref, mod_ref, wout_ref, g_ref, wr_hi_ref, wr_lo_ref, br_ref,
                  x1_ref, h_ref, gates_ref, *, tm):
    attn = (_dot(od_ref[...], wout_ref[0:W_DQ, :]) + _dot(of_ref[...], wout_ref[W_DQ:, :]))
    x1 = x_ref[...] + mod_ref[2:3, :] * attn
    x1_ref[...] = x1
    h = _rms(x1, g_ref[...]) * (1.0 + mod_ref[4:5, :]) + mod_ref[3:4, :]
    h_hi = h.astype(BF16)
    h_ref[...] = h_hi
    h_lo = (h - h_hi.astype(F32)).astype(BF16)
    w_hi = wr_hi_ref[...]
    logits = (_dot(h_hi, w_hi) + _dot(h_lo, w_hi) + _dot(h_hi, wr_lo_ref[...])) + br_ref[...]

    lane = lax.broadcasted_iota(jnp.int32, (tm, LANES), 1)
    big = jnp.int32(4 * LANES)
    neg = -jnp.inf
    is_g = (lane >= N_EXPERTS) & (lane < N_EXPERTS + N_GROUPS)
    gl = jnp.where(is_g, logits, neg)
    gmax = jnp.max(gl, axis=-1, keepdims=True)
    g_lane = jnp.min(jnp.where(gl == gmax, lane, big), axis=-1, keepdims=True)
    g_sel = g_lane - N_EXPERTS
    p_g = 1.0 / jnp.sum(jnp.where(is_g, jnp.exp(gl - gmax), 0.0), axis=-1, keepdims=True)
    in_grp = (lane < N_EXPERTS) & ((lane // EXPERTS_PER_GROUP) == g_sel)
    el = jnp.where(in_grp, logits, neg)
    v1 = jnp.max(el, axis=-1, keepdims=True)
    i1 = jnp.min(jnp.where(el == v1, lane, big), axis=-1, keepdims=True)
    el2 = jnp.where(lane == i1, neg, el)
    v2 = jnp.max(el2, axis=-1, keepdims=True)
    i2 = jnp.min(jnp.where(el2 == v2, lane, big), axis=-1, keepdims=True)
    e = jnp.exp(v2 - v1)
    w1 = 1.0 / (1.0 + e)
    w2 = e * w1
    gates_ref[...] = jnp.where(lane == i1, w1 * p_g, jnp.where(lane == i2, w2 * p_g, 0.0))


def _route_call(x, o_dsa, o_diff, mod3, w_out, g_moe, wr_hi, wr_lo, b_r, tm):
    B, S, D = x.shape
    kern = functools.partial(_route_kernel, tm=tm)
    tok = lambda w: pl.BlockSpec((None, tm, w), lambda b, i: (b, i, 0))
    full = lambda a: pl.BlockSpec(a.shape, lambda b, i: (0,) * a.ndim)
    return pl.pallas_call(
        kern,
        out_shape=[jax.ShapeDtypeStruct((B, S, D), F32), jax.ShapeDtypeStruct((B, S, D), BF16),
                   jax.ShapeDtypeStruct((B, S, LANES), F32)],
        grid=(B, S // tm),
        in_specs=[tok(D), tok(W_DQ), tok(DIFF_HEADS * DIFF_V_DIM),
                  pl.BlockSpec((None, 6, D), lambda b, i: (b, 0, 0)),
                  full(w_out), full(g_moe), full(wr_hi), full(wr_lo), full(b_r)],
        out_specs=[tok(D), tok(D), tok(LANES)],
        compiler_params=pltpu.CompilerParams(dimension_semantics=("arbitrary", "arbitrary"),
                                             vmem_limit_bytes=VMEM_LIMIT),
        name="route",
    )(x, o_dsa, o_diff, mod3, w_out, g_moe, wr_hi, wr_lo, b_r)


def _moe_kernel(h_ref, gates_ref, x1_ref, mod_ref, wg_ref, wu_ref, wd_ref, gf_ref, o_ref, acc_ref,
                *, tm, ec):
    j = pl.program_id(2)
    nc = pl.num_programs(2)

    @pl.when(j == 0)
    def _():
        acc_ref[...] = jnp.zeros_like(acc_ref)

    h = h_ref[...]
    hg = _dot(h, wg_ref[...])
    hu = _dot(h, wu_ref[...])
    hid = hg * jax.nn.sigmoid(hg) * hu
    gates = gates_ref[...]
    lane = lax.broadcasted_iota(jnp.int32, (tm, LANES), 1)
    parts = []
    for e in range(ec):
        gcol = jnp.sum(jnp.where(lane == j * ec + e, gates, 0.0), axis=-1, keepdims=True)
        parts.append((hid[:, e * D_EXPERT:(e + 1) * D_EXPERT] * gcol).astype(BF16))
    hs = jnp.concatenate(parts, axis=1)
    acc_ref[...] += _dot(hs, wd_ref[...])

    @pl.when(j == nc - 1)
    def _():
        x2 = x1_ref[...] + mod_ref[5:6, :] * acc_ref[...]
        o_ref[...] = _rms(x2, gf_ref[...])


def _moe_call(h, gates, x1, mod3, wg, wu, wd, g_final, tm, ec):
    B, S, D = x1.shape
    kern = functools.partial(_moe_kernel, tm=tm, ec=ec)
    tok = lambda w: pl.BlockSpec((None, tm, w), lambda b, i, j: (b, i, 0))
    return pl.pallas_call(
        kern,
        out_shape=jax.ShapeDtypeStruct((B, S, D), F32),
        grid=(B, S // tm, N_EXPERTS // ec),
        in_specs=[tok(D), tok(LANES), tok(D),
                  pl.BlockSpec((None, 6, D), lambda b, i, j: (b, 0, 0)),
                  pl.BlockSpec((D, ec * D_EXPERT), lambda b, i, j: (0, j)),
                  pl.BlockSpec((D, ec * D_EXPERT), lambda b, i, j: (0, j)),
                  pl.BlockSpec((ec * D_EXPERT, D), lambda b, i, j: (j, 0)),
                  pl.BlockSpec((1, D), lambda b, i, j: (0, 0))],
        out_specs=tok(D),
        scratch_shapes=[pltpu.VMEM((tm, D), F32)],
        compiler_params=pltpu.CompilerParams(
            dimension_semantics=("arbitrary", "arbitrary", "arbitrary"),
            vmem_limit_bytes=VMEM_LIMIT),
        name="moe",
    )(h, gates, x1, mod3, wg, wu, wd, g_final)


def _tile(n, pref):
    t = min(n, pref)
    assert n % t == 0, (n, t)
    return t


def kernel(x, c, w_ada, b_ada, g_attn, w_in, g_kv, w_uk, w_uv, lam_q1, lam_k1, lam_q2, lam_k2,
           g_sub, w_out, g_moe, w_group, b_group, w_router, b_router, w_gate, w_up, w_down, g_final):
    B, S, D = x.shape
    assert D == D_MODEL and w_ada.shape[0] == 1
    assert S % KEY_TILE == 0 and S <= POS_SPLIT * 256
    topk = min(TOPK_MAX, S // 4)
    l = 0
    lam0 = 0.8 - 0.6 * math.exp(-0.3 * l)

    mod3 = _mod_call(c, w_ada[l], b_ada[l].reshape(1, -1)).reshape(B, 6, D)

    wn, wt, w_pair = _proj_weights(w_in[l], w_uk[l])
    qabsT, iqT, iwT, ik4, ckv, ckvT, fqT, fk, fvT, kmax = _proj_call(
        x, mod3, g_attn[l].reshape(1, D), wn, wt, w_pair, g_kv[l].reshape(1, -1), KEY_TILE)

    lstrict = jnp.asarray(np.tril(np.ones((KEY_TILE, KEY_TILE), np.float32), -1), BF16)
    uvT = jnp.swapaxes(w_uv[l], 1, 2)
    z = jnp.zeros_like(uvT[0])
    wuvT_pair = jnp.stack([
        jnp.concatenate([jnp.concatenate([uvT[2 * p], z], axis=1),
                         jnp.concatenate([z, uvT[2 * p + 1]], axis=1)], axis=0)
        for p in range(DSA_HEADS // 2)]).astype(BF16)
    o_dsa = _dsa_call(qabsT, iqT, iwT, ik4, ckv, ckvT, kmax, lstrict, wuvT_pair, LANES, KEY_TILE, topk)

    lam_vecs = jnp.concatenate([lam_q1[l][None], lam_k1[l][None], lam_q2[l][None], lam_k2[l][None]],
                               axis=0).astype(F32)
    o_diff = _diff_call(fqT, fk, fvT, kmax, lam_vecs, g_sub[l].reshape(-1, 1), KEY_TILE, KEY_TILE, lam0)

    wr = jnp.pad(jnp.concatenate([w_router[l], w_group[l]], axis=1),
                 ((0, 0), (0, LANES - N_EXPERTS - N_GROUPS)))
    wr_hi = wr.astype(BF16)
    wr_lo = (wr - wr_hi.astype(F32)).astype(BF16)
    b_r = jnp.pad(jnp.concatenate([b_router[l], b_group[l]]), (0, LANES - N_EXPERTS - N_GROUPS))
    x1, h2, gates = _route_call(x, o_dsa, o_diff, mod3, w_out[l].astype(BF16),
                                g_moe[l].reshape(1, D), wr_hi, wr_lo, b_r.reshape(1, LANES),
                                _tile(S, 512))

    wg = jnp.swapaxes(w_gate[l], 0, 1).reshape(D, N_EXPERTS * D_EXPERT).astype(BF16)
    wu = jnp.swapaxes(w_up[l], 0, 1).reshape(D, N_EXPERTS * D_EXPERT).astype(BF16)
    wd = w_down[l].reshape(N_EXPERTS * D_EXPERT, D).astype(BF16)
    return _moe_call(h2, gates, x1, mod3, wg, wu, wd, g_final.reshape(1, D), _tile(S, 512), 4)
```

```python
import functools
import math

import jax
import jax.numpy as jnp
import numpy as np
from jax import lax
from jax.experimental import pallas as pl
from jax.experimental.pallas import tpu as pltpu

F32 = jnp.float32
BF16 = jnp.bfloat16

D_MODEL = 1024
DSA_HEADS = 8
DSA_HEAD_DIM = 64
DSA_LATENT = 128
IDX_HEADS = 4
IDX_DIM = 32
TOPK_MAX = 256
DIFF_HEADS = 4
DIFF_QK_DIM = 64
DIFF_V_DIM = 128
N_GROUPS = 4
EXPERTS_PER_GROUP = 8
N_EXPERTS = 32
D_EXPERT = 256
EPS = 1e-6

LANES = 128
BF16_ROWS = 16
POS_SPLIT = 64
NEG_BIG = -1e30
VMEM_LIMIT = 56 * 1024 * 1024
KEY_TILE = 512
VALUE_BISECT_STEPS = 24
UNCHECKED_BISECT_STEPS = 16
COUNT_CHAINS = 4
ATTN_CHAINS = 1

W_DQ = DSA_HEADS * DSA_HEAD_DIM
W_QABS = DSA_HEADS * DSA_LATENT
W_CKV = 2 * LANES
VT_ROWS = DSA_LATENT + BF16_ROWS
ONES_ROW = DSA_LATENT + 2
N_FEATS = 5
KMAX_LANE_CKV = 8
L_MIN = 1e-30
W_FQ = DIFF_HEADS * 2 * LANES
W_DIFF_QK = DIFF_HEADS * 2 * DIFF_QK_DIM

T_DQ = 0
T_DLAT = T_DQ + W_DQ
T_IQ = T_DLAT + DSA_LATENT
T_IW = T_IQ + IDX_HEADS * IDX_DIM
T_FQ = T_IW + BF16_ROWS
T_FV = T_FQ + W_DIFF_QK
T_ROWS = T_FV + DIFF_HEADS * DIFF_V_DIM
N_IK = 0
N_DLAT = N_IK + LANES
N_FK = N_DLAT + DSA_LATENT
N_COLS = N_FK + W_FQ


def _alibi_slopes(n):
    return [2.0 ** (-8.0 * (i + 1) / n) for i in range(n)]


def _rms(x, g):
    return x * lax.rsqrt(jnp.mean(x * x, axis=-1, keepdims=True) + EPS) * g


def _dot(a, b):
    return jnp.dot(a, b, preferred_element_type=F32)


def _mod_kernel(c_ref, w_ref, b_ref, o_ref):
    c = c_ref[...]
    act = c * jax.nn.sigmoid(c)
    o_ref[...] = jnp.dot(act, w_ref[...], preferred_element_type=F32,
                         precision=lax.Precision.HIGHEST) + b_ref[...]


def _mod_call(c, w_ada, b_ada):
    B, D = c.shape
    n = w_ada.shape[1] // D
    return pl.pallas_call(
        _mod_kernel,
        out_shape=jax.ShapeDtypeStruct((B, n * D), F32),
        grid=(n,),
        in_specs=[pl.BlockSpec((B, D), lambda j: (0, 0)),
                  pl.BlockSpec((D, D), lambda j: (0, j)),
                  pl.BlockSpec((1, D), lambda j: (0, j))],
        out_specs=pl.BlockSpec((B, D), lambda j: (0, j)),
        compiler_params=pltpu.CompilerParams(dimension_semantics=("arbitrary",),
                                             vmem_limit_bytes=VMEM_LIMIT),
        name="mod",
    )(c, w_ada, b_ada)


def _key_feature_rows(pos_row, n_rows):
    r = lax.broadcasted_iota(jnp.int32, (n_rows, pos_row.shape[1]), 0)
    pa = (pos_row // POS_SPLIT).astype(F32)
    pb = (pos_row % POS_SPLIT).astype(F32)
    return jnp.where(r == 0, pa, jnp.where(r == 1, pb, jnp.where(r < N_FEATS, 1.0, 0.0)))


def _query_feature_rows(pos_row, n_rows, slope, shift):
    r = lax.broadcasted_iota(jnp.int32, (n_rows, pos_row.shape[1]), 0)
    pa = (pos_row // POS_SPLIT).astype(F32)
    pb = (pos_row % POS_SPLIT).astype(F32)
    return jnp.where(r == 0, POS_SPLIT * slope,
                     jnp.where(r == 1, slope,
                               jnp.where(r == 2, -POS_SPLIT * slope * pa,
                                         jnp.where(r == 3, -slope * pb,
                                                   jnp.where(r == 4, -shift, 0.0)))))


def _shift_bound(qn2, kmax2, extra):
    b = jnp.sqrt(qn2 * kmax2) + extra
    return b + jnp.abs(b) * (2.0 ** -6) + 2.0 ** -20


def _tile_max(kmax_ref, last_tile, lane_idx):
    x = kmax_ref[...]
    t = lax.broadcasted_iota(jnp.int32, x.shape, 0)
    ln = lax.broadcasted_iota(jnp.int32, x.shape, 2)
    x = jnp.where((t <= last_tile) & (ln == lane_idx), x, 0.0)
    return jnp.max(jnp.max(x, axis=0), axis=1, keepdims=True)[0:1, :]


def _proj_kernel(x_ref, mod_ref, g_ref, wn_ref, wt_ref, wpair_ref, gkv_ref, gkvc_ref, fc_ref, sel_ref,
                 qabsT_ref, iqT_ref, iwT_ref, ik_ref, ckv_ref, ckvT_ref, fqT_ref, fk_ref, fvT_ref,
                 kmax_ref, *, tm):
    x = x_ref[...]
    h = _rms(x, g_ref[...]) * (1.0 + mod_ref[1:2, :]) + mod_ref[0:1, :]
    hb = h.astype(BF16)
    hT = h.T.astype(BF16)

    base = pl.program_id(1) * tm
    pos_c = base + lax.broadcasted_iota(jnp.int32, (tm, 1), 0)
    pos_r = base + lax.broadcasted_iota(jnp.int32, (1, tm), 1)
    pa = (pos_c // POS_SPLIT).astype(F32)
    pb = (pos_c % POS_SPLIT).astype(F32)

    def feats(row, width):
        return (fc_ref[row:row + 1, 0:width] + fc_ref[row + 1:row + 2, 0:width] * pa
                + fc_ref[row + 2:row + 3, 0:width] * pb)

    ik_ref[...] = _dot(hb, wn_ref[:, N_IK:N_IK + LANES]).astype(BF16)
    dlat = _dot(hb, wn_ref[:, N_DLAT:N_DLAT + DSA_LATENT])
    ckv_b = _rms(dlat, gkv_ref[...]).astype(BF16)
    ckv_ref[:, 0:LANES] = ckv_b
    ckv_ref[:, LANES:2 * LANES] = feats(0, LANES).astype(BF16)
    fk = _dot(hb, wn_ref[:, N_FK:N_FK + W_FQ])
    fk_ref[...] = (fk + feats(3, W_FQ)).astype(BF16)
    sq = jnp.concatenate([fk.astype(BF16).astype(F32), ckv_b.astype(F32)], axis=1)
    sq_up = (sq * sq * (1.0 + 2.0 ** -7)).astype(BF16)
    kmax_ref[...] = jnp.broadcast_to(jnp.max(_dot(sq_up, sel_ref[...]), axis=0, keepdims=True),
                                     (8, LANES))

    dqT = _dot(wt_ref[T_DQ:T_DQ + W_DQ, :], hT).astype(BF16)
    for p in range(DSA_HEADS // 2):
        qa = _dot(wpair_ref[p], dqT[p * LANES:(p + 1) * LANES, :])
        qabsT_ref[p * 2 * LANES:(p + 1) * 2 * LANES, :] = qa.astype(BF16)
    dlatT = _dot(wt_ref[T_DLAT:T_DLAT + DSA_LATENT, :], hT)
    inv = lax.rsqrt(jnp.mean(dlatT * dlatT, axis=0, keepdims=True) + EPS)
    kfeat = _key_feature_rows(pos_r, BF16_ROWS).astype(BF16)
    ckvT_ref[0:DSA_LATENT, :] = (dlatT * inv * gkvc_ref[...]).astype(BF16)
    ckvT_ref[DSA_LATENT:VT_ROWS, :] = kfeat
    iqT_ref[...] = _dot(wt_ref[T_IQ:T_IQ + IDX_HEADS * IDX_DIM, :], hT).astype(BF16)
    iwT = _dot(wt_ref[T_IW:T_IW + BF16_ROWS, :], hT)
    iwT_ref[...] = iwT[0:8, :] * (IDX_HEADS ** -0.5 * IDX_DIM ** -0.5)
    fqT = _dot(wt_ref[T_FQ:T_FQ + W_DIFF_QK, :], hT)
    zpad = jnp.zeros((LANES - DIFF_QK_DIM, tm), BF16)
    for s in range(DIFF_HEADS * 2):
        r0 = s * LANES
        fqT_ref[r0:r0 + DIFF_QK_DIM, :] = fqT[s * DIFF_QK_DIM:(s + 1) * DIFF_QK_DIM, :].astype(BF16)
        fqT_ref[r0 + DIFF_QK_DIM:r0 + LANES, :] = zpad
    fvT = _dot(wt_ref[T_FV:T_FV + DIFF_HEADS * DIFF_V_DIM, :], hT)
    for hh in range(DIFF_HEADS):
        fvT_ref[hh, 0:DIFF_V_DIM, :] = fvT[hh * DIFF_V_DIM:(hh + 1) * DIFF_V_DIM, :].astype(BF16)
        fvT_ref[hh, DIFF_V_DIM:VT_ROWS, :] = kfeat


def _feature_consts():
    fc = np.zeros((8, W_FQ), np.float32)
    fc[1, 0] = 1.0
    fc[2, 1] = 1.0
    fc[0, 2:N_FEATS] = 1.0
    for s in range(DIFF_HEADS * 2):
        base = s * LANES + DIFF_QK_DIM
        fc[4, base + 0] = 1.0
        fc[5, base + 1] = 1.0
        fc[3, base + 2:base + N_FEATS] = 1.0
    return jnp.asarray(fc)


def _norm_selector():
    sel = np.zeros((W_FQ + DSA_LATENT, LANES), np.float32)
    for s in range(DIFF_HEADS * 2):
        sel[s * LANES:s * LANES + DIFF_QK_DIM, s] = 1.0
    sel[W_FQ:, KMAX_LANE_CKV] = 1.0
    return jnp.asarray(sel, BF16)


def _proj_weights(w_in, w_uk):
    D = w_in.shape[0]
    pts = np.cumsum([W_DQ, DSA_LATENT, IDX_HEADS * IDX_DIM, IDX_DIM, IDX_HEADS,
                     W_DIFF_QK, W_DIFF_QK])
    dq, dlat, iq, ik, iw, fq, fk, fv = jnp.split(w_in, list(pts), axis=1)
    ik4 = jnp.tile(ik, (1, IDX_HEADS))
    fke = jnp.pad(fk.reshape(D, DIFF_HEADS * 2, DIFF_QK_DIM),
                  ((0, 0), (0, 0), (0, LANES - DIFF_QK_DIM))).reshape(D, W_FQ)
    wn = jnp.concatenate([ik4, dlat, fke], axis=1).astype(BF16)
    iwp = jnp.pad(iw, ((0, 0), (0, BF16_ROWS - IDX_HEADS)))
    wt = jnp.concatenate([dq, dlat, iq, iwp, fq * (DIFF_QK_DIM ** -0.5), fv], axis=1).T.astype(BF16)
    uk = w_uk * (DSA_HEAD_DIM ** -0.5)
    z = jnp.zeros_like(uk[0])
    pairs = [jnp.concatenate([jnp.concatenate([uk[2 * p], z], axis=1),
                              jnp.concatenate([z, uk[2 * p + 1]], axis=1)], axis=0)
             for p in range(DSA_HEADS // 2)]
    return wn, wt, jnp.stack(pairs).astype(BF16)


def _proj_call(x, mod3, g_attn, wn, wt, w_pair, g_kv, tm):
    B, S, D = x.shape
    nt = S // tm
    fc = _feature_consts()
    sel = _norm_selector()
    kern = functools.partial(_proj_kernel, tm=tm)
    tok = lambda w: pl.BlockSpec((None, tm, w), lambda b, i: (b, i, 0))
    tokT = lambda r: pl.BlockSpec((None, r, tm), lambda b, i: (b, 0, i))
    full = lambda a: pl.BlockSpec(a.shape, lambda b, i: (0,) * a.ndim)
    g_kv_col = g_kv.reshape(-1, 1)
    out_shape = [jax.ShapeDtypeStruct((B, W_QABS, S), BF16),
                 jax.ShapeDtypeStruct((B, LANES, S), BF16),
                 jax.ShapeDtypeStruct((B, 8, S), F32),
                 jax.ShapeDtypeStruct((B, S, LANES), BF16),
                 jax.ShapeDtypeStruct((B, S, W_CKV), BF16),
                 jax.ShapeDtypeStruct((B, nt, VT_ROWS, tm), BF16),
                 jax.ShapeDtypeStruct((B, W_FQ, S), BF16),
                 jax.ShapeDtypeStruct((B, S, W_FQ), BF16),
                 jax.ShapeDtypeStruct((B, DIFF_HEADS, nt, VT_ROWS, tm), BF16),
                 jax.ShapeDtypeStruct((B, nt, 8, LANES), F32)]
    out_specs = [tokT(W_QABS), tokT(LANES), tokT(8), tok(LANES), tok(W_CKV),
                 pl.BlockSpec((None, None, VT_ROWS, tm), lambda b, i: (b, i, 0, 0)),
                 tokT(W_FQ), tok(W_FQ),
                 pl.BlockSpec((None, DIFF_HEADS, None, VT_ROWS, tm), lambda b, i: (b, 0, i, 0, 0)),
                 pl.BlockSpec((None, None, 8, LANES), lambda b, i: (b, i, 0, 0))]
    return pl.pallas_call(
        kern,
        out_shape=out_shape,
        grid=(B, nt),
        in_specs=[tok(D),
                  pl.BlockSpec((None, 6, D), lambda b, i: (b, 0, 0)),
                  full(g_attn), full(wn), full(wt), full(w_pair), full(g_kv), full(g_kv_col),
                  full(fc), full(sel)],
        out_specs=out_specs,
        compiler_params=pltpu.CompilerParams(dimension_semantics=("arbitrary", "arbitrary"),
                                             vmem_limit_bytes=VMEM_LIMIT),
        name="proj",
    )(x, mod3, g_attn, wn, wt, w_pair, g_kv, g_kv_col, fc, sel)


def _float_code(x):
    b = lax.bitcast_convert_type(x, jnp.int32)
    return b ^ (lax.shift_right_arithmetic(b, 31) & jnp.int32(0x7FFFFFFF))


def _float_decode(c):
    b = c ^ (lax.shift_right_arithmetic(c, 31) & jnp.int32(0x7FFFFFFF))
    return lax.bitcast_convert_type(b, F32)


def _dsa_kernel(qabsT_ref, iqT_ref, iwT_ref, ik_ref, ckv_ref, ckvT_ref, kmax_ref, lstrict_ref, wuvT_ref,
                o_ref, sc_ref, qst_ref, qt_ref, acc_ref, m_ref, tmp_ref, *, tq, tk, topk):
    qi = pl.program_id(1)
    nkb = (qi * tq) // tk + 1
    kf = float(topk)
    slopes = _alibi_slopes(DSA_HEADS)
    q_pos = qi * tq + lax.broadcasted_iota(jnp.int32, (1, tq), 1)

    iqT = iqT_ref[...]
    rowi = lax.broadcasted_iota(jnp.int32, (LANES, tq), 0)
    for hh in range(IDX_HEADS):
        qst_ref[:, hh * tq:(hh + 1) * tq] = jnp.where((rowi // IDX_DIM) == hh, iqT, jnp.zeros_like(iqT))
    iw = iwT_ref[...]
    wrow = [iw[hh:hh + 1, :] for hh in range(IDX_HEADS)]

    def scores(kb):
        kblk = ik_ref[pl.ds(pl.multiple_of(kb * tk, tk), tk), :]
        a = _dot(kblk, qst_ref[...])
        sc = jnp.maximum(a[:, 0:tq], 0.0) * wrow[0]
        for hh in range(1, IDX_HEADS):
            sc = sc + jnp.maximum(a[:, hh * tq:(hh + 1) * tq], 0.0) * wrow[hh]
        return sc

    def score_body(kb, carry):
        mn, mx = carry
        sc = scores(kb)
        sc_ref[kb] = sc
        return (jnp.minimum(mn, jnp.min(sc, axis=0, keepdims=True)),
                jnp.maximum(mx, jnp.max(sc, axis=0, keepdims=True)))

    last = nkb - 1
    mn, mx = lax.fori_loop(0, last // 2, lambda j, c: score_body(2 * j + 1, score_body(2 * j, c)),
                           (jnp.full((1, tq), jnp.inf, F32), jnp.full((1, tq), -jnp.inf, F32)))
    mn, mx = lax.cond(last % 2 == 1, lambda c: score_body(last - 1, c), lambda c: c, (mn, mx))
    sc = scores(last)
    causal = lax.broadcasted_iota(jnp.int32, (tk, tq), 0) + last * tk <= q_pos
    sc_ref[last] = jnp.where(causal, sc, jnp.nan)
    mn = jnp.minimum(mn, jnp.min(jnp.where(causal, sc, jnp.inf), axis=0, keepdims=True))
    mx = jnp.maximum(mx, jnp.max(jnp.where(causal, sc, -jnp.inf), axis=0, keepdims=True))

    def block_count(kb, pred):
        parts = [None] * COUNT_CHAINS
        for r in range(tk // 8):
            v = jnp.where(pred(sc_ref[kb, r * 8:(r + 1) * 8, :]), 1.0, 0.0)
            c = r % COUNT_CHAINS
            parts[c] = v if parts[c] is None else parts[c] + v
        return (parts[0] + parts[1]) + (parts[2] + parts[3])

    def count(pred):
        acc = lax.fori_loop(0, nkb, lambda kb, acc: acc + block_count(kb, pred),
                            jnp.zeros((8, tq), F32))
        return jnp.sum(acc, axis=0, keepdims=True)

    n_causal = (q_pos + 1).astype(F32)
    done0 = n_causal <= kf

    def bisect(st, value_mid):
        lo, hi, c_lo, theta, done = st
        th8 = jnp.broadcast_to(theta, (8, tq))
        c = count(lambda v: v >= th8)
        ge = c >= kf
        live = done == 0.0
        lo = jnp.where(live & ge, theta, lo)
        c_lo = jnp.where(live & ge, c, c_lo)
        hi = jnp.where(live & jnp.logical_not(ge), theta, hi)
        if value_mid:
            nxt = 0.5 * lo + 0.5 * hi
        else:
            cl, ch = _float_code(lo), _float_code(hi)
            nxt = _float_decode((cl & ch) + lax.shift_right_arithmetic(cl ^ ch, 1))
        inside = (nxt > lo) & (nxt < hi)
        done = jnp.where((c_lo == kf) | jnp.logical_not(inside), 1.0, done)
        return lo, hi, c_lo, nxt, done

    def checked(value_mid, max_steps):
        def cond(c):
            return jnp.logical_and(c[1] > 0.0, c[2] < max_steps)

        def body(c):
            st = bisect(bisect(c[0], value_mid), value_mid)
            return st, jnp.sum(1.0 - st[4]), c[2] + 2
        return cond, body

    zero8 = jnp.zeros((8, tq), F32)
    c_ge0 = count(lambda v: v >= zero8)
    c_gt0 = count(lambda v: v > zero8)
    live0 = jnp.logical_not(done0)
    at0 = live0 & (c_ge0 >= kf)
    tie0 = at0 & (c_gt0 < kf)
    below0 = live0 & (c_ge0 < kf)
    lo0 = jnp.where(at0, 0.0, mn)
    hi0 = jnp.where(below0, 0.0, mx)
    st = (lo0, hi0, jnp.where(at0, c_ge0, n_causal),
          jnp.where(below0, 0.5 * lo0 + 0.5 * hi0, mx), jnp.where(done0 | tie0, 1.0, 0.0))
    st = lax.fori_loop(0, UNCHECKED_BISECT_STEPS, lambda i, s: bisect(s, True), st)
    cond, body = checked(True, VALUE_BISECT_STEPS - UNCHECKED_BISECT_STEPS)
    st, active, _ = lax.while_loop(cond, body, (st, jnp.sum(1.0 - st[4]), jnp.int32(0)))
    cond, body = checked(False, 34)
    st, _, _ = lax.while_loop(cond, body, (st, active, jnp.int32(0)))
    tau, c_ge = st[0], st[2]

    over = c_ge > kf

    @pl.when(jnp.max(c_ge) > kf)
    def _():
        tau8 = jnp.broadcast_to(tau, (8, tq))
        tmp_ref[0:1, :] = c_gt0

        @pl.when(jnp.max(jnp.where(over & jnp.logical_not(tie0), 1.0, 0.0)) > 0.0)
        def _():
            tmp_ref[0:1, :] = jnp.where(tie0, c_gt0, count(lambda v: v > tau8))

        quota = kf - tmp_ref[0:1, :]

        def tie_body(kb, seen):
            cnt = jnp.sum(block_count(kb, lambda v: v == tau8), axis=0, keepdims=True)
            inside = over & (seen < quota) & (seen + cnt > quota)
            gone = over & (seen >= quota) & (cnt > 0.0)
            flag = jnp.max(jnp.where(inside, 2.0, jnp.where(gone, 1.0, 0.0)))

            @pl.when(flag > 1.5)
            def _():
                s = sc_ref[kb]
                eq = s == tau
                rank = _dot(lstrict_ref[...], jnp.where(eq, 1.0, 0.0).astype(BF16)) + seen
                sc_ref[kb] = jnp.where(eq & over & (rank >= quota), jnp.nan, s)

            @pl.when(flag == 1.0)
            def _():
                s = sc_ref[kb]
                sc_ref[kb] = jnp.where((s == tau) & gone, jnp.nan, s)

            return seen + cnt

        lax.fori_loop(0, nkb, tie_body, jnp.zeros((1, tq), F32))

    tau8 = jnp.broadcast_to(tau, (8, tq))
    sub = lax.broadcasted_iota(jnp.int32, (8, tq), 0).astype(F32)

    def nearest_body(kb, acc):
        parts = [None] * COUNT_CHAINS
        for r in range(tk // 8):
            idx = sub + (kb * tk + r * 8).astype(F32)
            v = jnp.where(sc_ref[kb, r * 8:(r + 1) * 8, :] >= tau8, idx, -1.0)
            c = r % COUNT_CHAINS
            parts[c] = v if parts[c] is None else jnp.maximum(parts[c], v)
        return jnp.maximum(acc, jnp.maximum(jnp.maximum(parts[0], parts[1]),
                                            jnp.maximum(parts[2], parts[3])))

    last_sel = jnp.max(lax.fori_loop(0, nkb, nearest_body, jnp.full((8, tq), -1.0, F32)),
                       axis=0, keepdims=True)
    d_min = q_pos.astype(F32) - last_sel
    kmax2 = _tile_max(kmax_ref, nkb - 1, KMAX_LANE_CKV)
    for hh in range(DSA_HEADS):
        qh = qabsT_ref[hh * DSA_LATENT:(hh + 1) * DSA_LATENT, :]
        qf = qh.astype(F32)
        shift = _shift_bound(jnp.sum(qf * qf, axis=0, keepdims=True), kmax2, -slopes[hh] * d_min)
        qt_ref[0:DSA_LATENT, hh * tq:(hh + 1) * tq] = qh
        qt_ref[DSA_LATENT:VT_ROWS, hh * tq:(hh + 1) * tq] = _query_feature_rows(
            q_pos, BF16_ROWS, slopes[hh], shift).astype(BF16)
    qt_ref[VT_ROWS:2 * DSA_LATENT, :] = jnp.zeros((2 * DSA_LATENT - VT_ROWS, DSA_HEADS * tq), BF16)
    acc_ref[...] = jnp.zeros_like(acc_ref)

    def fast_pv(kb):
        kx = ckv_ref[pl.ds(pl.multiple_of(kb * tk, tk), tk), :]
        s = _dot(kx, qt_ref[...])
        keep = sc_ref[kb] >= tau
        ps = [jnp.exp(jnp.where(keep, s[:, hh * tq:(hh + 1) * tq], NEG_BIG)).astype(BF16)
              for hh in range(DSA_HEADS)]
        return _dot(ckvT_ref[kb], jnp.concatenate(ps, axis=1))

    def fast_pair(j, carry):
        acc_ref[...] += fast_pv(2 * j) + fast_pv(2 * j + 1)
        return carry

    lax.fori_loop(0, nkb // 2, fast_pair, 0)

    @pl.when(nkb % 2 == 1)
    def _():
        acc_ref[...] += fast_pv(nkb - 1)

    def attn_body(kb, carry):
        kx = ckv_ref[pl.ds(pl.multiple_of(kb * tk, tk), tk), :]
        vT = ckvT_ref[kb]
        keep = sc_ref[kb] >= tau
        qt = qt_ref[...]
        m_all = m_ref[...]
        m_out, alphas, pvs = [], [], []
        for ch in range(ATTN_CHAINS):
            hpc = DSA_HEADS // ATTN_CHAINS
            l0 = ch * hpc * tq
            s = _dot(kx, qt[:, l0:l0 + hpc * tq])
            ps = []
            for j in range(hpc):
                c0 = l0 + j * tq
                sh = jnp.where(keep, s[:, j * tq:(j + 1) * tq], NEG_BIG)
                m_old = m_all[:, c0:c0 + tq]
                m_new = jnp.maximum(m_old, jnp.max(sh, axis=0, keepdims=True))
                ps.append(jnp.exp(sh - m_new).astype(BF16))
                alphas.append(jnp.exp(m_old - m_new))
                m_out.append(m_new)
            pvs.append(_dot(vT, jnp.concatenate(ps, axis=1)))
        m_ref[...] = jnp.concatenate(m_out, axis=1)
        acc_ref[...] = jnp.concatenate(alphas, axis=1) * acc_ref[...] + jnp.concatenate(pvs, axis=1)
        return carry

    denom_ok = jnp.min(acc_ref[ONES_ROW:ONES_ROW + 1, :]) > L_MIN

    @pl.when(jnp.logical_not(denom_ok))
    def _():
        acc_ref[...] = jnp.zeros_like(acc_ref)
        m_ref[...] = jnp.full_like(m_ref, NEG_BIG)
        lax.fori_loop(0, nkb, attn_body, 0)

    acc = acc_ref[...]
    lat = acc[0:DSA_LATENT, :] / acc[ONES_ROW:ONES_ROW + 1, :]
    for p in range(DSA_HEADS // 2):
        pair = lat[:, 2 * p * tq:(2 * p + 2) * tq]
        pair = jnp.concatenate([pair[:, 0:tq], pair[:, tq:2 * tq]], axis=0).astype(BF16)
        o_ref[:, p * LANES:(p + 1) * LANES] = _dot(wuvT_ref[p], pair).T.astype(BF16)


def _dsa_call(qabsT, iqT, iwT, ik4, ckv, ckvT, kmax, lstrict, wuvT_pair, tq, tk, topk):
    B, _, S = qabsT.shape
    nkb = S // tk
    kern = functools.partial(_dsa_kernel, tq=tq, tk=tk, topk=topk)
    blkT = lambda r: pl.BlockSpec((None, r, tq), lambda b, i: (b, 0, i))
    per_b = lambda w: pl.BlockSpec((None, S, w), lambda b, i: (b, 0, 0))
    full = lambda a: pl.BlockSpec(a.shape, lambda b, i: (0,) * a.ndim)
    return pl.pallas_call(
        kern,
        out_shape=jax.ShapeDtypeStruct((B, S, W_DQ), BF16),
        grid=(B, S // tq),
        in_specs=[blkT(W_QABS), blkT(LANES), blkT(8), per_b(LANES), per_b(W_CKV),
                  pl.BlockSpec((None, nkb, VT_ROWS, tk), lambda b, i: (b, 0, 0, 0)),
                  pl.BlockSpec((None, nkb, 8, LANES), lambda b, i: (b, 0, 0, 0)),
                  full(lstrict), full(wuvT_pair)],
        out_specs=pl.BlockSpec((None, tq, W_DQ), lambda b, i: (b, i, 0)),
        scratch_shapes=[pltpu.VMEM((nkb, tk, tq), F32),
                        pltpu.VMEM((LANES, IDX_HEADS * tq), BF16),
                        pltpu.VMEM((2 * DSA_LATENT, DSA_HEADS * tq), BF16),
                        pltpu.VMEM((VT_ROWS, DSA_HEADS * tq), F32),
                        pltpu.VMEM((1, DSA_HEADS * tq), F32),
                        pltpu.VMEM((8, tq), F32)],
        compiler_params=pltpu.CompilerParams(dimension_semantics=("arbitrary", "arbitrary"),
                                             vmem_limit_bytes=VMEM_LIMIT),
        name="dsa",
    )(qabsT, iqT, iwT, ik4, ckv, ckvT, kmax, lstrict, wuvT_pair)


def _diff_kernel(fqT_ref, fk_ref, fvT_ref, kmax_ref, lam_ref, gsub_ref, o_ref, qt_ref, acc_ref, m_ref,
                 *, tq, tk, lam0):
    hd = pl.program_id(1)
    qi = pl.program_id(2)
    q_pos = qi * tq + lax.broadcasted_iota(jnp.int32, (1, tq), 1)
    n_full = (qi * tq) // tk
    n_diag = (tq + tk - 1) // tk
    slopes = _alibi_slopes(DIFF_HEADS)
    slope = jnp.float32(slopes[DIFF_HEADS - 1])
    for i in range(DIFF_HEADS - 1):
        slope = jnp.where(hd == i, slopes[i], slope)

    for m in range(2):
        r0 = m * LANES
        qh = fqT_ref[r0:r0 + DIFF_QK_DIM, :]
        qf = qh.astype(F32)
        shift = _shift_bound(jnp.sum(qf * qf, axis=0, keepdims=True),
                             _tile_max(kmax_ref, n_full + n_diag - 1, 2 * hd + m), 0.0)
        qt_ref[r0:r0 + DIFF_QK_DIM, :] = qh
        qt_ref[r0 + DIFF_QK_DIM:r0 + DIFF_QK_DIM + BF16_ROWS, :] = _query_feature_rows(
            q_pos, BF16_ROWS, slope, shift).astype(BF16)
        qt_ref[r0 + DIFF_QK_DIM + BF16_ROWS:r0 + LANES, :] = jnp.zeros(
            (LANES - DIFF_QK_DIM - BF16_ROWS, tq), BF16)
    acc_ref[...] = jnp.zeros_like(acc_ref)

    def fast_pv(kb, masked):
        kx = fk_ref[pl.ds(pl.multiple_of(kb * tk, tk), tk), :]
        vT = fvT_ref[kb]
        out = []
        for m in range(2):
            s = _dot(kx[:, m * LANES:(m + 1) * LANES], qt_ref[m * LANES:(m + 1) * LANES, :])
            if masked:
                causal = lax.broadcasted_iota(jnp.int32, (tk, tq), 0) + kb * tk <= q_pos
                s = jnp.where(causal, s, NEG_BIG)
            out.append(_dot(vT, jnp.exp(s).astype(BF16)))
        return out

    def accumulate(*pvs):
        for m in range(2):
            acc_ref[m] += functools.reduce(lambda a, b: a + b, [pv[m] for pv in pvs])

    def fast_pair(j, carry):
        accumulate(fast_pv(2 * j, False), fast_pv(2 * j + 1, False))
        return carry

    lax.fori_loop(0, n_full // 2, fast_pair, 0)
    assert n_diag == 1

    @pl.when(n_full % 2 == 1)
    def _():
        accumulate(fast_pv(n_full - 1, False), fast_pv(n_full, True))

    @pl.when(n_full % 2 == 0)
    def _():
        accumulate(fast_pv(n_full, True))

    def block(kb, masked):
        kx = fk_ref[pl.ds(pl.multiple_of(kb * tk, tk), tk), :]
        vT = fvT_ref[kb]
        qT = qt_ref[...]
        m_all = m_ref[...]
        new = []
        for m in range(2):
            s = _dot(kx[:, m * LANES:(m + 1) * LANES], qT[m * LANES:(m + 1) * LANES, :])
            if masked:
                causal = lax.broadcasted_iota(jnp.int32, (tk, tq), 0) + kb * tk <= q_pos
                s = jnp.where(causal, s, NEG_BIG)
            m_old = m_all[m]
            m_new = jnp.maximum(m_old, jnp.max(s, axis=0, keepdims=True))
            p = jnp.exp(s - m_new).astype(BF16)
            new.append((m_new, jnp.exp(m_old - m_new), _dot(vT, p)))
        for m in range(2):
            m_ref[m] = new[m][0]
            acc_ref[m] = new[m][1] * acc_ref[m] + new[m][2]

    def body(kb, carry):
        block(kb, False)
        return carry

    denom_ok = jnp.min(acc_ref[:, ONES_ROW:ONES_ROW + 1, :]) > L_MIN

    @pl.when(jnp.logical_not(denom_ok))
    def _():
        acc_ref[...] = jnp.zeros_like(acc_ref)
        m_ref[...] = jnp.full_like(m_ref, NEG_BIG)
        lax.fori_loop(0, n_full, body, 0)
        for d in range(n_diag):
            block(n_full + d, True)

    lv = lam_ref[...]
    lam = (jnp.exp(jnp.sum(lv[0:1] * lv[1:2], axis=-1, keepdims=True))
           - jnp.exp(jnp.sum(lv[2:3] * lv[3:4], axis=-1, keepdims=True)) + lam0)
    a1 = acc_ref[0]
    a2 = acc_ref[1]
    o = (a1[0:DIFF_V_DIM, :] / a1[ONES_ROW:ONES_ROW + 1, :]
         - lam * (a2[0:DIFF_V_DIM, :] / a2[ONES_ROW:ONES_ROW + 1, :]))
    o = o * lax.rsqrt(jnp.mean(o * o, axis=0, keepdims=True) + EPS) * gsub_ref[...] * (1.0 - lam0)
    o_ref[...] = o.T.astype(BF16)


def _diff_call(fqT, fk, fvT, kmax, lam_vecs, g_sub_col, tq, tk, lam0):
    B, S, _ = fk.shape
    nkb = S // tk
    kern = functools.partial(_diff_kernel, tq=tq, tk=tk, lam0=lam0)
    full = lambda a: pl.BlockSpec(a.shape, lambda b, h, i: (0,) * a.ndim)
    return pl.pallas_call(
        kern,
        out_shape=jax.ShapeDtypeStruct((B, S, DIFF_HEADS * DIFF_V_DIM), BF16),
        grid=(B, DIFF_HEADS, S // tq),
        in_specs=[pl.BlockSpec((None, 2 * LANES, tq), lambda b, h, i: (b, h, i)),
                  pl.BlockSpec((None, S, 2 * LANES), lambda b, h, i: (b, 0, h)),
                  pl.BlockSpec((None, None, nkb, VT_ROWS, tk), lambda b, h, i: (b, h, 0, 0, 0)),
                  pl.BlockSpec((None, nkb, 8, LANES), lambda b, h, i: (b, 0, 0, 0)),
                  full(lam_vecs), full(g_sub_col)],
        out_specs=pl.BlockSpec((None, tq, DIFF_V_DIM), lambda b, h, i: (b, i, h)),
        scratch_shapes=[pltpu.VMEM((2 * LANES, tq), BF16),
                        pltpu.VMEM((2, VT_ROWS, tq), F32),
                        pltpu.VMEM((2, 1, tq), F32)],
        compiler_params=pltpu.CompilerParams(
            dimension_semantics=("arbitrary", "arbitrary", "arbitrary"),
            vmem_limit_bytes=VMEM_LIMIT),
        name="diff",
    )(fqT, fk, fvT, kmax, lam_vecs, g_sub_col)


def _route_kernel(x_ref, od_ref, of_ref, mod_ref, wout_ref, g_ref, wr_hi_ref, wr_lo_ref, br_ref,
                  x1_ref, h_ref, gates_ref, *, tm):
    attn = (_dot(od_ref[...], wout_ref[0:W_DQ, :]) + _dot(of_ref[...], wout_ref[W_DQ:, :]))
    x1 = x_ref[...] + mod_ref[2:3, :] * attn
    x1_ref[...] = x1
    h = _rms(x1, g_ref[...]) * (1.0 + mod_ref[4:5, :]) + mod_ref[3:4, :]
    h_hi = h.astype(BF16)
    h_ref[...] = h_hi
    h_lo = (h - h_hi.astype(F32)).astype(BF16)
    w_hi = wr_hi_ref[...]
    logits = (_dot(h_hi, w_hi) + _dot(h_lo, w_hi) + _dot(h_hi, wr_lo_ref[...])) + br_ref[...]

    lane = lax.broadcasted_iota(jnp.int32, (tm, LANES), 1)
    big = jnp.int32(4 * LANES)
    neg = -jnp.inf
    is_g = (lane >= N_EXPERTS) & (lane < N_EXPERTS + N_GROUPS)
    gl = jnp.where(is_g, logits, neg)
    gmax = jnp.max(gl, axis=-1, keepdims=True)
    g_lane = jnp.min(jnp.where(gl == gmax, lane, big), axis=-1, keepdims=True)
    g_sel = g_lane - N_EXPERTS
    p_g = 1.0 / jnp.sum(jnp.where(is_g, jnp.exp(gl - gmax), 0.0), axis=-1, keepdims=True)
    in_grp = (lane < N_EXPERTS) & ((lane // EXPERTS_PER_GROUP) == g_sel)
    el = jnp.where(in_grp, logits, neg)
    v1 = jnp.max(el, axis=-1, keepdims=True)
    i1 = jnp.min(jnp.where(el == v1, lane, big), axis=-1, keepdims=True)
    el2 = jnp.where(lane == i1, neg, el)
    v2 = jnp.max(el2, axis=-1, keepdims=True)
    i2 = jnp.min(jnp.where(el2 == v2, lane, big), axis=-1, keepdims=True)
    e = jnp.exp(v2 - v1)
    w1 = 1.0 / (1.0 + e)
    w2 = e * w1
    gates_ref[...] = jnp.where(lane == i1, w1 * p_g, jnp.where(lane == i2, w2 * p_g, 0.0))


def _route_call(x, o_dsa, o_diff, mod3, w_out, g_moe, wr_hi, wr_lo, b_r, tm):
    B, S, D = x.shape
    kern = functools.partial(_route_kernel, tm=tm)
    tok = lambda w: pl.BlockSpec((None, tm, w), lambda b, i: (b, i, 0))
    full = lambda a: pl.BlockSpec(a.shape, lambda b, i: (0,) * a.ndim)
    return pl.pallas_call(
        kern,
        out_shape=[jax.ShapeDtypeStruct((B, S, D), F32), jax.ShapeDtypeStruct((B, S, D), BF16),
                   jax.ShapeDtypeStruct((B, S, LANES), F32)],
        grid=(B, S // tm),
        in_specs=[tok(D), tok(W_DQ), tok(DIFF_HEADS * DIFF_V_DIM),
                  pl.BlockSpec((None, 6, D), lambda b, i: (b, 0, 0)),
                  full(w_out), full(g_moe), full(wr_hi), full(wr_lo), full(b_r)],
        out_specs=[tok(D), tok(D), tok(LANES)],
        compiler_params=pltpu.CompilerParams(dimension_semantics=("arbitrary", "arbitrary"),
                                             vmem_limit_bytes=VMEM_LIMIT),
        name="route",
    )(x, o_dsa, o_diff, mod3, w_out, g_moe, wr_hi, wr_lo, b_r)


def _moe_kernel(h_ref, gates_ref, x1_ref, mod_ref, wg_ref, wu_ref, wd_ref, gf_ref, o_ref, acc_ref,
                *, tm, ec):
    j = pl.program_id(2)
    nc = pl.num_programs(2)

    @pl.when(j == 0)
    def _():
        acc_ref[...] = jnp.zeros_like(acc_ref)

    h = h_ref[...]
    hg = _dot(h, wg_ref[...])
    hu = _dot(h, wu_ref[...])
    hid = hg * jax.nn.sigmoid(hg) * hu
    gates = gates_ref[...]
    lane = lax.broadcasted_iota(jnp.int32, (tm, LANES), 1)
    parts = []
    for e in range(ec):
        gcol = jnp.sum(jnp.where(lane == j * ec + e, gates, 0.0), axis=-1, keepdims=True)
        parts.append((hid[:, e * D_EXPERT:(e + 1) * D_EXPERT] * gcol).astype(BF16))
    hs = jnp.concatenate(parts, axis=1)
    acc_ref[...] += _dot(hs, wd_ref[...])

    @pl.when(j == nc - 1)
    def _():
        x2 = x1_ref[...] + mod_ref[5:6, :] * acc_ref[...]
        o_ref[...] = _rms(x2, gf_ref[...])


def _moe_call(h, gates, x1, mod3, wg, wu, wd, g_final, tm, ec):
    B, S, D = x1.shape
    kern = functools.partial(_moe_kernel, tm=tm, ec=ec)
    tok = lambda w: pl.BlockSpec((None, tm, w), lambda b, i, j: (b, i, 0))
    return pl.pallas_call(
        kern,
        out_shape=jax.ShapeDtypeStruct((B, S, D), F32),
        grid=(B, S // tm, N_EXPERTS // ec),
        in_specs=[tok(D), tok(LANES), tok(D),
                  pl.BlockSpec((None, 6, D), lambda b, i, j: (b, 0, 0)),
                  pl.BlockSpec((D, ec * D_EXPERT), lambda b, i, j: (0, j)),
                  pl.BlockSpec((D, ec * D_EXPERT), lambda b, i, j: (0, j)),
                  pl.BlockSpec((ec * D_EXPERT, D), lambda b, i, j: (j, 0)),
                  pl.BlockSpec((1, D), lambda b, i, j: (0, 0))],
        out_specs=tok(D),
        scratch_shapes=[pltpu.VMEM((tm, D), F32)],
        compiler_params=pltpu.CompilerParams(
            dimension_semantics=("arbitrary", "arbitrary", "arbitrary"),
            vmem_limit_bytes=VMEM_LIMIT),
        name="moe",
    )(h, gates, x1, mod3, wg, wu, wd, g_final)


def _tile(n, pref):
    t = min(n, pref)
    assert n % t == 0, (n, t)
    return t


def kernel(x, c, w_ada, b_ada, g_attn, w_in, g_kv, w_uk, w_uv, lam_q1, lam_k1, lam_q2, lam_k2,
           g_sub, w_out, g_moe, w_group, b_group, w_router, b_router, w_gate, w_up, w_down, g_final):
    B, S, D = x.shape
    assert D == D_MODEL and w_ada.shape[0] == 1
    assert S % KEY_TILE == 0 and S <= POS_SPLIT * 256
    topk = min(TOPK_MAX, S // 4)
    l = 0
    lam0 = 0.8 - 0.6 * math.exp(-0.3 * l)

    mod3 = _mod_call(c, w_ada[l], b_ada[l].reshape(1, -1)).reshape(B, 6, D)

    wn, wt, w_pair = _proj_weights(w_in[l], w_uk[l])
    qabsT, iqT, iwT, ik4, ckv, ckvT, fqT, fk, fvT, kmax = _proj_call(
        x, mod3, g_attn[l].reshape(1, D), wn, wt, w_pair, g_kv[l].reshape(1, -1), KEY_TILE)

    lstrict = jnp.asarray(np.tril(np.ones((KEY_TILE, KEY_TILE), np.float32), -1), BF16)
    uvT = jnp.swapaxes(w_uv[l], 1, 2)
    z = jnp.zeros_like(uvT[0])
    wuvT_pair = jnp.stack([
        jnp.concatenate([jnp.concatenate([uvT[2 * p], z], axis=1),
                         jnp.concatenate([z, uvT[2 * p + 1]], axis=1)], axis=0)
        for p in range(DSA_HEADS // 2)]).astype(BF16)
    o_dsa = _dsa_call(qabsT, iqT, iwT, ik4, ckv, ckvT, kmax, lstrict, wuvT_pair, LANES, KEY_TILE, topk)

    lam_vecs = jnp.concatenate([lam_q1[l][None], lam_k1[l][None], lam_q2[l][None], lam_k2[l][None]],
                               axis=0).astype(F32)
    o_diff = _diff_call(fqT, fk, fvT, kmax, lam_vecs, g_sub[l].reshape(-1, 1), KEY_TILE, KEY_TILE, lam0)

    wr = jnp.pad(jnp.concatenate([w_router[l], w_group[l]], axis=1),
                 ((0, 0), (0, LANES - N_EXPERTS - N_GROUPS)))
    wr_hi = wr.astype(BF16)
    wr_lo = (wr - wr_hi.astype(F32)).astype(BF16)
    b_r = jnp.pad(jnp.concatenate([b_router[l], b_group[l]]), (0, LANES - N_EXPERTS - N_GROUPS))
    x1, h2, gates = _route_call(x, o_dsa, o_diff, mod3, w_out[l].astype(BF16),
                                g_moe[l].reshape(1, D), wr_hi, wr_lo, b_r.reshape(1, LANES),
                                _tile(S, 512))

    wg = jnp.swapaxes(w_gate[l], 0, 1).reshape(D, N_EXPERTS * D_EXPERT).astype(BF16)
    wu = jnp.swapaxes(w_up[l], 0, 1).reshape(D, N_EXPERTS * D_EXPERT).astype(BF16)
    wd = w_down[l].reshape(N_EXPERTS * D_EXPERT, D).astype(BF16)
    return _moe_call(h2, gates, x1, mod3, wg, wu, wd, g_final.reshape(1, D), _tile(S, 512), 4)
```

```python
import functools
import math

import jax
import jax.numpy as jnp
import numpy as np
from jax import lax
from jax.experimental import pallas as pl
from jax.experimental.pallas import tpu as pltpu

F32 = jnp.float32
BF16 = jnp.bfloat16

D_MODEL = 1024
DSA_HEADS = 8
DSA_HEAD_DIM = 64
DSA_LATENT = 128
IDX_HEADS = 4
IDX_DIM = 32
TOPK_MAX = 256
DIFF_HEADS = 4
DIFF_QK_DIM = 64
DIFF_V_DIM = 128
N_GROUPS = 4
EXPERTS_PER_GROUP = 8
N_EXPERTS = 32
D_EXPERT = 256
EPS = 1e-6

LANES = 128
BF16_ROWS = 16
POS_SPLIT = 64
NEG_BIG = -1e30
VMEM_LIMIT = 56 * 1024 * 1024
KEY_TILE = 512
VALUE_BISECT_STEPS = 24
UNCHECKED_BISECT_STEPS = 16
COUNT_CHAINS = 4
ATTN_CHAINS = 1

W_DQ = DSA_HEADS * DSA_HEAD_DIM
W_QABS = DSA_HEADS * DSA_LATENT
W_CKV = 2 * LANES
VT_ROWS = DSA_LATENT + BF16_ROWS
ONES_ROW = DSA_LATENT + 2
N_FEATS = 5
KMAX_LANE_CKV = 8
L_MIN = 1e-30
W_FQ = DIFF_HEADS * 2 * LANES
W_DIFF_QK = DIFF_HEADS * 2 * DIFF_QK_DIM

T_DQ = 0
T_DLAT = T_DQ + W_DQ
T_IQ = T_DLAT + DSA_LATENT
T_IW = T_IQ + IDX_HEADS * IDX_DIM
T_FQ = T_IW + BF16_ROWS
T_FV = T_FQ + W_DIFF_QK
T_ROWS = T_FV + DIFF_HEADS * DIFF_V_DIM
N_IK = 0
N_DLAT = N_IK + LANES
N_FK = N_DLAT + DSA_LATENT
N_COLS = N_FK + W_FQ


def _alibi_slopes(n):
    return [2.0 ** (-8.0 * (i + 1) / n) for i in range(n)]


def _rms(x, g):
    return x * lax.rsqrt(jnp.mean(x * x, axis=-1, keepdims=True) + EPS) * g


def _dot(a, b):
    return jnp.dot(a, b, preferred_element_type=F32)


def _mod_kernel(c_ref, w_ref, b_ref, o_ref):
    c = c_ref[...]
    act = c * jax.nn.sigmoid(c)
    o_ref[...] = jnp.dot(act, w_ref[...], preferred_element_type=F32,
                         precision=lax.Precision.HIGHEST) + b_ref[...]


def _mod_call(c, w_ada, b_ada):
    B, D = c.shape
    n = w_ada.shape[1] // D
    return pl.pallas_call(
        _mod_kernel,
        out_shape=jax.ShapeDtypeStruct((B, n * D), F32),
        grid=(n,),
        in_specs=[pl.BlockSpec((B, D), lambda j: (0, 0)),
                  pl.BlockSpec((D, D), lambda j: (0, j)),
                  pl.BlockSpec((1, D), lambda j: (0, j))],
        out_specs=pl.BlockSpec((B, D), lambda j: (0, j)),
        compiler_params=pltpu.CompilerParams(dimension_semantics=("arbitrary",),
                                             vmem_limit_bytes=VMEM_LIMIT),
        name="mod",
    )(c, w_ada, b_ada)


def _key_feature_rows(pos_row, n_rows):
    r = lax.broadcasted_iota(jnp.int32, (n_rows, pos_row.shape[1]), 0)
    pa = (pos_row // POS_SPLIT).astype(F32)
    pb = (pos_row % POS_SPLIT).astype(F32)
    return jnp.where(r == 0, pa, jnp.where(r == 1, pb, jnp.where(r < N_FEATS, 1.0, 0.0)))


def _query_feature_rows(pos_row, n_rows, slope, shift):
    r = lax.broadcasted_iota(jnp.int32, (n_rows, pos_row.shape[1]), 0)
    pa = (pos_row // POS_SPLIT).astype(F32)
    pb = (pos_row % POS_SPLIT).astype(F32)
    return jnp.where(r == 0, POS_SPLIT * slope,
                     jnp.where(r == 1, slope,
                               jnp.where(r == 2, -POS_SPLIT * slope * pa,
                                         jnp.where(r == 3, -slope * pb,
                                                   jnp.where(r == 4, -shift, 0.0)))))


def _shift_bound(qn2, kmax2, extra):
    b = jnp.sqrt(qn2 * kmax2) + extra
    return b + jnp.abs(b) * (2.0 ** -6) + 2.0 ** -20


def _tile_max(kmax_ref, last_tile, lane_idx):
    x = kmax_ref[...]
    t = lax.broadcasted_iota(jnp.int32, x.shape, 0)
    ln = lax.broadcasted_iota(jnp.int32, x.shape, 2)
    x = jnp.where((t <= last_tile) & (ln == lane_idx), x, 0.0)
    return jnp.max(jnp.max(x, axis=0), axis=1, keepdims=True)[0:1, :]


def _proj_kernel(x_ref, mod_ref, g_ref, wn_ref, wt_ref, wpair_ref, gkv_ref, gkvc_ref, fc_ref, sel_ref,
                 qabsT_ref, iqT_ref, iwT_ref, ik_ref, ckv_ref, ckvT_ref, fqT_ref, fk_ref, fvT_ref,
                 kmax_ref, *, tm):
    x = x_ref[...]
    h = _rms(x, g_ref[...]) * (1.0 + mod_ref[1:2, :]) + mod_ref[0:1, :]
    hb = h.astype(BF16)
    hT = h.T.astype(BF16)

    base = pl.program_id(1) * tm
    pos_c = base + lax.broadcasted_iota(jnp.int32, (tm, 1), 0)
    pos_r = base + lax.broadcasted_iota(jnp.int32, (1, tm), 1)
    pa = (pos_c // POS_SPLIT).astype(F32)
    pb = (pos_c % POS_SPLIT).astype(F32)

    def feats(row, width):
        return (fc_ref[row:row + 1, 0:width] + fc_ref[row + 1:row + 2, 0:width] * pa
                + fc_ref[row + 2:row + 3, 0:width] * pb)

    ik_ref[...] = _dot(hb, wn_ref[:, N_IK:N_IK + LANES]).astype(BF16)
    dlat = _dot(hb, wn_ref[:, N_DLAT:N_DLAT + DSA_LATENT])
    ckv_b = _rms(dlat, gkv_ref[...]).astype(BF16)
    ckv_ref[:, 0:LANES] = ckv_b
    ckv_ref[:, LANES:2 * LANES] = feats(0, LANES).astype(BF16)
    fk = _dot(hb, wn_ref[:, N_FK:N_FK + W_FQ])
    fk_ref[...] = (fk + feats(3, W_FQ)).astype(BF16)
    sq = jnp.concatenate([fk.astype(BF16).astype(F32), ckv_b.astype(F32)], axis=1)
    sq_up = (sq * sq * (1.0 + 2.0 ** -7)).astype(BF16)
    kmax_ref[...] = jnp.broadcast_to(jnp.max(_dot(sq_up, sel_ref[...]), axis=0, keepdims=True),
                                     (8, LANES))

    dqT = _dot(wt_ref[T_DQ:T_DQ + W_DQ, :], hT).astype(BF16)
    for p in range(DSA_HEADS // 2):
        qa = _dot(wpair_ref[p], dqT[p * LANES:(p + 1) * LANES, :])
        qabsT_ref[p * 2 * LANES:(p + 1) * 2 * LANES, :] = qa.astype(BF16)
    dlatT = _dot(wt_ref[T_DLAT:T_DLAT + DSA_LATENT, :], hT)
    inv = lax.rsqrt(jnp.mean(dlatT * dlatT, axis=0, keepdims=True) + EPS)
    kfeat = _key_feature_rows(pos_r, BF16_ROWS).astype(BF16)
    ckvT_ref[0:DSA_LATENT, :] = (dlatT * inv * gkvc_ref[...]).astype(BF16)
    ckvT_ref[DSA_LATENT:VT_ROWS, :] = kfeat
    iqT_ref[...] = _dot(wt_ref[T_IQ:T_IQ + IDX_HEADS * IDX_DIM, :], hT).astype(BF16)
    iwT = _dot(wt_ref[T_IW:T_IW + BF16_ROWS, :], hT)
    iwT_ref[...] = iwT[0:8, :] * (IDX_HEADS ** -0.5 * IDX_DIM ** -0.5)
    fqT = _dot(wt_ref[T_FQ:T_FQ + W_DIFF_QK, :], hT)
    zpad = jnp.zeros((LANES - DIFF_QK_DIM, tm), BF16)
    for s in range(DIFF_HEADS * 2):
        r0 = s * LANES
        fqT_ref[r0:r0 + DIFF_QK_DIM, :] = fqT[s * DIFF_QK_DIM:(s + 1) * DIFF_QK_DIM, :].astype(BF16)
        fqT_ref[r0 + DIFF_QK_DIM:r0 + LANES, :] = zpad
    fvT = _dot(wt_ref[T_FV:T_FV + DIFF_HEADS * DIFF_V_DIM, :], hT)
    for hh in range(DIFF_HEADS):
        fvT_ref[hh, 0:DIFF_V_DIM, :] = fvT[hh * DIFF_V_DIM:(hh + 1) * DIFF_V_DIM, :].astype(BF16)
        fvT_ref[hh, DIFF_V_DIM:VT_ROWS, :] = kfeat


def _feature_consts():
    fc = np.zeros((8, W_FQ), np.float32)
    fc[1, 0] = 1.0
    fc[2, 1] = 1.0
    fc[0, 2:N_FEATS] = 1.0
    for s in range(DIFF_HEADS * 2):
        base = s * LANES + DIFF_QK_DIM
        fc[4, base + 0] = 1.0
        fc[5, base + 1] = 1.0
        fc[3, base + 2:base + N_FEATS] = 1.0
    return jnp.asarray(fc)


def _norm_selector():
    sel = np.zeros((W_FQ + DSA_LATENT, LANES), np.float32)
    for s in range(DIFF_HEADS * 2):
        sel[s * LANES:s * LANES + DIFF_QK_DIM, s] = 1.0
    sel[W_FQ:, KMAX_LANE_CKV] = 1.0
    return jnp.asarray(sel, BF16)


def _proj_weights(w_in, w_uk):
    D = w_in.shape[0]
    pts = np.cumsum([W_DQ, DSA_LATENT, IDX_HEADS * IDX_DIM, IDX_DIM, IDX_HEADS,
                     W_DIFF_QK, W_DIFF_QK])
    dq, dlat, iq, ik, iw, fq, fk, fv = jnp.split(w_in, list(pts), axis=1)
    ik4 = jnp.tile(ik, (1, IDX_HEADS))
    fke = jnp.pad(fk.reshape(D, DIFF_HEADS * 2, DIFF_QK_DIM),
                  ((0, 0), (0, 0), (0, LANES - DIFF_QK_DIM))).reshape(D, W_FQ)
    wn = jnp.concatenate([ik4, dlat, fke], axis=1).astype(BF16)
    iwp = jnp.pad(iw, ((0, 0), (0, BF16_ROWS - IDX_HEADS)))
    wt = jnp.concatenate([dq, dlat, iq, iwp, fq * (DIFF_QK_DIM ** -0.5), fv], axis=1).T.astype(BF16)
    uk = w_uk * (DSA_HEAD_DIM ** -0.5)
    z = jnp.zeros_like(uk[0])
    pairs = [jnp.concatenate([jnp.concatenate([uk[2 * p], z], axis=1),
                              jnp.concatenate([z, uk[2 * p + 1]], axis=1)], axis=0)
             for p in range(DSA_HEADS // 2)]
    return wn, wt, jnp.stack(pairs).astype(BF16)


def _proj_call(x, mod3, g_attn, wn, wt, w_pair, g_kv, tm):
    B, S, D = x.shape
    nt = S // tm
    fc = _feature_consts()
    sel = _norm_selector()
    kern = functools.partial(_proj_kernel, tm=tm)
    tok = lambda w: pl.BlockSpec((None, tm, w), lambda b, i: (b, i, 0))
    tokT = lambda r: pl.BlockSpec((None, r, tm), lambda b, i: (b, 0, i))
    full = lambda a: pl.BlockSpec(a.shape, lambda b, i: (0,) * a.ndim)
    g_kv_col = g_kv.reshape(-1, 1)
    out_shape = [jax.ShapeDtypeStruct((B, W_QABS, S), BF16),
                 jax.ShapeDtypeStruct((B, LANES, S), BF16),
                 jax.ShapeDtypeStruct((B, 8, S), F32),
                 jax.ShapeDtypeStruct((B, S, LANES), BF16),
                 jax.ShapeDtypeStruct((B, S, W_CKV), BF16),
                 jax.ShapeDtypeStruct((B, nt, VT_ROWS, tm), BF16),
                 jax.ShapeDtypeStruct((B, W_FQ, S), BF16),
                 jax.ShapeDtypeStruct((B, S, W_FQ), BF16),
                 jax.ShapeDtypeStruct((B, DIFF_HEADS, nt, VT_ROWS, tm), BF16),
                 jax.ShapeDtypeStruct((B, nt, 8, LANES), F32)]
    out_specs = [tokT(W_QABS), tokT(LANES), tokT(8), tok(LANES), tok(W_CKV),
                 pl.BlockSpec((None, None, VT_ROWS, tm), lambda b, i: (b, i, 0, 0)),
                 tokT(W_FQ), tok(W_FQ),
                 pl.BlockSpec((None, DIFF_HEADS, None, VT_ROWS, tm), lambda b, i: (b, 0, i, 0, 0)),
                 pl.BlockSpec((None, None, 8, LANES), lambda b, i: (b, i, 0, 0))]
    return pl.pallas_call(
        kern,
        out_shape=out_shape,
        grid=(B, nt),
        in_specs=[tok(D),
                  pl.BlockSpec((None, 6, D), lambda b, i: (b, 0, 0)),
                  full(g_attn), full(wn), full(wt), full(w_pair), full(g_kv), full(g_kv_col),
                  full(fc), full(sel)],
        out_specs=out_specs,
        compiler_params=pltpu.CompilerParams(dimension_semantics=("arbitrary", "arbitrary"),
                                             vmem_limit_bytes=VMEM_LIMIT),
        name="proj",
    )(x, mod3, g_attn, wn, wt, w_pair, g_kv, g_kv_col, fc, sel)


def _float_code(x):
    b = lax.bitcast_convert_type(x, jnp.int32)
    return b ^ (lax.shift_right_arithmetic(b, 31) & jnp.int32(0x7FFFFFFF))


def _float_decode(c):
    b = c ^ (lax.shift_right_arithmetic(c, 31) & jnp.int32(0x7FFFFFFF))
    return lax.bitcast_convert_type(b, F32)


def _dsa_kernel(qabsT_ref, iqT_ref, iwT_ref, ik_ref, ckv_ref, ckvT_ref, kmax_ref, lstrict_ref, wuvT_ref,
                o_ref, sc_ref, qst_ref, qt_ref, acc_ref, m_ref, tmp_ref, *, tq, tk, topk):
    qi = pl.program_id(1)
    nkb = (qi * tq) // tk + 1
    kf = float(topk)
    slopes = _alibi_slopes(DSA_HEADS)
    q_pos = qi * tq + lax.broadcasted_iota(jnp.int32, (1, tq), 1)

    iqT = iqT_ref[...]
    rowi = lax.broadcasted_iota(jnp.int32, (LANES, tq), 0)
    for hh in range(IDX_HEADS):
        qst_ref[:, hh * tq:(hh + 1) * tq] = jnp.where((rowi // IDX_DIM) == hh, iqT, jnp.zeros_like(iqT))
    iw = iwT_ref[...]
    wrow = [iw[hh:hh + 1, :] for hh in range(IDX_HEADS)]

    def scores(kb):
        kblk = ik_ref[pl.ds(pl.multiple_of(kb * tk, tk), tk), :]
        a = _dot(kblk, qst_ref[...])
        sc = jnp.maximum(a[:, 0:tq], 0.0) * wrow[0]
        for hh in range(1, IDX_HEADS):
            sc = sc + jnp.maximum(a[:, hh * tq:(hh + 1) * tq], 0.0) * wrow[hh]
        return sc

    def score_body(kb, carry):
        mn, mx = carry
        sc = scores(kb)
        sc_ref[kb] = sc
        return (jnp.minimum(mn, jnp.min(sc, axis=0, keepdims=True)),
                jnp.maximum(mx, jnp.max(sc, axis=0, keepdims=True)))

    last = nkb - 1
    mn, mx = lax.fori_loop(0, last // 2, lambda j, c: score_body(2 * j + 1, score_body(2 * j, c)),
                           (jnp.full((1, tq), jnp.inf, F32), jnp.full((1, tq), -jnp.inf, F32)))
    mn, mx = lax.cond(last % 2 == 1, lambda c: score_body(last - 1, c), lambda c: c, (mn, mx))
    sc = scores(last)
    causal = lax.broadcasted_iota(jnp.int32, (tk, tq), 0) + last * tk <= q_pos
    sc_ref[last] = jnp.where(causal, sc, jnp.nan)
    mn = jnp.minimum(mn, jnp.min(jnp.where(causal, sc, jnp.inf), axis=0, keepdims=True))
    mx = jnp.maximum(mx, jnp.max(jnp.where(causal, sc, -jnp.inf), axis=0, keepdims=True))

    def block_count(kb, pred):
        parts = [None] * COUNT_CHAINS
        for r in range(tk // 8):
            v = jnp.where(pred(sc_ref[kb, r * 8:(r + 1) * 8, :]), 1.0, 0.0)
            c = r % COUNT_CHAINS
            parts[c] = v if parts[c] is None else parts[c] + v
        return (parts[0] + parts[1]) + (parts[2] + parts[3])

    def count(pred):
        acc = lax.fori_loop(0, nkb, lambda kb, acc: acc + block_count(kb, pred),
                            jnp.zeros((8, tq), F32))
        return jnp.sum(acc, axis=0, keepdims=True)

    n_causal = (q_pos + 1).astype(F32)
    done0 = n_causal <= kf

    def bisect(st, value_mid):
        lo, hi, c_lo, theta, done = st
        th8 = jnp.broadcast_to(theta, (8, tq))
        c = count(lambda v: v >= th8)
        ge = c >= kf
        live = done == 0.0
        lo = jnp.where(live & ge, theta, lo)
        c_lo = jnp.where(live & ge, c, c_lo)
        hi = jnp.where(live & jnp.logical_not(ge), theta, hi)
        if value_mid:
            nxt = 0.5 * lo + 0.5 * hi
        else:
            cl, ch = _float_code(lo), _float_code(hi)
            nxt = _float_decode((cl & ch) + lax.shift_right_arithmetic(cl ^ ch, 1))
        inside = (nxt > lo) & (nxt < hi)
        done = jnp.where((c_lo == kf) | jnp.logical_not(inside), 1.0, done)
        return lo, hi, c_lo, nxt, done

    def checked(value_mid, max_steps):
        def cond(c):
            return jnp.logical_and(c[1] > 0.0, c[2] < max_steps)

        def body(c):
            st = bisect(bisect(c[0], value_mid), value_mid)
            return st, jnp.sum(1.0 - st[4]), c[2] + 2
        return cond, body

    zero8 = jnp.zeros((8, tq), F32)
    c_ge0 = count(lambda v: v >= zero8)
    c_gt0 = count(lambda v: v > zero8)
    live0 = jnp.logical_not(done0)
    at0 = live0 & (c_ge0 >= kf)
    tie0 = at0 & (c_gt0 < kf)
    below0 = live0 & (c_ge0 < kf)
    lo0 = jnp.where(at0, 0.0, mn)
    hi0 = jnp.where(below0, 0.0, mx)
    st = (lo0, hi0, jnp.where(at0, c_ge0, n_causal),
          jnp.where(below0, 0.5 * lo0 + 0.5 * hi0, mx), jnp.where(done0 | tie0, 1.0, 0.0))
    st = lax.fori_loop(0, UNCHECKED_BISECT_STEPS, lambda i, s: bisect(s, True), st)
    cond, body = checked(True, VALUE_BISECT_STEPS - UNCHECKED_BISECT_STEPS)
    st, active, _ = lax.while_loop(cond, body, (st, jnp.sum(1.0 - st[4]), jnp.int32(0)))
    cond, body = checked(False, 34)
    st, _, _ = lax.while_loop(cond, body, (st, active, jnp.int32(0)))
    tau, c_ge = st[0], st[2]

    over = c_ge > kf

    @pl.when(jnp.max(c_ge) > kf)
    def _():
        tau8 = jnp.broadcast_to(tau, (8, tq))
        tmp_ref[0:1, :] = c_gt0

        @pl.when(jnp.max(jnp.where(over & jnp.logical_not(tie0), 1.0, 0.0)) > 0.0)
        def _():
            tmp_ref[0:1, :] = jnp.where(tie0, c_gt0, count(lambda v: v > tau8))

        quota = kf - tmp_ref[0:1, :]

        def tie_body(kb, seen):
            cnt = jnp.sum(block_count(kb, lambda v: v == tau8), axis=0, keepdims=True)
            inside = over & (seen < quota) & (seen + cnt > quota)
            gone = over & (seen >= quota) & (cnt > 0.0)
            flag = jnp.max(jnp.where(inside, 2.0, jnp.where(gone, 1.0, 0.0)))

            @pl.when(flag > 1.5)
            def _():
                s = sc_ref[kb]
                eq = s == tau
                rank = _dot(lstrict_ref[...], jnp.where(eq, 1.0, 0.0).astype(BF16)) + seen
                sc_ref[kb] = jnp.where(eq & over & (rank >= quota), jnp.nan, s)

            @pl.when(flag == 1.0)
            def _():
                s = sc_ref[kb]
                sc_ref[kb] = jnp.where((s == tau) & gone, jnp.nan, s)

            return seen + cnt

        lax.fori_loop(0, nkb, tie_body, jnp.zeros((1, tq), F32))

    tau8 = jnp.broadcast_to(tau, (8, tq))
    sub = lax.broadcasted_iota(jnp.int32, (8, tq), 0).astype(F32)

    def nearest_body(kb, acc):
        parts = [None] * COUNT_CHAINS
        for r in range(tk // 8):
            idx = sub + (kb * tk + r * 8).astype(F32)
            v = jnp.where(sc_ref[kb, r * 8:(r + 1) * 8, :] >= tau8, idx, -1.0)
            c = r % COUNT_CHAINS
            parts[c] = v if parts[c] is None else jnp.maximum(parts[c], v)
        return jnp.maximum(acc, jnp.maximum(jnp.maximum(parts[0], parts[1]),
                                            jnp.maximum(parts[2], parts[3])))

    last_sel = jnp.max(lax.fori_loop(0, nkb, nearest_body, jnp.full((8, tq), -1.0, F32)),
                       axis=0, keepdims=True)
    d_min = q_pos.astype(F32) - last_sel
    kmax2 = _tile_max(kmax_ref, nkb - 1, KMAX_LANE_CKV)
    for hh in range(DSA_HEADS):
        qh = qabsT_ref[hh * DSA_LATENT:(hh + 1) * DSA_LATENT, :]
        qf = qh.astype(F32)
        shift = _shift_bound(jnp.sum(qf * qf, axis=0, keepdims=True), kmax2, -slopes[hh] * d_min)
        qt_ref[0:DSA_LATENT, hh * tq:(hh + 1) * tq] = qh
        qt_ref[DSA_LATENT:VT_ROWS, hh * tq:(hh + 1) * tq] = _query_feature_rows(
            q_pos, BF16_ROWS, slopes[hh], shift).astype(BF16)
    qt_ref[VT_ROWS:2 * DSA_LATENT, :] = jnp.zeros((2 * DSA_LATENT - VT_ROWS, DSA_HEADS * tq), BF16)
    acc_ref[...] = jnp.zeros_like(acc_ref)

    def fast_pv(kb):
        kx = ckv_ref[pl.ds(pl.multiple_of(kb * tk, tk), tk), :]
        s = _dot(kx, qt_ref[...])
        keep = sc_ref[kb] >= tau
        ps = [jnp.exp(jnp.where(keep, s[:, hh * tq:(hh + 1) * tq], NEG_BIG)).astype(BF16)
              for hh in range(DSA_HEADS)]
        return _dot(ckvT_ref[kb], jnp.concatenate(ps, axis=1))

    def fast_quad(j, carry):
        acc_ref[...] += ((fast_pv(4 * j) + fast_pv(4 * j + 1))
                         + (fast_pv(4 * j + 2) + fast_pv(4 * j + 3)))
        return carry

    lax.fori_loop(0, nkb // 4, fast_quad, 0)
    rem = nkb % 4

    @pl.when(rem >= 2)
    def _():
        acc_ref[...] += fast_pv(nkb - rem) + fast_pv(nkb - rem + 1)

    @pl.when(rem % 2 == 1)
    def _():
        acc_ref[...] += fast_pv(nkb - 1)

    def attn_body(kb, carry):
        kx = ckv_ref[pl.ds(pl.multiple_of(kb * tk, tk), tk), :]
        vT = ckvT_ref[kb]
        keep = sc_ref[kb] >= tau
        qt = qt_ref[...]
        m_all = m_ref[...]
        m_out, alphas, pvs = [], [], []
        for ch in range(ATTN_CHAINS):
            hpc = DSA_HEADS // ATTN_CHAINS
            l0 = ch * hpc * tq
            s = _dot(kx, qt[:, l0:l0 + hpc * tq])
            ps = []
            for j in range(hpc):
                c0 = l0 + j * tq
                sh = jnp.where(keep, s[:, j * tq:(j + 1) * tq], NEG_BIG)
                m_old = m_all[:, c0:c0 + tq]
                m_new = jnp.maximum(m_old, jnp.max(sh, axis=0, keepdims=True))
                ps.append(jnp.exp(sh - m_new).astype(BF16))
                alphas.append(jnp.exp(m_old - m_new))
                m_out.append(m_new)
            pvs.append(_dot(vT, jnp.concatenate(ps, axis=1)))
        m_ref[...] = jnp.concatenate(m_out, axis=1)
        acc_ref[...] = jnp.concatenate(alphas, axis=1) * acc_ref[...] + jnp.concatenate(pvs, axis=1)
        return carry

    denom_ok = jnp.min(acc_ref[ONES_ROW:ONES_ROW + 1, :]) > L_MIN

    @pl.when(jnp.logical_not(denom_ok))
    def _():
        acc_ref[...] = jnp.zeros_like(acc_ref)
        m_ref[...] = jnp.full_like(m_ref, NEG_BIG)
        lax.fori_loop(0, nkb, attn_body, 0)

    acc = acc_ref[...]
    lat = acc[0:DSA_LATENT, :] / acc[ONES_ROW:ONES_ROW + 1, :]
    for p in range(DSA_HEADS // 2):
        pair = lat[:, 2 * p * tq:(2 * p + 2) * tq]
        pair = jnp.concatenate([pair[:, 0:tq], pair[:, tq:2 * tq]], axis=0).astype(BF16)
        o_ref[:, p * LANES:(p + 1) * LANES] = _dot(wuvT_ref[p], pair).T.astype(BF16)


def _dsa_call(qabsT, iqT, iwT, ik4, ckv, ckvT, kmax, lstrict, wuvT_pair, tq, tk, topk):
    B, _, S = qabsT.shape
    nkb = S // tk
    kern = functools.partial(_dsa_kernel, tq=tq, tk=tk, topk=topk)
    blkT = lambda r: pl.BlockSpec((None, r, tq), lambda b, i: (b, 0, i))
    per_b = lambda w: pl.BlockSpec((None, S, w), lambda b, i: (b, 0, 0))
    full = lambda a: pl.BlockSpec(a.shape, lambda b, i: (0,) * a.ndim)
    return pl.pallas_call(
        kern,
        out_shape=jax.ShapeDtypeStruct((B, S, W_DQ), BF16),
        grid=(B, S // tq),
        in_specs=[blkT(W_QABS), blkT(LANES), blkT(8), per_b(LANES), per_b(W_CKV),
                  pl.BlockSpec((None, nkb, VT_ROWS, tk), lambda b, i: (b, 0, 0, 0)),
                  pl.BlockSpec((None, nkb, 8, LANES), lambda b, i: (b, 0, 0, 0)),
                  full(lstrict), full(wuvT_pair)],
        out_specs=pl.BlockSpec((None, tq, W_DQ), lambda b, i: (b, i, 0)),
        scratch_shapes=[pltpu.VMEM((nkb, tk, tq), F32),
                        pltpu.VMEM((LANES, IDX_HEADS * tq), BF16),
                        pltpu.VMEM((2 * DSA_LATENT, DSA_HEADS * tq), BF16),
                        pltpu.VMEM((VT_ROWS, DSA_HEADS * tq), F32),
                        pltpu.VMEM((1, DSA_HEADS * tq), F32),
                        pltpu.VMEM((8, tq), F32)],
        compiler_params=pltpu.CompilerParams(dimension_semantics=("arbitrary", "arbitrary"),
                                             vmem_limit_bytes=VMEM_LIMIT),
        name="dsa",
    )(qabsT, iqT, iwT, ik4, ckv, ckvT, kmax, lstrict, wuvT_pair)


def _diff_kernel(fqT_ref, fk_ref, fvT_ref, kmax_ref, lam_ref, gsub_ref, o_ref, qt_ref, acc_ref, m_ref,
                 *, tq, tk, lam0):
    hd = pl.program_id(1)
    qi = pl.program_id(2)
    q_pos = qi * tq + lax.broadcasted_iota(jnp.int32, (1, tq), 1)
    n_full = (qi * tq) // tk
    n_diag = (tq + tk - 1) // tk
    slopes = _alibi_slopes(DIFF_HEADS)
    slope = jnp.float32(slopes[DIFF_HEADS - 1])
    for i in range(DIFF_HEADS - 1):
        slope = jnp.where(hd == i, slopes[i], slope)

    for m in range(2):
        r0 = m * LANES
        qh = fqT_ref[r0:r0 + DIFF_QK_DIM, :]
        qf = qh.astype(F32)
        shift = _shift_bound(jnp.sum(qf * qf, axis=0, keepdims=True),
                             _tile_max(kmax_ref, n_full + n_diag - 1, 2 * hd + m), 0.0)
        qt_ref[r0:r0 + DIFF_QK_DIM, :] = qh
        qt_ref[r0 + DIFF_QK_DIM:r0 + DIFF_QK_DIM + BF16_ROWS, :] = _query_feature_rows(
            q_pos, BF16_ROWS, slope, shift).astype(BF16)
        qt_ref[r0 + DIFF_QK_DIM + BF16_ROWS:r0 + LANES, :] = jnp.zeros(
            (LANES - DIFF_QK_DIM - BF16_ROWS, tq), BF16)
    acc_ref[...] = jnp.zeros_like(acc_ref)

    def fast_pv(kb, masked):
        kx = fk_ref[pl.ds(pl.multiple_of(kb * tk, tk), tk), :]
        vT = fvT_ref[kb]
        out = []
        for m in range(2):
            s = _dot(kx[:, m * LANES:(m + 1) * LANES], qt_ref[m * LANES:(m + 1) * LANES, :])
            if masked:
                causal = lax.broadcasted_iota(jnp.int32, (tk, tq), 0) + kb * tk <= q_pos
                s = jnp.where(causal, s, NEG_BIG)
            out.append(_dot(vT, jnp.exp(s).astype(BF16)))
        return out

    def accumulate(*pvs):
        for m in range(2):
            acc_ref[m] += functools.reduce(lambda a, b: a + b, [pv[m] for pv in pvs])

    def fast_pair(j, carry):
        accumulate(fast_pv(2 * j, False), fast_pv(2 * j + 1, False))
        return carry

    lax.fori_loop(0, n_full // 2, fast_pair, 0)
    assert n_diag == 1

    @pl.when(n_full % 2 == 1)
    def _():
        accumulate(fast_pv(n_full - 1, False), fast_pv(n_full, True))

    @pl.when(n_full % 2 == 0)
    def _():
        accumulate(fast_pv(n_full, True))

    def block(kb, masked):
        kx = fk_ref[pl.ds(pl.multiple_of(kb * tk, tk), tk), :]
        vT = fvT_ref[kb]
        qT = qt_ref[...]
        m_all = m_ref[...]
        new = []
        for m in range(2):
            s = _dot(kx[:, m * LANES:(m + 1) * LANES], qT[m * LANES:(m + 1) * LANES, :])
            if masked:
                causal = lax.broadcasted_iota(jnp.int32, (tk, tq), 0) + kb * tk <= q_pos
                s = jnp.where(causal, s, NEG_BIG)
            m_old = m_all[m]
            m_new = jnp.maximum(m_old, jnp.max(s, axis=0, keepdims=True))
            p = jnp.exp(s - m_new).astype(BF16)
            new.append((m_new, jnp.exp(m_old - m_new), _dot(vT, p)))
        for m in range(2):
            m_ref[m] = new[m][0]
            acc_ref[m] = new[m][1] * acc_ref[m] + new[m][2]

    def body(kb, carry):
        block(kb, False)
        return carry

    denom_ok = jnp.min(acc_ref[:, ONES_ROW:ONES_ROW + 1, :]) > L_MIN

    @pl.when(jnp.logical_not(denom_ok))
    def _():
        acc_ref[...] = jnp.zeros_like(acc_ref)
        m_ref[...] = jnp.full_like(m_ref, NEG_BIG)
        lax.fori_loop(0, n_full, body, 0)
        for d in range(n_diag):
            block(n_full + d, True)

    lv = lam_ref[...]
    lam = (jnp.exp(jnp.sum(lv[0:1] * lv[1:2], axis=-1, keepdims=True))
           - jnp.exp(jnp.sum(lv[2:3] * lv[3:4], axis=-1, keepdims=True)) + lam0)
    a1 = acc_ref[0]
    a2 = acc_ref[1]
    o = (a1[0:DIFF_V_DIM, :] / a1[ONES_ROW:ONES_ROW + 1, :]
         - lam * (a2[0:DIFF_V_DIM, :] / a2[ONES_ROW:ONES_ROW + 1, :]))
    o = o * lax.rsqrt(jnp.mean(o * o, axis=0, keepdims=True) + EPS) * gsub_ref[...] * (1.0 - lam0)
    o_ref[...] = o.T.astype(BF16)


def _diff_call(fqT, fk, fvT, kmax, lam_vecs, g_sub_col, tq, tk, lam0):
    B, S, _ = fk.shape
    nkb = S // tk
    kern = functools.partial(_diff_kernel, tq=tq, tk=tk, lam0=lam0)
    full = lambda a: pl.BlockSpec(a.shape, lambda b, h, i: (0,) * a.ndim)
    return pl.pallas_call(
        kern,
        out_shape=jax.ShapeDtypeStruct((B, S, DIFF_HEADS * DIFF_V_DIM), BF16),
        grid=(B, DIFF_HEADS, S // tq),
        in_specs=[pl.BlockSpec((None, 2 * LANES, tq), lambda b, h, i: (b, h, i)),
                  pl.BlockSpec((None, S, 2 * LANES), lambda b, h, i: (b, 0, h)),
                  pl.BlockSpec((None, None, nkb, VT_ROWS, tk), lambda b, h, i: (b, h, 0, 0, 0)),
                  pl.BlockSpec((None, nkb, 8, LANES), lambda b, h, i: (b, 0, 0, 0)),
                  full(lam_vecs), full(g_sub_col)],
        out_specs=pl.BlockSpec((None, tq, DIFF_V_DIM), lambda b, h, i: (b, i, h)),
        scratch_shapes=[pltpu.VMEM((2 * LANES, tq), BF16),
                        pltpu.VMEM((2, VT_ROWS, tq), F32),
                        pltpu.VMEM((2, 1, tq), F32)],
        compiler_params=pltpu.CompilerParams(
            dimension_semantics=("arbitrary", "arbitrary", "arbitrary"),
            vmem_limit_bytes=VMEM_LIMIT),
        name="diff",
    )(fqT, fk, fvT, kmax, lam_vecs, g_sub_col)


def _route_kernel(x_ref, od_ref, of_ref, mod_ref, wout_ref, g_ref, wr_hi_ref, wr_lo_ref, br_ref,
                  x1_ref, h_ref, gates_ref, *, tm):
    attn = (_dot(od_ref[...], wout_ref[0:W_DQ, :]) + _dot(of_ref[...], wout_ref[W_DQ:, :]))
    x1 = x_ref[...] + mod_ref[2:3, :] * attn
    x1_ref[...] = x1
    h = _rms(x1, g_ref[...]) * (1.0 + mod_ref[4:5, :]) + mod_ref[3:4, :]
    h_hi = h.astype(BF16)
    h_ref[...] = h_hi
    h_lo = (h - h_hi.astype(F32)).astype(BF16)
    w_hi = wr_hi_ref[...]
    logits = (_dot(h_hi, w_hi) + _dot(h_lo, w_hi) + _dot(h_hi, wr_lo_ref[...])) + br_ref[...]

    lane = lax.broadcasted_iota(jnp.int32, (tm, LANES), 1)
    big = jnp.int32(4 * LANES)
    neg = -jnp.inf
    is_g = (lane >= N_EXPERTS) & (lane < N_EXPERTS + N_GROUPS)
    gl = jnp.where(is_g, logits, neg)
    gmax = jnp.max(gl, axis=-1, keepdims=True)
    g_lane = jnp.min(jnp.where(gl == gmax, lane, big), axis=-1, keepdims=True)
    g_sel = g_lane - N_EXPERTS
    p_g = 1.0 / jnp.sum(jnp.where(is_g, jnp.exp(gl - gmax), 0.0), axis=-1, keepdims=True)
    in_grp = (lane < N_EXPERTS) & ((lane // EXPERTS_PER_GROUP) == g_sel)
    el = jnp.where(in_grp, logits, neg)
    v1 = jnp.max(el, axis=-1, keepdims=True)
    i1 = jnp.min(jnp.where(el == v1, lane, big), axis=-1, keepdims=True)
    el2 = jnp.where(lane == i1, neg, el)
    v2 = jnp.max(el2, axis=-1, keepdims=True)
    i2 = jnp.min(jnp.where(el2 == v2, lane, big), axis=-1, keepdims=True)
    e = jnp.exp(v2 - v1)
    w1 = 1.0 / (1.0 + e)
    w2 = e * w1
    gates_ref[...] = jnp.where(lane == i1, w1 * p_g, jnp.where(lane == i2, w2 * p_g, 0.0))


def _route_call(x, o_dsa, o_diff, mod3, w_out, g_moe, wr_hi, wr_lo, b_r, tm):
    B, S, D = x.shape
    kern = functools.partial(_route_kernel, tm=tm)
    tok = lambda w: pl.BlockSpec((None, tm, w), lambda b, i: (b, i, 0))
    full = lambda a: pl.BlockSpec(a.shape, lambda b, i: (0,) * a.ndim)
    return pl.pallas_call(
        kern,
        out_shape=[jax.ShapeDtypeStruct((B, S, D), F32), jax.ShapeDtypeStruct((B, S, D), BF16),
                   jax.ShapeDtypeStruct((B, S, LANES), F32)],
        grid=(B, S // tm),
        in_specs=[tok(D), tok(W_DQ), tok(DIFF_HEADS * DIFF_V_DIM),
                  pl.BlockSpec((None, 6, D), lambda b, i: (b, 0, 0)),
                  full(w_out), full(g_moe), full(wr_hi), full(wr_lo), full(b_r)],
        out_specs=[tok(D), tok(D), tok(LANES)],
        compiler_params=pltpu.CompilerParams(dimension_semantics=("arbitrary", "arbitrary"),
                                             vmem_limit_bytes=VMEM_LIMIT),
        name="route",
    )(x, o_dsa, o_diff, mod3, w_out, g_moe, wr_hi, wr_lo, b_r)


def _moe_kernel(h_ref, gates_ref, x1_ref, mod_ref, wg_ref, wu_ref, wd_ref, gf_ref, o_ref, acc_ref,
                *, tm, ec):
    j = pl.program_id(2)
    nc = pl.num_programs(2)

    @pl.when(j == 0)
    def _():
        acc_ref[...] = jnp.zeros_like(acc_ref)

    h = h_ref[...]
    hg = _dot(h, wg_ref[...])
    hu = _dot(h, wu_ref[...])
    hid = hg * jax.nn.sigmoid(hg) * hu
    gates = gates_ref[...]
    lane = lax.broadcasted_iota(jnp.int32, (tm, LANES), 1)
    parts = []
    for e in range(ec):
        gcol = jnp.sum(jnp.where(lane == j * ec + e, gates, 0.0), axis=-1, keepdims=True)
        parts.append((hid[:, e * D_EXPERT:(e + 1) * D_EXPERT] * gcol).astype(BF16))
    hs = jnp.concatenate(parts, axis=1)
    acc_ref[...] += _dot(hs, wd_ref[...])

    @pl.when(j == nc - 1)
    def _():
        x2 = x1_ref[...] + mod_ref[5:6, :] * acc_ref[...]
        o_ref[...] = _rms(x2, gf_ref[...])


def _moe_call(h, gates, x1, mod3, wg, wu, wd, g_final, tm, ec):
    B, S, D = x1.shape
    kern = functools.partial(_moe_kernel, tm=tm, ec=ec)
    tok = lambda w: pl.BlockSpec((None, tm, w), lambda b, i, j: (b, i, 0))
    return pl.pallas_call(
        kern,
        out_shape=jax.ShapeDtypeStruct((B, S, D), F32),
        grid=(B, S // tm, N_EXPERTS // ec),
        in_specs=[tok(D), tok(LANES), tok(D),
                  pl.BlockSpec((None, 6, D), lambda b, i, j: (b, 0, 0)),
                  pl.BlockSpec((D, ec * D_EXPERT), lambda b, i, j: (0, j)),
                  pl.BlockSpec((D, ec * D_EXPERT), lambda b, i, j: (0, j)),
                  pl.BlockSpec((ec * D_EXPERT, D), lambda b, i, j: (j, 0)),
                  pl.BlockSpec((1, D), lambda b, i, j: (0, 0))],
        out_specs=tok(D),
        scratch_shapes=[pltpu.VMEM((tm, D), F32)],
        compiler_params=pltpu.CompilerParams(
            dimension_semantics=("arbitrary", "arbitrary", "arbitrary"),
            vmem_limit_bytes=VMEM_LIMIT),
        name="moe",
    )(h, gates, x1, mod3, wg, wu, wd, g_final)


def _tile(n, pref):
    t = min(n, pref)
    assert n % t == 0, (n, t)
    return t


def kernel(x, c, w_ada, b_ada, g_attn, w_in, g_kv, w_uk, w_uv, lam_q1, lam_k1, lam_q2, lam_k2,
           g_sub, w_out, g_moe, w_group, b_group, w_router, b_router, w_gate, w_up, w_down, g_final):
    B, S, D = x.shape
    assert D == D_MODEL and w_ada.shape[0] == 1
    assert S % KEY_TILE == 0 and S <= POS_SPLIT * 256
    topk = min(TOPK_MAX, S // 4)
    l = 0
    lam0 = 0.8 - 0.6 * math.exp(-0.3 * l)

    mod3 = _mod_call(c, w_ada[l], b_ada[l].reshape(1, -1)).reshape(B, 6, D)

    wn, wt, w_pair = _proj_weights(w_in[l], w_uk[l])
    qabsT, iqT, iwT, ik4, ckv, ckvT, fqT, fk, fvT, kmax = _proj_call(
        x, mod3, g_attn[l].reshape(1, D), wn, wt, w_pair, g_kv[l].reshape(1, -1), KEY_TILE)

    lstrict = jnp.asarray(np.tril(np.ones((KEY_TILE, KEY_TILE), np.float32), -1), BF16)
    uvT = jnp.swapaxes(w_uv[l], 1, 2)
    z = jnp.zeros_like(uvT[0])
    wuvT_pair = jnp.stack([
        jnp.concatenate([jnp.concatenate([uvT[2 * p], z], axis=1),
                         jnp.concatenate([z, uvT[2 * p + 1]], axis=1)], axis=0)
        for p in range(DSA_HEADS // 2)]).astype(BF16)
    o_dsa = _dsa_call(qabsT, iqT, iwT, ik4, ckv, ckvT, kmax, lstrict, wuvT_pair, LANES, KEY_TILE, topk)

    lam_vecs = jnp.concatenate([lam_q1[l][None], lam_k1[l][None], lam_q2[l][None], lam_k2[l][None]],
                               axis=0).astype(F32)
    o_diff = _diff_call(fqT, fk, fvT, kmax, lam_vecs, g_sub[l].reshape(-1, 1), KEY_TILE, KEY_TILE, lam0)

    wr = jnp.pad(jnp.concatenate([w_router[l], w_group[l]], axis=1),
                 ((0, 0), (0, LANES - N_EXPERTS - N_GROUPS)))
    wr_hi = wr.astype(BF16)
    wr_lo = (wr - wr_hi.astype(F32)).astype(BF16)
    b_r = jnp.pad(jnp.concatenate([b_router[l], b_group[l]]), (0, LANES - N_EXPERTS - N_GROUPS))
    x1, h2, gates = _route_call(x, o_dsa, o_diff, mod3, w_out[l].astype(BF16),
                                g_moe[l].reshape(1, D), wr_hi, wr_lo, b_r.reshape(1, LANES),
                                _tile(S, 512))

    wg = jnp.swapaxes(w_gate[l], 0, 1).reshape(D, N_EXPERTS * D_EXPERT).astype(BF16)
    wu = jnp.swapaxes(w_up[l], 0, 1).reshape(D, N_EXPERTS * D_EXPERT).astype(BF16)
    wd = w_down[l].reshape(N_EXPERTS * D_EXPERT, D).astype(BF16)
    return _moe_call(h2, gates, x1, mod3, wg, wu, wd, g_final.reshape(1, D), _tile(S, 512), 4)
```

```python
import functools
import math

import jax
import jax.numpy as jnp
import numpy as np
from jax import lax
from jax.experimental import pallas as pl
from jax.experimental.pallas import tpu as pltpu

F32 = jnp.float32
BF16 = jnp.bfloat16

D_MODEL = 1024
DSA_HEADS = 8
DSA_HEAD_DIM = 64
DSA_LATENT = 128
IDX_HEADS = 4
IDX_DIM = 32
TOPK_MAX = 256
DIFF_HEADS = 4
DIFF_QK_DIM = 64
DIFF_V_DIM = 128
N_GROUPS = 4
EXPERTS_PER_GROUP = 8
N_EXPERTS = 32
D_EXPERT = 256
EPS = 1e-6

LANES = 128
BF16_ROWS = 16
POS_SPLIT = 64
NEG_BIG = -1e30
VMEM_LIMIT = 56 * 1024 * 1024
KEY_TILE = 512
VALUE_BISECT_STEPS = 24
UNCHECKED_BISECT_STEPS = 16
COUNT_CHAINS = 4
ATTN_CHAINS = 1

W_DQ = DSA_HEADS * DSA_HEAD_DIM
W_QABS = DSA_HEADS * DSA_LATENT
W_CKV = 2 * LANES
VT_ROWS = DSA_LATENT + BF16_ROWS
ONES_ROW = DSA_LATENT + 2
N_FEATS = 5
KMAX_LANE_CKV = 8
L_MIN = 1e-30
BF16_ROUND_UP = 1.0 + 2.0 ** -7
BOUND_SLACK = 2.0 ** -6
CODE_BISECT_STEPS = 34
TOKEN_TILE = 512
DSA_QUERY_TILE = 256
MOE_EXPERTS_PER_STEP = 4
W_FQ = DIFF_HEADS * 2 * LANES
W_DIFF_QK = DIFF_HEADS * 2 * DIFF_QK_DIM

T_DQ = 0
T_DLAT = T_DQ + W_DQ
T_IQ = T_DLAT + DSA_LATENT
T_IW = T_IQ + IDX_HEADS * IDX_DIM
T_FQ = T_IW + BF16_ROWS
T_FV = T_FQ + W_DIFF_QK
T_ROWS = T_FV + DIFF_HEADS * DIFF_V_DIM
N_IK = 0
N_DLAT = N_IK + LANES
N_FK = N_DLAT + DSA_LATENT
N_COLS = N_FK + W_FQ


def _alibi_slopes(n):
    return [2.0 ** (-8.0 * (i + 1) / n) for i in range(n)]


def _rms(x, g):
    return x * lax.rsqrt(jnp.mean(x * x, axis=-1, keepdims=True) + EPS) * g


def _dot(a, b):
    return jnp.dot(a, b, preferred_element_type=F32)


def _mod_kernel(c_ref, w_ref, b_ref, o_ref):
    c = c_ref[...]
    act = c * jax.nn.sigmoid(c)
    o_ref[...] = jnp.dot(act, w_ref[...], preferred_element_type=F32,
                         precision=lax.Precision.HIGHEST) + b_ref[...]


def _mod_call(c, w_ada, b_ada):
    B, D = c.shape
    n = w_ada.shape[1] // D
    return pl.pallas_call(
        _mod_kernel,
        out_shape=jax.ShapeDtypeStruct((B, n * D), F32),
        grid=(n,),
        in_specs=[pl.BlockSpec((B, D), lambda j: (0, 0)),
                  pl.BlockSpec((D, D), lambda j: (0, j)),
                  pl.BlockSpec((1, D), lambda j: (0, j))],
        out_specs=pl.BlockSpec((B, D), lambda j: (0, j)),
        compiler_params=pltpu.CompilerParams(dimension_semantics=("arbitrary",),
                                             vmem_limit_bytes=VMEM_LIMIT),
        name="mod",
    )(c, w_ada, b_ada)


def _key_feature_rows(pos_row, n_rows):
    r = lax.broadcasted_iota(jnp.int32, (n_rows, pos_row.shape[1]), 0)
    pa = (pos_row // POS_SPLIT).astype(F32)
    pb = (pos_row % POS_SPLIT).astype(F32)
    return jnp.where(r == 0, pa, jnp.where(r == 1, pb, jnp.where(r < N_FEATS, 1.0, 0.0)))


def _query_feature_rows(pos_row, n_rows, slope, shift):
    r = lax.broadcasted_iota(jnp.int32, (n_rows, pos_row.shape[1]), 0)
    pa = (pos_row // POS_SPLIT).astype(F32)
    pb = (pos_row % POS_SPLIT).astype(F32)
    return jnp.where(r == 0, POS_SPLIT * slope,
                     jnp.where(r == 1, slope,
                               jnp.where(r == 2, -POS_SPLIT * slope * pa,
                                         jnp.where(r == 3, -slope * pb,
                                                   jnp.where(r == 4, -shift, 0.0)))))


def _shift_bound(qn2, kmax2, extra):
    b = jnp.sqrt(qn2 * kmax2) + extra
    return b + jnp.abs(b) * BOUND_SLACK + BOUND_SLACK


def _tile_max(kmax_ref, last_tile, lane_idx):
    x = kmax_ref[...]
    t = lax.broadcasted_iota(jnp.int32, x.shape, 0)
    ln = lax.broadcasted_iota(jnp.int32, x.shape, 2)
    x = jnp.where((t <= last_tile) & (ln == lane_idx), x, 0.0)
    return jnp.max(jnp.max(x, axis=0), axis=1, keepdims=True)[0:1, :]


def _proj_kernel(x_ref, mod_ref, g_ref, wn_ref, wt_ref, wpair_ref, gkv_ref, gkvc_ref, fc_ref, sel_ref,
                 qabsT_ref, iqT_ref, iwT_ref, ik_ref, ckv_ref, ckvT_ref, fqT_ref, fk_ref, fvT_ref,
                 kmax_ref, *, tm):
    x = x_ref[...]
    h = _rms(x, g_ref[...]) * (1.0 + mod_ref[1:2, :]) + mod_ref[0:1, :]
    hb = h.astype(BF16)
    hT = h.T.astype(BF16)

    base = pl.program_id(1) * tm
    pos_c = base + lax.broadcasted_iota(jnp.int32, (tm, 1), 0)
    pos_r = base + lax.broadcasted_iota(jnp.int32, (1, tm), 1)
    pa = (pos_c // POS_SPLIT).astype(F32)
    pb = (pos_c % POS_SPLIT).astype(F32)

    def feats(row, width):
        return (fc_ref[row:row + 1, 0:width] + fc_ref[row + 1:row + 2, 0:width] * pa
                + fc_ref[row + 2:row + 3, 0:width] * pb)

    ik_ref[...] = _dot(hb, wn_ref[:, N_IK:N_IK + LANES]).astype(BF16)
    dlat = _dot(hb, wn_ref[:, N_DLAT:N_DLAT + DSA_LATENT])
    ckv_b = _rms(dlat, gkv_ref[...]).astype(BF16)
    ckv_ref[:, 0:LANES] = ckv_b
    ckv_ref[:, LANES:2 * LANES] = feats(0, LANES).astype(BF16)
    fk = _dot(hb, wn_ref[:, N_FK:N_FK + W_FQ])
    fk_ref[...] = (fk + feats(3, W_FQ)).astype(BF16)
    sq = jnp.concatenate([fk.astype(BF16).astype(F32), ckv_b.astype(F32)], axis=1)
    sq_up = (sq * sq * BF16_ROUND_UP).astype(BF16)
    kmax_ref[...] = jnp.broadcast_to(jnp.max(_dot(sq_up, sel_ref[...]), axis=0, keepdims=True),
                                     (8, LANES))

    dqT = _dot(wt_ref[T_DQ:T_DQ + W_DQ, :], hT).astype(BF16)
    for p in range(DSA_HEADS // 2):
        qa = _dot(wpair_ref[p], dqT[p * LANES:(p + 1) * LANES, :])
        qabsT_ref[p * 2 * LANES:(p + 1) * 2 * LANES, :] = qa.astype(BF16)
    dlatT = _dot(wt_ref[T_DLAT:T_DLAT + DSA_LATENT, :], hT)
    inv = lax.rsqrt(jnp.mean(dlatT * dlatT, axis=0, keepdims=True) + EPS)
    kfeat = _key_feature_rows(pos_r, BF16_ROWS).astype(BF16)
    ckvT_ref[0:DSA_LATENT, :] = (dlatT * inv * gkvc_ref[...]).astype(BF16)
    ckvT_ref[DSA_LATENT:VT_ROWS, :] = kfeat
    iqT_ref[...] = _dot(wt_ref[T_IQ:T_IQ + IDX_HEADS * IDX_DIM, :], hT).astype(BF16)
    iwT = _dot(wt_ref[T_IW:T_IW + BF16_ROWS, :], hT)
    iwT_ref[...] = iwT[0:8, :] * (IDX_HEADS ** -0.5 * IDX_DIM ** -0.5)
    fqT = _dot(wt_ref[T_FQ:T_FQ + W_DIFF_QK, :], hT)
    zpad = jnp.zeros((LANES - DIFF_QK_DIM, tm), BF16)
    for s in range(DIFF_HEADS * 2):
        r0 = s * LANES
        fqT_ref[r0:r0 + DIFF_QK_DIM, :] = fqT[s * DIFF_QK_DIM:(s + 1) * DIFF_QK_DIM, :].astype(BF16)
        fqT_ref[r0 + DIFF_QK_DIM:r0 + LANES, :] = zpad
    fvT = _dot(wt_ref[T_FV:T_FV + DIFF_HEADS * DIFF_V_DIM, :], hT)
    for hh in range(DIFF_HEADS):
        fvT_ref[hh, 0:DIFF_V_DIM, :] = fvT[hh * DIFF_V_DIM:(hh + 1) * DIFF_V_DIM, :].astype(BF16)
        fvT_ref[hh, DIFF_V_DIM:VT_ROWS, :] = kfeat


def _feature_consts():
    fc = np.zeros((8, W_FQ), np.float32)
    fc[1, 0] = 1.0
    fc[2, 1] = 1.0
    fc[0, 2:N_FEATS] = 1.0
    for s in range(DIFF_HEADS * 2):
        base = s * LANES + DIFF_QK_DIM
        fc[4, base + 0] = 1.0
        fc[5, base + 1] = 1.0
        fc[3, base + 2:base + N_FEATS] = 1.0
    return jnp.asarray(fc)


def _norm_selector():
    sel = np.zeros((W_FQ + DSA_LATENT, LANES), np.float32)
    for s in range(DIFF_HEADS * 2):
        sel[s * LANES:s * LANES + DIFF_QK_DIM, s] = 1.0
    sel[W_FQ:, KMAX_LANE_CKV] = 1.0
    return jnp.asarray(sel, BF16)


def _proj_weights(w_in, w_uk):
    D = w_in.shape[0]
    pts = np.cumsum([W_DQ, DSA_LATENT, IDX_HEADS * IDX_DIM, IDX_DIM, IDX_HEADS,
                     W_DIFF_QK, W_DIFF_QK])
    dq, dlat, iq, ik, iw, fq, fk, fv = jnp.split(w_in, list(pts), axis=1)
    ik4 = jnp.tile(ik, (1, IDX_HEADS))
    fke = jnp.pad(fk.reshape(D, DIFF_HEADS * 2, DIFF_QK_DIM),
                  ((0, 0), (0, 0), (0, LANES - DIFF_QK_DIM))).reshape(D, W_FQ)
    wn = jnp.concatenate([ik4, dlat, fke], axis=1).astype(BF16)
    iwp = jnp.pad(iw, ((0, 0), (0, BF16_ROWS - IDX_HEADS)))
    wt = jnp.concatenate([dq, dlat, iq, iwp, fq * (DIFF_QK_DIM ** -0.5), fv], axis=1).T.astype(BF16)
    uk = w_uk * (DSA_HEAD_DIM ** -0.5)
    z = jnp.zeros_like(uk[0])
    pairs = [jnp.concatenate([jnp.concatenate([uk[2 * p], z], axis=1),
                              jnp.concatenate([z, uk[2 * p + 1]], axis=1)], axis=0)
             for p in range(DSA_HEADS // 2)]
    return wn, wt, jnp.stack(pairs).astype(BF16)


def _proj_call(x, mod3, g_attn, wn, wt, w_pair, g_kv, tm):
    B, S, D = x.shape
    nt = S // tm
    fc = _feature_consts()
    sel = _norm_selector()
    kern = functools.partial(_proj_kernel, tm=tm)
    tok = lambda w: pl.BlockSpec((None, tm, w), lambda b, i: (b, i, 0))
    tokT = lambda r: pl.BlockSpec((None, r, tm), lambda b, i: (b, 0, i))
    full = lambda a: pl.BlockSpec(a.shape, lambda b, i: (0,) * a.ndim)
    g_kv_col = g_kv.reshape(-1, 1)
    out_shape = [jax.ShapeDtypeStruct((B, W_QABS, S), BF16),
                 jax.ShapeDtypeStruct((B, LANES, S), BF16),
                 jax.ShapeDtypeStruct((B, 8, S), F32),
                 jax.ShapeDtypeStruct((B, S, LANES), BF16),
                 jax.ShapeDtypeStruct((B, S, W_CKV), BF16),
                 jax.ShapeDtypeStruct((B, nt, VT_ROWS, tm), BF16),
                 jax.ShapeDtypeStruct((B, W_FQ, S), BF16),
                 jax.ShapeDtypeStruct((B, S, W_FQ), BF16),
                 jax.ShapeDtypeStruct((B, DIFF_HEADS, nt, VT_ROWS, tm), BF16),
                 jax.ShapeDtypeStruct((B, nt, 8, LANES), F32)]
    out_specs = [tokT(W_QABS), tokT(LANES), tokT(8), tok(LANES), tok(W_CKV),
                 pl.BlockSpec((None, None, VT_ROWS, tm), lambda b, i: (b, i, 0, 0)),
                 tokT(W_FQ), tok(W_FQ),
                 pl.BlockSpec((None, DIFF_HEADS, None, VT_ROWS, tm), lambda b, i: (b, 0, i, 0, 0)),
                 pl.BlockSpec((None, None, 8, LANES), lambda b, i: (b, i, 0, 0))]
    return pl.pallas_call(
        kern,
        out_shape=out_shape,
        grid=(B, nt),
        in_specs=[tok(D),
                  pl.BlockSpec((None, 6, D), lambda b, i: (b, 0, 0)),
                  full(g_attn), full(wn), full(wt), full(w_pair), full(g_kv), full(g_kv_col),
                  full(fc), full(sel)],
        out_specs=out_specs,
        compiler_params=pltpu.CompilerParams(dimension_semantics=("arbitrary", "arbitrary"),
                                             vmem_limit_bytes=VMEM_LIMIT),
        name="proj",
    )(x, mod3, g_attn, wn, wt, w_pair, g_kv, g_kv_col, fc, sel)


def _float_code(x):
    b = lax.bitcast_convert_type(x, jnp.int32)
    return b ^ (lax.shift_right_arithmetic(b, 31) & jnp.int32(0x7FFFFFFF))


def _float_decode(c):
    b = c ^ (lax.shift_right_arithmetic(c, 31) & jnp.int32(0x7FFFFFFF))
    return lax.bitcast_convert_type(b, F32)


def _dsa_kernel(qabsT_ref, iqT_ref, iwT_ref, ik_ref, ckv_ref, ckvT_ref, kmax_ref, lstrict_ref, wuvT_ref,
                o_ref, sc_ref, qst_ref, qt_ref, acc_ref, m_ref, tmp_ref, *, tq, tk, topk):
    qi = pl.program_id(1)
    nkb = (qi * tq) // tk + 1
    kf = float(topk)
    slopes = _alibi_slopes(DSA_HEADS)
    q_pos = qi * tq + lax.broadcasted_iota(jnp.int32, (1, tq), 1)

    iqT = iqT_ref[...]
    rowi = lax.broadcasted_iota(jnp.int32, (LANES, tq), 0)
    for hh in range(IDX_HEADS):
        qst_ref[:, hh * tq:(hh + 1) * tq] = jnp.where((rowi // IDX_DIM) == hh, iqT, jnp.zeros_like(iqT))
    iw = iwT_ref[...]
    wrow = [iw[hh:hh + 1, :] for hh in range(IDX_HEADS)]

    def scores(kb):
        kblk = ik_ref[pl.ds(pl.multiple_of(kb * tk, tk), tk), :]
        a = _dot(kblk, qst_ref[...])
        sc = jnp.maximum(a[:, 0:tq], 0.0) * wrow[0]
        for hh in range(1, IDX_HEADS):
            sc = sc + jnp.maximum(a[:, hh * tq:(hh + 1) * tq], 0.0) * wrow[hh]
        return sc

    def score_body(kb, carry):
        mn, mx = carry
        sc = scores(kb)
        sc_ref[kb] = sc
        return (jnp.minimum(mn, jnp.min(sc, axis=0, keepdims=True)),
                jnp.maximum(mx, jnp.max(sc, axis=0, keepdims=True)))

    last = nkb - 1
    mn, mx = lax.fori_loop(0, last // 2, lambda j, c: score_body(2 * j + 1, score_body(2 * j, c)),
                           (jnp.full((1, tq), jnp.inf, F32), jnp.full((1, tq), -jnp.inf, F32)))
    mn, mx = lax.cond(last % 2 == 1, lambda c: score_body(last - 1, c), lambda c: c, (mn, mx))
    sc = scores(last)
    causal = lax.broadcasted_iota(jnp.int32, (tk, tq), 0) + last * tk <= q_pos
    sc_ref[last] = jnp.where(causal, sc, jnp.nan)
    mn = jnp.minimum(mn, jnp.min(jnp.where(causal, sc, jnp.inf), axis=0, keepdims=True))
    mx = jnp.maximum(mx, jnp.max(jnp.where(causal, sc, -jnp.inf), axis=0, keepdims=True))

    def block_count(kb, pred):
        parts = [None] * COUNT_CHAINS
        for r in range(tk // 8):
            v = jnp.where(pred(sc_ref[kb, r * 8:(r + 1) * 8, :]), 1.0, 0.0)
            c = r % COUNT_CHAINS
            parts[c] = v if parts[c] is None else parts[c] + v
        return (parts[0] + parts[1]) + (parts[2] + parts[3])

    def count(pred):
        acc = lax.fori_loop(0, nkb, lambda kb, acc: acc + block_count(kb, pred),
                            jnp.zeros((8, tq), F32))
        return jnp.sum(acc, axis=0, keepdims=True)

    n_causal = (q_pos + 1).astype(F32)
    done0 = n_causal <= kf

    def bisect(st, value_mid):
        lo, hi, c_lo, theta, done = st
        th8 = jnp.broadcast_to(theta, (8, tq))
        c = count(lambda v: v >= th8)
        ge = c >= kf
        live = done == 0.0
        lo = jnp.where(live & ge, theta, lo)
        c_lo = jnp.where(live & ge, c, c_lo)
        hi = jnp.where(live & jnp.logical_not(ge), theta, hi)
        if value_mid:
            nxt = 0.5 * lo + 0.5 * hi
        else:
            cl, ch = _float_code(lo), _float_code(hi)
            nxt = _float_decode((cl & ch) + lax.shift_right_arithmetic(cl ^ ch, 1))
        inside = (nxt > lo) & (nxt < hi)
        done = jnp.where((c_lo == kf) | jnp.logical_not(inside), 1.0, done)
        return lo, hi, c_lo, nxt, done

    def checked(value_mid, max_steps):
        def cond(c):
            return jnp.logical_and(c[1] > 0.0, c[2] < max_steps)

        def body(c):
            st = bisect(bisect(c[0], value_mid), value_mid)
            return st, jnp.sum(1.0 - st[4]), c[2] + 2
        return cond, body

    zero8 = jnp.zeros((8, tq), F32)
    c_ge0 = count(lambda v: v >= zero8)
    c_gt0 = count(lambda v: v > zero8)
    live0 = jnp.logical_not(done0)
    at0 = live0 & (c_ge0 >= kf)
    tie0 = at0 & (c_gt0 < kf)
    below0 = live0 & (c_ge0 < kf)
    lo0 = jnp.where(at0, 0.0, mn)
    hi0 = jnp.where(below0, 0.0, mx)
    st = (lo0, hi0, jnp.where(at0, c_ge0, n_causal),
          jnp.where(below0, 0.5 * lo0 + 0.5 * hi0, mx), jnp.where(done0 | tie0, 1.0, 0.0))
    st = lax.fori_loop(0, UNCHECKED_BISECT_STEPS, lambda i, s: bisect(s, True), st)
    cond, body = checked(True, VALUE_BISECT_STEPS - UNCHECKED_BISECT_STEPS)
    st, active, _ = lax.while_loop(cond, body, (st, jnp.sum(1.0 - st[4]), jnp.int32(0)))
    cond, body = checked(False, CODE_BISECT_STEPS)
    st, _, _ = lax.while_loop(cond, body, (st, active, jnp.int32(0)))
    tau, c_ge = st[0], st[2]

    over = c_ge > kf

    @pl.when(jnp.max(c_ge) > kf)
    def _():
        tau8 = jnp.broadcast_to(tau, (8, tq))
        tmp_ref[0:1, :] = c_gt0

        @pl.when(jnp.max(jnp.where(over & jnp.logical_not(tie0), 1.0, 0.0)) > 0.0)
        def _():
            tmp_ref[0:1, :] = jnp.where(tie0, c_gt0, count(lambda v: v > tau8))

        quota = kf - tmp_ref[0:1, :]

        def tie_body(kb, seen):
            cnt = jnp.sum(block_count(kb, lambda v: v == tau8), axis=0, keepdims=True)
            inside = over & (seen < quota) & (seen + cnt > quota)
            gone = over & (seen >= quota) & (cnt > 0.0)
            flag = jnp.max(jnp.where(inside, 2.0, jnp.where(gone, 1.0, 0.0)))

            @pl.when(flag > 1.5)
            def _():
                s = sc_ref[kb]
                eq = s == tau
                rank = _dot(lstrict_ref[...], jnp.where(eq, 1.0, 0.0).astype(BF16)) + seen
                sc_ref[kb] = jnp.where(eq & over & (rank >= quota), jnp.nan, s)

            @pl.when(flag == 1.0)
            def _():
                s = sc_ref[kb]
                sc_ref[kb] = jnp.where((s == tau) & gone, jnp.nan, s)

            return seen + cnt

        lax.fori_loop(0, nkb, tie_body, jnp.zeros((1, tq), F32))

    tau8 = jnp.broadcast_to(tau, (8, tq))
    sub = lax.broadcasted_iota(jnp.int32, (8, tq), 0).astype(F32)

    def nearest_body(kb, acc):
        parts = [None] * COUNT_CHAINS
        for r in range(tk // 8):
            idx = sub + (kb * tk + r * 8).astype(F32)
            v = jnp.where(sc_ref[kb, r * 8:(r + 1) * 8, :] >= tau8, idx, -1.0)
            c = r % COUNT_CHAINS
            parts[c] = v if parts[c] is None else jnp.maximum(parts[c], v)
        return jnp.maximum(acc, jnp.maximum(jnp.maximum(parts[0], parts[1]),
                                            jnp.maximum(parts[2], parts[3])))

    last_sel = jnp.max(lax.fori_loop(0, nkb, nearest_body, jnp.full((8, tq), -1.0, F32)),
                       axis=0, keepdims=True)
    d_min = q_pos.astype(F32) - last_sel
    kmax2 = _tile_max(kmax_ref, nkb - 1, KMAX_LANE_CKV)
    for hh in range(DSA_HEADS):
        qh = qabsT_ref[hh * DSA_LATENT:(hh + 1) * DSA_LATENT, :]
        qf = qh.astype(F32)
        shift = _shift_bound(jnp.sum(qf * qf, axis=0, keepdims=True), kmax2, -slopes[hh] * d_min)
        qt_ref[0:DSA_LATENT, hh * tq:(hh + 1) * tq] = qh
        qt_ref[DSA_LATENT:VT_ROWS, hh * tq:(hh + 1) * tq] = _query_feature_rows(
            q_pos, BF16_ROWS, slopes[hh], shift).astype(BF16)
    qt_ref[VT_ROWS:2 * DSA_LATENT, :] = jnp.zeros((2 * DSA_LATENT - VT_ROWS, DSA_HEADS * tq), BF16)
    acc_ref[...] = jnp.zeros_like(acc_ref)

    def fast_pv(kb):
        kx = ckv_ref[pl.ds(pl.multiple_of(kb * tk, tk), tk), :]
        s = _dot(kx, qt_ref[...])
        keep = sc_ref[kb] >= tau
        ps = [jnp.exp(jnp.where(keep, s[:, hh * tq:(hh + 1) * tq], NEG_BIG)).astype(BF16)
              for hh in range(DSA_HEADS)]
        return _dot(ckvT_ref[kb], jnp.concatenate(ps, axis=1))

    def fast_quad(j, carry):
        acc_ref[...] += ((fast_pv(4 * j) + fast_pv(4 * j + 1))
                         + (fast_pv(4 * j + 2) + fast_pv(4 * j + 3)))
        return carry

    lax.fori_loop(0, nkb // 4, fast_quad, 0)
    rem = nkb % 4

    @pl.when(rem >= 2)
    def _():
        acc_ref[...] += fast_pv(nkb - rem) + fast_pv(nkb - rem + 1)

    @pl.when(rem % 2 == 1)
    def _():
        acc_ref[...] += fast_pv(nkb - 1)

    def attn_body(kb, carry):
        kx = ckv_ref[pl.ds(pl.multiple_of(kb * tk, tk), tk), :]
        vT = ckvT_ref[kb]
        keep = sc_ref[kb] >= tau
        qt = qt_ref[...]
        m_all = m_ref[...]
        m_out, alphas, pvs = [], [], []
        for ch in range(ATTN_CHAINS):
            hpc = DSA_HEADS // ATTN_CHAINS
            l0 = ch * hpc * tq
            s = _dot(kx, qt[:, l0:l0 + hpc * tq])
            ps = []
            for j in range(hpc):
                c0 = l0 + j * tq
                sh = jnp.where(keep, s[:, j * tq:(j + 1) * tq], NEG_BIG)
                m_old = m_all[:, c0:c0 + tq]
                m_new = jnp.maximum(m_old, jnp.max(sh, axis=0, keepdims=True))
                ps.append(jnp.exp(sh - m_new).astype(BF16))
                alphas.append(jnp.exp(m_old - m_new))
                m_out.append(m_new)
            pvs.append(_dot(vT, jnp.concatenate(ps, axis=1)))
        m_ref[...] = jnp.concatenate(m_out, axis=1)
        acc_ref[...] = jnp.concatenate(alphas, axis=1) * acc_ref[...] + jnp.concatenate(pvs, axis=1)
        return carry

    denom_ok = jnp.min(acc_ref[ONES_ROW:ONES_ROW + 1, :]) > L_MIN

    @pl.when(jnp.logical_not(denom_ok))
    def _():
        acc_ref[...] = jnp.zeros_like(acc_ref)
        m_ref[...] = jnp.full_like(m_ref, NEG_BIG)
        lax.fori_loop(0, nkb, attn_body, 0)

    acc = acc_ref[...]
    lat = acc[0:DSA_LATENT, :] / acc[ONES_ROW:ONES_ROW + 1, :]
    for p in range(DSA_HEADS // 2):
        pair = lat[:, 2 * p * tq:(2 * p + 2) * tq]
        pair = jnp.concatenate([pair[:, 0:tq], pair[:, tq:2 * tq]], axis=0).astype(BF16)
        o_ref[:, p * LANES:(p + 1) * LANES] = _dot(wuvT_ref[p], pair).T.astype(BF16)


def _dsa_call(qabsT, iqT, iwT, ik4, ckv, ckvT, kmax, lstrict, wuvT_pair, tq, tk, topk):
    B, _, S = qabsT.shape
    nkb = S // tk
    kern = functools.partial(_dsa_kernel, tq=tq, tk=tk, topk=topk)
    blkT = lambda r: pl.BlockSpec((None, r, tq), lambda b, i: (b, 0, i))
    per_b = lambda w: pl.BlockSpec((None, S, w), lambda b, i: (b, 0, 0))
    full = lambda a: pl.BlockSpec(a.shape, lambda b, i: (0,) * a.ndim)
    return pl.pallas_call(
        kern,
        out_shape=jax.ShapeDtypeStruct((B, S, W_DQ), BF16),
        grid=(B, S // tq),
        in_specs=[blkT(W_QABS), blkT(LANES), blkT(8), per_b(LANES), per_b(W_CKV),
                  pl.BlockSpec((None, nkb, VT_ROWS, tk), lambda b, i: (b, 0, 0, 0)),
                  pl.BlockSpec((None, nkb, 8, LANES), lambda b, i: (b, 0, 0, 0)),
                  full(lstrict), full(wuvT_pair)],
        out_specs=pl.BlockSpec((None, tq, W_DQ), lambda b, i: (b, i, 0)),
        scratch_shapes=[pltpu.VMEM((nkb, tk, tq), F32),
                        pltpu.VMEM((LANES, IDX_HEADS * tq), BF16),
                        pltpu.VMEM((2 * DSA_LATENT, DSA_HEADS * tq), BF16),
                        pltpu.VMEM((VT_ROWS, DSA_HEADS * tq), F32),
                        pltpu.VMEM((1, DSA_HEADS * tq), F32),
                        pltpu.VMEM((8, tq), F32)],
        compiler_params=pltpu.CompilerParams(dimension_semantics=("arbitrary", "arbitrary"),
                                             vmem_limit_bytes=VMEM_LIMIT),
        name="dsa",
    )(qabsT, iqT, iwT, ik4, ckv, ckvT, kmax, lstrict, wuvT_pair)


def _diff_kernel(fqT_ref, fk_ref, fvT_ref, kmax_ref, lam_ref, gsub_ref, o_ref, qt_ref, acc_ref, m_ref,
                 *, tq, tk, lam0):
    hd = pl.program_id(1)
    qi = pl.program_id(2)
    q_pos = qi * tq + lax.broadcasted_iota(jnp.int32, (1, tq), 1)
    n_full = (qi * tq) // tk
    n_diag = (tq + tk - 1) // tk
    slopes = _alibi_slopes(DIFF_HEADS)
    slope = jnp.float32(slopes[DIFF_HEADS - 1])
    for i in range(DIFF_HEADS - 1):
        slope = jnp.where(hd == i, slopes[i], slope)

    for m in range(2):
        r0 = m * LANES
        qh = fqT_ref[r0:r0 + DIFF_QK_DIM, :]
        qf = qh.astype(F32)
        shift = _shift_bound(jnp.sum(qf * qf, axis=0, keepdims=True),
                             _tile_max(kmax_ref, n_full + n_diag - 1, 2 * hd + m), 0.0)
        qt_ref[r0:r0 + DIFF_QK_DIM, :] = qh
        qt_ref[r0 + DIFF_QK_DIM:r0 + DIFF_QK_DIM + BF16_ROWS, :] = _query_feature_rows(
            q_pos, BF16_ROWS, slope, shift).astype(BF16)
        qt_ref[r0 + DIFF_QK_DIM + BF16_ROWS:r0 + LANES, :] = jnp.zeros(
            (LANES - DIFF_QK_DIM - BF16_ROWS, tq), BF16)
    acc_ref[...] = jnp.zeros_like(acc_ref)

    def fast_pv(kb, masked):
        kx = fk_ref[pl.ds(pl.multiple_of(kb * tk, tk), tk), :]
        vT = fvT_ref[kb]
        out = []
        for m in range(2):
            s = _dot(kx[:, m * LANES:(m + 1) * LANES], qt_ref[m * LANES:(m + 1) * LANES, :])
            if masked:
                causal = lax.broadcasted_iota(jnp.int32, (tk, tq), 0) + kb * tk <= q_pos
                s = jnp.where(causal, s, NEG_BIG)
            out.append(_dot(vT, jnp.exp(s).astype(BF16)))
        return out

    def accumulate(*pvs):
        for m in range(2):
            acc_ref[m] += functools.reduce(lambda a, b: a + b, [pv[m] for pv in pvs])

    def fast_pair(j, carry):
        accumulate(fast_pv(2 * j, False), fast_pv(2 * j + 1, False))
        return carry

    lax.fori_loop(0, n_full // 2, fast_pair, 0)
    assert n_diag == 1

    @pl.when(n_full % 2 == 1)
    def _():
        accumulate(fast_pv(n_full - 1, False), fast_pv(n_full, True))

    @pl.when(n_full % 2 == 0)
    def _():
        accumulate(fast_pv(n_full, True))

    def block(kb, masked):
        kx = fk_ref[pl.ds(pl.multiple_of(kb * tk, tk), tk), :]
        vT = fvT_ref[kb]
        qT = qt_ref[...]
        m_all = m_ref[...]
        new = []
        for m in range(2):
            s = _dot(kx[:, m * LANES:(m + 1) * LANES], qT[m * LANES:(m + 1) * LANES, :])
            if masked:
                causal = lax.broadcasted_iota(jnp.int32, (tk, tq), 0) + kb * tk <= q_pos
                s = jnp.where(causal, s, NEG_BIG)
            m_old = m_all[m]
            m_new = jnp.maximum(m_old, jnp.max(s, axis=0, keepdims=True))
            p = jnp.exp(s - m_new).astype(BF16)
            new.append((m_new, jnp.exp(m_old - m_new), _dot(vT, p)))
        for m in range(2):
            m_ref[m] = new[m][0]
            acc_ref[m] = new[m][1] * acc_ref[m] + new[m][2]

    def body(kb, carry):
        block(kb, False)
        return carry

    denom_ok = jnp.min(acc_ref[:, ONES_ROW:ONES_ROW + 1, :]) > L_MIN

    @pl.when(jnp.logical_not(denom_ok))
    def _():
        acc_ref[...] = jnp.zeros_like(acc_ref)
        m_ref[...] = jnp.full_like(m_ref, NEG_BIG)
        lax.fori_loop(0, n_full, body, 0)
        for d in range(n_diag):
            block(n_full + d, True)

    lv = lam_ref[...]
    lam = (jnp.exp(jnp.sum(lv[0:1] * lv[1:2], axis=-1, keepdims=True))
           - jnp.exp(jnp.sum(lv[2:3] * lv[3:4], axis=-1, keepdims=True)) + lam0)
    a1 = acc_ref[0]
    a2 = acc_ref[1]
    o = (a1[0:DIFF_V_DIM, :] / a1[ONES_ROW:ONES_ROW + 1, :]
         - lam * (a2[0:DIFF_V_DIM, :] / a2[ONES_ROW:ONES_ROW + 1, :]))
    o = o * lax.rsqrt(jnp.mean(o * o, axis=0, keepdims=True) + EPS) * gsub_ref[...] * (1.0 - lam0)
    o_ref[...] = o.T.astype(BF16)


def _diff_call(fqT, fk, fvT, kmax, lam_vecs, g_sub_col, tq, tk, lam0):
    B, S, _ = fk.shape
    nkb = S // tk
    kern = functools.partial(_diff_kernel, tq=tq, tk=tk, lam0=lam0)
    full = lambda a: pl.BlockSpec(a.shape, lambda b, h, i: (0,) * a.ndim)
    return pl.pallas_call(
        kern,
        out_shape=jax.ShapeDtypeStruct((B, S, DIFF_HEADS * DIFF_V_DIM), BF16),
        grid=(B, DIFF_HEADS, S // tq),
        in_specs=[pl.BlockSpec((None, 2 * LANES, tq), lambda b, h, i: (b, h, i)),
                  pl.BlockSpec((None, S, 2 * LANES), lambda b, h, i: (b, 0, h)),
                  pl.BlockSpec((None, None, nkb, VT_ROWS, tk), lambda b, h, i: (b, h, 0, 0, 0)),
                  pl.BlockSpec((None, nkb, 8, LANES), lambda b, h, i: (b, 0, 0, 0)),
                  full(lam_vecs), full(g_sub_col)],
        out_specs=pl.BlockSpec((None, tq, DIFF_V_DIM), lambda b, h, i: (b, i, h)),
        scratch_shapes=[pltpu.VMEM((2 * LANES, tq), BF16),
                        pltpu.VMEM((2, VT_ROWS, tq), F32),
                        pltpu.VMEM((2, 1, tq), F32)],
        compiler_params=pltpu.CompilerParams(
            dimension_semantics=("arbitrary", "arbitrary", "arbitrary"),
            vmem_limit_bytes=VMEM_LIMIT),
        name="diff",
    )(fqT, fk, fvT, kmax, lam_vecs, g_sub_col)


def _route_kernel(x_ref, od_ref, of_ref, mod_ref, wout_ref, g_ref, wr_hi_ref, wr_lo_ref, br_ref,
                  x1_ref, h_ref, gates_ref, *, tm):
    attn = (_dot(od_ref[...], wout_ref[0:W_DQ, :]) + _dot(of_ref[...], wout_ref[W_DQ:, :]))
    x1 = x_ref[...] + mod_ref[2:3, :] * attn
    x1_ref[...] = x1
    h = _rms(x1, g_ref[...]) * (1.0 + mod_ref[4:5, :]) + mod_ref[3:4, :]
    h_hi = h.astype(BF16)
    h_ref[...] = h_hi
    h_lo = (h - h_hi.astype(F32)).astype(BF16)
    w_hi = wr_hi_ref[...]
    logits = (_dot(h_hi, w_hi) + _dot(h_lo, w_hi) + _dot(h_hi, wr_lo_ref[...])) + br_ref[...]

    lane = lax.broadcasted_iota(jnp.int32, (tm, LANES), 1)
    big = jnp.int32(4 * LANES)
    neg = -jnp.inf
    is_g = (lane >= N_EXPERTS) & (lane < N_EXPERTS + N_GROUPS)
    gl = jnp.where(is_g, logits, neg)
    gmax = jnp.max(gl, axis=-1, keepdims=True)
    g_lane = jnp.min(jnp.where(gl == gmax, lane, big), axis=-1, keepdims=True)
    g_sel = g_lane - N_EXPERTS
    p_g = 1.0 / jnp.sum(jnp.where(is_g, jnp.exp(gl - gmax), 0.0), axis=-1, keepdims=True)
    in_grp = (lane < N_EXPERTS) & ((lane // EXPERTS_PER_GROUP) == g_sel)
    el = jnp.where(in_grp, logits, neg)
    v1 = jnp.max(el, axis=-1, keepdims=True)
    i1 = jnp.min(jnp.where(el == v1, lane, big), axis=-1, keepdims=True)
    el2 = jnp.where(lane == i1, neg, el)
    v2 = jnp.max(el2, axis=-1, keepdims=True)
    i2 = jnp.min(jnp.where(el2 == v2, lane, big), axis=-1, keepdims=True)
    e = jnp.exp(v2 - v1)
    w1 = 1.0 / (1.0 + e)
    w2 = e * w1
    gates_ref[...] = jnp.where(lane == i1, w1 * p_g, jnp.where(lane == i2, w2 * p_g, 0.0))


def _route_call(x, o_dsa, o_diff, mod3, w_out, g_moe, wr_hi, wr_lo, b_r, tm):
    B, S, D = x.shape
    kern = functools.partial(_route_kernel, tm=tm)
    tok = lambda w: pl.BlockSpec((None, tm, w), lambda b, i: (b, i, 0))
    full = lambda a: pl.BlockSpec(a.shape, lambda b, i: (0,) * a.ndim)
    return pl.pallas_call(
        kern,
        out_shape=[jax.ShapeDtypeStruct((B, S, D), F32), jax.ShapeDtypeStruct((B, S, D), BF16),
                   jax.ShapeDtypeStruct((B, S, LANES), F32)],
        grid=(B, S // tm),
        in_specs=[tok(D), tok(W_DQ), tok(DIFF_HEADS * DIFF_V_DIM),
                  pl.BlockSpec((None, 6, D), lambda b, i: (b, 0, 0)),
                  full(w_out), full(g_moe), full(wr_hi), full(wr_lo), full(b_r)],
        out_specs=[tok(D), tok(D), tok(LANES)],
        compiler_params=pltpu.CompilerParams(dimension_semantics=("arbitrary", "arbitrary"),
                                             vmem_limit_bytes=VMEM_LIMIT),
        name="route",
    )(x, o_dsa, o_diff, mod3, w_out, g_moe, wr_hi, wr_lo, b_r)


def _moe_kernel(h_ref, gates_ref, x1_ref, mod_ref, wg_ref, wu_ref, wd_ref, gf_ref, o_ref, acc_ref,
                *, tm, ec):
    j = pl.program_id(2)
    nc = pl.num_programs(2)

    @pl.when(j == 0)
    def _():
        acc_ref[...] = jnp.zeros_like(acc_ref)

    h = h_ref[...]
    hg = _dot(h, wg_ref[...])
    hu = _dot(h, wu_ref[...])
    hid = hg * jax.nn.sigmoid(hg) * hu
    gates = gates_ref[...]
    lane = lax.broadcasted_iota(jnp.int32, (tm, LANES), 1)
    parts = []
    for e in range(ec):
        gcol = jnp.sum(jnp.where(lane == j * ec + e, gates, 0.0), axis=-1, keepdims=True)
        parts.append((hid[:, e * D_EXPERT:(e + 1) * D_EXPERT] * gcol).astype(BF16))
    hs = jnp.concatenate(parts, axis=1)
    acc_ref[...] += _dot(hs, wd_ref[...])

    @pl.when(j == nc - 1)
    def _():
        x2 = x1_ref[...] + mod_ref[5:6, :] * acc_ref[...]
        o_ref[...] = _rms(x2, gf_ref[...])


def _moe_call(h, gates, x1, mod3, wg, wu, wd, g_final, tm, ec):
    B, S, D = x1.shape
    kern = functools.partial(_moe_kernel, tm=tm, ec=ec)
    tok = lambda w: pl.BlockSpec((None, tm, w), lambda b, i, j: (b, i, 0))
    return pl.pallas_call(
        kern,
        out_shape=jax.ShapeDtypeStruct((B, S, D), F32),
        grid=(B, S // tm, N_EXPERTS // ec),
        in_specs=[tok(D), tok(LANES), tok(D),
                  pl.BlockSpec((None, 6, D), lambda b, i, j: (b, 0, 0)),
                  pl.BlockSpec((D, ec * D_EXPERT), lambda b, i, j: (0, j)),
                  pl.BlockSpec((D, ec * D_EXPERT), lambda b, i, j: (0, j)),
                  pl.BlockSpec((ec * D_EXPERT, D), lambda b, i, j: (j, 0)),
                  pl.BlockSpec((1, D), lambda b, i, j: (0, 0))],
        out_specs=tok(D),
        scratch_shapes=[pltpu.VMEM((tm, D), F32)],
        compiler_params=pltpu.CompilerParams(
            dimension_semantics=("arbitrary", "arbitrary", "arbitrary"),
            vmem_limit_bytes=VMEM_LIMIT),
        name="moe",
    )(h, gates, x1, mod3, wg, wu, wd, g_final)


def _tile(n, pref):
    t = min(n, pref)
    assert n % t == 0, (n, t)
    return t


def kernel(x, c, w_ada, b_ada, g_attn, w_in, g_kv, w_uk, w_uv, lam_q1, lam_k1, lam_q2, lam_k2,
           g_sub, w_out, g_moe, w_group, b_group, w_router, b_router, w_gate, w_up, w_down, g_final):
    B, S, D = x.shape
    assert D == D_MODEL and w_ada.shape[0] == 1
    assert S % KEY_TILE == 0 and S <= POS_SPLIT * 256
    topk = min(TOPK_MAX, S // 4)
    l = 0
    lam0 = 0.8 - 0.6 * math.exp(-0.3 * l)

    mod3 = _mod_call(c, w_ada[l], b_ada[l].reshape(1, -1)).reshape(B, 6, D)

    wn, wt, w_pair = _proj_weights(w_in[l], w_uk[l])
    qabsT, iqT, iwT, ik4, ckv, ckvT, fqT, fk, fvT, kmax = _proj_call(
        x, mod3, g_attn[l].reshape(1, D), wn, wt, w_pair, g_kv[l].reshape(1, -1), KEY_TILE)

    lstrict = jnp.asarray(np.tril(np.ones((KEY_TILE, KEY_TILE), np.float32), -1), BF16)
    uvT = jnp.swapaxes(w_uv[l], 1, 2)
    z = jnp.zeros_like(uvT[0])
    wuvT_pair = jnp.stack([
        jnp.concatenate([jnp.concatenate([uvT[2 * p], z], axis=1),
                         jnp.concatenate([z, uvT[2 * p + 1]], axis=1)], axis=0)
        for p in range(DSA_HEADS // 2)]).astype(BF16)
    o_dsa = _dsa_call(qabsT, iqT, iwT, ik4, ckv, ckvT, kmax, lstrict, wuvT_pair, DSA_QUERY_TILE, KEY_TILE,
                      topk)

    lam_vecs = jnp.concatenate([lam_q1[l][None], lam_k1[l][None], lam_q2[l][None], lam_k2[l][None]],
                               axis=0).astype(F32)
    o_diff = _diff_call(fqT, fk, fvT, kmax, lam_vecs, g_sub[l].reshape(-1, 1), KEY_TILE, KEY_TILE, lam0)

    wr = jnp.pad(jnp.concatenate([w_router[l], w_group[l]], axis=1),
                 ((0, 0), (0, LANES - N_EXPERTS - N_GROUPS)))
    wr_hi = wr.astype(BF16)
    wr_lo = (wr - wr_hi.astype(F32)).astype(BF16)
    b_r = jnp.pad(jnp.concatenate([b_router[l], b_group[l]]), (0, LANES - N_EXPERTS - N_GROUPS))
    x1, h2, gates = _route_call(x, o_dsa, o_diff, mod3, w_out[l].astype(BF16),
                                g_moe[l].reshape(1, D), wr_hi, wr_lo, b_r.reshape(1, LANES),
                                _tile(S, TOKEN_TILE))

    wg = jnp.swapaxes(w_gate[l], 0, 1).reshape(D, N_EXPERTS * D_EXPERT).astype(BF16)
    wu = jnp.swapaxes(w_up[l], 0, 1).reshape(D, N_EXPERTS * D_EXPERT).astype(BF16)
    wd = w_down[l].reshape(N_EXPERTS * D_EXPERT, D).astype(BF16)
    return _moe_call(h2, gates, x1, mod3, wg, wu, wd, g_final.reshape(1, D), _tile(S, TOKEN_TILE),
                     MOE_EXPERTS_PER_STEP)
```

```python
import functools
import math

import jax
import jax.numpy as jnp
import numpy as np
from jax import lax
from jax.experimental import pallas as pl
from jax.experimental.pallas import tpu as pltpu

F32 = jnp.float32
BF16 = jnp.bfloat16

D_MODEL = 1024
DSA_HEADS = 8
DSA_HEAD_DIM = 64
DSA_LATENT = 128
IDX_HEADS = 4
IDX_DIM = 32
TOPK_MAX = 256
DIFF_HEADS = 4
DIFF_QK_DIM = 64
DIFF_V_DIM = 128
N_GROUPS = 4
EXPERTS_PER_GROUP = 8
N_EXPERTS = 32
D_EXPERT = 256
EPS = 1e-6

LANES = 128
BF16_ROWS = 16
POS_SPLIT = 64
NEG_BIG = -1e30
VMEM_LIMIT = 56 * 1024 * 1024
KEY_TILE = 512
VALUE_BISECT_STEPS = 24
UNCHECKED_BISECT_STEPS = 16
COUNT_CHAINS = 4
ATTN_CHAINS = 1

W_DQ = DSA_HEADS * DSA_HEAD_DIM
W_QABS = DSA_HEADS * DSA_LATENT
W_CKV = 2 * LANES
VT_ROWS = DSA_LATENT + BF16_ROWS
ONES_ROW = DSA_LATENT + 2
N_FEATS = 5
KMAX_LANE_CKV = 8
L_MIN = 1e-30
BF16_ROUND_UP = 1.0 + 2.0 ** -7
BOUND_SLACK = 2.0 ** -6
CODE_BISECT_STEPS = 34
TOKEN_TILE = 512
DSA_QUERY_TILE = 256
DIFF_QUERY_TILE = 1024
MOE_EXPERTS_PER_STEP = 4
W_FQ = DIFF_HEADS * 2 * LANES
W_DIFF_QK = DIFF_HEADS * 2 * DIFF_QK_DIM

T_DQ = 0
T_DLAT = T_DQ + W_DQ
T_IQ = T_DLAT + DSA_LATENT
T_IW = T_IQ + IDX_HEADS * IDX_DIM
T_FQ = T_IW + BF16_ROWS
T_FV = T_FQ + W_DIFF_QK
T_ROWS = T_FV + DIFF_HEADS * DIFF_V_DIM
N_IK = 0
N_DLAT = N_IK + LANES
N_FK = N_DLAT + DSA_LATENT
N_COLS = N_FK + W_FQ


def _alibi_slopes(n):
    return [2.0 ** (-8.0 * (i + 1) / n) for i in range(n)]


def _rms(x, g):
    return x * lax.rsqrt(jnp.mean(x * x, axis=-1, keepdims=True) + EPS) * g


def _dot(a, b):
    return jnp.dot(a, b, preferred_element_type=F32)


def _mod_kernel(c_ref, w_ref, b_ref, o_ref):
    c = c_ref[...]
    act = c * jax.nn.sigmoid(c)
    o_ref[...] = jnp.dot(act, w_ref[...], preferred_element_type=F32,
                         precision=lax.Precision.HIGHEST) + b_ref[...]


def _mod_call(c, w_ada, b_ada):
    B, D = c.shape
    n = w_ada.shape[1] // D
    return pl.pallas_call(
        _mod_kernel,
        out_shape=jax.ShapeDtypeStruct((B, n * D), F32),
        grid=(n,),
        in_specs=[pl.BlockSpec((B, D), lambda j: (0, 0)),
                  pl.BlockSpec((D, D), lambda j: (0, j)),
                  pl.BlockSpec((1, D), lambda j: (0, j))],
        out_specs=pl.BlockSpec((B, D), lambda j: (0, j)),
        compiler_params=pltpu.CompilerParams(dimension_semantics=("arbitrary",),
                                             vmem_limit_bytes=VMEM_LIMIT),
        name="mod",
    )(c, w_ada, b_ada)


def _key_feature_rows(pos_row, n_rows):
    r = lax.broadcasted_iota(jnp.int32, (n_rows, pos_row.shape[1]), 0)
    pa = (pos_row // POS_SPLIT).astype(F32)
    pb = (pos_row % POS_SPLIT).astype(F32)
    return jnp.where(r == 0, pa, jnp.where(r == 1, pb, jnp.where(r < N_FEATS, 1.0, 0.0)))


def _query_feature_rows(pos_row, n_rows, slope, shift):
    r = lax.broadcasted_iota(jnp.int32, (n_rows, pos_row.shape[1]), 0)
    pa = (pos_row // POS_SPLIT).astype(F32)
    pb = (pos_row % POS_SPLIT).astype(F32)
    return jnp.where(r == 0, POS_SPLIT * slope,
                     jnp.where(r == 1, slope,
                               jnp.where(r == 2, -POS_SPLIT * slope * pa,
                                         jnp.where(r == 3, -slope * pb,
                                                   jnp.where(r == 4, -shift, 0.0)))))


def _shift_bound(qn2, kmax2, extra):
    b = jnp.sqrt(qn2 * kmax2) + extra
    return b + jnp.abs(b) * BOUND_SLACK + BOUND_SLACK


def _tile_max(kmax_ref, last_tile, lane_idx):
    x = kmax_ref[...]
    t = lax.broadcasted_iota(jnp.int32, x.shape, 0)
    ln = lax.broadcasted_iota(jnp.int32, x.shape, 2)
    x = jnp.where((t <= last_tile) & (ln == lane_idx), x, 0.0)
    return jnp.max(jnp.max(x, axis=0), axis=1, keepdims=True)[0:1, :]


def _proj_kernel(x_ref, mod_ref, g_ref, wn_ref, wt_ref, wpair_ref, gkv_ref, gkvc_ref, fc_ref, sel_ref,
                 qabsT_ref, iqT_ref, iwT_ref, ik_ref, ckv_ref, ckvT_ref, fqT_ref, fk_ref, fvT_ref,
                 kmax_ref, *, tm):
    x = x_ref[...]
    h = _rms(x, g_ref[...]) * (1.0 + mod_ref[1:2, :]) + mod_ref[0:1, :]
    hb = h.astype(BF16)
    hT = h.T.astype(BF16)

    base = pl.program_id(1) * tm
    pos_c = base + lax.broadcasted_iota(jnp.int32, (tm, 1), 0)
    pos_r = base + lax.broadcasted_iota(jnp.int32, (1, tm), 1)
    pa = (pos_c // POS_SPLIT).astype(F32)
    pb = (pos_c % POS_SPLIT).astype(F32)

    def feats(row, width):
        return (fc_ref[row:row + 1, 0:width] + fc_ref[row + 1:row + 2, 0:width] * pa
                + fc_ref[row + 2:row + 3, 0:width] * pb)

    ik_ref[...] = _dot(hb, wn_ref[:, N_IK:N_IK + LANES]).astype(BF16)
    dlat = _dot(hb, wn_ref[:, N_DLAT:N_DLAT + DSA_LATENT])
    ckv_b = _rms(dlat, gkv_ref[...]).astype(BF16)
    ckv_ref[:, 0:LANES] = ckv_b
    ckv_ref[:, LANES:2 * LANES] = feats(0, LANES).astype(BF16)
    fk = _dot(hb, wn_ref[:, N_FK:N_FK + W_FQ])
    fk_ref[...] = (fk + feats(3, W_FQ)).astype(BF16)
    sq = jnp.concatenate([fk.astype(BF16).astype(F32), ckv_b.astype(F32)], axis=1)
    sq_up = (sq * sq * BF16_ROUND_UP).astype(BF16)
    kmax_ref[...] = jnp.broadcast_to(jnp.max(_dot(sq_up, sel_ref[...]), axis=0, keepdims=True),
                                     (8, LANES))

    dqT = _dot(wt_ref[T_DQ:T_DQ + W_DQ, :], hT).astype(BF16)
    for p in range(DSA_HEADS // 2):
        qa = _dot(wpair_ref[p], dqT[p * LANES:(p + 1) * LANES, :])
        qabsT_ref[p * 2 * LANES:(p + 1) * 2 * LANES, :] = qa.astype(BF16)
    dlatT = _dot(wt_ref[T_DLAT:T_DLAT + DSA_LATENT, :], hT)
    inv = lax.rsqrt(jnp.mean(dlatT * dlatT, axis=0, keepdims=True) + EPS)
    kfeat = _key_feature_rows(pos_r, BF16_ROWS).astype(BF16)
    ckvT_ref[0:DSA_LATENT, :] = (dlatT * inv * gkvc_ref[...]).astype(BF16)
    ckvT_ref[DSA_LATENT:VT_ROWS, :] = kfeat
    iqT_ref[...] = _dot(wt_ref[T_IQ:T_IQ + IDX_HEADS * IDX_DIM, :], hT).astype(BF16)
    iwT = _dot(wt_ref[T_IW:T_IW + BF16_ROWS, :], hT)
    iwT_ref[...] = iwT[0:8, :] * (IDX_HEADS ** -0.5 * IDX_DIM ** -0.5)
    fqT = _dot(wt_ref[T_FQ:T_FQ + W_DIFF_QK, :], hT)
    zpad = jnp.zeros((LANES - DIFF_QK_DIM, tm), BF16)
    for s in range(DIFF_HEADS * 2):
        r0 = s * LANES
        fqT_ref[r0:r0 + DIFF_QK_DIM, :] = fqT[s * DIFF_QK_DIM:(s + 1) * DIFF_QK_DIM, :].astype(BF16)
        fqT_ref[r0 + DIFF_QK_DIM:r0 + LANES, :] = zpad
    fvT = _dot(wt_ref[T_FV:T_FV + DIFF_HEADS * DIFF_V_DIM, :], hT)
    for hh in range(DIFF_HEADS):
        fvT_ref[hh, 0:DIFF_V_DIM, :] = fvT[hh * DIFF_V_DIM:(hh + 1) * DIFF_V_DIM, :].astype(BF16)
        fvT_ref[hh, DIFF_V_DIM:VT_ROWS, :] = kfeat


def _feature_consts():
    fc = np.zeros((8, W_FQ), np.float32)
    fc[1, 0] = 1.0
    fc[2, 1] = 1.0
    fc[0, 2:N_FEATS] = 1.0
    for s in range(DIFF_HEADS * 2):
        base = s * LANES + DIFF_QK_DIM
        fc[4, base + 0] = 1.0
        fc[5, base + 1] = 1.0
        fc[3, base + 2:base + N_FEATS] = 1.0
    return jnp.asarray(fc)


def _norm_selector():
    sel = np.zeros((W_FQ + DSA_LATENT, LANES), np.float32)
    for s in range(DIFF_HEADS * 2):
        sel[s * LANES:s * LANES + DIFF_QK_DIM, s] = 1.0
    sel[W_FQ:, KMAX_LANE_CKV] = 1.0
    return jnp.asarray(sel, BF16)


def _proj_weights(w_in, w_uk):
    D = w_in.shape[0]
    pts = np.cumsum([W_DQ, DSA_LATENT, IDX_HEADS * IDX_DIM, IDX_DIM, IDX_HEADS,
                     W_DIFF_QK, W_DIFF_QK])
    dq, dlat, iq, ik, iw, fq, fk, fv = jnp.split(w_in, list(pts), axis=1)
    ik4 = jnp.tile(ik, (1, IDX_HEADS))
    fke = jnp.pad(fk.reshape(D, DIFF_HEADS * 2, DIFF_QK_DIM),
                  ((0, 0), (0, 0), (0, LANES - DIFF_QK_DIM))).reshape(D, W_FQ)
    wn = jnp.concatenate([ik4, dlat, fke], axis=1).astype(BF16)
    iwp = jnp.pad(iw, ((0, 0), (0, BF16_ROWS - IDX_HEADS)))
    wt = jnp.concatenate([dq, dlat, iq, iwp, fq * (DIFF_QK_DIM ** -0.5), fv], axis=1).T.astype(BF16)
    uk = w_uk * (DSA_HEAD_DIM ** -0.5)
    z = jnp.zeros_like(uk[0])
    pairs = [jnp.concatenate([jnp.concatenate([uk[2 * p], z], axis=1),
                              jnp.concatenate([z, uk[2 * p + 1]], axis=1)], axis=0)
             for p in range(DSA_HEADS // 2)]
    return wn, wt, jnp.stack(pairs).astype(BF16)


def _proj_call(x, mod3, g_attn, wn, wt, w_pair, g_kv, tm):
    B, S, D = x.shape
    nt = S // tm
    fc = _feature_consts()
    sel = _norm_selector()
    kern = functools.partial(_proj_kernel, tm=tm)
    tok = lambda w: pl.BlockSpec((None, tm, w), lambda b, i: (b, i, 0))
    tokT = lambda r: pl.BlockSpec((None, r, tm), lambda b, i: (b, 0, i))
    full = lambda a: pl.BlockSpec(a.shape, lambda b, i: (0,) * a.ndim)
    g_kv_col = g_kv.reshape(-1, 1)
    out_shape = [jax.ShapeDtypeStruct((B, W_QABS, S), BF16),
                 jax.ShapeDtypeStruct((B, LANES, S), BF16),
                 jax.ShapeDtypeStruct((B, 8, S), F32),
                 jax.ShapeDtypeStruct((B, S, LANES), BF16),
                 jax.ShapeDtypeStruct((B, S, W_CKV), BF16),
                 jax.ShapeDtypeStruct((B, nt, VT_ROWS, tm), BF16),
                 jax.ShapeDtypeStruct((B, W_FQ, S), BF16),
                 jax.ShapeDtypeStruct((B, S, W_FQ), BF16),
                 jax.ShapeDtypeStruct((B, DIFF_HEADS, nt, VT_ROWS, tm), BF16),
                 jax.ShapeDtypeStruct((B, nt, 8, LANES), F32)]
    out_specs = [tokT(W_QABS), tokT(LANES), tokT(8), tok(LANES), tok(W_CKV),
                 pl.BlockSpec((None, None, VT_ROWS, tm), lambda b, i: (b, i, 0, 0)),
                 tokT(W_FQ), tok(W_FQ),
                 pl.BlockSpec((None, DIFF_HEADS, None, VT_ROWS, tm), lambda b, i: (b, 0, i, 0, 0)),
                 pl.BlockSpec((None, None, 8, LANES), lambda b, i: (b, i, 0, 0))]
    return pl.pallas_call(
        kern,
        out_shape=out_shape,
        grid=(B, nt),
        in_specs=[tok(D),
                  pl.BlockSpec((None, 6, D), lambda b, i: (b, 0, 0)),
                  full(g_attn), full(wn), full(wt), full(w_pair), full(g_kv), full(g_kv_col),
                  full(fc), full(sel)],
        out_specs=out_specs,
        compiler_params=pltpu.CompilerParams(dimension_semantics=("arbitrary", "arbitrary"),
                                             vmem_limit_bytes=VMEM_LIMIT),
        name="proj",
    )(x, mod3, g_attn, wn, wt, w_pair, g_kv, g_kv_col, fc, sel)


def _float_code(x):
    b = lax.bitcast_convert_type(x, jnp.int32)
    return b ^ (lax.shift_right_arithmetic(b, 31) & jnp.int32(0x7FFFFFFF))


def _float_decode(c):
    b = c ^ (lax.shift_right_arithmetic(c, 31) & jnp.int32(0x7FFFFFFF))
    return lax.bitcast_convert_type(b, F32)


def _dsa_kernel(qabsT_ref, iqT_ref, iwT_ref, ik_ref, ckv_ref, ckvT_ref, kmax_ref, lstrict_ref, wuvT_ref,
                o_ref, sc_ref, qst_ref, qt_ref, acc_ref, m_ref, tmp_ref, *, tq, tk, topk):
    qi = pl.program_id(1)
    nkb = (qi * tq) // tk + 1
    kf = float(topk)
    slopes = _alibi_slopes(DSA_HEADS)
    q_pos = qi * tq + lax.broadcasted_iota(jnp.int32, (1, tq), 1)

    iqT = iqT_ref[...]
    rowi = lax.broadcasted_iota(jnp.int32, (LANES, tq), 0)
    for hh in range(IDX_HEADS):
        qst_ref[:, hh * tq:(hh + 1) * tq] = jnp.where((rowi // IDX_DIM) == hh, iqT, jnp.zeros_like(iqT))
    iw = iwT_ref[...]
    wrow = [iw[hh:hh + 1, :] for hh in range(IDX_HEADS)]

    def scores(kb):
        kblk = ik_ref[pl.ds(pl.multiple_of(kb * tk, tk), tk), :]
        a = _dot(kblk, qst_ref[...])
        sc = jnp.maximum(a[:, 0:tq], 0.0) * wrow[0]
        for hh in range(1, IDX_HEADS):
            sc = sc + jnp.maximum(a[:, hh * tq:(hh + 1) * tq], 0.0) * wrow[hh]
        return sc

    def score_body(kb, carry):
        mn, mx = carry
        sc = scores(kb)
        sc_ref[kb] = sc
        return (jnp.minimum(mn, jnp.min(sc, axis=0, keepdims=True)),
                jnp.maximum(mx, jnp.max(sc, axis=0, keepdims=True)))

    last = nkb - 1
    mn, mx = lax.fori_loop(0, last // 2, lambda j, c: score_body(2 * j + 1, score_body(2 * j, c)),
                           (jnp.full((1, tq), jnp.inf, F32), jnp.full((1, tq), -jnp.inf, F32)))
    mn, mx = lax.cond(last % 2 == 1, lambda c: score_body(last - 1, c), lambda c: c, (mn, mx))
    sc = scores(last)
    causal = lax.broadcasted_iota(jnp.int32, (tk, tq), 0) + last * tk <= q_pos
    sc_ref[last] = jnp.where(causal, sc, jnp.nan)
    mn = jnp.minimum(mn, jnp.min(jnp.where(causal, sc, jnp.inf), axis=0, keepdims=True))
    mx = jnp.maximum(mx, jnp.max(jnp.where(causal, sc, -jnp.inf), axis=0, keepdims=True))

    def block_count(kb, pred):
        parts = [None] * COUNT_CHAINS
        for r in range(tk // 8):
            v = jnp.where(pred(sc_ref[kb, r * 8:(r + 1) * 8, :]), 1.0, 0.0)
            c = r % COUNT_CHAINS
            parts[c] = v if parts[c] is None else parts[c] + v
        return (parts[0] + parts[1]) + (parts[2] + parts[3])

    def count(pred):
        acc = lax.fori_loop(0, nkb, lambda kb, acc: acc + block_count(kb, pred),
                            jnp.zeros((8, tq), F32))
        return jnp.sum(acc, axis=0, keepdims=True)

    n_causal = (q_pos + 1).astype(F32)
    done0 = n_causal <= kf

    def bisect(st, value_mid):
        lo, hi, c_lo, theta, done = st
        th8 = jnp.broadcast_to(theta, (8, tq))
        c = count(lambda v: v >= th8)
        ge = c >= kf
        live = done == 0.0
        lo = jnp.where(live & ge, theta, lo)
        c_lo = jnp.where(live & ge, c, c_lo)
        hi = jnp.where(live & jnp.logical_not(ge), theta, hi)
        if value_mid:
            nxt = 0.5 * lo + 0.5 * hi
        else:
            cl, ch = _float_code(lo), _float_code(hi)
            nxt = _float_decode((cl & ch) + lax.shift_right_arithmetic(cl ^ ch, 1))
        inside = (nxt > lo) & (nxt < hi)
        done = jnp.where((c_lo == kf) | jnp.logical_not(inside), 1.0, done)
        return lo, hi, c_lo, nxt, done

    def checked(value_mid, max_steps):
        def cond(c):
            return jnp.logical_and(c[1] > 0.0, c[2] < max_steps)

        def body(c):
            st = bisect(bisect(c[0], value_mid), value_mid)
            return st, jnp.sum(1.0 - st[4]), c[2] + 2
        return cond, body

    zero8 = jnp.zeros((8, tq), F32)
    c_ge0 = count(lambda v: v >= zero8)
    c_gt0 = count(lambda v: v > zero8)
    live0 = jnp.logical_not(done0)
    at0 = live0 & (c_ge0 >= kf)
    tie0 = at0 & (c_gt0 < kf)
    below0 = live0 & (c_ge0 < kf)
    lo0 = jnp.where(at0, 0.0, mn)
    hi0 = jnp.where(below0, 0.0, mx)
    st = (lo0, hi0, jnp.where(at0, c_ge0, n_causal),
          jnp.where(below0, 0.5 * lo0 + 0.5 * hi0, mx), jnp.where(done0 | tie0, 1.0, 0.0))
    st = lax.fori_loop(0, UNCHECKED_BISECT_STEPS, lambda i, s: bisect(s, True), st)
    cond, body = checked(True, VALUE_BISECT_STEPS - UNCHECKED_BISECT_STEPS)
    st, active, _ = lax.while_loop(cond, body, (st, jnp.sum(1.0 - st[4]), jnp.int32(0)))
    cond, body = checked(False, CODE_BISECT_STEPS)
    st, _, _ = lax.while_loop(cond, body, (st, active, jnp.int32(0)))
    tau, c_ge = st[0], st[2]

    over = c_ge > kf

    @pl.when(jnp.max(c_ge) > kf)
    def _():
        tau8 = jnp.broadcast_to(tau, (8, tq))
        tmp_ref[0:1, :] = c_gt0

        @pl.when(jnp.max(jnp.where(over & jnp.logical_not(tie0), 1.0, 0.0)) > 0.0)
        def _():
            tmp_ref[0:1, :] = jnp.where(tie0, c_gt0, count(lambda v: v > tau8))

        quota = kf - tmp_ref[0:1, :]

        def tie_body(kb, seen):
            cnt = jnp.sum(block_count(kb, lambda v: v == tau8), axis=0, keepdims=True)
            inside = over & (seen < quota) & (seen + cnt > quota)
            gone = over & (seen >= quota) & (cnt > 0.0)
            flag = jnp.max(jnp.where(inside, 2.0, jnp.where(gone, 1.0, 0.0)))

            @pl.when(flag > 1.5)
            def _():
                s = sc_ref[kb]
                eq = s == tau
                rank = _dot(lstrict_ref[...], jnp.where(eq, 1.0, 0.0).astype(BF16)) + seen
                sc_ref[kb] = jnp.where(eq & over & (rank >= quota), jnp.nan, s)

            @pl.when(flag == 1.0)
            def _():
                s = sc_ref[kb]
                sc_ref[kb] = jnp.where((s == tau) & gone, jnp.nan, s)

            return seen + cnt

        lax.fori_loop(0, nkb, tie_body, jnp.zeros((1, tq), F32))

    tau8 = jnp.broadcast_to(tau, (8, tq))
    sub = lax.broadcasted_iota(jnp.int32, (8, tq), 0).astype(F32)

    def nearest_body(kb, acc):
        parts = [None] * COUNT_CHAINS
        for r in range(tk // 8):
            idx = sub + (kb * tk + r * 8).astype(F32)
            v = jnp.where(sc_ref[kb, r * 8:(r + 1) * 8, :] >= tau8, idx, -1.0)
            c = r % COUNT_CHAINS
            parts[c] = v if parts[c] is None else jnp.maximum(parts[c], v)
        return jnp.maximum(acc, jnp.maximum(jnp.maximum(parts[0], parts[1]),
                                            jnp.maximum(parts[2], parts[3])))

    last_sel = jnp.max(lax.fori_loop(0, nkb, nearest_body, jnp.full((8, tq), -1.0, F32)),
                       axis=0, keepdims=True)
    d_min = q_pos.astype(F32) - last_sel
    kmax2 = _tile_max(kmax_ref, nkb - 1, KMAX_LANE_CKV)
    for hh in range(DSA_HEADS):
        qh = qabsT_ref[hh * DSA_LATENT:(hh + 1) * DSA_LATENT, :]
        qf = qh.astype(F32)
        shift = _shift_bound(jnp.sum(qf * qf, axis=0, keepdims=True), kmax2, -slopes[hh] * d_min)
        qt_ref[0:DSA_LATENT, hh * tq:(hh + 1) * tq] = qh
        qt_ref[DSA_LATENT:VT_ROWS, hh * tq:(hh + 1) * tq] = _query_feature_rows(
            q_pos, BF16_ROWS, slopes[hh], shift).astype(BF16)
    qt_ref[VT_ROWS:2 * DSA_LATENT, :] = jnp.zeros((2 * DSA_LATENT - VT_ROWS, DSA_HEADS * tq), BF16)
    acc_ref[...] = jnp.zeros_like(acc_ref)

    def fast_pv(kb):
        kx = ckv_ref[pl.ds(pl.multiple_of(kb * tk, tk), tk), :]
        s = _dot(kx, qt_ref[...])
        keep = sc_ref[kb] >= tau
        ps = [jnp.exp(jnp.where(keep, s[:, hh * tq:(hh + 1) * tq], NEG_BIG)).astype(BF16)
              for hh in range(DSA_HEADS)]
        return _dot(ckvT_ref[kb], jnp.concatenate(ps, axis=1))

    def fast_quad(j, carry):
        acc_ref[...] += ((fast_pv(4 * j) + fast_pv(4 * j + 1))
                         + (fast_pv(4 * j + 2) + fast_pv(4 * j + 3)))
        return carry

    lax.fori_loop(0, nkb // 4, fast_quad, 0)
    rem = nkb % 4

    @pl.when(rem >= 2)
    def _():
        acc_ref[...] += fast_pv(nkb - rem) + fast_pv(nkb - rem + 1)

    @pl.when(rem % 2 == 1)
    def _():
        acc_ref[...] += fast_pv(nkb - 1)

    def attn_body(kb, carry):
        kx = ckv_ref[pl.ds(pl.multiple_of(kb * tk, tk), tk), :]
        vT = ckvT_ref[kb]
        keep = sc_ref[kb] >= tau
        qt = qt_ref[...]
        m_all = m_ref[...]
        m_out, alphas, pvs = [], [], []
        for ch in range(ATTN_CHAINS):
            hpc = DSA_HEADS // ATTN_CHAINS
            l0 = ch * hpc * tq
            s = _dot(kx, qt[:, l0:l0 + hpc * tq])
            ps = []
            for j in range(hpc):
                c0 = l0 + j * tq
                sh = jnp.where(keep, s[:, j * tq:(j + 1) * tq], NEG_BIG)
                m_old = m_all[:, c0:c0 + tq]
                m_new = jnp.maximum(m_old, jnp.max(sh, axis=0, keepdims=True))
                ps.append(jnp.exp(sh - m_new).astype(BF16))
                alphas.append(jnp.exp(m_old - m_new))
                m_out.append(m_new)
            pvs.append(_dot(vT, jnp.concatenate(ps, axis=1)))
        m_ref[...] = jnp.concatenate(m_out, axis=1)
        acc_ref[...] = jnp.concatenate(alphas, axis=1) * acc_ref[...] + jnp.concatenate(pvs, axis=1)
        return carry

    denom_ok = jnp.min(acc_ref[ONES_ROW:ONES_ROW + 1, :]) > L_MIN

    @pl.when(jnp.logical_not(denom_ok))
    def _():
        acc_ref[...] = jnp.zeros_like(acc_ref)
        m_ref[...] = jnp.full_like(m_ref, NEG_BIG)
        lax.fori_loop(0, nkb, attn_body, 0)

    acc = acc_ref[...]
    lat = acc[0:DSA_LATENT, :] / acc[ONES_ROW:ONES_ROW + 1, :]
    for p in range(DSA_HEADS // 2):
        pair = lat[:, 2 * p * tq:(2 * p + 2) * tq]
        pair = jnp.concatenate([pair[:, 0:tq], pair[:, tq:2 * tq]], axis=0).astype(BF16)
        o_ref[:, p * LANES:(p + 1) * LANES] = _dot(wuvT_ref[p], pair).T.astype(BF16)


def _dsa_call(qabsT, iqT, iwT, ik4, ckv, ckvT, kmax, lstrict, wuvT_pair, tq, tk, topk):
    B, _, S = qabsT.shape
    nkb = S // tk
    kern = functools.partial(_dsa_kernel, tq=tq, tk=tk, topk=topk)
    blkT = lambda r: pl.BlockSpec((None, r, tq), lambda b, i: (b, 0, i))
    per_b = lambda w: pl.BlockSpec((None, S, w), lambda b, i: (b, 0, 0))
    full = lambda a: pl.BlockSpec(a.shape, lambda b, i: (0,) * a.ndim)
    return pl.pallas_call(
        kern,
        out_shape=jax.ShapeDtypeStruct((B, S, W_DQ), BF16),
        grid=(B, S // tq),
        in_specs=[blkT(W_QABS), blkT(LANES), blkT(8), per_b(LANES), per_b(W_CKV),
                  pl.BlockSpec((None, nkb, VT_ROWS, tk), lambda b, i: (b, 0, 0, 0)),
                  pl.BlockSpec((None, nkb, 8, LANES), lambda b, i: (b, 0, 0, 0)),
                  full(lstrict), full(wuvT_pair)],
        out_specs=pl.BlockSpec((None, tq, W_DQ), lambda b, i: (b, i, 0)),
        scratch_shapes=[pltpu.VMEM((nkb, tk, tq), F32),
                        pltpu.VMEM((LANES, IDX_HEADS * tq), BF16),
                        pltpu.VMEM((2 * DSA_LATENT, DSA_HEADS * tq), BF16),
                        pltpu.VMEM((VT_ROWS, DSA_HEADS * tq), F32),
                        pltpu.VMEM((1, DSA_HEADS * tq), F32),
                        pltpu.VMEM((8, tq), F32)],
        compiler_params=pltpu.CompilerParams(dimension_semantics=("arbitrary", "arbitrary"),
                                             vmem_limit_bytes=VMEM_LIMIT),
        name="dsa",
    )(qabsT, iqT, iwT, ik4, ckv, ckvT, kmax, lstrict, wuvT_pair)


def _diff_kernel(fqT_ref, fk_ref, fvT_ref, kmax_ref, lam_ref, gsub_ref, o_ref, qt_ref, acc_ref, m_ref,
                 *, tq, tk, lam0):
    hd = pl.program_id(1)
    qi = pl.program_id(2)
    q_pos = qi * tq + lax.broadcasted_iota(jnp.int32, (1, tq), 1)
    n_full = (qi * tq) // tk
    n_diag = (tq + tk - 1) // tk
    slopes = _alibi_slopes(DIFF_HEADS)
    slope = jnp.float32(slopes[DIFF_HEADS - 1])
    for i in range(DIFF_HEADS - 1):
        slope = jnp.where(hd == i, slopes[i], slope)

    for m in range(2):
        r0 = m * LANES
        qh = fqT_ref[r0:r0 + DIFF_QK_DIM, :]
        qf = qh.astype(F32)
        shift = _shift_bound(jnp.sum(qf * qf, axis=0, keepdims=True),
                             _tile_max(kmax_ref, n_full + n_diag - 1, 2 * hd + m), 0.0)
        qt_ref[r0:r0 + DIFF_QK_DIM, :] = qh
        qt_ref[r0 + DIFF_QK_DIM:r0 + DIFF_QK_DIM + BF16_ROWS, :] = _query_feature_rows(
            q_pos, BF16_ROWS, slope, shift).astype(BF16)
        qt_ref[r0 + DIFF_QK_DIM + BF16_ROWS:r0 + LANES, :] = jnp.zeros(
            (LANES - DIFF_QK_DIM - BF16_ROWS, tq), BF16)
    acc_ref[...] = jnp.zeros_like(acc_ref)

    def fast_pv(kb, masked):
        kx = fk_ref[pl.ds(pl.multiple_of(kb * tk, tk), tk), :]
        vT = fvT_ref[kb]
        out = []
        for m in range(2):
            s = _dot(kx[:, m * LANES:(m + 1) * LANES], qt_ref[m * LANES:(m + 1) * LANES, :])
            if masked:
                causal = lax.broadcasted_iota(jnp.int32, (tk, tq), 0) + kb * tk <= q_pos
                s = jnp.where(causal, s, NEG_BIG)
            out.append(_dot(vT, jnp.exp(s).astype(BF16)))
        return out

    def accumulate(*pvs):
        for m in range(2):
            acc_ref[m] += functools.reduce(lambda a, b: a + b, [pv[m] for pv in pvs])

    def fast_pair(j, carry):
        accumulate(fast_pv(2 * j, False), fast_pv(2 * j + 1, False))
        return carry

    lax.fori_loop(0, n_full // 2, fast_pair, 0)
    assert n_diag in (1, 2)

    @pl.when(n_full % 2 == 1)
    def _():
        accumulate(fast_pv(n_full - 1, False), fast_pv(n_full, True))
        if n_diag == 2:
            accumulate(fast_pv(n_full + 1, True))

    @pl.when(n_full % 2 == 0)
    def _():
        accumulate(*[fast_pv(n_full + d, True) for d in range(n_diag)])

    def block(kb, masked):
        kx = fk_ref[pl.ds(pl.multiple_of(kb * tk, tk), tk), :]
        vT = fvT_ref[kb]
        qT = qt_ref[...]
        m_all = m_ref[...]
        new = []
        for m in range(2):
            s = _dot(kx[:, m * LANES:(m + 1) * LANES], qT[m * LANES:(m + 1) * LANES, :])
            if masked:
                causal = lax.broadcasted_iota(jnp.int32, (tk, tq), 0) + kb * tk <= q_pos
                s = jnp.where(causal, s, NEG_BIG)
            m_old = m_all[m]
            m_new = jnp.maximum(m_old, jnp.max(s, axis=0, keepdims=True))
            p = jnp.exp(s - m_new).astype(BF16)
            new.append((m_new, jnp.exp(m_old - m_new), _dot(vT, p)))
        for m in range(2):
            m_ref[m] = new[m][0]
            acc_ref[m] = new[m][1] * acc_ref[m] + new[m][2]

    def body(kb, carry):
        block(kb, False)
        return carry

    denom_ok = jnp.min(acc_ref[:, ONES_ROW:ONES_ROW + 1, :]) > L_MIN

    @pl.when(jnp.logical_not(denom_ok))
    def _():
        acc_ref[...] = jnp.zeros_like(acc_ref)
        m_ref[...] = jnp.full_like(m_ref, NEG_BIG)
        lax.fori_loop(0, n_full, body, 0)
        for d in range(n_diag):
            block(n_full + d, True)

    lv = lam_ref[...]
    lam = (jnp.exp(jnp.sum(lv[0:1] * lv[1:2], axis=-1, keepdims=True))
           - jnp.exp(jnp.sum(lv[2:3] * lv[3:4], axis=-1, keepdims=True)) + lam0)
    a1 = acc_ref[0]
    a2 = acc_ref[1]
    o = (a1[0:DIFF_V_DIM, :] / a1[ONES_ROW:ONES_ROW + 1, :]
         - lam * (a2[0:DIFF_V_DIM, :] / a2[ONES_ROW:ONES_ROW + 1, :]))
    o = o * lax.rsqrt(jnp.mean(o * o, axis=0, keepdims=True) + EPS) * gsub_ref[...] * (1.0 - lam0)
    o_ref[...] = o.T.astype(BF16)


def _diff_call(fqT, fk, fvT, kmax, lam_vecs, g_sub_col, tq, tk, lam0):
    B, S, _ = fk.shape
    nkb = S // tk
    kern = functools.partial(_diff_kernel, tq=tq, tk=tk, lam0=lam0)
    full = lambda a: pl.BlockSpec(a.shape, lambda b, h, i: (0,) * a.ndim)
    return pl.pallas_call(
        kern,
        out_shape=jax.ShapeDtypeStruct((B, S, DIFF_HEADS * DIFF_V_DIM), BF16),
        grid=(B, DIFF_HEADS, S // tq),
        in_specs=[pl.BlockSpec((None, 2 * LANES, tq), lambda b, h, i: (b, h, i)),
                  pl.BlockSpec((None, S, 2 * LANES), lambda b, h, i: (b, 0, h)),
                  pl.BlockSpec((None, None, nkb, VT_ROWS, tk), lambda b, h, i: (b, h, 0, 0, 0)),
                  pl.BlockSpec((None, nkb, 8, LANES), lambda b, h, i: (b, 0, 0, 0)),
                  full(lam_vecs), full(g_sub_col)],
        out_specs=pl.BlockSpec((None, tq, DIFF_V_DIM), lambda b, h, i: (b, i, h)),
        scratch_shapes=[pltpu.VMEM((2 * LANES, tq), BF16),
                        pltpu.VMEM((2, VT_ROWS, tq), F32),
                        pltpu.VMEM((2, 1, tq), F32)],
        compiler_params=pltpu.CompilerParams(
            dimension_semantics=("arbitrary", "arbitrary", "arbitrary"),
            vmem_limit_bytes=VMEM_LIMIT),
        name="diff",
    )(fqT, fk, fvT, kmax, lam_vecs, g_sub_col)


def _route_kernel(x_ref, od_ref, of_ref, mod_ref, wout_ref, g_ref, wr_hi_ref, wr_lo_ref, br_ref,
                  x1_ref, h_ref, gates_ref, *, tm):
    attn = (_dot(od_ref[...], wout_ref[0:W_DQ, :]) + _dot(of_ref[...], wout_ref[W_DQ:, :]))
    x1 = x_ref[...] + mod_ref[2:3, :] * attn
    x1_ref[...] = x1
    h = _rms(x1, g_ref[...]) * (1.0 + mod_ref[4:5, :]) + mod_ref[3:4, :]
    h_hi = h.astype(BF16)
    h_ref[...] = h_hi
    h_lo = (h - h_hi.astype(F32)).astype(BF16)
    w_hi = wr_hi_ref[...]
    logits = (_dot(h_hi, w_hi) + _dot(h_lo, w_hi) + _dot(h_hi, wr_lo_ref[...])) + br_ref[...]

    lane = lax.broadcasted_iota(jnp.int32, (tm, LANES), 1)
    big = jnp.int32(4 * LANES)
    neg = -jnp.inf
    is_g = (lane >= N_EXPERTS) & (lane < N_EXPERTS + N_GROUPS)
    gl = jnp.where(is_g, logits, neg)
    gmax = jnp.max(gl, axis=-1, keepdims=True)
    g_lane = jnp.min(jnp.where(gl == gmax, lane, big), axis=-1, keepdims=True)
    g_sel = g_lane - N_EXPERTS
    p_g = 1.0 / jnp.sum(jnp.where(is_g, jnp.exp(gl - gmax), 0.0), axis=-1, keepdims=True)
    in_grp = (lane < N_EXPERTS) & ((lane // EXPERTS_PER_GROUP) == g_sel)
    el = jnp.where(in_grp, logits, neg)
    v1 = jnp.max(el, axis=-1, keepdims=True)
    i1 = jnp.min(jnp.where(el == v1, lane, big), axis=-1, keepdims=True)
    el2 = jnp.where(lane == i1, neg, el)
    v2 = jnp.max(el2, axis=-1, keepdims=True)
    i2 = jnp.min(jnp.where(el2 == v2, lane, big), axis=-1, keepdims=True)
    e = jnp.exp(v2 - v1)
    w1 = 1.0 / (1.0 + e)
    w2 = e * w1
    gates_ref[...] = jnp.where(lane == i1, w1 * p_g, jnp.where(lane == i2, w2 * p_g, 0.0))


def _route_call(x, o_dsa, o_diff, mod3, w_out, g_moe, wr_hi, wr_lo, b_r, tm):
    B, S, D = x.shape
    kern = functools.partial(_route_kernel, tm=tm)
    tok = lambda w: pl.BlockSpec((None, tm, w), lambda b, i: (b, i, 0))
    full = lambda a: pl.BlockSpec(a.shape, lambda b, i: (0,) * a.ndim)
    return pl.pallas_call(
        kern,
        out_shape=[jax.ShapeDtypeStruct((B, S, D), F32), jax.ShapeDtypeStruct((B, S, D), BF16),
                   jax.ShapeDtypeStruct((B, S, LANES), F32)],
        grid=(B, S // tm),
        in_specs=[tok(D), tok(W_DQ), tok(DIFF_HEADS * DIFF_V_DIM),
                  pl.BlockSpec((None, 6, D), lambda b, i: (b, 0, 0)),
                  full(w_out), full(g_moe), full(wr_hi), full(wr_lo), full(b_r)],
        out_specs=[tok(D), tok(D), tok(LANES)],
        compiler_params=pltpu.CompilerParams(dimension_semantics=("arbitrary", "arbitrary"),
                                             vmem_limit_bytes=VMEM_LIMIT),
        name="route",
    )(x, o_dsa, o_diff, mod3, w_out, g_moe, wr_hi, wr_lo, b_r)


def _moe_kernel(h_ref, gates_ref, x1_ref, mod_ref, wg_ref, wu_ref, wd_ref, gf_ref, o_ref, acc_ref,
                *, tm, ec):
    j = pl.program_id(2)
    nc = pl.num_programs(2)

    @pl.when(j == 0)
    def _():
        acc_ref[...] = jnp.zeros_like(acc_ref)

    h = h_ref[...]
    hg = _dot(h, wg_ref[...])
    hu = _dot(h, wu_ref[...])
    hid = hg * jax.nn.sigmoid(hg) * hu
    gates = gates_ref[...]
    lane = lax.broadcasted_iota(jnp.int32, (tm, LANES), 1)
    parts = []
    for e in range(ec):
        gcol = jnp.sum(jnp.where(lane == j * ec + e, gates, 0.0), axis=-1, keepdims=True)
        parts.append((hid[:, e * D_EXPERT:(e + 1) * D_EXPERT] * gcol).astype(BF16))
    hs = jnp.concatenate(parts, axis=1)
    acc_ref[...] += _dot(hs, wd_ref[...])

    @pl.when(j == nc - 1)
    def _():
        x2 = x1_ref[...] + mod_ref[5:6, :] * acc_ref[...]
        o_ref[...] = _rms(x2, gf_ref[...])


def _moe_call(h, gates, x1, mod3, wg, wu, wd, g_final, tm, ec):
    B, S, D = x1.shape
    kern = functools.partial(_moe_kernel, tm=tm, ec=ec)
    tok = lambda w: pl.BlockSpec((None, tm, w), lambda b, i, j: (b, i, 0))
    return pl.pallas_call(
        kern,
        out_shape=jax.ShapeDtypeStruct((B, S, D), F32),
        grid=(B, S // tm, N_EXPERTS // ec),
        in_specs=[tok(D), tok(LANES), tok(D),
                  pl.BlockSpec((None, 6, D), lambda b, i, j: (b, 0, 0)),
                  pl.BlockSpec((D, ec * D_EXPERT), lambda b, i, j: (0, j)),
                  pl.BlockSpec((D, ec * D_EXPERT), lambda b, i, j: (0, j)),
                  pl.BlockSpec((ec * D_EXPERT, D), lambda b, i, j: (j, 0)),
                  pl.BlockSpec((1, D), lambda b, i, j: (0, 0))],
        out_specs=tok(D),
        scratch_shapes=[pltpu.VMEM((tm, D), F32)],
        compiler_params=pltpu.CompilerParams(
            dimension_semantics=("arbitrary", "arbitrary", "arbitrary"),
            vmem_limit_bytes=VMEM_LIMIT),
        name="moe",
    )(h, gates, x1, mod3, wg, wu, wd, g_final)


def _tile(n, pref):
    t = min(n, pref)
    assert n % t == 0, (n, t)
    return t


def kernel(x, c, w_ada, b_ada, g_attn, w_in, g_kv, w_uk, w_uv, lam_q1, lam_k1, lam_q2, lam_k2,
           g_sub, w_out, g_moe, w_group, b_group, w_router, b_router, w_gate, w_up, w_down, g_final):
    B, S, D = x.shape
    assert D == D_MODEL and w_ada.shape[0] == 1
    assert S % KEY_TILE == 0 and S <= POS_SPLIT * 256
    topk = min(TOPK_MAX, S // 4)
    l = 0
    lam0 = 0.8 - 0.6 * math.exp(-0.3 * l)

    mod3 = _mod_call(c, w_ada[l], b_ada[l].reshape(1, -1)).reshape(B, 6, D)

    wn, wt, w_pair = _proj_weights(w_in[l], w_uk[l])
    qabsT, iqT, iwT, ik4, ckv, ckvT, fqT, fk, fvT, kmax = _proj_call(
        x, mod3, g_attn[l].reshape(1, D), wn, wt, w_pair, g_kv[l].reshape(1, -1), KEY_TILE)

    lstrict = jnp.asarray(np.tril(np.ones((KEY_TILE, KEY_TILE), np.float32), -1), BF16)
    uvT = jnp.swapaxes(w_uv[l], 1, 2)
    z = jnp.zeros_like(uvT[0])
    wuvT_pair = jnp.stack([
        jnp.concatenate([jnp.concatenate([uvT[2 * p], z], axis=1),
                         jnp.concatenate([z, uvT[2 * p + 1]], axis=1)], axis=0)
        for p in range(DSA_HEADS // 2)]).astype(BF16)
    o_dsa = _dsa_call(qabsT, iqT, iwT, ik4, ckv, ckvT, kmax, lstrict, wuvT_pair, DSA_QUERY_TILE, KEY_TILE,
                      topk)

    lam_vecs = jnp.concatenate([lam_q1[l][None], lam_k1[l][None], lam_q2[l][None], lam_k2[l][None]],
                               axis=0).astype(F32)
    o_diff = _diff_call(fqT, fk, fvT, kmax, lam_vecs, g_sub[l].reshape(-1, 1),
                        _tile(S, DIFF_QUERY_TILE), KEY_TILE, lam0)

    wr = jnp.pad(jnp.concatenate([w_router[l], w_group[l]], axis=1),
                 ((0, 0), (0, LANES - N_EXPERTS - N_GROUPS)))
    wr_hi = wr.astype(BF16)
    wr_lo = (wr - wr_hi.astype(F32)).astype(BF16)
    b_r = jnp.pad(jnp.concatenate([b_router[l], b_group[l]]), (0, LANES - N_EXPERTS - N_GROUPS))
    x1, h2, gates = _route_call(x, o_dsa, o_diff, mod3, w_out[l].astype(BF16),
                                g_moe[l].reshape(1, D), wr_hi, wr_lo, b_r.reshape(1, LANES),
                                _tile(S, TOKEN_TILE))

    wg = jnp.swapaxes(w_gate[l], 0, 1).reshape(D, N_EXPERTS * D_EXPERT).astype(BF16)
    wu = jnp.swapaxes(w_up[l], 0, 1).reshape(D, N_EXPERTS * D_EXPERT).astype(BF16)
    wd = w_down[l].reshape(N_EXPERTS * D_EXPERT, D).astype(BF16)
    return _moe_call(h2, gates, x1, mod3, wg, wu, wd, g_final.reshape(1, D), _tile(S, TOKEN_TILE),
                     MOE_EXPERTS_PER_STEP)
```

```python
import functools
import math

import jax
import jax.numpy as jnp
import numpy as np
from jax import lax
from jax.experimental import pallas as pl
from jax.experimental.pallas import tpu as pltpu

F32 = jnp.float32
BF16 = jnp.bfloat16

D_MODEL = 1024
DSA_HEADS = 8
DSA_HEAD_DIM = 64
DSA_LATENT = 128
IDX_HEADS = 4
IDX_DIM = 32
TOPK_MAX = 256
DIFF_HEADS = 4
DIFF_QK_DIM = 64
DIFF_V_DIM = 128
N_GROUPS = 4
EXPERTS_PER_GROUP = 8
N_EXPERTS = 32
D_EXPERT = 256
EPS = 1e-6

LANES = 128
BF16_ROWS = 16
POS_SPLIT = 64
NEG_BIG = -1e30
VMEM_LIMIT = 56 * 1024 * 1024
KEY_TILE = 512
VALUE_BISECT_STEPS = 24
UNCHECKED_BISECT_STEPS = 16
COUNT_CHAINS = 4
ATTN_CHAINS = 1

W_DQ = DSA_HEADS * DSA_HEAD_DIM
W_QABS = DSA_HEADS * DSA_LATENT
W_CKV = 2 * LANES
VT_ROWS = DSA_LATENT + BF16_ROWS
ONES_ROW = DSA_LATENT + 2
N_FEATS = 5
KMAX_LANE_CKV = 8
L_MIN = 1e-30
BF16_ROUND_UP = 1.0 + 2.0 ** -7
BOUND_SLACK = 2.0 ** -6
CODE_BISECT_STEPS = 34
TOKEN_TILE = 512
DSA_QUERY_TILE = 256
DIFF_QUERY_TILE = 1024
MOE_EXPERTS_PER_STEP = 8
W_FQ = DIFF_HEADS * 2 * LANES
W_DIFF_QK = DIFF_HEADS * 2 * DIFF_QK_DIM

T_DQ = 0
T_DLAT = T_DQ + W_DQ
T_IQ = T_DLAT + DSA_LATENT
T_IW = T_IQ + IDX_HEADS * IDX_DIM
T_FQ = T_IW + BF16_ROWS
T_FV = T_FQ + W_DIFF_QK
T_ROWS = T_FV + DIFF_HEADS * DIFF_V_DIM
N_IK = 0
N_DLAT = N_IK + LANES
N_FK = N_DLAT + DSA_LATENT
N_COLS = N_FK + W_FQ


def _alibi_slopes(n):
    return [2.0 ** (-8.0 * (i + 1) / n) for i in range(n)]


def _rms(x, g):
    return x * lax.rsqrt(jnp.mean(x * x, axis=-1, keepdims=True) + EPS) * g


def _dot(a, b):
    return jnp.dot(a, b, preferred_element_type=F32)


def _mod_kernel(c_ref, w_ref, b_ref, o_ref):
    c = c_ref[...]
    act = c * jax.nn.sigmoid(c)
    o_ref[...] = jnp.dot(act, w_ref[...], preferred_element_type=F32,
                         precision=lax.Precision.HIGHEST) + b_ref[...]


def _mod_call(c, w_ada, b_ada):
    B, D = c.shape
    n = w_ada.shape[1] // D
    return pl.pallas_call(
        _mod_kernel,
        out_shape=jax.ShapeDtypeStruct((B, n * D), F32),
        grid=(n,),
        in_specs=[pl.BlockSpec((B, D), lambda j: (0, 0)),
                  pl.BlockSpec((D, D), lambda j: (0, j)),
                  pl.BlockSpec((1, D), lambda j: (0, j))],
        out_specs=pl.BlockSpec((B, D), lambda j: (0, j)),
        compiler_params=pltpu.CompilerParams(dimension_semantics=("arbitrary",),
                                             vmem_limit_bytes=VMEM_LIMIT),
        name="mod",
    )(c, w_ada, b_ada)


def _key_feature_rows(pos_row, n_rows):
    r = lax.broadcasted_iota(jnp.int32, (n_rows, pos_row.shape[1]), 0)
    pa = (pos_row // POS_SPLIT).astype(F32)
    pb = (pos_row % POS_SPLIT).astype(F32)
    return jnp.where(r == 0, pa, jnp.where(r == 1, pb, jnp.where(r < N_FEATS, 1.0, 0.0)))


def _query_feature_rows(pos_row, n_rows, slope, shift):
    r = lax.broadcasted_iota(jnp.int32, (n_rows, pos_row.shape[1]), 0)
    pa = (pos_row // POS_SPLIT).astype(F32)
    pb = (pos_row % POS_SPLIT).astype(F32)
    return jnp.where(r == 0, POS_SPLIT * slope,
                     jnp.where(r == 1, slope,
                               jnp.where(r == 2, -POS_SPLIT * slope * pa,
                                         jnp.where(r == 3, -slope * pb,
                                                   jnp.where(r == 4, -shift, 0.0)))))


def _shift_bound(qn2, kmax2, extra):
    b = jnp.sqrt(qn2 * kmax2) + extra
    return b + jnp.abs(b) * BOUND_SLACK + BOUND_SLACK


def _tile_max(kmax_ref, last_tile, lane_idx):
    x = kmax_ref[...]
    t = lax.broadcasted_iota(jnp.int32, x.shape, 0)
    ln = lax.broadcasted_iota(jnp.int32, x.shape, 2)
    x = jnp.where((t <= last_tile) & (ln == lane_idx), x, 0.0)
    return jnp.max(jnp.max(x, axis=0), axis=1, keepdims=True)[0:1, :]


def _proj_kernel(x_ref, mod_ref, g_ref, wn_ref, wt_ref, wpair_ref, gkv_ref, gkvc_ref, fc_ref, sel_ref,
                 qabsT_ref, iqT_ref, iwT_ref, ik_ref, ckv_ref, ckvT_ref, fqT_ref, fk_ref, fvT_ref,
                 kmax_ref, *, tm):
    x = x_ref[...]
    h = _rms(x, g_ref[...]) * (1.0 + mod_ref[1:2, :]) + mod_ref[0:1, :]
    hb = h.astype(BF16)
    hT = h.T.astype(BF16)

    base = pl.program_id(1) * tm
    pos_c = base + lax.broadcasted_iota(jnp.int32, (tm, 1), 0)
    pos_r = base + lax.broadcasted_iota(jnp.int32, (1, tm), 1)
    pa = (pos_c // POS_SPLIT).astype(F32)
    pb = (pos_c % POS_SPLIT).astype(F32)

    def feats(row, width):
        return (fc_ref[row:row + 1, 0:width] + fc_ref[row + 1:row + 2, 0:width] * pa
                + fc_ref[row + 2:row + 3, 0:width] * pb)

    ik_ref[...] = _dot(hb, wn_ref[:, N_IK:N_IK + LANES]).astype(BF16)
    dlat = _dot(hb, wn_ref[:, N_DLAT:N_DLAT + DSA_LATENT])
    ckv_b = _rms(dlat, gkv_ref[...]).astype(BF16)
    ckv_ref[:, 0:LANES] = ckv_b
    ckv_ref[:, LANES:2 * LANES] = feats(0, LANES).astype(BF16)
    fk = _dot(hb, wn_ref[:, N_FK:N_FK + W_FQ])
    fk_ref[...] = (fk + feats(3, W_FQ)).astype(BF16)
    sq = jnp.concatenate([fk.astype(BF16).astype(F32), ckv_b.astype(F32)], axis=1)
    sq_up = (sq * sq * BF16_ROUND_UP).astype(BF16)
    kmax_ref[...] = jnp.broadcast_to(jnp.max(_dot(sq_up, sel_ref[...]), axis=0, keepdims=True),
                                     (8, LANES))

    dqT = _dot(wt_ref[T_DQ:T_DQ + W_DQ, :], hT).astype(BF16)
    for p in range(DSA_HEADS // 2):
        qa = _dot(wpair_ref[p], dqT[p * LANES:(p + 1) * LANES, :])
        qabsT_ref[p * 2 * LANES:(p + 1) * 2 * LANES, :] = qa.astype(BF16)
    dlatT = _dot(wt_ref[T_DLAT:T_DLAT + DSA_LATENT, :], hT)
    inv = lax.rsqrt(jnp.mean(dlatT * dlatT, axis=0, keepdims=True) + EPS)
    kfeat = _key_feature_rows(pos_r, BF16_ROWS).astype(BF16)
    ckvT_ref[0:DSA_LATENT, :] = (dlatT * inv * gkvc_ref[...]).astype(BF16)
    ckvT_ref[DSA_LATENT:VT_ROWS, :] = kfeat
    iqT_ref[...] = _dot(wt_ref[T_IQ:T_IQ + IDX_HEADS * IDX_DIM, :], hT).astype(BF16)
    iwT = _dot(wt_ref[T_IW:T_IW + BF16_ROWS, :], hT)
    iwT_ref[...] = iwT[0:8, :] * (IDX_HEADS ** -0.5 * IDX_DIM ** -0.5)
    fqT = _dot(wt_ref[T_FQ:T_FQ + W_DIFF_QK, :], hT)
    zpad = jnp.zeros((LANES - DIFF_QK_DIM, tm), BF16)
    for s in range(DIFF_HEADS * 2):
        r0 = s * LANES
        fqT_ref[r0:r0 + DIFF_QK_DIM, :] = fqT[s * DIFF_QK_DIM:(s + 1) * DIFF_QK_DIM, :].astype(BF16)
        fqT_ref[r0 + DIFF_QK_DIM:r0 + LANES, :] = zpad
    fvT = _dot(wt_ref[T_FV:T_FV + DIFF_HEADS * DIFF_V_DIM, :], hT)
    for hh in range(DIFF_HEADS):
        fvT_ref[hh, 0:DIFF_V_DIM, :] = fvT[hh * DIFF_V_DIM:(hh + 1) * DIFF_V_DIM, :].astype(BF16)
        fvT_ref[hh, DIFF_V_DIM:VT_ROWS, :] = kfeat


def _feature_consts():
    fc = np.zeros((8, W_FQ), np.float32)
    fc[1, 0] = 1.0
    fc[2, 1] = 1.0
    fc[0, 2:N_FEATS] = 1.0
    for s in range(DIFF_HEADS * 2):
        base = s * LANES + DIFF_QK_DIM
        fc[4, base + 0] = 1.0
        fc[5, base + 1] = 1.0
        fc[3, base + 2:base + N_FEATS] = 1.0
    return jnp.asarray(fc)


def _norm_selector():
    sel = np.zeros((W_FQ + DSA_LATENT, LANES), np.float32)
    for s in range(DIFF_HEADS * 2):
        sel[s * LANES:s * LANES + DIFF_QK_DIM, s] = 1.0
    sel[W_FQ:, KMAX_LANE_CKV] = 1.0
    return jnp.asarray(sel, BF16)


def _proj_weights(w_in, w_uk):
    D = w_in.shape[0]
    pts = np.cumsum([W_DQ, DSA_LATENT, IDX_HEADS * IDX_DIM, IDX_DIM, IDX_HEADS,
                     W_DIFF_QK, W_DIFF_QK])
    dq, dlat, iq, ik, iw, fq, fk, fv = jnp.split(w_in, list(pts), axis=1)
    ik4 = jnp.tile(ik, (1, IDX_HEADS))
    fke = jnp.pad(fk.reshape(D, DIFF_HEADS * 2, DIFF_QK_DIM),
                  ((0, 0), (0, 0), (0, LANES - DIFF_QK_DIM))).reshape(D, W_FQ)
    wn = jnp.concatenate([ik4, dlat, fke], axis=1).astype(BF16)
    iwp = jnp.pad(iw, ((0, 0), (0, BF16_ROWS - IDX_HEADS)))
    wt = jnp.concatenate([dq, dlat, iq, iwp, fq * (DIFF_QK_DIM ** -0.5), fv], axis=1).T.astype(BF16)
    uk = w_uk * (DSA_HEAD_DIM ** -0.5)
    z = jnp.zeros_like(uk[0])
    pairs = [jnp.concatenate([jnp.concatenate([uk[2 * p], z], axis=1),
                              jnp.concatenate([z, uk[2 * p + 1]], axis=1)], axis=0)
             for p in range(DSA_HEADS // 2)]
    return wn, wt, jnp.stack(pairs).astype(BF16)


def _proj_call(x, mod3, g_attn, wn, wt, w_pair, g_kv, tm):
    B, S, D = x.shape
    nt = S // tm
    fc = _feature_consts()
    sel = _norm_selector()
    kern = functools.partial(_proj_kernel, tm=tm)
    tok = lambda w: pl.BlockSpec((None, tm, w), lambda b, i: (b, i, 0))
    tokT = lambda r: pl.BlockSpec((None, r, tm), lambda b, i: (b, 0, i))
    full = lambda a: pl.BlockSpec(a.shape, lambda b, i: (0,) * a.ndim)
    g_kv_col = g_kv.reshape(-1, 1)
    out_shape = [jax.ShapeDtypeStruct((B, W_QABS, S), BF16),
                 jax.ShapeDtypeStruct((B, LANES, S), BF16),
                 jax.ShapeDtypeStruct((B, 8, S), F32),
                 jax.ShapeDtypeStruct((B, S, LANES), BF16),
                 jax.ShapeDtypeStruct((B, S, W_CKV), BF16),
                 jax.ShapeDtypeStruct((B, nt, VT_ROWS, tm), BF16),
                 jax.ShapeDtypeStruct((B, W_FQ, S), BF16),
                 jax.ShapeDtypeStruct((B, S, W_FQ), BF16),
                 jax.ShapeDtypeStruct((B, DIFF_HEADS, nt, VT_ROWS, tm), BF16),
                 jax.ShapeDtypeStruct((B, nt, 8, LANES), F32)]
    out_specs = [tokT(W_QABS), tokT(LANES), tokT(8), tok(LANES), tok(W_CKV),
                 pl.BlockSpec((None, None, VT_ROWS, tm), lambda b, i: (b, i, 0, 0)),
                 tokT(W_FQ), tok(W_FQ),
                 pl.BlockSpec((None, DIFF_HEADS, None, VT_ROWS, tm), lambda b, i: (b, 0, i, 0, 0)),
                 pl.BlockSpec((None, None, 8, LANES), lambda b, i: (b, i, 0, 0))]
    return pl.pallas_call(
        kern,
        out_shape=out_shape,
        grid=(B, nt),
        in_specs=[tok(D),
                  pl.BlockSpec((None, 6, D), lambda b, i: (b, 0, 0)),
                  full(g_attn), full(wn), full(wt), full(w_pair), full(g_kv), full(g_kv_col),
                  full(fc), full(sel)],
        out_specs=out_specs,
        compiler_params=pltpu.CompilerParams(dimension_semantics=("arbitrary", "arbitrary"),
                                             vmem_limit_bytes=VMEM_LIMIT),
        name="proj",
    )(x, mod3, g_attn, wn, wt, w_pair, g_kv, g_kv_col, fc, sel)


def _float_code(x):
    b = lax.bitcast_convert_type(x, jnp.int32)
    return b ^ (lax.shift_right_arithmetic(b, 31) & jnp.int32(0x7FFFFFFF))


def _float_decode(c):
    b = c ^ (lax.shift_right_arithmetic(c, 31) & jnp.int32(0x7FFFFFFF))
    return lax.bitcast_convert_type(b, F32)


def _dsa_kernel(qabsT_ref, iqT_ref, iwT_ref, ik_ref, ckv_ref, ckvT_ref, kmax_ref, lstrict_ref, wuvT_ref,
                o_ref, sc_ref, qst_ref, qt_ref, acc_ref, m_ref, tmp_ref, *, tq, tk, topk):
    qi = pl.program_id(1)
    nkb = (qi * tq) // tk + 1
    kf = float(topk)
    slopes = _alibi_slopes(DSA_HEADS)
    q_pos = qi * tq + lax.broadcasted_iota(jnp.int32, (1, tq), 1)

    iqT = iqT_ref[...]
    rowi = lax.broadcasted_iota(jnp.int32, (LANES, tq), 0)
    for hh in range(IDX_HEADS):
        qst_ref[:, hh * tq:(hh + 1) * tq] = jnp.where((rowi // IDX_DIM) == hh, iqT, jnp.zeros_like(iqT))
    iw = iwT_ref[...]
    wrow = [iw[hh:hh + 1, :] for hh in range(IDX_HEADS)]

    def scores(kb):
        kblk = ik_ref[pl.ds(pl.multiple_of(kb * tk, tk), tk), :]
        a = _dot(kblk, qst_ref[...])
        sc = jnp.maximum(a[:, 0:tq], 0.0) * wrow[0]
        for hh in range(1, IDX_HEADS):
            sc = sc + jnp.maximum(a[:, hh * tq:(hh + 1) * tq], 0.0) * wrow[hh]
        return sc

    def score_body(kb, carry):
        mn, mx = carry
        sc = scores(kb)
        sc_ref[kb] = sc
        return (jnp.minimum(mn, jnp.min(sc, axis=0, keepdims=True)),
                jnp.maximum(mx, jnp.max(sc, axis=0, keepdims=True)))

    last = nkb - 1
    mn, mx = lax.fori_loop(0, last // 2, lambda j, c: score_body(2 * j + 1, score_body(2 * j, c)),
                           (jnp.full((1, tq), jnp.inf, F32), jnp.full((1, tq), -jnp.inf, F32)))
    mn, mx = lax.cond(last % 2 == 1, lambda c: score_body(last - 1, c), lambda c: c, (mn, mx))
    sc = scores(last)
    causal = lax.broadcasted_iota(jnp.int32, (tk, tq), 0) + last * tk <= q_pos
    sc_ref[last] = jnp.where(causal, sc, jnp.nan)
    mn = jnp.minimum(mn, jnp.min(jnp.where(causal, sc, jnp.inf), axis=0, keepdims=True))
    mx = jnp.maximum(mx, jnp.max(jnp.where(causal, sc, -jnp.inf), axis=0, keepdims=True))

    def block_count(kb, pred):
        parts = [None] * COUNT_CHAINS
        for r in range(tk // 8):
            v = jnp.where(pred(sc_ref[kb, r * 8:(r + 1) * 8, :]), 1.0, 0.0)
            c = r % COUNT_CHAINS
            parts[c] = v if parts[c] is None else parts[c] + v
        return (parts[0] + parts[1]) + (parts[2] + parts[3])

    def count(pred):
        acc = lax.fori_loop(0, nkb, lambda kb, acc: acc + block_count(kb, pred),
                            jnp.zeros((8, tq), F32))
        return jnp.sum(acc, axis=0, keepdims=True)

    n_causal = (q_pos + 1).astype(F32)
    done0 = n_causal <= kf

    def bisect(st, value_mid):
        lo, hi, c_lo, theta, done = st
        th8 = jnp.broadcast_to(theta, (8, tq))
        c = count(lambda v: v >= th8)
        ge = c >= kf
        live = done == 0.0
        lo = jnp.where(live & ge, theta, lo)
        c_lo = jnp.where(live & ge, c, c_lo)
        hi = jnp.where(live & jnp.logical_not(ge), theta, hi)
        if value_mid:
            nxt = 0.5 * lo + 0.5 * hi
        else:
            cl, ch = _float_code(lo), _float_code(hi)
            nxt = _float_decode((cl & ch) + lax.shift_right_arithmetic(cl ^ ch, 1))
        inside = (nxt > lo) & (nxt < hi)
        done = jnp.where((c_lo == kf) | jnp.logical_not(inside), 1.0, done)
        return lo, hi, c_lo, nxt, done

    def checked(value_mid, max_steps):
        def cond(c):
            return jnp.logical_and(c[1] > 0.0, c[2] < max_steps)

        def body(c):
            st = bisect(bisect(c[0], value_mid), value_mid)
            return st, jnp.sum(1.0 - st[4]), c[2] + 2
        return cond, body

    zero8 = jnp.zeros((8, tq), F32)
    c_ge0 = count(lambda v: v >= zero8)
    c_gt0 = count(lambda v: v > zero8)
    live0 = jnp.logical_not(done0)
    at0 = live0 & (c_ge0 >= kf)
    tie0 = at0 & (c_gt0 < kf)
    below0 = live0 & (c_ge0 < kf)
    lo0 = jnp.where(at0, 0.0, mn)
    hi0 = jnp.where(below0, 0.0, mx)
    st = (lo0, hi0, jnp.where(at0, c_ge0, n_causal),
          jnp.where(below0, 0.5 * lo0 + 0.5 * hi0, mx), jnp.where(done0 | tie0, 1.0, 0.0))
    st = lax.fori_loop(0, UNCHECKED_BISECT_STEPS, lambda i, s: bisect(s, True), st)
    cond, body = checked(True, VALUE_BISECT_STEPS - UNCHECKED_BISECT_STEPS)
    st, active, _ = lax.while_loop(cond, body, (st, jnp.sum(1.0 - st[4]), jnp.int32(0)))
    cond, body = checked(False, CODE_BISECT_STEPS)
    st, _, _ = lax.while_loop(cond, body, (st, active, jnp.int32(0)))
    tau, c_ge = st[0], st[2]

    over = c_ge > kf

    @pl.when(jnp.max(c_ge) > kf)
    def _():
        tau8 = jnp.broadcast_to(tau, (8, tq))
        tmp_ref[0:1, :] = c_gt0

        @pl.when(jnp.max(jnp.where(over & jnp.logical_not(tie0), 1.0, 0.0)) > 0.0)
        def _():
            tmp_ref[0:1, :] = jnp.where(tie0, c_gt0, count(lambda v: v > tau8))

        quota = kf - tmp_ref[0:1, :]

        def tie_body(kb, seen):
            cnt = jnp.sum(block_count(kb, lambda v: v == tau8), axis=0, keepdims=True)
            inside = over & (seen < quota) & (seen + cnt > quota)
            gone = over & (seen >= quota) & (cnt > 0.0)
            flag = jnp.max(jnp.where(inside, 2.0, jnp.where(gone, 1.0, 0.0)))

            @pl.when(flag > 1.5)
            def _():
                s = sc_ref[kb]
                eq = s == tau
                rank = _dot(lstrict_ref[...], jnp.where(eq, 1.0, 0.0).astype(BF16)) + seen
                sc_ref[kb] = jnp.where(eq & over & (rank >= quota), jnp.nan, s)

            @pl.when(flag == 1.0)
            def _():
                s = sc_ref[kb]
                sc_ref[kb] = jnp.where((s == tau) & gone, jnp.nan, s)

            return seen + cnt

        lax.fori_loop(0, nkb, tie_body, jnp.zeros((1, tq), F32))

    tau8 = jnp.broadcast_to(tau, (8, tq))
    sub = lax.broadcasted_iota(jnp.int32, (8, tq), 0).astype(F32)

    def nearest_body(kb, acc):
        parts = [None] * COUNT_CHAINS
        for r in range(tk // 8):
            idx = sub + (kb * tk + r * 8).astype(F32)
            v = jnp.where(sc_ref[kb, r * 8:(r + 1) * 8, :] >= tau8, idx, -1.0)
            c = r % COUNT_CHAINS
            parts[c] = v if parts[c] is None else jnp.maximum(parts[c], v)
        return jnp.maximum(acc, jnp.maximum(jnp.maximum(parts[0], parts[1]),
                                            jnp.maximum(parts[2], parts[3])))

    last_sel = jnp.max(lax.fori_loop(0, nkb, nearest_body, jnp.full((8, tq), -1.0, F32)),
                       axis=0, keepdims=True)
    d_min = q_pos.astype(F32) - last_sel
    kmax2 = _tile_max(kmax_ref, nkb - 1, KMAX_LANE_CKV)
    for hh in range(DSA_HEADS):
        qh = qabsT_ref[hh * DSA_LATENT:(hh + 1) * DSA_LATENT, :]
        qf = qh.astype(F32)
        shift = _shift_bound(jnp.sum(qf * qf, axis=0, keepdims=True), kmax2, -slopes[hh] * d_min)
        qt_ref[0:DSA_LATENT, hh * tq:(hh + 1) * tq] = qh
        qt_ref[DSA_LATENT:VT_ROWS, hh * tq:(hh + 1) * tq] = _query_feature_rows(
            q_pos, BF16_ROWS, slopes[hh], shift).astype(BF16)
    qt_ref[VT_ROWS:2 * DSA_LATENT, :] = jnp.zeros((2 * DSA_LATENT - VT_ROWS, DSA_HEADS * tq), BF16)
    acc_ref[...] = jnp.zeros_like(acc_ref)

    def fast_pv(kb):
        kx = ckv_ref[pl.ds(pl.multiple_of(kb * tk, tk), tk), :]
        s = _dot(kx, qt_ref[...])
        keep = sc_ref[kb] >= tau
        ps = [jnp.exp(jnp.where(keep, s[:, hh * tq:(hh + 1) * tq], NEG_BIG)).astype(BF16)
              for hh in range(DSA_HEADS)]
        return _dot(ckvT_ref[kb], jnp.concatenate(ps, axis=1))

    def fast_pair(j, carry):
        acc_ref[...] += fast_pv(2 * j) + fast_pv(2 * j + 1)
        return carry

    lax.fori_loop(0, nkb // 2, fast_pair, 0)

    @pl.when(nkb % 2 == 1)
    def _():
        acc_ref[...] += fast_pv(nkb - 1)

    def attn_body(kb, carry):
        kx = ckv_ref[pl.ds(pl.multiple_of(kb * tk, tk), tk), :]
        vT = ckvT_ref[kb]
        keep = sc_ref[kb] >= tau
        qt = qt_ref[...]
        m_all = m_ref[...]
        m_out, alphas, pvs = [], [], []
        for ch in range(ATTN_CHAINS):
            hpc = DSA_HEADS // ATTN_CHAINS
            l0 = ch * hpc * tq
            s = _dot(kx, qt[:, l0:l0 + hpc * tq])
            ps = []
            for j in range(hpc):
                c0 = l0 + j * tq
                sh = jnp.where(keep, s[:, j * tq:(j + 1) * tq], NEG_BIG)
                m_old = m_all[:, c0:c0 + tq]
                m_new = jnp.maximum(m_old, jnp.max(sh, axis=0, keepdims=True))
                ps.append(jnp.exp(sh - m_new).astype(BF16))
                alphas.append(jnp.exp(m_old - m_new))
                m_out.append(m_new)
            pvs.append(_dot(vT, jnp.concatenate(ps, axis=1)))
        m_ref[...] = jnp.concatenate(m_out, axis=1)
        acc_ref[...] = jnp.concatenate(alphas, axis=1) * acc_ref[...] + jnp.concatenate(pvs, axis=1)
        return carry

    denom_ok = jnp.min(acc_ref[ONES_ROW:ONES_ROW + 1, :]) > L_MIN

    @pl.when(jnp.logical_not(denom_ok))
    def _():
        acc_ref[...] = jnp.zeros_like(acc_ref)
        m_ref[...] = jnp.full_like(m_ref, NEG_BIG)
        lax.fori_loop(0, nkb, attn_body, 0)

    acc = acc_ref[...]
    lat = acc[0:DSA_LATENT, :] / acc[ONES_ROW:ONES_ROW + 1, :]
    for p in range(DSA_HEADS // 2):
        pair = lat[:, 2 * p * tq:(2 * p + 2) * tq]
        pair = jnp.concatenate([pair[:, 0:tq], pair[:, tq:2 * tq]], axis=0).astype(BF16)
        o_ref[:, p * LANES:(p + 1) * LANES] = _dot(wuvT_ref[p], pair).T.astype(BF16)


def _dsa_call(qabsT, iqT, iwT, ik4, ckv, ckvT, kmax, lstrict, wuvT_pair, tq, tk, topk):
    B, _, S = qabsT.shape
    nkb = S // tk
    kern = functools.partial(_dsa_kernel, tq=tq, tk=tk, topk=topk)
    blkT = lambda r: pl.BlockSpec((None, r, tq), lambda b, i: (b, 0, i))
    per_b = lambda w: pl.BlockSpec((None, S, w), lambda b, i: (b, 0, 0))
    full = lambda a: pl.BlockSpec(a.shape, lambda b, i: (0,) * a.ndim)
    return pl.pallas_call(
        kern,
        out_shape=jax.ShapeDtypeStruct((B, S, W_DQ), BF16),
        grid=(B, S // tq),
        in_specs=[blkT(W_QABS), blkT(LANES), blkT(8), per_b(LANES), per_b(W_CKV),
                  pl.BlockSpec((None, nkb, VT_ROWS, tk), lambda b, i: (b, 0, 0, 0)),
                  pl.BlockSpec((None, nkb, 8, LANES), lambda b, i: (b, 0, 0, 0)),
                  full(lstrict), full(wuvT_pair)],
        out_specs=pl.BlockSpec((None, tq, W_DQ), lambda b, i: (b, i, 0)),
        scratch_shapes=[pltpu.VMEM((nkb, tk, tq), F32),
                        pltpu.VMEM((LANES, IDX_HEADS * tq), BF16),
                        pltpu.VMEM((2 * DSA_LATENT, DSA_HEADS * tq), BF16),
                        pltpu.VMEM((VT_ROWS, DSA_HEADS * tq), F32),
                        pltpu.VMEM((1, DSA_HEADS * tq), F32),
                        pltpu.VMEM((8, tq), F32)],
        compiler_params=pltpu.CompilerParams(dimension_semantics=("arbitrary", "arbitrary"),
                                             vmem_limit_bytes=VMEM_LIMIT),
        name="dsa",
    )(qabsT, iqT, iwT, ik4, ckv, ckvT, kmax, lstrict, wuvT_pair)


def _diff_kernel(fqT_ref, fk_ref, fvT_ref, kmax_ref, lam_ref, gsub_ref, o_ref, qt_ref, acc_ref, m_ref,
                 *, tq, tk, lam0):
    hd = pl.program_id(1)
    qi = pl.program_id(2)
    q_pos = qi * tq + lax.broadcasted_iota(jnp.int32, (1, tq), 1)
    n_full = (qi * tq) // tk
    n_diag = (tq + tk - 1) // tk
    slopes = _alibi_slopes(DIFF_HEADS)
    slope = jnp.float32(slopes[DIFF_HEADS - 1])
    for i in range(DIFF_HEADS - 1):
        slope = jnp.where(hd == i, slopes[i], slope)

    for m in range(2):
        r0 = m * LANES
        qh = fqT_ref[r0:r0 + DIFF_QK_DIM, :]
        qf = qh.astype(F32)
        shift = _shift_bound(jnp.sum(qf * qf, axis=0, keepdims=True),
                             _tile_max(kmax_ref, n_full + n_diag - 1, 2 * hd + m), 0.0)
        qt_ref[r0:r0 + DIFF_QK_DIM, :] = qh
        qt_ref[r0 + DIFF_QK_DIM:r0 + DIFF_QK_DIM + BF16_ROWS, :] = _query_feature_rows(
            q_pos, BF16_ROWS, slope, shift).astype(BF16)
        qt_ref[r0 + DIFF_QK_DIM + BF16_ROWS:r0 + LANES, :] = jnp.zeros(
            (LANES - DIFF_QK_DIM - BF16_ROWS, tq), BF16)
    acc_ref[...] = jnp.zeros_like(acc_ref)

    def fast_pv(kb, masked):
        kx = fk_ref[pl.ds(pl.multiple_of(kb * tk, tk), tk), :]
        vT = fvT_ref[kb]
        out = []
        for m in range(2):
            s = _dot(kx[:, m * LANES:(m + 1) * LANES], qt_ref[m * LANES:(m + 1) * LANES, :])
            if masked:
                causal = lax.broadcasted_iota(jnp.int32, (tk, tq), 0) + kb * tk <= q_pos
                s = jnp.where(causal, s, NEG_BIG)
            out.append(_dot(vT, jnp.exp(s).astype(BF16)))
        return out

    def accumulate(*pvs):
        for m in range(2):
            acc_ref[m] += functools.reduce(lambda a, b: a + b, [pv[m] for pv in pvs])

    def fast_pair(j, carry):
        accumulate(fast_pv(2 * j, False), fast_pv(2 * j + 1, False))
        return carry

    lax.fori_loop(0, n_full // 2, fast_pair, 0)
    assert n_diag in (1, 2)

    @pl.when(n_full % 2 == 1)
    def _():
        accumulate(fast_pv(n_full - 1, False), fast_pv(n_full, True))
        if n_diag == 2:
            accumulate(fast_pv(n_full + 1, True))

    @pl.when(n_full % 2 == 0)
    def _():
        accumulate(*[fast_pv(n_full + d, True) for d in range(n_diag)])

    def block(kb, masked):
        kx = fk_ref[pl.ds(pl.multiple_of(kb * tk, tk), tk), :]
        vT = fvT_ref[kb]
        qT = qt_ref[...]
        m_all = m_ref[...]
        new = []
        for m in range(2):
            s = _dot(kx[:, m * LANES:(m + 1) * LANES], qT[m * LANES:(m + 1) * LANES, :])
            if masked:
                causal = lax.broadcasted_iota(jnp.int32, (tk, tq), 0) + kb * tk <= q_pos
                s = jnp.where(causal, s, NEG_BIG)
            m_old = m_all[m]
            m_new = jnp.maximum(m_old, jnp.max(s, axis=0, keepdims=True))
            p = jnp.exp(s - m_new).astype(BF16)
            new.append((m_new, jnp.exp(m_old - m_new), _dot(vT, p)))
        for m in range(2):
            m_ref[m] = new[m][0]
            acc_ref[m] = new[m][1] * acc_ref[m] + new[m][2]

    def body(kb, carry):
        block(kb, False)
        return carry

    denom_ok = jnp.min(acc_ref[:, ONES_ROW:ONES_ROW + 1, :]) > L_MIN

    @pl.when(jnp.logical_not(denom_ok))
    def _():
        acc_ref[...] = jnp.zeros_like(acc_ref)
        m_ref[...] = jnp.full_like(m_ref, NEG_BIG)
        lax.fori_loop(0, n_full, body, 0)
        for d in range(n_diag):
            block(n_full + d, True)

    lv = lam_ref[...]
    lam = (jnp.exp(jnp.sum(lv[0:1] * lv[1:2], axis=-1, keepdims=True))
           - jnp.exp(jnp.sum(lv[2:3] * lv[3:4], axis=-1, keepdims=True)) + lam0)
    a1 = acc_ref[0]
    a2 = acc_ref[1]
    o = (a1[0:DIFF_V_DIM, :] / a1[ONES_ROW:ONES_ROW + 1, :]
         - lam * (a2[0:DIFF_V_DIM, :] / a2[ONES_ROW:ONES_ROW + 1, :]))
    o = o * lax.rsqrt(jnp.mean(o * o, axis=0, keepdims=True) + EPS) * gsub_ref[...] * (1.0 - lam0)
    o_ref[...] = o.T.astype(BF16)


def _diff_call(fqT, fk, fvT, kmax, lam_vecs, g_sub_col, tq, tk, lam0):
    B, S, _ = fk.shape
    nkb = S // tk
    kern = functools.partial(_diff_kernel, tq=tq, tk=tk, lam0=lam0)
    full = lambda a: pl.BlockSpec(a.shape, lambda b, h, i: (0,) * a.ndim)
    return pl.pallas_call(
        kern,
        out_shape=jax.ShapeDtypeStruct((B, S, DIFF_HEADS * DIFF_V_DIM), BF16),
        grid=(B, DIFF_HEADS, S // tq),
        in_specs=[pl.BlockSpec((None, 2 * LANES, tq), lambda b, h, i: (b, h, i)),
                  pl.BlockSpec((None, S, 2 * LANES), lambda b, h, i: (b, 0, h)),
                  pl.BlockSpec((None, None, nkb, VT_ROWS, tk), lambda b, h, i: (b, h, 0, 0, 0)),
                  pl.BlockSpec((None, nkb, 8, LANES), lambda b, h, i: (b, 0, 0, 0)),
                  full(lam_vecs), full(g_sub_col)],
        out_specs=pl.BlockSpec((None, tq, DIFF_V_DIM), lambda b, h, i: (b, i, h)),
        scratch_shapes=[pltpu.VMEM((2 * LANES, tq), BF16),
                        pltpu.VMEM((2, VT_ROWS, tq), F32),
                        pltpu.VMEM((2, 1, tq), F32)],
        compiler_params=pltpu.CompilerParams(
            dimension_semantics=("arbitrary", "arbitrary", "arbitrary"),
            vmem_limit_bytes=VMEM_LIMIT),
        name="diff",
    )(fqT, fk, fvT, kmax, lam_vecs, g_sub_col)


def _route_kernel(x_ref, od_ref, of_ref, mod_ref, wout_ref, g_ref, wr_hi_ref, wr_lo_ref, br_ref,
                  x1_ref, h_ref, gates_ref, *, tm):
    attn = (_dot(od_ref[...], wout_ref[0:W_DQ, :]) + _dot(of_ref[...], wout_ref[W_DQ:, :]))
    x1 = x_ref[...] + mod_ref[2:3, :] * attn
    x1_ref[...] = x1
    h = _rms(x1, g_ref[...]) * (1.0 + mod_ref[4:5, :]) + mod_ref[3:4, :]
    h_hi = h.astype(BF16)
    h_ref[...] = h_hi
    h_lo = (h - h_hi.astype(F32)).astype(BF16)
    w_hi = wr_hi_ref[...]
    logits = (_dot(h_hi, w_hi) + _dot(h_lo, w_hi) + _dot(h_hi, wr_lo_ref[...])) + br_ref[...]

    lane = lax.broadcasted_iota(jnp.int32, (tm, LANES), 1)
    big = jnp.int32(4 * LANES)
    neg = -jnp.inf
    is_g = (lane >= N_EXPERTS) & (lane < N_EXPERTS + N_GROUPS)
    gl = jnp.where(is_g, logits, neg)
    gmax = jnp.max(gl, axis=-1, keepdims=True)
    g_lane = jnp.min(jnp.where(gl == gmax, lane, big), axis=-1, keepdims=True)
    g_sel = g_lane - N_EXPERTS
    p_g = 1.0 / jnp.sum(jnp.where(is_g, jnp.exp(gl - gmax), 0.0), axis=-1, keepdims=True)
    in_grp = (lane < N_EXPERTS) & ((lane // EXPERTS_PER_GROUP) == g_sel)
    el = jnp.where(in_grp, logits, neg)
    v1 = jnp.max(el, axis=-1, keepdims=True)
    i1 = jnp.min(jnp.where(el == v1, lane, big), axis=-1, keepdims=True)
    el2 = jnp.where(lane == i1, neg, el)
    v2 = jnp.max(el2, axis=-1, keepdims=True)
    i2 = jnp.min(jnp.where(el2 == v2, lane, big), axis=-1, keepdims=True)
    e = jnp.exp(v2 - v1)
    w1 = 1.0 / (1.0 + e)
    w2 = e * w1
    gates_ref[...] = jnp.where(lane == i1, w1 * p_g, jnp.where(lane == i2, w2 * p_g, 0.0))


def _route_call(x, o_dsa, o_diff, mod3, w_out, g_moe, wr_hi, wr_lo, b_r, tm):
    B, S, D = x.shape
    kern = functools.partial(_route_kernel, tm=tm)
    tok = lambda w: pl.BlockSpec((None, tm, w), lambda b, i: (b, i, 0))
    full = lambda a: pl.BlockSpec(a.shape, lambda b, i: (0,) * a.ndim)
    return pl.pallas_call(
        kern,
        out_shape=[jax.ShapeDtypeStruct((B, S, D), F32), jax.ShapeDtypeStruct((B, S, D), BF16),
                   jax.ShapeDtypeStruct((B, S, LANES), F32)],
        grid=(B, S // tm),
        in_specs=[tok(D), tok(W_DQ), tok(DIFF_HEADS * DIFF_V_DIM),
                  pl.BlockSpec((None, 6, D), lambda b, i: (b, 0, 0)),
                  full(w_out), full(g_moe), full(wr_hi), full(wr_lo), full(b_r)],
        out_specs=[tok(D), tok(D), tok(LANES)],
        compiler_params=pltpu.CompilerParams(dimension_semantics=("arbitrary", "arbitrary"),
                                             vmem_limit_bytes=VMEM_LIMIT),
        name="route",
    )(x, o_dsa, o_diff, mod3, w_out, g_moe, wr_hi, wr_lo, b_r)


def _moe_kernel(h_ref, gates_ref, x1_ref, mod_ref, wg_ref, wu_ref, wd_ref, gf_ref, o_ref, acc_ref,
                *, tm, ec):
    j = pl.program_id(2)
    nc = pl.num_programs(2)

    @pl.when(j == 0)
    def _():
        acc_ref[...] = jnp.zeros_like(acc_ref)

    h = h_ref[...]
    hg = _dot(h, wg_ref[...])
    hu = _dot(h, wu_ref[...])
    hid = hg * jax.nn.sigmoid(hg) * hu
    gates = gates_ref[...]
    lane = lax.broadcasted_iota(jnp.int32, (tm, LANES), 1)
    parts = []
    for e in range(ec):
        gcol = jnp.sum(jnp.where(lane == j * ec + e, gates, 0.0), axis=-1, keepdims=True)
        parts.append((hid[:, e * D_EXPERT:(e + 1) * D_EXPERT] * gcol).astype(BF16))
    hs = jnp.concatenate(parts, axis=1)
    acc_ref[...] += _dot(hs, wd_ref[...])

    @pl.when(j == nc - 1)
    def _():
        x2 = x1_ref[...] + mod_ref[5:6, :] * acc_ref[...]
        o_ref[...] = _rms(x2, gf_ref[...])


def _moe_call(h, gates, x1, mod3, wg, wu, wd, g_final, tm, ec):
    B, S, D = x1.shape
    kern = functools.partial(_moe_kernel, tm=tm, ec=ec)
    tok = lambda w: pl.BlockSpec((None, tm, w), lambda b, i, j: (b, i, 0))
    return pl.pallas_call(
        kern,
        out_shape=jax.ShapeDtypeStruct((B, S, D), F32),
        grid=(B, S // tm, N_EXPERTS // ec),
        in_specs=[tok(D), tok(LANES), tok(D),
                  pl.BlockSpec((None, 6, D), lambda b, i, j: (b, 0, 0)),
                  pl.BlockSpec((D, ec * D_EXPERT), lambda b, i, j: (0, j)),
                  pl.BlockSpec((D, ec * D_EXPERT), lambda b, i, j: (0, j)),
                  pl.BlockSpec((ec * D_EXPERT, D), lambda b, i, j: (j, 0)),
                  pl.BlockSpec((1, D), lambda b, i, j: (0, 0))],
        out_specs=tok(D),
        scratch_shapes=[pltpu.VMEM((tm, D), F32)],
        compiler_params=pltpu.CompilerParams(
            dimension_semantics=("arbitrary", "arbitrary", "arbitrary"),
            vmem_limit_bytes=VMEM_LIMIT),
        name="moe",
    )(h, gates, x1, mod3, wg, wu, wd, g_final)


def _tile(n, pref):
    t = min(n, pref)
    assert n % t == 0, (n, t)
    return t


def kernel(x, c, w_ada, b_ada, g_attn, w_in, g_kv, w_uk, w_uv, lam_q1, lam_k1, lam_q2, lam_k2,
           g_sub, w_out, g_moe, w_group, b_group, w_router, b_router, w_gate, w_up, w_down, g_final):
    B, S, D = x.shape
    assert D == D_MODEL and w_ada.shape[0] == 1
    assert S % KEY_TILE == 0 and S <= POS_SPLIT * 256
    topk = min(TOPK_MAX, S // 4)
    l = 0
    lam0 = 0.8 - 0.6 * math.exp(-0.3 * l)

    mod3 = _mod_call(c, w_ada[l], b_ada[l].reshape(1, -1)).reshape(B, 6, D)

    wn, wt, w_pair = _proj_weights(w_in[l], w_uk[l])
    qabsT, iqT, iwT, ik4, ckv, ckvT, fqT, fk, fvT, kmax = _proj_call(
        x, mod3, g_attn[l].reshape(1, D), wn, wt, w_pair, g_kv[l].reshape(1, -1), KEY_TILE)

    lstrict = jnp.asarray(np.tril(np.ones((KEY_TILE, KEY_TILE), np.float32), -1), BF16)
    uvT = jnp.swapaxes(w_uv[l], 1, 2)
    z = jnp.zeros_like(uvT[0])
    wuvT_pair = jnp.stack([
        jnp.concatenate([jnp.concatenate([uvT[2 * p], z], axis=1),
                         jnp.concatenate([z, uvT[2 * p + 1]], axis=1)], axis=0)
        for p in range(DSA_HEADS // 2)]).astype(BF16)
    o_dsa = _dsa_call(qabsT, iqT, iwT, ik4, ckv, ckvT, kmax, lstrict, wuvT_pair, DSA_QUERY_TILE, KEY_TILE,
                      topk)

    lam_vecs = jnp.concatenate([lam_q1[l][None], lam_k1[l][None], lam_q2[l][None], lam_k2[l][None]],
                               axis=0).astype(F32)
    o_diff = _diff_call(fqT, fk, fvT, kmax, lam_vecs, g_sub[l].reshape(-1, 1),
                        _tile(S, DIFF_QUERY_TILE), KEY_TILE, lam0)

    wr = jnp.pad(jnp.concatenate([w_router[l], w_group[l]], axis=1),
                 ((0, 0), (0, LANES - N_EXPERTS - N_GROUPS)))
    wr_hi = wr.astype(BF16)
    wr_lo = (wr - wr_hi.astype(F32)).astype(BF16)
    b_r = jnp.pad(jnp.concatenate([b_router[l], b_group[l]]), (0, LANES - N_EXPERTS - N_GROUPS))
    x1, h2, gates = _route_call(x, o_dsa, o_diff, mod3, w_out[l].astype(BF16),
                                g_moe[l].reshape(1, D), wr_hi, wr_lo, b_r.reshape(1, LANES),
                                _tile(S, TOKEN_TILE))

    wg = jnp.swapaxes(w_gate[l], 0, 1).reshape(D, N_EXPERTS * D_EXPERT).astype(BF16)
    wu = jnp.swapaxes(w_up[l], 0, 1).reshape(D, N_EXPERTS * D_EXPERT).astype(BF16)
    wd = w_down[l].reshape(N_EXPERTS * D_EXPERT, D).astype(BF16)
    return _moe_call(h2, gates, x1, mod3, wg, wu, wd, g_final.reshape(1, D), _tile(S, TOKEN_TILE),
                     MOE_EXPERTS_PER_STEP)
```

```python
import functools
import math

import jax
import jax.numpy as jnp
import numpy as np
from jax import lax
from jax.experimental import pallas as pl
from jax.experimental.pallas import tpu as pltpu

F32 = jnp.float32
BF16 = jnp.bfloat16

D_MODEL = 1024
DSA_HEADS = 8
DSA_HEAD_DIM = 64
DSA_LATENT = 128
IDX_HEADS = 4
IDX_DIM = 32
TOPK_MAX = 256
DIFF_HEADS = 4
DIFF_QK_DIM = 64
DIFF_V_DIM = 128
N_GROUPS = 4
EXPERTS_PER_GROUP = 8
N_EXPERTS = 32
D_EXPERT = 256
EPS = 1e-6

LANES = 128
BF16_ROWS = 16
POS_SPLIT = 64
NEG_BIG = -1e30
VMEM_LIMIT = 56 * 1024 * 1024
KEY_TILE = 512
VALUE_BISECT_STEPS = 24
UNCHECKED_BISECT_STEPS = 16
COUNT_CHAINS = 4

W_DQ = DSA_HEADS * DSA_HEAD_DIM
W_QABS = DSA_HEADS * DSA_LATENT
W_CKV = 2 * LANES
VT_ROWS = DSA_LATENT + BF16_ROWS
ONES_ROW = DSA_LATENT + 2
N_FEATS = 5
KMAX_LANE_CKV = 8
L_MIN = 1e-30
BF16_ROUND_UP = 1.0 + 2.0 ** -7
BOUND_SLACK = 2.0 ** -6
CODE_BISECT_STEPS = 34
TOKEN_TILE = 512
DSA_QUERY_TILE = 256
DIFF_QUERY_TILE = 1024
MOE_EXPERTS_PER_STEP = 8
MOE_TOKEN_TILE = 512
W_FQ = DIFF_HEADS * 2 * LANES
W_DIFF_QK = DIFF_HEADS * 2 * DIFF_QK_DIM

T_DQ = 0
T_DLAT = T_DQ + W_DQ
T_IQ = T_DLAT + DSA_LATENT
T_IW = T_IQ + IDX_HEADS * IDX_DIM
T_FQ = T_IW + BF16_ROWS
T_FV = T_FQ + W_DIFF_QK
T_ROWS = T_FV + DIFF_HEADS * DIFF_V_DIM
N_IK = 0
N_DLAT = N_IK + LANES
N_FK = N_DLAT + DSA_LATENT
N_COLS = N_FK + W_FQ


def _alibi_slopes(n):
    return [2.0 ** (-8.0 * (i + 1) / n) for i in range(n)]


def _rms(x, g):
    return x * lax.rsqrt(jnp.mean(x * x, axis=-1, keepdims=True) + EPS) * g


def _dot(a, b):
    return jnp.dot(a, b, preferred_element_type=F32)


def _mod_kernel(c_ref, w_ref, b_ref, o_ref):
    c = c_ref[...]
    act = c * jax.nn.sigmoid(c)
    o_ref[...] = jnp.dot(act, w_ref[...], preferred_element_type=F32,
                         precision=lax.Precision.HIGHEST) + b_ref[...]


def _mod_call(c, w_ada, b_ada):
    B, D = c.shape
    n = w_ada.shape[1] // D
    return pl.pallas_call(
        _mod_kernel,
        out_shape=jax.ShapeDtypeStruct((B, n * D), F32),
        grid=(n,),
        in_specs=[pl.BlockSpec((B, D), lambda j: (0, 0)),
                  pl.BlockSpec((D, D), lambda j: (0, j)),
                  pl.BlockSpec((1, D), lambda j: (0, j))],
        out_specs=pl.BlockSpec((B, D), lambda j: (0, j)),
        compiler_params=pltpu.CompilerParams(dimension_semantics=("arbitrary",),
                                             vmem_limit_bytes=VMEM_LIMIT),
        name="mod",
    )(c, w_ada, b_ada)


def _key_feature_rows(pos_row, n_rows):
    r = lax.broadcasted_iota(jnp.int32, (n_rows, pos_row.shape[1]), 0)
    pa = (pos_row // POS_SPLIT).astype(F32)
    pb = (pos_row % POS_SPLIT).astype(F32)
    return jnp.where(r == 0, pa, jnp.where(r == 1, pb, jnp.where(r < N_FEATS, 1.0, 0.0)))


def _query_feature_rows(pos_row, n_rows, slope, shift):
    r = lax.broadcasted_iota(jnp.int32, (n_rows, pos_row.shape[1]), 0)
    pa = (pos_row // POS_SPLIT).astype(F32)
    pb = (pos_row % POS_SPLIT).astype(F32)
    return jnp.where(r == 0, POS_SPLIT * slope,
                     jnp.where(r == 1, slope,
                               jnp.where(r == 2, -POS_SPLIT * slope * pa,
                                         jnp.where(r == 3, -slope * pb,
                                                   jnp.where(r == 4, -shift, 0.0)))))


def _shift_bound(qn2, kmax2, extra):
    b = jnp.sqrt(qn2 * kmax2) + extra
    return b + jnp.abs(b) * BOUND_SLACK + BOUND_SLACK


def _tile_max(kmax_ref, last_tile, lane_idx):
    x = kmax_ref[...]
    t = lax.broadcasted_iota(jnp.int32, x.shape, 0)
    ln = lax.broadcasted_iota(jnp.int32, x.shape, 2)
    x = jnp.where((t <= last_tile) & (ln == lane_idx), x, 0.0)
    return jnp.max(jnp.max(x, axis=0), axis=1, keepdims=True)[0:1, :]


def _proj_kernel(x_ref, mod_ref, g_ref, wn_ref, wt_ref, wpair_ref, gkv_ref, gkvc_ref, fc_ref, sel_ref,
                 qabsT_ref, iqT_ref, iwT_ref, ik_ref, ckv_ref, ckvT_ref, fqT_ref, fk_ref, fvT_ref,
                 kmax_ref, *, tm):
    x = x_ref[...]
    h = _rms(x, g_ref[...]) * (1.0 + mod_ref[1:2, :]) + mod_ref[0:1, :]
    hb = h.astype(BF16)
    hT = h.T.astype(BF16)

    base = pl.program_id(1) * tm
    pos_c = base + lax.broadcasted_iota(jnp.int32, (tm, 1), 0)
    pos_r = base + lax.broadcasted_iota(jnp.int32, (1, tm), 1)
    pa = (pos_c // POS_SPLIT).astype(F32)
    pb = (pos_c % POS_SPLIT).astype(F32)

    def feats(row, width):
        return (fc_ref[row:row + 1, 0:width] + fc_ref[row + 1:row + 2, 0:width] * pa
                + fc_ref[row + 2:row + 3, 0:width] * pb)

    ik_ref[...] = _dot(hb, wn_ref[:, N_IK:N_IK + LANES]).astype(BF16)
    dlat = _dot(hb, wn_ref[:, N_DLAT:N_DLAT + DSA_LATENT])
    ckv_b = _rms(dlat, gkv_ref[...]).astype(BF16)
    ckv_ref[:, 0:LANES] = ckv_b
    ckv_ref[:, LANES:2 * LANES] = feats(0, LANES).astype(BF16)
    fk = _dot(hb, wn_ref[:, N_FK:N_FK + W_FQ])
    fk_ref[...] = (fk + feats(3, W_FQ)).astype(BF16)
    sq = jnp.concatenate([fk.astype(BF16).astype(F32), ckv_b.astype(F32)], axis=1)
    sq_up = (sq * sq * BF16_ROUND_UP).astype(BF16)
    kmax_ref[...] = jnp.broadcast_to(jnp.max(_dot(sq_up, sel_ref[...]), axis=0, keepdims=True),
                                     (8, LANES))

    dqT = _dot(wt_ref[T_DQ:T_DQ + W_DQ, :], hT).astype(BF16)
    for p in range(DSA_HEADS // 2):
        qa = _dot(wpair_ref[p], dqT[p * LANES:(p + 1) * LANES, :])
        qabsT_ref[p * 2 * LANES:(p + 1) * 2 * LANES, :] = qa.astype(BF16)
    dlatT = _dot(wt_ref[T_DLAT:T_DLAT + DSA_LATENT, :], hT)
    inv = lax.rsqrt(jnp.mean(dlatT * dlatT, axis=0, keepdims=True) + EPS)
    kfeat = _key_feature_rows(pos_r, BF16_ROWS).astype(BF16)
    ckvT_ref[0:DSA_LATENT, :] = (dlatT * inv * gkvc_ref[...]).astype(BF16)
    ckvT_ref[DSA_LATENT:VT_ROWS, :] = kfeat
    iqT_ref[...] = _dot(wt_ref[T_IQ:T_IQ + IDX_HEADS * IDX_DIM, :], hT).astype(BF16)
    iwT = _dot(wt_ref[T_IW:T_IW + BF16_ROWS, :], hT)
    iwT_ref[...] = iwT[0:8, :] * (IDX_HEADS ** -0.5 * IDX_DIM ** -0.5)
    fqT = _dot(wt_ref[T_FQ:T_FQ + W_DIFF_QK, :], hT)
    zpad = jnp.zeros((LANES - DIFF_QK_DIM, tm), BF16)
    for s in range(DIFF_HEADS * 2):
        r0 = s * LANES
        fqT_ref[r0:r0 + DIFF_QK_DIM, :] = fqT[s * DIFF_QK_DIM:(s + 1) * DIFF_QK_DIM, :].astype(BF16)
        fqT_ref[r0 + DIFF_QK_DIM:r0 + LANES, :] = zpad
    fvT = _dot(wt_ref[T_FV:T_FV + DIFF_HEADS * DIFF_V_DIM, :], hT)
    for hh in range(DIFF_HEADS):
        fvT_ref[hh, 0:DIFF_V_DIM, :] = fvT[hh * DIFF_V_DIM:(hh + 1) * DIFF_V_DIM, :].astype(BF16)
        fvT_ref[hh, DIFF_V_DIM:VT_ROWS, :] = kfeat


def _feature_consts():
    fc = np.zeros((8, W_FQ), np.float32)
    fc[1, 0] = 1.0
    fc[2, 1] = 1.0
    fc[0, 2:N_FEATS] = 1.0
    for s in range(DIFF_HEADS * 2):
        base = s * LANES + DIFF_QK_DIM
        fc[4, base + 0] = 1.0
        fc[5, base + 1] = 1.0
        fc[3, base + 2:base + N_FEATS] = 1.0
    return jnp.asarray(fc)


def _norm_selector():
    sel = np.zeros((W_FQ + DSA_LATENT, LANES), np.float32)
    for s in range(DIFF_HEADS * 2):
        sel[s * LANES:s * LANES + DIFF_QK_DIM, s] = 1.0
    sel[W_FQ:, KMAX_LANE_CKV] = 1.0
    return jnp.asarray(sel, BF16)


def _proj_weights(w_in, w_uk):
    D = w_in.shape[0]
    pts = np.cumsum([W_DQ, DSA_LATENT, IDX_HEADS * IDX_DIM, IDX_DIM, IDX_HEADS,
                     W_DIFF_QK, W_DIFF_QK])
    dq, dlat, iq, ik, iw, fq, fk, fv = jnp.split(w_in, list(pts), axis=1)
    ik4 = jnp.tile(ik, (1, IDX_HEADS))
    fke = jnp.pad(fk.reshape(D, DIFF_HEADS * 2, DIFF_QK_DIM),
                  ((0, 0), (0, 0), (0, LANES - DIFF_QK_DIM))).reshape(D, W_FQ)
    wn = jnp.concatenate([ik4, dlat, fke], axis=1).astype(BF16)
    iwp = jnp.pad(iw, ((0, 0), (0, BF16_ROWS - IDX_HEADS)))
    wt = jnp.concatenate([dq, dlat, iq, iwp, fq * (DIFF_QK_DIM ** -0.5), fv], axis=1).T.astype(BF16)
    uk = w_uk * (DSA_HEAD_DIM ** -0.5)
    z = jnp.zeros_like(uk[0])
    pairs = [jnp.concatenate([jnp.concatenate([uk[2 * p], z], axis=1),
                              jnp.concatenate([z, uk[2 * p + 1]], axis=1)], axis=0)
             for p in range(DSA_HEADS // 2)]
    return wn, wt, jnp.stack(pairs).astype(BF16)


def _proj_call(x, mod3, g_attn, wn, wt, w_pair, g_kv, tm):
    B, S, D = x.shape
    nt = S // tm
    fc = _feature_consts()
    sel = _norm_selector()
    kern = functools.partial(_proj_kernel, tm=tm)
    tok = lambda w: pl.BlockSpec((None, tm, w), lambda b, i: (b, i, 0))
    tokT = lambda r: pl.BlockSpec((None, r, tm), lambda b, i: (b, 0, i))
    full = lambda a: pl.BlockSpec(a.shape, lambda b, i: (0,) * a.ndim)
    g_kv_col = g_kv.reshape(-1, 1)
    out_shape = [jax.ShapeDtypeStruct((B, W_QABS, S), BF16),
                 jax.ShapeDtypeStruct((B, LANES, S), BF16),
                 jax.ShapeDtypeStruct((B, 8, S), F32),
                 jax.ShapeDtypeStruct((B, S, LANES), BF16),
                 jax.ShapeDtypeStruct((B, S, W_CKV), BF16),
                 jax.ShapeDtypeStruct((B, nt, VT_ROWS, tm), BF16),
                 jax.ShapeDtypeStruct((B, W_FQ, S), BF16),
                 jax.ShapeDtypeStruct((B, S, W_FQ), BF16),
                 jax.ShapeDtypeStruct((B, DIFF_HEADS, nt, VT_ROWS, tm), BF16),
                 jax.ShapeDtypeStruct((B, nt, 8, LANES), F32)]
    out_specs = [tokT(W_QABS), tokT(LANES), tokT(8), tok(LANES), tok(W_CKV),
                 pl.BlockSpec((None, None, VT_ROWS, tm), lambda b, i: (b, i, 0, 0)),
                 tokT(W_FQ), tok(W_FQ),
                 pl.BlockSpec((None, DIFF_HEADS, None, VT_ROWS, tm), lambda b, i: (b, 0, i, 0, 0)),
                 pl.BlockSpec((None, None, 8, LANES), lambda b, i: (b, i, 0, 0))]
    return pl.pallas_call(
        kern,
        out_shape=out_shape,
        grid=(B, nt),
        in_specs=[tok(D),
                  pl.BlockSpec((None, 6, D), lambda b, i: (b, 0, 0)),
                  full(g_attn), full(wn), full(wt), full(w_pair), full(g_kv), full(g_kv_col),
                  full(fc), full(sel)],
        out_specs=out_specs,
        compiler_params=pltpu.CompilerParams(dimension_semantics=("arbitrary", "arbitrary"),
                                             vmem_limit_bytes=VMEM_LIMIT),
        name="proj",
    )(x, mod3, g_attn, wn, wt, w_pair, g_kv, g_kv_col, fc, sel)


def _float_code(x):
    b = lax.bitcast_convert_type(x, jnp.int32)
    return b ^ (lax.shift_right_arithmetic(b, 31) & jnp.int32(0x7FFFFFFF))


def _float_decode(c):
    b = c ^ (lax.shift_right_arithmetic(c, 31) & jnp.int32(0x7FFFFFFF))
    return lax.bitcast_convert_type(b, F32)


def _dsa_kernel(qabsT_ref, iqT_ref, iwT_ref, ik_ref, ckv_ref, ckvT_ref, kmax_ref, lstrict_ref, wuvT_ref,
                o_ref, sc_ref, qst_ref, qt_ref, acc_ref, m_ref, tmp_ref, *, tq, tk, topk):
    qi = pl.program_id(1)
    nkb = (qi * tq) // tk + 1
    kf = float(topk)
    slopes = _alibi_slopes(DSA_HEADS)
    q_pos = qi * tq + lax.broadcasted_iota(jnp.int32, (1, tq), 1)

    iqT = iqT_ref[...]
    rowi = lax.broadcasted_iota(jnp.int32, (LANES, tq), 0)
    for hh in range(IDX_HEADS):
        qst_ref[:, hh * tq:(hh + 1) * tq] = jnp.where((rowi // IDX_DIM) == hh, iqT, jnp.zeros_like(iqT))
    iw = iwT_ref[...]
    wrow = [iw[hh:hh + 1, :] for hh in range(IDX_HEADS)]

    def scores(kb):
        kblk = ik_ref[pl.ds(pl.multiple_of(kb * tk, tk), tk), :]
        a = _dot(kblk, qst_ref[...])
        sc = jnp.maximum(a[:, 0:tq], 0.0) * wrow[0]
        for hh in range(1, IDX_HEADS):
            sc = sc + jnp.maximum(a[:, hh * tq:(hh + 1) * tq], 0.0) * wrow[hh]
        return sc

    def score_body(kb, carry):
        mn, mx = carry
        sc = scores(kb)
        sc_ref[kb] = sc
        return (jnp.minimum(mn, jnp.min(sc, axis=0, keepdims=True)),
                jnp.maximum(mx, jnp.max(sc, axis=0, keepdims=True)))

    last = nkb - 1
    mn, mx = lax.fori_loop(0, last // 2, lambda j, c: score_body(2 * j + 1, score_body(2 * j, c)),
                           (jnp.full((1, tq), jnp.inf, F32), jnp.full((1, tq), -jnp.inf, F32)))
    mn, mx = lax.cond(last % 2 == 1, lambda c: score_body(last - 1, c), lambda c: c, (mn, mx))
    sc = scores(last)
    causal = lax.broadcasted_iota(jnp.int32, (tk, tq), 0) + last * tk <= q_pos
    sc_ref[last] = jnp.where(causal, sc, jnp.nan)
    mn = jnp.minimum(mn, jnp.min(jnp.where(causal, sc, jnp.inf), axis=0, keepdims=True))
    mx = jnp.maximum(mx, jnp.max(jnp.where(causal, sc, -jnp.inf), axis=0, keepdims=True))

    def block_count(kb, pred):
        parts = [None] * COUNT_CHAINS
        for r in range(tk // 8):
            v = jnp.where(pred(sc_ref[kb, r * 8:(r + 1) * 8, :]), 1.0, 0.0)
            c = r % COUNT_CHAINS
            parts[c] = v if parts[c] is None else parts[c] + v
        return (parts[0] + parts[1]) + (parts[2] + parts[3])

    def count(pred):
        acc = lax.fori_loop(0, nkb, lambda kb, acc: acc + block_count(kb, pred),
                            jnp.zeros((8, tq), F32))
        return jnp.sum(acc, axis=0, keepdims=True)

    n_causal = (q_pos + 1).astype(F32)
    done0 = n_causal <= kf

    def bisect(st, value_mid):
        lo, hi, c_lo, theta, done = st
        th8 = jnp.broadcast_to(theta, (8, tq))
        c = count(lambda v: v >= th8)
        ge = c >= kf
        live = done == 0.0
        lo = jnp.where(live & ge, theta, lo)
        c_lo = jnp.where(live & ge, c, c_lo)
        hi = jnp.where(live & jnp.logical_not(ge), theta, hi)
        if value_mid:
            nxt = 0.5 * lo + 0.5 * hi
        else:
            cl, ch = _float_code(lo), _float_code(hi)
            nxt = _float_decode((cl & ch) + lax.shift_right_arithmetic(cl ^ ch, 1))
        inside = (nxt > lo) & (nxt < hi)
        done = jnp.where((c_lo == kf) | jnp.logical_not(inside), 1.0, done)
        return lo, hi, c_lo, nxt, done

    def checked(value_mid, max_steps):
        def cond(c):
            return jnp.logical_and(c[1] > 0.0, c[2] < max_steps)

        def body(c):
            st = bisect(bisect(c[0], value_mid), value_mid)
            return st, jnp.sum(1.0 - st[4]), c[2] + 2
        return cond, body

    zero8 = jnp.zeros((8, tq), F32)
    c_ge0 = count(lambda v: v >= zero8)
    c_gt0 = count(lambda v: v > zero8)
    live0 = jnp.logical_not(done0)
    at0 = live0 & (c_ge0 >= kf)
    tie0 = at0 & (c_gt0 < kf)
    below0 = live0 & (c_ge0 < kf)
    lo0 = jnp.where(at0, 0.0, mn)
    hi0 = jnp.where(below0, 0.0, mx)
    st = (lo0, hi0, jnp.where(at0, c_ge0, n_causal),
          jnp.where(below0, 0.5 * lo0 + 0.5 * hi0, mx), jnp.where(done0 | tie0, 1.0, 0.0))
    st = lax.fori_loop(0, UNCHECKED_BISECT_STEPS, lambda i, s: bisect(s, True), st)
    cond, body = checked(True, VALUE_BISECT_STEPS - UNCHECKED_BISECT_STEPS)
    st, active, _ = lax.while_loop(cond, body, (st, jnp.sum(1.0 - st[4]), jnp.int32(0)))
    cond, body = checked(False, CODE_BISECT_STEPS)
    st, _, _ = lax.while_loop(cond, body, (st, active, jnp.int32(0)))
    tau, c_ge = st[0], st[2]

    over = c_ge > kf

    @pl.when(jnp.max(c_ge) > kf)
    def _():
        tau8 = jnp.broadcast_to(tau, (8, tq))
        tmp_ref[0:1, :] = c_gt0

        @pl.when(jnp.max(jnp.where(over & jnp.logical_not(tie0), 1.0, 0.0)) > 0.0)
        def _():
            tmp_ref[0:1, :] = jnp.where(tie0, c_gt0, count(lambda v: v > tau8))

        quota = kf - tmp_ref[0:1, :]

        def tie_body(kb, seen):
            cnt = jnp.sum(block_count(kb, lambda v: v == tau8), axis=0, keepdims=True)
            inside = over & (seen < quota) & (seen + cnt > quota)
            gone = over & (seen >= quota) & (cnt > 0.0)
            flag = jnp.max(jnp.where(inside, 2.0, jnp.where(gone, 1.0, 0.0)))

            @pl.when(flag > 1.5)
            def _():
                s = sc_ref[kb]
                eq = s == tau
                rank = _dot(lstrict_ref[...], jnp.where(eq, 1.0, 0.0).astype(BF16)) + seen
                sc_ref[kb] = jnp.where(eq & over & (rank >= quota), jnp.nan, s)

            @pl.when(flag == 1.0)
            def _():
                s = sc_ref[kb]
                sc_ref[kb] = jnp.where((s == tau) & gone, jnp.nan, s)

            return seen + cnt

        lax.fori_loop(0, nkb, tie_body, jnp.zeros((1, tq), F32))

    tau8 = jnp.broadcast_to(tau, (8, tq))
    sub = lax.broadcasted_iota(jnp.int32, (8, tq), 0).astype(F32)

    def nearest_body(kb, acc):
        parts = [None] * COUNT_CHAINS
        for r in range(tk // 8):
            idx = sub + (kb * tk + r * 8).astype(F32)
            v = jnp.where(sc_ref[kb, r * 8:(r + 1) * 8, :] >= tau8, idx, -1.0)
            c = r % COUNT_CHAINS
            parts[c] = v if parts[c] is None else jnp.maximum(parts[c], v)
        return jnp.maximum(acc, jnp.maximum(jnp.maximum(parts[0], parts[1]),
                                            jnp.maximum(parts[2], parts[3])))

    last_sel = jnp.max(lax.fori_loop(0, nkb, nearest_body, jnp.full((8, tq), -1.0, F32)),
                       axis=0, keepdims=True)
    d_min = q_pos.astype(F32) - last_sel
    kmax2 = _tile_max(kmax_ref, nkb - 1, KMAX_LANE_CKV)
    for hh in range(DSA_HEADS):
        qh = qabsT_ref[hh * DSA_LATENT:(hh + 1) * DSA_LATENT, :]
        qf = qh.astype(F32)
        shift = _shift_bound(jnp.sum(qf * qf, axis=0, keepdims=True), kmax2, -slopes[hh] * d_min)
        qt_ref[0:DSA_LATENT, hh * tq:(hh + 1) * tq] = qh
        qt_ref[DSA_LATENT:VT_ROWS, hh * tq:(hh + 1) * tq] = _query_feature_rows(
            q_pos, BF16_ROWS, slopes[hh], shift).astype(BF16)
    qt_ref[VT_ROWS:2 * DSA_LATENT, :] = jnp.zeros((2 * DSA_LATENT - VT_ROWS, DSA_HEADS * tq), BF16)
    acc_ref[...] = jnp.zeros_like(acc_ref)

    def fast_pv(kb):
        kx = ckv_ref[pl.ds(pl.multiple_of(kb * tk, tk), tk), :]
        s = _dot(kx, qt_ref[...])
        keep = sc_ref[kb] >= tau
        ps = [jnp.exp(jnp.where(keep, s[:, hh * tq:(hh + 1) * tq], NEG_BIG)).astype(BF16)
              for hh in range(DSA_HEADS)]
        return _dot(ckvT_ref[kb], jnp.concatenate(ps, axis=1))

    def fast_pair(j, carry):
        acc_ref[...] += fast_pv(2 * j) + fast_pv(2 * j + 1)
        return carry

    lax.fori_loop(0, nkb // 2, fast_pair, 0)

    @pl.when(nkb % 2 == 1)
    def _():
        acc_ref[...] += fast_pv(nkb - 1)

    def attn_body(kb, carry):
        kx = ckv_ref[pl.ds(pl.multiple_of(kb * tk, tk), tk), :]
        vT = ckvT_ref[kb]
        keep = sc_ref[kb] >= tau
        s = _dot(kx, qt_ref[...])
        m_all = m_ref[...]
        m_out, alphas, ps = [], [], []
        for hh in range(DSA_HEADS):
            c0 = hh * tq
            sh = jnp.where(keep, s[:, c0:c0 + tq], NEG_BIG)
            m_old = m_all[:, c0:c0 + tq]
            m_new = jnp.maximum(m_old, jnp.max(sh, axis=0, keepdims=True))
            ps.append(jnp.exp(sh - m_new).astype(BF16))
            alphas.append(jnp.exp(m_old - m_new))
            m_out.append(m_new)
        m_ref[...] = jnp.concatenate(m_out, axis=1)
        acc_ref[...] = (jnp.concatenate(alphas, axis=1) * acc_ref[...]
                        + _dot(vT, jnp.concatenate(ps, axis=1)))
        return carry

    denom_ok = jnp.min(acc_ref[ONES_ROW:ONES_ROW + 1, :]) > L_MIN

    @pl.when(jnp.logical_not(denom_ok))
    def _():
        acc_ref[...] = jnp.zeros_like(acc_ref)
        m_ref[...] = jnp.full_like(m_ref, NEG_BIG)
        lax.fori_loop(0, nkb, attn_body, 0)

    acc = acc_ref[...]
    lat = acc[0:DSA_LATENT, :] / acc[ONES_ROW:ONES_ROW + 1, :]
    for p in range(DSA_HEADS // 2):
        pair = lat[:, 2 * p * tq:(2 * p + 2) * tq]
        pair = jnp.concatenate([pair[:, 0:tq], pair[:, tq:2 * tq]], axis=0).astype(BF16)
        o_ref[:, p * LANES:(p + 1) * LANES] = _dot(wuvT_ref[p], pair).T.astype(BF16)


def _dsa_call(qabsT, iqT, iwT, ik4, ckv, ckvT, kmax, lstrict, wuvT_pair, tq, tk, topk):
    B, _, S = qabsT.shape
    nkb = S // tk
    kern = functools.partial(_dsa_kernel, tq=tq, tk=tk, topk=topk)
    blkT = lambda r: pl.BlockSpec((None, r, tq), lambda b, i: (b, 0, i))
    per_b = lambda w: pl.BlockSpec((None, S, w), lambda b, i: (b, 0, 0))
    full = lambda a: pl.BlockSpec(a.shape, lambda b, i: (0,) * a.ndim)
    return pl.pallas_call(
        kern,
        out_shape=jax.ShapeDtypeStruct((B, S, W_DQ), BF16),
        grid=(B, S // tq),
        in_specs=[blkT(W_QABS), blkT(LANES), blkT(8), per_b(LANES), per_b(W_CKV),
                  pl.BlockSpec((None, nkb, VT_ROWS, tk), lambda b, i: (b, 0, 0, 0)),
                  pl.BlockSpec((None, nkb, 8, LANES), lambda b, i: (b, 0, 0, 0)),
                  full(lstrict), full(wuvT_pair)],
        out_specs=pl.BlockSpec((None, tq, W_DQ), lambda b, i: (b, i, 0)),
        scratch_shapes=[pltpu.VMEM((nkb, tk, tq), F32),
                        pltpu.VMEM((LANES, IDX_HEADS * tq), BF16),
                        pltpu.VMEM((2 * DSA_LATENT, DSA_HEADS * tq), BF16),
                        pltpu.VMEM((VT_ROWS, DSA_HEADS * tq), F32),
                        pltpu.VMEM((1, DSA_HEADS * tq), F32),
                        pltpu.VMEM((8, tq), F32)],
        compiler_params=pltpu.CompilerParams(dimension_semantics=("arbitrary", "arbitrary"),
                                             vmem_limit_bytes=VMEM_LIMIT),
        name="dsa",
    )(qabsT, iqT, iwT, ik4, ckv, ckvT, kmax, lstrict, wuvT_pair)


def _diff_kernel(fqT_ref, fk_ref, fvT_ref, kmax_ref, lam_ref, gsub_ref, o_ref, qt_ref, acc_ref, m_ref,
                 *, tq, tk, lam0):
    hd = pl.program_id(1)
    qi = pl.program_id(2)
    q_pos = qi * tq + lax.broadcasted_iota(jnp.int32, (1, tq), 1)
    n_full = (qi * tq) // tk
    n_diag = (tq + tk - 1) // tk
    slopes = _alibi_slopes(DIFF_HEADS)
    slope = jnp.float32(slopes[DIFF_HEADS - 1])
    for i in range(DIFF_HEADS - 1):
        slope = jnp.where(hd == i, slopes[i], slope)

    for m in range(2):
        r0 = m * LANES
        qh = fqT_ref[r0:r0 + DIFF_QK_DIM, :]
        qf = qh.astype(F32)
        shift = _shift_bound(jnp.sum(qf * qf, axis=0, keepdims=True),
                             _tile_max(kmax_ref, n_full + n_diag - 1, 2 * hd + m), 0.0)
        qt_ref[r0:r0 + DIFF_QK_DIM, :] = qh
        qt_ref[r0 + DIFF_QK_DIM:r0 + DIFF_QK_DIM + BF16_ROWS, :] = _query_feature_rows(
            q_pos, BF16_ROWS, slope, shift).astype(BF16)
        qt_ref[r0 + DIFF_QK_DIM + BF16_ROWS:r0 + LANES, :] = jnp.zeros(
            (LANES - DIFF_QK_DIM - BF16_ROWS, tq), BF16)
    acc_ref[...] = jnp.zeros_like(acc_ref)

    def fast_pv(kb, masked):
        kx = fk_ref[pl.ds(pl.multiple_of(kb * tk, tk), tk), :]
        vT = fvT_ref[kb]
        out = []
        for m in range(2):
            s = _dot(kx[:, m * LANES:(m + 1) * LANES], qt_ref[m * LANES:(m + 1) * LANES, :])
            if masked:
                causal = lax.broadcasted_iota(jnp.int32, (tk, tq), 0) + kb * tk <= q_pos
                s = jnp.where(causal, s, NEG_BIG)
            out.append(_dot(vT, jnp.exp(s).astype(BF16)))
        return out

    def accumulate(*pvs):
        for m in range(2):
            acc_ref[m] += functools.reduce(lambda a, b: a + b, [pv[m] for pv in pvs])

    def fast_pair(j, carry):
        accumulate(fast_pv(2 * j, False), fast_pv(2 * j + 1, False))
        return carry

    lax.fori_loop(0, n_full // 2, fast_pair, 0)
    assert n_diag in (1, 2)

    @pl.when(n_full % 2 == 1)
    def _():
        accumulate(fast_pv(n_full - 1, False), fast_pv(n_full, True))
        if n_diag == 2:
            accumulate(fast_pv(n_full + 1, True))

    @pl.when(n_full % 2 == 0)
    def _():
        accumulate(*[fast_pv(n_full + d, True) for d in range(n_diag)])

    def block(kb, masked):
        kx = fk_ref[pl.ds(pl.multiple_of(kb * tk, tk), tk), :]
        vT = fvT_ref[kb]
        qT = qt_ref[...]
        m_all = m_ref[...]
        new = []
        for m in range(2):
            s = _dot(kx[:, m * LANES:(m + 1) * LANES], qT[m * LANES:(m + 1) * LANES, :])
            if masked:
                causal = lax.broadcasted_iota(jnp.int32, (tk, tq), 0) + kb * tk <= q_pos
                s = jnp.where(causal, s, NEG_BIG)
            m_old = m_all[m]
            m_new = jnp.maximum(m_old, jnp.max(s, axis=0, keepdims=True))
            p = jnp.exp(s - m_new).astype(BF16)
            new.append((m_new, jnp.exp(m_old - m_new), _dot(vT, p)))
        for m in range(2):
            m_ref[m] = new[m][0]
            acc_ref[m] = new[m][1] * acc_ref[m] + new[m][2]

    def body(kb, carry):
        block(kb, False)
        return carry

    denom_ok = jnp.min(acc_ref[:, ONES_ROW:ONES_ROW + 1, :]) > L_MIN

    @pl.when(jnp.logical_not(denom_ok))
    def _():
        acc_ref[...] = jnp.zeros_like(acc_ref)
        m_ref[...] = jnp.full_like(m_ref, NEG_BIG)
        lax.fori_loop(0, n_full, body, 0)
        for d in range(n_diag):
            block(n_full + d, True)

    lv = lam_ref[...]
    lam = (jnp.exp(jnp.sum(lv[0:1] * lv[1:2], axis=-1, keepdims=True))
           - jnp.exp(jnp.sum(lv[2:3] * lv[3:4], axis=-1, keepdims=True)) + lam0)
    a1 = acc_ref[0]
    a2 = acc_ref[1]
    o = (a1[0:DIFF_V_DIM, :] / a1[ONES_ROW:ONES_ROW + 1, :]
         - lam * (a2[0:DIFF_V_DIM, :] / a2[ONES_ROW:ONES_ROW + 1, :]))
    o = o * lax.rsqrt(jnp.mean(o * o, axis=0, keepdims=True) + EPS) * gsub_ref[...] * (1.0 - lam0)
    o_ref[...] = o.T.astype(BF16)


def _diff_call(fqT, fk, fvT, kmax, lam_vecs, g_sub_col, tq, tk, lam0):
    B, S, _ = fk.shape
    nkb = S // tk
    kern = functools.partial(_diff_kernel, tq=tq, tk=tk, lam0=lam0)
    full = lambda a: pl.BlockSpec(a.shape, lambda b, h, i: (0,) * a.ndim)
    return pl.pallas_call(
        kern,
        out_shape=jax.ShapeDtypeStruct((B, S, DIFF_HEADS * DIFF_V_DIM), BF16),
        grid=(B, DIFF_HEADS, S // tq),
        in_specs=[pl.BlockSpec((None, 2 * LANES, tq), lambda b, h, i: (b, h, i)),
                  pl.BlockSpec((None, S, 2 * LANES), lambda b, h, i: (b, 0, h)),
                  pl.BlockSpec((None, None, nkb, VT_ROWS, tk), lambda b, h, i: (b, h, 0, 0, 0)),
                  pl.BlockSpec((None, nkb, 8, LANES), lambda b, h, i: (b, 0, 0, 0)),
                  full(lam_vecs), full(g_sub_col)],
        out_specs=pl.BlockSpec((None, tq, DIFF_V_DIM), lambda b, h, i: (b, i, h)),
        scratch_shapes=[pltpu.VMEM((2 * LANES, tq), BF16),
                        pltpu.VMEM((2, VT_ROWS, tq), F32),
                        pltpu.VMEM((2, 1, tq), F32)],
        compiler_params=pltpu.CompilerParams(
            dimension_semantics=("arbitrary", "arbitrary", "arbitrary"),
            vmem_limit_bytes=VMEM_LIMIT),
        name="diff",
    )(fqT, fk, fvT, kmax, lam_vecs, g_sub_col)


def _route_kernel(x_ref, od_ref, of_ref, mod_ref, wout_ref, g_ref, wr_hi_ref, wr_lo_ref, br_ref,
                  x1_ref, h_ref, gates_ref, *, tm):
    attn = (_dot(od_ref[...], wout_ref[0:W_DQ, :]) + _dot(of_ref[...], wout_ref[W_DQ:, :]))
    x1 = x_ref[...] + mod_ref[2:3, :] * attn
    x1_ref[...] = x1
    h = _rms(x1, g_ref[...]) * (1.0 + mod_ref[4:5, :]) + mod_ref[3:4, :]
    h_hi = h.astype(BF16)
    h_ref[...] = h_hi
    h_lo = (h - h_hi.astype(F32)).astype(BF16)
    w_hi = wr_hi_ref[...]
    logits = (_dot(h_hi, w_hi) + _dot(h_lo, w_hi) + _dot(h_hi, wr_lo_ref[...])) + br_ref[...]

    lane = lax.broadcasted_iota(jnp.int32, (tm, LANES), 1)
    big = jnp.int32(4 * LANES)
    neg = -jnp.inf
    is_g = (lane >= N_EXPERTS) & (lane < N_EXPERTS + N_GROUPS)
    gl = jnp.where(is_g, logits, neg)
    gmax = jnp.max(gl, axis=-1, keepdims=True)
    g_lane = jnp.min(jnp.where(gl == gmax, lane, big), axis=-1, keepdims=True)
    g_sel = g_lane - N_EXPERTS
    p_g = 1.0 / jnp.sum(jnp.where(is_g, jnp.exp(gl - gmax), 0.0), axis=-1, keepdims=True)
    in_grp = (lane < N_EXPERTS) & ((lane // EXPERTS_PER_GROUP) == g_sel)
    el = jnp.where(in_grp, logits, neg)
    v1 = jnp.max(el, axis=-1, keepdims=True)
    i1 = jnp.min(jnp.where(el == v1, lane, big), axis=-1, keepdims=True)
    el2 = jnp.where(lane == i1, neg, el)
    v2 = jnp.max(el2, axis=-1, keepdims=True)
    i2 = jnp.min(jnp.where(el2 == v2, lane, big), axis=-1, keepdims=True)
    e = jnp.exp(v2 - v1)
    w1 = 1.0 / (1.0 + e)
    w2 = e * w1
    gates_ref[...] = jnp.where(lane == i1, w1 * p_g, jnp.where(lane == i2, w2 * p_g, 0.0))


def _route_call(x, o_dsa, o_diff, mod3, w_out, g_moe, wr_hi, wr_lo, b_r, tm):
    B, S, D = x.shape
    kern = functools.partial(_route_kernel, tm=tm)
    tok = lambda w: pl.BlockSpec((None, tm, w), lambda b, i: (b, i, 0))
    full = lambda a: pl.BlockSpec(a.shape, lambda b, i: (0,) * a.ndim)
    return pl.pallas_call(
        kern,
        out_shape=[jax.ShapeDtypeStruct((B, S, D), F32), jax.ShapeDtypeStruct((B, S, D), BF16),
                   jax.ShapeDtypeStruct((B, S, LANES), F32)],
        grid=(B, S // tm),
        in_specs=[tok(D), tok(W_DQ), tok(DIFF_HEADS * DIFF_V_DIM),
                  pl.BlockSpec((None, 6, D), lambda b, i: (b, 0, 0)),
                  full(w_out), full(g_moe), full(wr_hi), full(wr_lo), full(b_r)],
        out_specs=[tok(D), tok(D), tok(LANES)],
        compiler_params=pltpu.CompilerParams(dimension_semantics=("arbitrary", "arbitrary"),
                                             vmem_limit_bytes=VMEM_LIMIT),
        name="route",
    )(x, o_dsa, o_diff, mod3, w_out, g_moe, wr_hi, wr_lo, b_r)


def _moe_kernel(h_ref, gates_ref, x1_ref, mod_ref, wg_ref, wu_ref, wd_ref, gf_ref, o_ref, acc_ref,
                *, tm, ec):
    j = pl.program_id(2)
    nc = pl.num_programs(2)

    @pl.when(j == 0)
    def _():
        acc_ref[...] = jnp.zeros_like(acc_ref)

    h = h_ref[...]
    hg = _dot(h, wg_ref[...])
    hu = _dot(h, wu_ref[...])
    hid = hg * jax.nn.sigmoid(hg) * hu
    gates = gates_ref[...]
    lane = lax.broadcasted_iota(jnp.int32, (tm, LANES), 1)
    parts = []
    for e in range(ec):
        gcol = jnp.sum(jnp.where(lane == j * ec + e, gates, 0.0), axis=-1, keepdims=True)
        parts.append((hid[:, e * D_EXPERT:(e + 1) * D_EXPERT] * gcol).astype(BF16))
    hs = jnp.concatenate(parts, axis=1)
    acc_ref[...] += _dot(hs, wd_ref[...])

    @pl.when(j == nc - 1)
    def _():
        x2 = x1_ref[...] + mod_ref[5:6, :] * acc_ref[...]
        o_ref[...] = _rms(x2, gf_ref[...])


def _moe_call(h, gates, x1, mod3, wg, wu, wd, g_final, tm, ec):
    B, S, D = x1.shape
    kern = functools.partial(_moe_kernel, tm=tm, ec=ec)
    tok = lambda w: pl.BlockSpec((None, tm, w), lambda b, i, j: (b, i, 0))
    return pl.pallas_call(
        kern,
        out_shape=jax.ShapeDtypeStruct((B, S, D), F32),
        grid=(B, S // tm, N_EXPERTS // ec),
        in_specs=[tok(D), tok(LANES), tok(D),
                  pl.BlockSpec((None, 6, D), lambda b, i, j: (b, 0, 0)),
                  pl.BlockSpec((D, ec * D_EXPERT), lambda b, i, j: (0, j)),
                  pl.BlockSpec((D, ec * D_EXPERT), lambda b, i, j: (0, j)),
                  pl.BlockSpec((ec * D_EXPERT, D), lambda b, i, j: (j, 0)),
                  pl.BlockSpec((1, D), lambda b, i, j: (0, 0))],
        out_specs=tok(D),
        scratch_shapes=[pltpu.VMEM((tm, D), F32)],
        compiler_params=pltpu.CompilerParams(
            dimension_semantics=("arbitrary", "arbitrary", "arbitrary"),
            vmem_limit_bytes=VMEM_LIMIT),
        name="moe",
    )(h, gates, x1, mod3, wg, wu, wd, g_final)


def _tile(n, pref):
    t = min(n, pref)
    assert n % t == 0, (n, t)
    return t


def kernel(x, c, w_ada, b_ada, g_attn, w_in, g_kv, w_uk, w_uv, lam_q1, lam_k1, lam_q2, lam_k2,
           g_sub, w_out, g_moe, w_group, b_group, w_router, b_router, w_gate, w_up, w_down, g_final):
    B, S, D = x.shape
    assert D == D_MODEL and w_ada.shape[0] == 1
    assert S % KEY_TILE == 0 and S <= POS_SPLIT * 256
    topk = min(TOPK_MAX, S // 4)
    l = 0
    lam0 = 0.8 - 0.6 * math.exp(-0.3 * l)

    mod3 = _mod_call(c, w_ada[l], b_ada[l].reshape(1, -1)).reshape(B, 6, D)

    wn, wt, w_pair = _proj_weights(w_in[l], w_uk[l])
    qabsT, iqT, iwT, ik4, ckv, ckvT, fqT, fk, fvT, kmax = _proj_call(
        x, mod3, g_attn[l].reshape(1, D), wn, wt, w_pair, g_kv[l].reshape(1, -1), KEY_TILE)

    lstrict = jnp.asarray(np.tril(np.ones((KEY_TILE, KEY_TILE), np.float32), -1), BF16)
    uvT = jnp.swapaxes(w_uv[l], 1, 2)
    z = jnp.zeros_like(uvT[0])
    wuvT_pair = jnp.stack([
        jnp.concatenate([jnp.concatenate([uvT[2 * p], z], axis=1),
                         jnp.concatenate([z, uvT[2 * p + 1]], axis=1)], axis=0)
        for p in range(DSA_HEADS // 2)]).astype(BF16)
    o_dsa = _dsa_call(qabsT, iqT, iwT, ik4, ckv, ckvT, kmax, lstrict, wuvT_pair, DSA_QUERY_TILE, KEY_TILE,
                      topk)

    lam_vecs = jnp.concatenate([lam_q1[l][None], lam_k1[l][None], lam_q2[l][None], lam_k2[l][None]],
                               axis=0).astype(F32)
    o_diff = _diff_call(fqT, fk, fvT, kmax, lam_vecs, g_sub[l].reshape(-1, 1),
                        _tile(S, DIFF_QUERY_TILE), KEY_TILE, lam0)

    wr = jnp.pad(jnp.concatenate([w_router[l], w_group[l]], axis=1),
                 ((0, 0), (0, LANES - N_EXPERTS - N_GROUPS)))
    wr_hi = wr.astype(BF16)
    wr_lo = (wr - wr_hi.astype(F32)).astype(BF16)
    b_r = jnp.pad(jnp.concatenate([b_router[l], b_group[l]]), (0, LANES - N_EXPERTS - N_GROUPS))
    x1, h2, gates = _route_call(x, o_dsa, o_diff, mod3, w_out[l].astype(BF16),
                                g_moe[l].reshape(1, D), wr_hi, wr_lo, b_r.reshape(1, LANES),
                                _tile(S, TOKEN_TILE))

    wg = jnp.swapaxes(w_gate[l], 0, 1).reshape(D, N_EXPERTS * D_EXPERT).astype(BF16)
    wu = jnp.swapaxes(w_up[l], 0, 1).reshape(D, N_EXPERTS * D_EXPERT).astype(BF16)
    wd = w_down[l].reshape(N_EXPERTS * D_EXPERT, D).astype(BF16)
    return _moe_call(h2, gates, x1, mod3, wg, wu, wd, g_final.reshape(1, D), _tile(S, MOE_TOKEN_TILE),
                     MOE_EXPERTS_PER_STEP)
```

```python
import functools
import math

import jax
import jax.numpy as jnp
import numpy as np
from jax import lax
from jax.experimental import pallas as pl
from jax.experimental.pallas import tpu as pltpu

F32 = jnp.float32
BF16 = jnp.bfloat16

D_MODEL = 1024
DSA_HEADS = 8
DSA_HEAD_DIM = 64
DSA_LATENT = 128
IDX_HEADS = 4
IDX_DIM = 32
TOPK_MAX = 256
DIFF_HEADS = 4
DIFF_QK_DIM = 64
DIFF_V_DIM = 128
N_GROUPS = 4
EXPERTS_PER_GROUP = 8
N_EXPERTS = 32
D_EXPERT = 256
EPS = 1e-6

LANES = 128
BF16_ROWS = 16
POS_SPLIT = 64
NEG_BIG = -1e30
VMEM_LIMIT = 56 * 1024 * 1024
KEY_TILE = 512
VALUE_BISECT_STEPS = 24
UNCHECKED_BISECT_STEPS = 16
COUNT_CHAINS = 4

W_DQ = DSA_HEADS * DSA_HEAD_DIM
W_QABS = DSA_HEADS * DSA_LATENT
W_CKV = 2 * LANES
VT_ROWS = DSA_LATENT + BF16_ROWS
ONES_ROW = DSA_LATENT + 2
N_FEATS = 5
KMAX_LANE_CKV = 8
L_MIN = 1e-30
BF16_ROUND_UP = 1.0 + 2.0 ** -7
BOUND_SLACK = 2.0 ** -6
CODE_BISECT_STEPS = 34
TOKEN_TILE = 512
DSA_QUERY_TILE = 256
DIFF_QUERY_TILE = 1024
MOE_EXPERTS_PER_STEP = 8
MOE_TOKEN_TILE = 512
W_FQ = DIFF_HEADS * 2 * LANES
W_DIFF_QK = DIFF_HEADS * 2 * DIFF_QK_DIM

T_DQ = 0
T_DLAT = T_DQ + W_DQ
T_IQ = T_DLAT + DSA_LATENT
T_IW = T_IQ + IDX_HEADS * IDX_DIM
T_FQ = T_IW + BF16_ROWS
T_FV = T_FQ + W_DIFF_QK
T_ROWS = T_FV + DIFF_HEADS * DIFF_V_DIM
N_IK = 0
N_DLAT = N_IK + LANES
N_FK = N_DLAT + DSA_LATENT
N_COLS = N_FK + W_FQ


def _alibi_slopes(n):
    return [2.0 ** (-8.0 * (i + 1) / n) for i in range(n)]


def _rms(x, g):
    return x * lax.rsqrt(jnp.mean(x * x, axis=-1, keepdims=True) + EPS) * g


def _dot(a, b):
    return jnp.dot(a, b, preferred_element_type=F32)


def _mod_kernel(c_ref, w_ref, b_ref, o_ref):
    c = c_ref[...]
    act = c * jax.nn.sigmoid(c)
    o_ref[...] = jnp.dot(act, w_ref[...], preferred_element_type=F32,
                         precision=lax.Precision.HIGHEST) + b_ref[...]


def _mod_call(c, w_ada, b_ada):
    B, D = c.shape
    n = w_ada.shape[1] // D
    return pl.pallas_call(
        _mod_kernel,
        out_shape=jax.ShapeDtypeStruct((B, n * D), F32),
        grid=(n,),
        in_specs=[pl.BlockSpec((B, D), lambda j: (0, 0)),
                  pl.BlockSpec((D, D), lambda j: (0, j)),
                  pl.BlockSpec((1, D), lambda j: (0, j))],
        out_specs=pl.BlockSpec((B, D), lambda j: (0, j)),
        compiler_params=pltpu.CompilerParams(dimension_semantics=("arbitrary",),
                                             vmem_limit_bytes=VMEM_LIMIT),
        name="mod",
    )(c, w_ada, b_ada)


def _key_feature_rows(pos_row, n_rows):
    r = lax.broadcasted_iota(jnp.int32, (n_rows, pos_row.shape[1]), 0)
    pa = (pos_row // POS_SPLIT).astype(F32)
    pb = (pos_row % POS_SPLIT).astype(F32)
    return jnp.where(r == 0, pa, jnp.where(r == 1, pb, jnp.where(r < N_FEATS, 1.0, 0.0)))


def _query_feature_rows(pos_row, n_rows, slope, shift):
    r = lax.broadcasted_iota(jnp.int32, (n_rows, pos_row.shape[1]), 0)
    pa = (pos_row // POS_SPLIT).astype(F32)
    pb = (pos_row % POS_SPLIT).astype(F32)
    return jnp.where(r == 0, POS_SPLIT * slope,
                     jnp.where(r == 1, slope,
                               jnp.where(r == 2, -POS_SPLIT * slope * pa,
                                         jnp.where(r == 3, -slope * pb,
                                                   jnp.where(r == 4, -shift, 0.0)))))


def _shift_bound(qn2, kmax2, extra):
    b = jnp.sqrt(qn2 * kmax2) + extra
    return b + jnp.abs(b) * BOUND_SLACK + BOUND_SLACK


def _tile_max(kmax_ref, last_tile, lane_idx):
    x = kmax_ref[...]
    t = lax.broadcasted_iota(jnp.int32, x.shape, 0)
    ln = lax.broadcasted_iota(jnp.int32, x.shape, 2)
    x = jnp.where((t <= last_tile) & (ln == lane_idx), x, 0.0)
    return jnp.max(jnp.max(x, axis=0), axis=1, keepdims=True)[0:1, :]


def _proj_kernel(x_ref, mod_ref, g_ref, wn_ref, wt_ref, wpair_ref, gkv_ref, gkvc_ref, fc_ref, sel_ref,
                 qabsT_ref, iqT_ref, iwT_ref, ik_ref, ckv_ref, ckvT_ref, fqT_ref, fk_ref, fvT_ref,
                 kmax_ref, *, tm):
    x = x_ref[...]
    h = _rms(x, g_ref[...]) * (1.0 + mod_ref[1:2, :]) + mod_ref[0:1, :]
    hb = h.astype(BF16)
    hT = h.T.astype(BF16)

    base = pl.program_id(1) * tm
    pos_c = base + lax.broadcasted_iota(jnp.int32, (tm, 1), 0)
    pos_r = base + lax.broadcasted_iota(jnp.int32, (1, tm), 1)
    pa = (pos_c // POS_SPLIT).astype(F32)
    pb = (pos_c % POS_SPLIT).astype(F32)

    def feats(row, width):
        return (fc_ref[row:row + 1, 0:width] + fc_ref[row + 1:row + 2, 0:width] * pa
                + fc_ref[row + 2:row + 3, 0:width] * pb)

    ik_ref[...] = _dot(hb, wn_ref[:, N_IK:N_IK + LANES]).astype(BF16)
    dlat = _dot(hb, wn_ref[:, N_DLAT:N_DLAT + DSA_LATENT])
    ckv_b = _rms(dlat, gkv_ref[...]).astype(BF16)
    ckv_ref[:, 0:LANES] = ckv_b
    ckv_ref[:, LANES:2 * LANES] = feats(0, LANES).astype(BF16)
    fk = _dot(hb, wn_ref[:, N_FK:N_FK + W_FQ])
    fk_ref[...] = (fk + feats(3, W_FQ)).astype(BF16)
    sq = jnp.concatenate([fk.astype(BF16).astype(F32), ckv_b.astype(F32)], axis=1)
    sq_up = (sq * sq * BF16_ROUND_UP).astype(BF16)
    kmax_ref[...] = jnp.broadcast_to(jnp.max(_dot(sq_up, sel_ref[...]), axis=0, keepdims=True),
                                     (8, LANES))

    dqT = _dot(wt_ref[T_DQ:T_DQ + W_DQ, :], hT).astype(BF16)
    for p in range(DSA_HEADS // 2):
        qa = _dot(wpair_ref[p], dqT[p * LANES:(p + 1) * LANES, :])
        qabsT_ref[p * 2 * LANES:(p + 1) * 2 * LANES, :] = qa.astype(BF16)
    dlatT = _dot(wt_ref[T_DLAT:T_DLAT + DSA_LATENT, :], hT)
    inv = lax.rsqrt(jnp.mean(dlatT * dlatT, axis=0, keepdims=True) + EPS)
    kfeat = _key_feature_rows(pos_r, BF16_ROWS).astype(BF16)
    ckvT_ref[0:DSA_LATENT, :] = (dlatT * inv * gkvc_ref[...]).astype(BF16)
    ckvT_ref[DSA_LATENT:VT_ROWS, :] = kfeat
    iqT_ref[...] = _dot(wt_ref[T_IQ:T_IQ + IDX_HEADS * IDX_DIM, :], hT).astype(BF16)
    iwT = _dot(wt_ref[T_IW:T_IW + BF16_ROWS, :], hT)
    iwT_ref[...] = iwT[0:8, :] * (IDX_HEADS ** -0.5 * IDX_DIM ** -0.5)
    fqT = _dot(wt_ref[T_FQ:T_FQ + W_DIFF_QK, :], hT)
    zpad = jnp.zeros((LANES - DIFF_QK_DIM, tm), BF16)
    for s in range(DIFF_HEADS * 2):
        r0 = s * LANES
        fqT_ref[r0:r0 + DIFF_QK_DIM, :] = fqT[s * DIFF_QK_DIM:(s + 1) * DIFF_QK_DIM, :].astype(BF16)
        fqT_ref[r0 + DIFF_QK_DIM:r0 + LANES, :] = zpad
    fvT = _dot(wt_ref[T_FV:T_FV + DIFF_HEADS * DIFF_V_DIM, :], hT)
    for hh in range(DIFF_HEADS):
        fvT_ref[hh, 0:DIFF_V_DIM, :] = fvT[hh * DIFF_V_DIM:(hh + 1) * DIFF_V_DIM, :].astype(BF16)
        fvT_ref[hh, DIFF_V_DIM:VT_ROWS, :] = kfeat


def _feature_consts():
    fc = np.zeros((8, W_FQ), np.float32)
    fc[1, 0] = 1.0
    fc[2, 1] = 1.0
    fc[0, 2:N_FEATS] = 1.0
    for s in range(DIFF_HEADS * 2):
        base = s * LANES + DIFF_QK_DIM
        fc[4, base + 0] = 1.0
        fc[5, base + 1] = 1.0
        fc[3, base + 2:base + N_FEATS] = 1.0
    return jnp.asarray(fc)


def _norm_selector():
    sel = np.zeros((W_FQ + DSA_LATENT, LANES), np.float32)
    for s in range(DIFF_HEADS * 2):
        sel[s * LANES:s * LANES + DIFF_QK_DIM, s] = 1.0
    sel[W_FQ:, KMAX_LANE_CKV] = 1.0
    return jnp.asarray(sel, BF16)


def _proj_weights(w_in, w_uk):
    D = w_in.shape[0]
    pts = np.cumsum([W_DQ, DSA_LATENT, IDX_HEADS * IDX_DIM, IDX_DIM, IDX_HEADS,
                     W_DIFF_QK, W_DIFF_QK])
    dq, dlat, iq, ik, iw, fq, fk, fv = jnp.split(w_in, list(pts), axis=1)
    ik4 = jnp.tile(ik, (1, IDX_HEADS))
    fke = jnp.pad(fk.reshape(D, DIFF_HEADS * 2, DIFF_QK_DIM),
                  ((0, 0), (0, 0), (0, LANES - DIFF_QK_DIM))).reshape(D, W_FQ)
    wn = jnp.concatenate([ik4, dlat, fke], axis=1).astype(BF16)
    iwp = jnp.pad(iw, ((0, 0), (0, BF16_ROWS - IDX_HEADS)))
    wt = jnp.concatenate([dq, dlat, iq, iwp, fq * (DIFF_QK_DIM ** -0.5), fv], axis=1).T.astype(BF16)
    uk = w_uk * (DSA_HEAD_DIM ** -0.5)
    z = jnp.zeros_like(uk[0])
    pairs = [jnp.concatenate([jnp.concatenate([uk[2 * p], z], axis=1),
                              jnp.concatenate([z, uk[2 * p + 1]], axis=1)], axis=0)
             for p in range(DSA_HEADS // 2)]
    return wn, wt, jnp.stack(pairs).astype(BF16)


def _proj_call(x, mod3, g_attn, wn, wt, w_pair, g_kv, tm):
    B, S, D = x.shape
    nt = S // tm
    fc = _feature_consts()
    sel = _norm_selector()
    kern = functools.partial(_proj_kernel, tm=tm)
    tok = lambda w: pl.BlockSpec((None, tm, w), lambda b, i: (b, i, 0))
    tokT = lambda r: pl.BlockSpec((None, r, tm), lambda b, i: (b, 0, i))
    full = lambda a: pl.BlockSpec(a.shape, lambda b, i: (0,) * a.ndim)
    g_kv_col = g_kv.reshape(-1, 1)
    out_shape = [jax.ShapeDtypeStruct((B, W_QABS, S), BF16),
                 jax.ShapeDtypeStruct((B, LANES, S), BF16),
                 jax.ShapeDtypeStruct((B, 8, S), F32),
                 jax.ShapeDtypeStruct((B, S, LANES), BF16),
                 jax.ShapeDtypeStruct((B, S, W_CKV), BF16),
                 jax.ShapeDtypeStruct((B, nt, VT_ROWS, tm), BF16),
                 jax.ShapeDtypeStruct((B, W_FQ, S), BF16),
                 jax.ShapeDtypeStruct((B, S, W_FQ), BF16),
                 jax.ShapeDtypeStruct((B, DIFF_HEADS, nt, VT_ROWS, tm), BF16),
                 jax.ShapeDtypeStruct((B, nt, 8, LANES), F32)]
    out_specs = [tokT(W_QABS), tokT(LANES), tokT(8), tok(LANES), tok(W_CKV),
                 pl.BlockSpec((None, None, VT_ROWS, tm), lambda b, i: (b, i, 0, 0)),
                 tokT(W_FQ), tok(W_FQ),
                 pl.BlockSpec((None, DIFF_HEADS, None, VT_ROWS, tm), lambda b, i: (b, 0, i, 0, 0)),
                 pl.BlockSpec((None, None, 8, LANES), lambda b, i: (b, i, 0, 0))]
    return pl.pallas_call(
        kern,
        out_shape=out_shape,
        grid=(B, nt),
        in_specs=[tok(D),
                  pl.BlockSpec((None, 6, D), lambda b, i: (b, 0, 0)),
                  full(g_attn), full(wn), full(wt), full(w_pair), full(g_kv), full(g_kv_col),
                  full(fc), full(sel)],
        out_specs=out_specs,
        compiler_params=pltpu.CompilerParams(dimension_semantics=("arbitrary", "arbitrary"),
                                             vmem_limit_bytes=VMEM_LIMIT),
        name="proj",
    )(x, mod3, g_attn, wn, wt, w_pair, g_kv, g_kv_col, fc, sel)


def _float_code(x):
    b = lax.bitcast_convert_type(x, jnp.int32)
    return b ^ (lax.shift_right_arithmetic(b, 31) & jnp.int32(0x7FFFFFFF))


def _float_decode(c):
    b = c ^ (lax.shift_right_arithmetic(c, 31) & jnp.int32(0x7FFFFFFF))
    return lax.bitcast_convert_type(b, F32)


def _dsa_kernel(qabsT_ref, iqT_ref, iwT_ref, ik_ref, ckv_ref, ckvT_ref, kmax_ref, lstrict_ref, wuvT_ref,
                o_ref, sc_ref, qst_ref, qt_ref, acc_ref, m_ref, tmp_ref, *, tq, tk, topk):
    qi = pl.program_id(1)
    nkb = (qi * tq) // tk + 1
    kf = float(topk)
    slopes = _alibi_slopes(DSA_HEADS)
    q_pos = qi * tq + lax.broadcasted_iota(jnp.int32, (1, tq), 1)

    iqT = iqT_ref[...]
    rowi = lax.broadcasted_iota(jnp.int32, (LANES, tq), 0)
    for hh in range(IDX_HEADS):
        qst_ref[:, hh * tq:(hh + 1) * tq] = jnp.where((rowi // IDX_DIM) == hh, iqT, jnp.zeros_like(iqT))
    iw = iwT_ref[...]
    wrow = [iw[hh:hh + 1, :] for hh in range(IDX_HEADS)]

    def scores(kb):
        kblk = ik_ref[pl.ds(pl.multiple_of(kb * tk, tk), tk), :]
        a = _dot(kblk, qst_ref[...])
        sc = jnp.maximum(a[:, 0:tq], 0.0) * wrow[0]
        for hh in range(1, IDX_HEADS):
            sc = sc + jnp.maximum(a[:, hh * tq:(hh + 1) * tq], 0.0) * wrow[hh]
        return sc

    def score_body(kb, carry):
        mn, mx = carry
        sc = scores(kb)
        sc_ref[kb] = sc
        return (jnp.minimum(mn, jnp.min(sc, axis=0, keepdims=True)),
                jnp.maximum(mx, jnp.max(sc, axis=0, keepdims=True)))

    last = nkb - 1
    mn, mx = lax.fori_loop(0, last // 2, lambda j, c: score_body(2 * j + 1, score_body(2 * j, c)),
                           (jnp.full((1, tq), jnp.inf, F32), jnp.full((1, tq), -jnp.inf, F32)))
    mn, mx = lax.cond(last % 2 == 1, lambda c: score_body(last - 1, c), lambda c: c, (mn, mx))
    sc = scores(last)
    causal = lax.broadcasted_iota(jnp.int32, (tk, tq), 0) + last * tk <= q_pos
    sc_ref[last] = jnp.where(causal, sc, jnp.nan)
    mn = jnp.minimum(mn, jnp.min(jnp.where(causal, sc, jnp.inf), axis=0, keepdims=True))
    mx = jnp.maximum(mx, jnp.max(jnp.where(causal, sc, -jnp.inf), axis=0, keepdims=True))

    def block_count(kb, pred):
        parts = [None] * COUNT_CHAINS
        for r in range(tk // 8):
            v = jnp.where(pred(sc_ref[kb, r * 8:(r + 1) * 8, :]), 1.0, 0.0)
            c = r % COUNT_CHAINS
            parts[c] = v if parts[c] is None else parts[c] + v
        return (parts[0] + parts[1]) + (parts[2] + parts[3])

    def count(pred):
        acc = lax.fori_loop(0, nkb, lambda kb, acc: acc + block_count(kb, pred),
                            jnp.zeros((8, tq), F32))
        return jnp.sum(acc, axis=0, keepdims=True)

    n_causal = (q_pos + 1).astype(F32)
    done0 = n_causal <= kf

    def bisect(st, value_mid):
        lo, hi, c_lo, c_hi, theta, done = st
        th8 = jnp.broadcast_to(theta, (8, tq))
        c = count(lambda v: v >= th8)
        ge = c >= kf
        live = done == 0.0
        lo = jnp.where(live & ge, theta, lo)
        c_lo = jnp.where(live & ge, c, c_lo)
        hi = jnp.where(live & jnp.logical_not(ge), theta, hi)
        c_hi = jnp.where(live & jnp.logical_not(ge), c, c_hi)
        if value_mid:
            nxt = 0.5 * lo + 0.5 * hi
        else:
            cl, ch = _float_code(lo), _float_code(hi)
            nxt = _float_decode((cl & ch) + lax.shift_right_arithmetic(cl ^ ch, 1))
        inside = (nxt > lo) & (nxt < hi)
        done = jnp.where((c_lo == kf) | jnp.logical_not(inside), 1.0, done)
        return lo, hi, c_lo, c_hi, nxt, done

    def unresolved(st):
        return jnp.sum(jnp.where((st[5] == 0.0) & (st[3] != kf - 1.0), 1.0, 0.0))

    def checked(value_mid, max_steps):
        def cond(c):
            return jnp.logical_and(c[1] > 0.0, c[2] < max_steps)

        def body(c):
            st = bisect(bisect(c[0], value_mid), value_mid)
            return st, unresolved(st), c[2] + 2
        return cond, body

    zero8 = jnp.zeros((8, tq), F32)
    c_ge0 = count(lambda v: v >= zero8)
    c_gt0 = count(lambda v: v > zero8)
    live0 = jnp.logical_not(done0)
    at0 = live0 & (c_ge0 >= kf)
    tie0 = at0 & (c_gt0 < kf)
    below0 = live0 & (c_ge0 < kf)
    lo0 = jnp.where(at0, 0.0, mn)
    hi0 = jnp.where(below0, 0.0, mx)
    st = (lo0, hi0, jnp.where(at0, c_ge0, n_causal), jnp.where(below0, c_ge0, -1.0),
          jnp.where(below0, 0.5 * lo0 + 0.5 * hi0, mx), jnp.where(done0 | tie0, 1.0, 0.0))
    st = lax.fori_loop(0, UNCHECKED_BISECT_STEPS, lambda i, s: bisect(s, True), st)
    cond, body = checked(True, VALUE_BISECT_STEPS - UNCHECKED_BISECT_STEPS)
    st, active, _ = lax.while_loop(cond, body, (st, unresolved(st), jnp.int32(0)))
    cond, body = checked(False, CODE_BISECT_STEPS)
    st, _, _ = lax.while_loop(cond, body, (st, active, jnp.int32(0)))
    tmp_ref[1:2, :] = st[0]
    tmp_ref[2:3, :] = st[2]
    from_hi = (st[5] == 0.0) & (st[3] == kf - 1.0)

    @pl.when(jnp.max(jnp.where(from_hi, 1.0, 0.0)) > 0.0)
    def _():
        hi8 = jnp.broadcast_to(st[1], (8, tq))

        def below_hi(kb, acc):
            parts = [None] * COUNT_CHAINS
            for r in range(tk // 8):
                v = sc_ref[kb, r * 8:(r + 1) * 8, :]
                v = jnp.where(v < hi8, v, -jnp.inf)
                c = r % COUNT_CHAINS
                parts[c] = v if parts[c] is None else jnp.maximum(parts[c], v)
            return jnp.maximum(acc, jnp.maximum(jnp.maximum(parts[0], parts[1]),
                                                jnp.maximum(parts[2], parts[3])))

        nxt = jnp.max(lax.fori_loop(0, nkb, below_hi, jnp.full((8, tq), -jnp.inf, F32)),
                      axis=0, keepdims=True)
        t_new = jnp.where(from_hi, nxt, st[0])
        t8 = jnp.broadcast_to(t_new, (8, tq))
        tmp_ref[1:2, :] = t_new
        tmp_ref[2:3, :] = jnp.where(from_hi, count(lambda v: v >= t8), st[2])

    tau, c_ge = tmp_ref[1:2, :], tmp_ref[2:3, :]

    over = c_ge > kf

    @pl.when(jnp.max(c_ge) > kf)
    def _():
        tau8 = jnp.broadcast_to(tau, (8, tq))
        tmp_ref[0:1, :] = c_gt0

        @pl.when(jnp.max(jnp.where(over & jnp.logical_not(tie0), 1.0, 0.0)) > 0.0)
        def _():
            tmp_ref[0:1, :] = jnp.where(tie0, c_gt0, count(lambda v: v > tau8))

        quota = kf - tmp_ref[0:1, :]

        def tie_body(kb, seen):
            cnt = jnp.sum(block_count(kb, lambda v: v == tau8), axis=0, keepdims=True)
            inside = over & (seen < quota) & (seen + cnt > quota)
            gone = over & (seen >= quota) & (cnt > 0.0)
            flag = jnp.max(jnp.where(inside, 2.0, jnp.where(gone, 1.0, 0.0)))

            @pl.when(flag > 1.5)
            def _():
                s = sc_ref[kb]
                eq = s == tau
                rank = _dot(lstrict_ref[...], jnp.where(eq, 1.0, 0.0).astype(BF16)) + seen
                sc_ref[kb] = jnp.where(eq & over & (rank >= quota), jnp.nan, s)

            @pl.when(flag == 1.0)
            def _():
                s = sc_ref[kb]
                sc_ref[kb] = jnp.where((s == tau) & gone, jnp.nan, s)

            return seen + cnt

        lax.fori_loop(0, nkb, tie_body, jnp.zeros((1, tq), F32))

    tau8 = jnp.broadcast_to(tau, (8, tq))
    sub = lax.broadcasted_iota(jnp.int32, (8, tq), 0).astype(F32)

    def nearest_body(kb, acc):
        parts = [None] * COUNT_CHAINS
        for r in range(tk // 8):
            idx = sub + (kb * tk + r * 8).astype(F32)
            v = jnp.where(sc_ref[kb, r * 8:(r + 1) * 8, :] >= tau8, idx, -1.0)
            c = r % COUNT_CHAINS
            parts[c] = v if parts[c] is None else jnp.maximum(parts[c], v)
        return jnp.maximum(acc, jnp.maximum(jnp.maximum(parts[0], parts[1]),
                                            jnp.maximum(parts[2], parts[3])))

    last_sel = jnp.max(lax.fori_loop(0, nkb, nearest_body, jnp.full((8, tq), -1.0, F32)),
                       axis=0, keepdims=True)
    d_min = q_pos.astype(F32) - last_sel
    kmax2 = _tile_max(kmax_ref, nkb - 1, KMAX_LANE_CKV)
    for hh in range(DSA_HEADS):
        qh = qabsT_ref[hh * DSA_LATENT:(hh + 1) * DSA_LATENT, :]
        qf = qh.astype(F32)
        shift = _shift_bound(jnp.sum(qf * qf, axis=0, keepdims=True), kmax2, -slopes[hh] * d_min)
        qt_ref[0:DSA_LATENT, hh * tq:(hh + 1) * tq] = qh
        qt_ref[DSA_LATENT:VT_ROWS, hh * tq:(hh + 1) * tq] = _query_feature_rows(
            q_pos, BF16_ROWS, slopes[hh], shift).astype(BF16)
    qt_ref[VT_ROWS:2 * DSA_LATENT, :] = jnp.zeros((2 * DSA_LATENT - VT_ROWS, DSA_HEADS * tq), BF16)
    acc_ref[...] = jnp.zeros_like(acc_ref)

    def fast_pv(kb):
        kx = ckv_ref[pl.ds(pl.multiple_of(kb * tk, tk), tk), :]
        s = _dot(kx, qt_ref[...])
        keep = sc_ref[kb] >= tau
        ps = [jnp.exp(jnp.where(keep, s[:, hh * tq:(hh + 1) * tq], NEG_BIG)).astype(BF16)
              for hh in range(DSA_HEADS)]
        return _dot(ckvT_ref[kb], jnp.concatenate(ps, axis=1))

    def fast_pair(j, carry):
        acc_ref[...] += fast_pv(2 * j) + fast_pv(2 * j + 1)
        return carry

    lax.fori_loop(0, nkb // 2, fast_pair, 0)

    @pl.when(nkb % 2 == 1)
    def _():
        acc_ref[...] += fast_pv(nkb - 1)

    def attn_body(kb, carry):
        kx = ckv_ref[pl.ds(pl.multiple_of(kb * tk, tk), tk), :]
        vT = ckvT_ref[kb]
        keep = sc_ref[kb] >= tau
        s = _dot(kx, qt_ref[...])
        m_all = m_ref[...]
        m_out, alphas, ps = [], [], []
        for hh in range(DSA_HEADS):
            c0 = hh * tq
            sh = jnp.where(keep, s[:, c0:c0 + tq], NEG_BIG)
            m_old = m_all[:, c0:c0 + tq]
            m_new = jnp.maximum(m_old, jnp.max(sh, axis=0, keepdims=True))
            ps.append(jnp.exp(sh - m_new).astype(BF16))
            alphas.append(jnp.exp(m_old - m_new))
            m_out.append(m_new)
        m_ref[...] = jnp.concatenate(m_out, axis=1)
        acc_ref[...] = (jnp.concatenate(alphas, axis=1) * acc_ref[...]
                        + _dot(vT, jnp.concatenate(ps, axis=1)))
        return carry

    denom_ok = jnp.min(acc_ref[ONES_ROW:ONES_ROW + 1, :]) > L_MIN

    @pl.when(jnp.logical_not(denom_ok))
    def _():
        acc_ref[...] = jnp.zeros_like(acc_ref)
        m_ref[...] = jnp.full_like(m_ref, NEG_BIG)
        lax.fori_loop(0, nkb, attn_body, 0)

    acc = acc_ref[...]
    lat = acc[0:DSA_LATENT, :] / acc[ONES_ROW:ONES_ROW + 1, :]
    for p in range(DSA_HEADS // 2):
        pair = lat[:, 2 * p * tq:(2 * p + 2) * tq]
        pair = jnp.concatenate([pair[:, 0:tq], pair[:, tq:2 * tq]], axis=0).astype(BF16)
        o_ref[:, p * LANES:(p + 1) * LANES] = _dot(wuvT_ref[p], pair).T.astype(BF16)


def _dsa_call(qabsT, iqT, iwT, ik4, ckv, ckvT, kmax, lstrict, wuvT_pair, tq, tk, topk):
    B, _, S = qabsT.shape
    nkb = S // tk
    kern = functools.partial(_dsa_kernel, tq=tq, tk=tk, topk=topk)
    blkT = lambda r: pl.BlockSpec((None, r, tq), lambda b, i: (b, 0, i))
    per_b = lambda w: pl.BlockSpec((None, S, w), lambda b, i: (b, 0, 0))
    full = lambda a: pl.BlockSpec(a.shape, lambda b, i: (0,) * a.ndim)
    return pl.pallas_call(
        kern,
        out_shape=jax.ShapeDtypeStruct((B, S, W_DQ), BF16),
        grid=(B, S // tq),
        in_specs=[blkT(W_QABS), blkT(LANES), blkT(8), per_b(LANES), per_b(W_CKV),
                  pl.BlockSpec((None, nkb, VT_ROWS, tk), lambda b, i: (b, 0, 0, 0)),
                  pl.BlockSpec((None, nkb, 8, LANES), lambda b, i: (b, 0, 0, 0)),
                  full(lstrict), full(wuvT_pair)],
        out_specs=pl.BlockSpec((None, tq, W_DQ), lambda b, i: (b, i, 0)),
        scratch_shapes=[pltpu.VMEM((nkb, tk, tq), F32),
                        pltpu.VMEM((LANES, IDX_HEADS * tq), BF16),
                        pltpu.VMEM((2 * DSA_LATENT, DSA_HEADS * tq), BF16),
                        pltpu.VMEM((VT_ROWS, DSA_HEADS * tq), F32),
                        pltpu.VMEM((1, DSA_HEADS * tq), F32),
                        pltpu.VMEM((8, tq), F32)],
        compiler_params=pltpu.CompilerParams(dimension_semantics=("arbitrary", "arbitrary"),
                                             vmem_limit_bytes=VMEM_LIMIT),
        name="dsa",
    )(qabsT, iqT, iwT, ik4, ckv, ckvT, kmax, lstrict, wuvT_pair)


def _diff_kernel(fqT_ref, fk_ref, fvT_ref, kmax_ref, lam_ref, gsub_ref, o_ref, qt_ref, acc_ref, m_ref,
                 *, tq, tk, lam0):
    hd = pl.program_id(1)
    qi = pl.program_id(2)
    q_pos = qi * tq + lax.broadcasted_iota(jnp.int32, (1, tq), 1)
    n_full = (qi * tq) // tk
    n_diag = (tq + tk - 1) // tk
    slopes = _alibi_slopes(DIFF_HEADS)
    slope = jnp.float32(slopes[DIFF_HEADS - 1])
    for i in range(DIFF_HEADS - 1):
        slope = jnp.where(hd == i, slopes[i], slope)

    for m in range(2):
        r0 = m * LANES
        qh = fqT_ref[r0:r0 + DIFF_QK_DIM, :]
        qf = qh.astype(F32)
        shift = _shift_bound(jnp.sum(qf * qf, axis=0, keepdims=True),
                             _tile_max(kmax_ref, n_full + n_diag - 1, 2 * hd + m), 0.0)
        qt_ref[r0:r0 + DIFF_QK_DIM, :] = qh
        qt_ref[r0 + DIFF_QK_DIM:r0 + DIFF_QK_DIM + BF16_ROWS, :] = _query_feature_rows(
            q_pos, BF16_ROWS, slope, shift).astype(BF16)
        qt_ref[r0 + DIFF_QK_DIM + BF16_ROWS:r0 + LANES, :] = jnp.zeros(
            (LANES - DIFF_QK_DIM - BF16_ROWS, tq), BF16)
    acc_ref[...] = jnp.zeros_like(acc_ref)

    def fast_pv(kb, masked):
        kx = fk_ref[pl.ds(pl.multiple_of(kb * tk, tk), tk), :]
        vT = fvT_ref[kb]
        out = []
        for m in range(2):
            s = _dot(kx[:, m * LANES:(m + 1) * LANES], qt_ref[m * LANES:(m + 1) * LANES, :])
            if masked:
                causal = lax.broadcasted_iota(jnp.int32, (tk, tq), 0) + kb * tk <= q_pos
                s = jnp.where(causal, s, NEG_BIG)
            out.append(_dot(vT, jnp.exp(s).astype(BF16)))
        return out

    def accumulate(*pvs):
        for m in range(2):
            acc_ref[m] += functools.reduce(lambda a, b: a + b, [pv[m] for pv in pvs])

    def fast_pair(j, carry):
        accumulate(fast_pv(2 * j, False), fast_pv(2 * j + 1, False))
        return carry

    lax.fori_loop(0, n_full // 2, fast_pair, 0)
    assert n_diag in (1, 2)

    @pl.when(n_full % 2 == 1)
    def _():
        accumulate(fast_pv(n_full - 1, False), fast_pv(n_full, True))
        if n_diag == 2:
            accumulate(fast_pv(n_full + 1, True))

    @pl.when(n_full % 2 == 0)
    def _():
        accumulate(*[fast_pv(n_full + d, True) for d in range(n_diag)])

    def block(kb, masked):
        kx = fk_ref[pl.ds(pl.multiple_of(kb * tk, tk), tk), :]
        vT = fvT_ref[kb]
        qT = qt_ref[...]
        m_all = m_ref[...]
        new = []
        for m in range(2):
            s = _dot(kx[:, m * LANES:(m + 1) * LANES], qT[m * LANES:(m + 1) * LANES, :])
            if masked:
                causal = lax.broadcasted_iota(jnp.int32, (tk, tq), 0) + kb * tk <= q_pos
                s = jnp.where(causal, s, NEG_BIG)
            m_old = m_all[m]
            m_new = jnp.maximum(m_old, jnp.max(s, axis=0, keepdims=True))
            p = jnp.exp(s - m_new).astype(BF16)
            new.append((m_new, jnp.exp(m_old - m_new), _dot(vT, p)))
        for m in range(2):
            m_ref[m] = new[m][0]
            acc_ref[m] = new[m][1] * acc_ref[m] + new[m][2]

    def body(kb, carry):
        block(kb, False)
        return carry

    denom_ok = jnp.min(acc_ref[:, ONES_ROW:ONES_ROW + 1, :]) > L_MIN

    @pl.when(jnp.logical_not(denom_ok))
    def _():
        acc_ref[...] = jnp.zeros_like(acc_ref)
        m_ref[...] = jnp.full_like(m_ref, NEG_BIG)
        lax.fori_loop(0, n_full, body, 0)
        for d in range(n_diag):
            block(n_full + d, True)

    lv = lam_ref[...]
    lam = (jnp.exp(jnp.sum(lv[0:1] * lv[1:2], axis=-1, keepdims=True))
           - jnp.exp(jnp.sum(lv[2:3] * lv[3:4], axis=-1, keepdims=True)) + lam0)
    a1 = acc_ref[0]
    a2 = acc_ref[1]
    o = (a1[0:DIFF_V_DIM, :] / a1[ONES_ROW:ONES_ROW + 1, :]
         - lam * (a2[0:DIFF_V_DIM, :] / a2[ONES_ROW:ONES_ROW + 1, :]))
    o = o * lax.rsqrt(jnp.mean(o * o, axis=0, keepdims=True) + EPS) * gsub_ref[...] * (1.0 - lam0)
    o_ref[...] = o.T.astype(BF16)


def _diff_call(fqT, fk, fvT, kmax, lam_vecs, g_sub_col, tq, tk, lam0):
    B, S, _ = fk.shape
    nkb = S // tk
    kern = functools.partial(_diff_kernel, tq=tq, tk=tk, lam0=lam0)
    full = lambda a: pl.BlockSpec(a.shape, lambda b, h, i: (0,) * a.ndim)
    return pl.pallas_call(
        kern,
        out_shape=jax.ShapeDtypeStruct((B, S, DIFF_HEADS * DIFF_V_DIM), BF16),
        grid=(B, DIFF_HEADS, S // tq),
        in_specs=[pl.BlockSpec((None, 2 * LANES, tq), lambda b, h, i: (b, h, i)),
                  pl.BlockSpec((None, S, 2 * LANES), lambda b, h, i: (b, 0, h)),
                  pl.BlockSpec((None, None, nkb, VT_ROWS, tk), lambda b, h, i: (b, h, 0, 0, 0)),
                  pl.BlockSpec((None, nkb, 8, LANES), lambda b, h, i: (b, 0, 0, 0)),
                  full(lam_vecs), full(g_sub_col)],
        out_specs=pl.BlockSpec((None, tq, DIFF_V_DIM), lambda b, h, i: (b, i, h)),
        scratch_shapes=[pltpu.VMEM((2 * LANES, tq), BF16),
                        pltpu.VMEM((2, VT_ROWS, tq), F32),
                        pltpu.VMEM((2, 1, tq), F32)],
        compiler_params=pltpu.CompilerParams(
            dimension_semantics=("arbitrary", "arbitrary", "arbitrary"),
            vmem_limit_bytes=VMEM_LIMIT),
        name="diff",
    )(fqT, fk, fvT, kmax, lam_vecs, g_sub_col)


def _route_kernel(x_ref, od_ref, of_ref, mod_ref, wout_ref, g_ref, wr_hi_ref, wr_lo_ref, br_ref,
                  x1_ref, h_ref, gates_ref, *, tm):
    attn = (_dot(od_ref[...], wout_ref[0:W_DQ, :]) + _dot(of_ref[...], wout_ref[W_DQ:, :]))
    x1 = x_ref[...] + mod_ref[2:3, :] * attn
    x1_ref[...] = x1
    h = _rms(x1, g_ref[...]) * (1.0 + mod_ref[4:5, :]) + mod_ref[3:4, :]
    h_hi = h.astype(BF16)
    h_ref[...] = h_hi
    h_lo = (h - h_hi.astype(F32)).astype(BF16)
    w_hi = wr_hi_ref[...]
    logits = (_dot(h_hi, w_hi) + _dot(h_lo, w_hi) + _dot(h_hi, wr_lo_ref[...])) + br_ref[...]

    lane = lax.broadcasted_iota(jnp.int32, (tm, LANES), 1)
    big = jnp.int32(4 * LANES)
    neg = -jnp.inf
    is_g = (lane >= N_EXPERTS) & (lane < N_EXPERTS + N_GROUPS)
    gl = jnp.where(is_g, logits, neg)
    gmax = jnp.max(gl, axis=-1, keepdims=True)
    g_lane = jnp.min(jnp.where(gl == gmax, lane, big), axis=-1, keepdims=True)
    g_sel = g_lane - N_EXPERTS
    p_g = 1.0 / jnp.sum(jnp.where(is_g, jnp.exp(gl - gmax), 0.0), axis=-1, keepdims=True)
    in_grp = (lane < N_EXPERTS) & ((lane // EXPERTS_PER_GROUP) == g_sel)
    el = jnp.where(in_grp, logits, neg)
    v1 = jnp.max(el, axis=-1, keepdims=True)
    i1 = jnp.min(jnp.where(el == v1, lane, big), axis=-1, keepdims=True)
    el2 = jnp.where(lane == i1, neg, el)
    v2 = jnp.max(el2, axis=-1, keepdims=True)
    i2 = jnp.min(jnp.where(el2 == v2, lane, big), axis=-1, keepdims=True)
    e = jnp.exp(v2 - v1)
    w1 = 1.0 / (1.0 + e)
    w2 = e * w1
    gates_ref[...] = jnp.where(lane == i1, w1 * p_g, jnp.where(lane == i2, w2 * p_g, 0.0))


def _route_call(x, o_dsa, o_diff, mod3, w_out, g_moe, wr_hi, wr_lo, b_r, tm):
    B, S, D = x.shape
    kern = functools.partial(_route_kernel, tm=tm)
    tok = lambda w: pl.BlockSpec((None, tm, w), lambda b, i: (b, i, 0))
    full = lambda a: pl.BlockSpec(a.shape, lambda b, i: (0,) * a.ndim)
    return pl.pallas_call(
        kern,
        out_shape=[jax.ShapeDtypeStruct((B, S, D), F32), jax.ShapeDtypeStruct((B, S, D), BF16),
                   jax.ShapeDtypeStruct((B, S, LANES), F32)],
        grid=(B, S // tm),
        in_specs=[tok(D), tok(W_DQ), tok(DIFF_HEADS * DIFF_V_DIM),
                  pl.BlockSpec((None, 6, D), lambda b, i: (b, 0, 0)),
                  full(w_out), full(g_moe), full(wr_hi), full(wr_lo), full(b_r)],
        out_specs=[tok(D), tok(D), tok(LANES)],
        compiler_params=pltpu.CompilerParams(dimension_semantics=("arbitrary", "arbitrary"),
                                             vmem_limit_bytes=VMEM_LIMIT),
        name="route",
    )(x, o_dsa, o_diff, mod3, w_out, g_moe, wr_hi, wr_lo, b_r)


def _moe_kernel(h_ref, gates_ref, x1_ref, mod_ref, wg_ref, wu_ref, wd_ref, gf_ref, o_ref, acc_ref,
                *, tm, ec):
    j = pl.program_id(2)
    nc = pl.num_programs(2)

    @pl.when(j == 0)
    def _():
        acc_ref[...] = jnp.zeros_like(acc_ref)

    h = h_ref[...]
    hg = _dot(h, wg_ref[...])
    hu = _dot(h, wu_ref[...])
    hid = hg * jax.nn.sigmoid(hg) * hu
    gates = gates_ref[...]
    lane = lax.broadcasted_iota(jnp.int32, (tm, LANES), 1)
    parts = []
    for e in range(ec):
        gcol = jnp.sum(jnp.where(lane == j * ec + e, gates, 0.0), axis=-1, keepdims=True)
        parts.append((hid[:, e * D_EXPERT:(e + 1) * D_EXPERT] * gcol).astype(BF16))
    hs = jnp.concatenate(parts, axis=1)
    acc_ref[...] += _dot(hs, wd_ref[...])

    @pl.when(j == nc - 1)
    def _():
        x2 = x1_ref[...] + mod_ref[5:6, :] * acc_ref[...]
        o_ref[...] = _rms(x2, gf_ref[...])


def _moe_call(h, gates, x1, mod3, wg, wu, wd, g_final, tm, ec):
    B, S, D = x1.shape
    kern = functools.partial(_moe_kernel, tm=tm, ec=ec)
    tok = lambda w: pl.BlockSpec((None, tm, w), lambda b, i, j: (b, i, 0))
    return pl.pallas_call(
        kern,
        out_shape=jax.ShapeDtypeStruct((B, S, D), F32),
        grid=(B, S // tm, N_EXPERTS // ec),
        in_specs=[tok(D), tok(LANES), tok(D),
                  pl.BlockSpec((None, 6, D), lambda b, i, j: (b, 0, 0)),
                  pl.BlockSpec((D, ec * D_EXPERT), lambda b, i, j: (0, j)),
                  pl.BlockSpec((D, ec * D_EXPERT), lambda b, i, j: (0, j)),
                  pl.BlockSpec((ec * D_EXPERT, D), lambda b, i, j: (j, 0)),
                  pl.BlockSpec((1, D), lambda b, i, j: (0, 0))],
        out_specs=tok(D),
        scratch_shapes=[pltpu.VMEM((tm, D), F32)],
        compiler_params=pltpu.CompilerParams(
            dimension_semantics=("arbitrary", "arbitrary", "arbitrary"),
            vmem_limit_bytes=VMEM_LIMIT),
        name="moe",
    )(h, gates, x1, mod3, wg, wu, wd, g_final)


def _tile(n, pref):
    t = min(n, pref)
    assert n % t == 0, (n, t)
    return t


def kernel(x, c, w_ada, b_ada, g_attn, w_in, g_kv, w_uk, w_uv, lam_q1, lam_k1, lam_q2, lam_k2,
           g_sub, w_out, g_moe, w_group, b_group, w_router, b_router, w_gate, w_up, w_down, g_final):
    B, S, D = x.shape
    assert D == D_MODEL and w_ada.shape[0] == 1
    assert S % KEY_TILE == 0 and S <= POS_SPLIT * 256
    topk = min(TOPK_MAX, S // 4)
    l = 0
    lam0 = 0.8 - 0.6 * math.exp(-0.3 * l)

    mod3 = _mod_call(c, w_ada[l], b_ada[l].reshape(1, -1)).reshape(B, 6, D)

    wn, wt, w_pair = _proj_weights(w_in[l], w_uk[l])
    qabsT, iqT, iwT, ik4, ckv, ckvT, fqT, fk, fvT, kmax = _proj_call(
        x, mod3, g_attn[l].reshape(1, D), wn, wt, w_pair, g_kv[l].reshape(1, -1), KEY_TILE)

    lstrict = jnp.asarray(np.tril(np.ones((KEY_TILE, KEY_TILE), np.float32), -1), BF16)
    uvT = jnp.swapaxes(w_uv[l], 1, 2)
    z = jnp.zeros_like(uvT[0])
    wuvT_pair = jnp.stack([
        jnp.concatenate([jnp.concatenate([uvT[2 * p], z], axis=1),
                         jnp.concatenate([z, uvT[2 * p + 1]], axis=1)], axis=0)
        for p in range(DSA_HEADS // 2)]).astype(BF16)
    o_dsa = _dsa_call(qabsT, iqT, iwT, ik4, ckv, ckvT, kmax, lstrict, wuvT_pair, DSA_QUERY_TILE, KEY_TILE,
                      topk)

    lam_vecs = jnp.concatenate([lam_q1[l][None], lam_k1[l][None], lam_q2[l][None], lam_k2[l][None]],
                               axis=0).astype(F32)
    o_diff = _diff_call(fqT, fk, fvT, kmax, lam_vecs, g_sub[l].reshape(-1, 1),
                        _tile(S, DIFF_QUERY_TILE), KEY_TILE, lam0)

    wr = jnp.pad(jnp.concatenate([w_router[l], w_group[l]], axis=1),
                 ((0, 0), (0, LANES - N_EXPERTS - N_GROUPS)))
    wr_hi = wr.astype(BF16)
    wr_lo = (wr - wr_hi.astype(F32)).astype(BF16)
    b_r = jnp.pad(jnp.concatenate([b_router[l], b_group[l]]), (0, LANES - N_EXPERTS - N_GROUPS))
    x1, h2, gates = _route_call(x, o_dsa, o_diff, mod3, w_out[l].astype(BF16),
                                g_moe[l].reshape(1, D), wr_hi, wr_lo, b_r.reshape(1, LANES),
                                _tile(S, TOKEN_TILE))

    wg = jnp.swapaxes(w_gate[l], 0, 1).reshape(D, N_EXPERTS * D_EXPERT).astype(BF16)
    wu = jnp.swapaxes(w_up[l], 0, 1).reshape(D, N_EXPERTS * D_EXPERT).astype(BF16)
    wd = w_down[l].reshape(N_EXPERTS * D_EXPERT, D).astype(BF16)
    return _moe_call(h2, gates, x1, mod3, wg, wu, wd, g_final.reshape(1, D), _tile(S, MOE_TOKEN_TILE),
                     MOE_EXPERTS_PER_STEP)
```

```python
import functools
import math

import jax
import jax.numpy as jnp
import numpy as np
from jax import lax
from jax.experimental import pallas as pl
from jax.experimental.pallas import tpu as pltpu

F32 = jnp.float32
BF16 = jnp.bfloat16

D_MODEL = 1024
DSA_HEADS = 8
DSA_HEAD_DIM = 64
DSA_LATENT = 128
IDX_HEADS = 4
IDX_DIM = 32
TOPK_MAX = 256
DIFF_HEADS = 4
DIFF_QK_DIM = 64
DIFF_V_DIM = 128
N_GROUPS = 4
EXPERTS_PER_GROUP = 8
N_EXPERTS = 32
D_EXPERT = 256
EPS = 1e-6

LANES = 128
BF16_ROWS = 16
POS_SPLIT = 64
NEG_BIG = -1e30
VMEM_LIMIT = 56 * 1024 * 1024
KEY_TILE = 512
VALUE_BISECT_STEPS = 24
UNCHECKED_BISECT_STEPS = 14
COUNT_CHAINS = 4

W_DQ = DSA_HEADS * DSA_HEAD_DIM
W_QABS = DSA_HEADS * DSA_LATENT
W_CKV = 2 * LANES
VT_ROWS = DSA_LATENT + BF16_ROWS
ONES_ROW = DSA_LATENT + 2
N_FEATS = 5
KMAX_LANE_CKV = 8
L_MIN = 1e-30
BF16_ROUND_UP = 1.0 + 2.0 ** -7
BOUND_SLACK = 2.0 ** -6
CODE_BISECT_STEPS = 34
TOKEN_TILE = 512
DSA_QUERY_TILE = 256
DIFF_QUERY_TILE = 1024
MOE_EXPERTS_PER_STEP = 8
MOE_TOKEN_TILE = 512
W_FQ = DIFF_HEADS * 2 * LANES
W_DIFF_QK = DIFF_HEADS * 2 * DIFF_QK_DIM

T_DQ = 0
T_DLAT = T_DQ + W_DQ
T_IQ = T_DLAT + DSA_LATENT
T_IW = T_IQ + IDX_HEADS * IDX_DIM
T_FQ = T_IW + BF16_ROWS
T_FV = T_FQ + W_DIFF_QK
T_ROWS = T_FV + DIFF_HEADS * DIFF_V_DIM
N_IK = 0
N_DLAT = N_IK + LANES
N_FK = N_DLAT + DSA_LATENT
N_COLS = N_FK + W_FQ


def _alibi_slopes(n):
    return [2.0 ** (-8.0 * (i + 1) / n) for i in range(n)]


def _rms(x, g):
    return x * lax.rsqrt(jnp.mean(x * x, axis=-1, keepdims=True) + EPS) * g


def _dot(a, b):
    return jnp.dot(a, b, preferred_element_type=F32)


def _mod_kernel(c_ref, w_ref, b_ref, o_ref):
    c = c_ref[...]
    act = c * jax.nn.sigmoid(c)
    o_ref[...] = jnp.dot(act, w_ref[...], preferred_element_type=F32,
                         precision=lax.Precision.HIGHEST) + b_ref[...]


def _mod_call(c, w_ada, b_ada):
    B, D = c.shape
    n = w_ada.shape[1] // D
    return pl.pallas_call(
        _mod_kernel,
        out_shape=jax.ShapeDtypeStruct((B, n * D), F32),
        grid=(n,),
        in_specs=[pl.BlockSpec((B, D), lambda j: (0, 0)),
                  pl.BlockSpec((D, D), lambda j: (0, j)),
                  pl.BlockSpec((1, D), lambda j: (0, j))],
        out_specs=pl.BlockSpec((B, D), lambda j: (0, j)),
        compiler_params=pltpu.CompilerParams(dimension_semantics=("arbitrary",),
                                             vmem_limit_bytes=VMEM_LIMIT),
        name="mod",
    )(c, w_ada, b_ada)


def _key_feature_rows(pos_row, n_rows):
    r = lax.broadcasted_iota(jnp.int32, (n_rows, pos_row.shape[1]), 0)
    pa = (pos_row // POS_SPLIT).astype(F32)
    pb = (pos_row % POS_SPLIT).astype(F32)
    return jnp.where(r == 0, pa, jnp.where(r == 1, pb, jnp.where(r < N_FEATS, 1.0, 0.0)))


def _query_feature_rows(pos_row, n_rows, slope, shift):
    r = lax.broadcasted_iota(jnp.int32, (n_rows, pos_row.shape[1]), 0)
    pa = (pos_row // POS_SPLIT).astype(F32)
    pb = (pos_row % POS_SPLIT).astype(F32)
    return jnp.where(r == 0, POS_SPLIT * slope,
                     jnp.where(r == 1, slope,
                               jnp.where(r == 2, -POS_SPLIT * slope * pa,
                                         jnp.where(r == 3, -slope * pb,
                                                   jnp.where(r == 4, -shift, 0.0)))))


def _shift_bound(qn2, kmax2, extra):
    b = jnp.sqrt(qn2 * kmax2) + extra
    return b + jnp.abs(b) * BOUND_SLACK + BOUND_SLACK


def _tile_max(kmax_ref, last_tile, lane_idx):
    x = kmax_ref[...]
    t = lax.broadcasted_iota(jnp.int32, x.shape, 0)
    ln = lax.broadcasted_iota(jnp.int32, x.shape, 2)
    x = jnp.where((t <= last_tile) & (ln == lane_idx), x, 0.0)
    return jnp.max(jnp.max(x, axis=0), axis=1, keepdims=True)[0:1, :]


def _proj_kernel(x_ref, mod_ref, g_ref, wn_ref, wt_ref, wpair_ref, gkv_ref, gkvc_ref, fc_ref, sel_ref,
                 qabsT_ref, iqT_ref, iwT_ref, ik_ref, ckv_ref, ckvT_ref, fqT_ref, fk_ref, fvT_ref,
                 kmax_ref, *, tm):
    x = x_ref[...]
    h = _rms(x, g_ref[...]) * (1.0 + mod_ref[1:2, :]) + mod_ref[0:1, :]
    hb = h.astype(BF16)
    hT = h.T.astype(BF16)

    base = pl.program_id(1) * tm
    pos_c = base + lax.broadcasted_iota(jnp.int32, (tm, 1), 0)
    pos_r = base + lax.broadcasted_iota(jnp.int32, (1, tm), 1)
    pa = (pos_c // POS_SPLIT).astype(F32)
    pb = (pos_c % POS_SPLIT).astype(F32)

    def feats(row, width):
        return (fc_ref[row:row + 1, 0:width] + fc_ref[row + 1:row + 2, 0:width] * pa
                + fc_ref[row + 2:row + 3, 0:width] * pb)

    ik_ref[...] = _dot(hb, wn_ref[:, N_IK:N_IK + LANES]).astype(BF16)
    dlat = _dot(hb, wn_ref[:, N_DLAT:N_DLAT + DSA_LATENT])
    ckv_b = _rms(dlat, gkv_ref[...]).astype(BF16)
    ckv_ref[:, 0:LANES] = ckv_b
    ckv_ref[:, LANES:2 * LANES] = feats(0, LANES).astype(BF16)
    fk = _dot(hb, wn_ref[:, N_FK:N_FK + W_FQ])
    fk_ref[...] = (fk + feats(3, W_FQ)).astype(BF16)
    sq = jnp.concatenate([fk.astype(BF16).astype(F32), ckv_b.astype(F32)], axis=1)
    sq_up = (sq * sq * BF16_ROUND_UP).astype(BF16)
    kmax_ref[...] = jnp.broadcast_to(jnp.max(_dot(sq_up, sel_ref[...]), axis=0, keepdims=True),
                                     (8, LANES))

    dqT = _dot(wt_ref[T_DQ:T_DQ + W_DQ, :], hT).astype(BF16)
    for p in range(DSA_HEADS // 2):
        qa = _dot(wpair_ref[p], dqT[p * LANES:(p + 1) * LANES, :])
        qabsT_ref[p * 2 * LANES:(p + 1) * 2 * LANES, :] = qa.astype(BF16)
    dlatT = _dot(wt_ref[T_DLAT:T_DLAT + DSA_LATENT, :], hT)
    inv = lax.rsqrt(jnp.mean(dlatT * dlatT, axis=0, keepdims=True) + EPS)
    kfeat = _key_feature_rows(pos_r, BF16_ROWS).astype(BF16)
    ckvT_ref[0:DSA_LATENT, :] = (dlatT * inv * gkvc_ref[...]).astype(BF16)
    ckvT_ref[DSA_LATENT:VT_ROWS, :] = kfeat
    iqT_ref[...] = _dot(wt_ref[T_IQ:T_IQ + IDX_HEADS * IDX_DIM, :], hT).astype(BF16)
    iwT = _dot(wt_ref[T_IW:T_IW + BF16_ROWS, :], hT)
    iwT_ref[...] = iwT[0:8, :] * (IDX_HEADS ** -0.5 * IDX_DIM ** -0.5)
    fqT = _dot(wt_ref[T_FQ:T_FQ + W_DIFF_QK, :], hT)
    zpad = jnp.zeros((LANES - DIFF_QK_DIM, tm), BF16)
    for s in range(DIFF_HEADS * 2):
        r0 = s * LANES
        fqT_ref[r0:r0 + DIFF_QK_DIM, :] = fqT[s * DIFF_QK_DIM:(s + 1) * DIFF_QK_DIM, :].astype(BF16)
        fqT_ref[r0 + DIFF_QK_DIM:r0 + LANES, :] = zpad
    fvT = _dot(wt_ref[T_FV:T_FV + DIFF_HEADS * DIFF_V_DIM, :], hT)
    for hh in range(DIFF_HEADS):
        fvT_ref[hh, 0:DIFF_V_DIM, :] = fvT[hh * DIFF_V_DIM:(hh + 1) * DIFF_V_DIM, :].astype(BF16)
        fvT_ref[hh, DIFF_V_DIM:VT_ROWS, :] = kfeat


def _feature_consts():
    fc = np.zeros((8, W_FQ), np.float32)
    fc[1, 0] = 1.0
    fc[2, 1] = 1.0
    fc[0, 2:N_FEATS] = 1.0
    for s in range(DIFF_HEADS * 2):
        base = s * LANES + DIFF_QK_DIM
        fc[4, base + 0] = 1.0
        fc[5, base + 1] = 1.0
        fc[3, base + 2:base + N_FEATS] = 1.0
    return jnp.asarray(fc)


def _norm_selector():
    sel = np.zeros((W_FQ + DSA_LATENT, LANES), np.float32)
    for s in range(DIFF_HEADS * 2):
        sel[s * LANES:s * LANES + DIFF_QK_DIM, s] = 1.0
    sel[W_FQ:, KMAX_LANE_CKV] = 1.0
    return jnp.asarray(sel, BF16)


def _proj_weights(w_in, w_uk):
    D = w_in.shape[0]
    pts = np.cumsum([W_DQ, DSA_LATENT, IDX_HEADS * IDX_DIM, IDX_DIM, IDX_HEADS,
                     W_DIFF_QK, W_DIFF_QK])
    dq, dlat, iq, ik, iw, fq, fk, fv = jnp.split(w_in, list(pts), axis=1)
    ik4 = jnp.tile(ik, (1, IDX_HEADS))
    fke = jnp.pad(fk.reshape(D, DIFF_HEADS * 2, DIFF_QK_DIM),
                  ((0, 0), (0, 0), (0, LANES - DIFF_QK_DIM))).reshape(D, W_FQ)
    wn = jnp.concatenate([ik4, dlat, fke], axis=1).astype(BF16)
    iwp = jnp.pad(iw, ((0, 0), (0, BF16_ROWS - IDX_HEADS)))
    wt = jnp.concatenate([dq, dlat, iq, iwp, fq * (DIFF_QK_DIM ** -0.5), fv], axis=1).T.astype(BF16)
    uk = w_uk * (DSA_HEAD_DIM ** -0.5)
    z = jnp.zeros_like(uk[0])
    pairs = [jnp.concatenate([jnp.concatenate([uk[2 * p], z], axis=1),
                              jnp.concatenate([z, uk[2 * p + 1]], axis=1)], axis=0)
             for p in range(DSA_HEADS // 2)]
    return wn, wt, jnp.stack(pairs).astype(BF16)


def _proj_call(x, mod3, g_attn, wn, wt, w_pair, g_kv, tm):
    B, S, D = x.shape
    nt = S // tm
    fc = _feature_consts()
    sel = _norm_selector()
    kern = functools.partial(_proj_kernel, tm=tm)
    tok = lambda w: pl.BlockSpec((None, tm, w), lambda b, i: (b, i, 0))
    tokT = lambda r: pl.BlockSpec((None, r, tm), lambda b, i: (b, 0, i))
    full = lambda a: pl.BlockSpec(a.shape, lambda b, i: (0,) * a.ndim)
    g_kv_col = g_kv.reshape(-1, 1)
    out_shape = [jax.ShapeDtypeStruct((B, W_QABS, S), BF16),
                 jax.ShapeDtypeStruct((B, LANES, S), BF16),
                 jax.ShapeDtypeStruct((B, 8, S), F32),
                 jax.ShapeDtypeStruct((B, S, LANES), BF16),
                 jax.ShapeDtypeStruct((B, S, W_CKV), BF16),
                 jax.ShapeDtypeStruct((B, nt, VT_ROWS, tm), BF16),
                 jax.ShapeDtypeStruct((B, W_FQ, S), BF16),
                 jax.ShapeDtypeStruct((B, S, W_FQ), BF16),
                 jax.ShapeDtypeStruct((B, DIFF_HEADS, nt, VT_ROWS, tm), BF16),
                 jax.ShapeDtypeStruct((B, nt, 8, LANES), F32)]
    out_specs = [tokT(W_QABS), tokT(LANES), tokT(8), tok(LANES), tok(W_CKV),
                 pl.BlockSpec((None, None, VT_ROWS, tm), lambda b, i: (b, i, 0, 0)),
                 tokT(W_FQ), tok(W_FQ),
                 pl.BlockSpec((None, DIFF_HEADS, None, VT_ROWS, tm), lambda b, i: (b, 0, i, 0, 0)),
                 pl.BlockSpec((None, None, 8, LANES), lambda b, i: (b, i, 0, 0))]
    return pl.pallas_call(
        kern,
        out_shape=out_shape,
        grid=(B, nt),
        in_specs=[tok(D),
                  pl.BlockSpec((None, 6, D), lambda b, i: (b, 0, 0)),
                  full(g_attn), full(wn), full(wt), full(w_pair), full(g_kv), full(g_kv_col),
                  full(fc), full(sel)],
        out_specs=out_specs,
        compiler_params=pltpu.CompilerParams(dimension_semantics=("arbitrary", "arbitrary"),
                                             vmem_limit_bytes=VMEM_LIMIT),
        name="proj",
    )(x, mod3, g_attn, wn, wt, w_pair, g_kv, g_kv_col, fc, sel)


def _float_code(x):
    b = lax.bitcast_convert_type(x, jnp.int32)
    return b ^ (lax.shift_right_arithmetic(b, 31) & jnp.int32(0x7FFFFFFF))


def _float_decode(c):
    b = c ^ (lax.shift_right_arithmetic(c, 31) & jnp.int32(0x7FFFFFFF))
    return lax.bitcast_convert_type(b, F32)


def _dsa_kernel(qabsT_ref, iqT_ref, iwT_ref, ik_ref, ckv_ref, ckvT_ref, kmax_ref, lstrict_ref, wuvT_ref,
                o_ref, sc_ref, qst_ref, qt_ref, acc_ref, m_ref, tmp_ref, *, tq, tk, topk):
    qi = pl.program_id(1)
    nkb = (qi * tq) // tk + 1
    kf = float(topk)
    slopes = _alibi_slopes(DSA_HEADS)
    q_pos = qi * tq + lax.broadcasted_iota(jnp.int32, (1, tq), 1)

    iqT = iqT_ref[...]
    rowi = lax.broadcasted_iota(jnp.int32, (LANES, tq), 0)
    for hh in range(IDX_HEADS):
        qst_ref[:, hh * tq:(hh + 1) * tq] = jnp.where((rowi // IDX_DIM) == hh, iqT, jnp.zeros_like(iqT))
    iw = iwT_ref[...]
    wrow = [iw[hh:hh + 1, :] for hh in range(IDX_HEADS)]

    def scores(kb):
        kblk = ik_ref[pl.ds(pl.multiple_of(kb * tk, tk), tk), :]
        a = _dot(kblk, qst_ref[...])
        sc = jnp.maximum(a[:, 0:tq], 0.0) * wrow[0]
        for hh in range(1, IDX_HEADS):
            sc = sc + jnp.maximum(a[:, hh * tq:(hh + 1) * tq], 0.0) * wrow[hh]
        return sc

    def score_body(kb, carry):
        mn, mx = carry
        sc = scores(kb)
        sc_ref[kb] = sc
        return (jnp.minimum(mn, jnp.min(sc, axis=0, keepdims=True)),
                jnp.maximum(mx, jnp.max(sc, axis=0, keepdims=True)))

    last = nkb - 1
    mn, mx = lax.fori_loop(0, last // 2, lambda j, c: score_body(2 * j + 1, score_body(2 * j, c)),
                           (jnp.full((1, tq), jnp.inf, F32), jnp.full((1, tq), -jnp.inf, F32)))
    mn, mx = lax.cond(last % 2 == 1, lambda c: score_body(last - 1, c), lambda c: c, (mn, mx))
    sc = scores(last)
    causal = lax.broadcasted_iota(jnp.int32, (tk, tq), 0) + last * tk <= q_pos
    sc_ref[last] = jnp.where(causal, sc, jnp.nan)
    mn = jnp.minimum(mn, jnp.min(jnp.where(causal, sc, jnp.inf), axis=0, keepdims=True))
    mx = jnp.maximum(mx, jnp.max(jnp.where(causal, sc, -jnp.inf), axis=0, keepdims=True))

    def block_count(kb, pred):
        parts = [None] * COUNT_CHAINS
        for r in range(tk // 8):
            v = jnp.where(pred(sc_ref[kb, r * 8:(r + 1) * 8, :]), 1.0, 0.0)
            c = r % COUNT_CHAINS
            parts[c] = v if parts[c] is None else parts[c] + v
        return (parts[0] + parts[1]) + (parts[2] + parts[3])

    def count(pred):
        acc = lax.fori_loop(0, nkb, lambda kb, acc: acc + block_count(kb, pred),
                            jnp.zeros((8, tq), F32))
        return jnp.sum(acc, axis=0, keepdims=True)

    n_causal = (q_pos + 1).astype(F32)
    done0 = n_causal <= kf

    def bisect(st, value_mid):
        lo, hi, c_lo, c_hi, theta, done = st
        th8 = jnp.broadcast_to(theta, (8, tq))
        c = count(lambda v: v >= th8)
        ge = c >= kf
        live = done == 0.0
        lo = jnp.where(live & ge, theta, lo)
        c_lo = jnp.where(live & ge, c, c_lo)
        hi = jnp.where(live & jnp.logical_not(ge), theta, hi)
        c_hi = jnp.where(live & jnp.logical_not(ge), c, c_hi)
        if value_mid:
            nxt = 0.5 * lo + 0.5 * hi
        else:
            cl, ch = _float_code(lo), _float_code(hi)
            nxt = _float_decode((cl & ch) + lax.shift_right_arithmetic(cl ^ ch, 1))
        inside = (nxt > lo) & (nxt < hi)
        done = jnp.where((c_lo == kf) | jnp.logical_not(inside), 1.0, done)
        return lo, hi, c_lo, c_hi, nxt, done

    def unresolved(st):
        return jnp.sum(jnp.where((st[5] == 0.0) & (st[3] != kf - 1.0), 1.0, 0.0))

    def checked(value_mid, max_steps):
        def cond(c):
            return jnp.logical_and(c[1] > 0.0, c[2] < max_steps)

        def body(c):
            st = bisect(bisect(c[0], value_mid), value_mid)
            return st, unresolved(st), c[2] + 2
        return cond, body

    zero8 = jnp.zeros((8, tq), F32)
    c_ge0 = count(lambda v: v >= zero8)
    c_gt0 = count(lambda v: v > zero8)
    live0 = jnp.logical_not(done0)
    at0 = live0 & (c_ge0 >= kf)
    tie0 = at0 & (c_gt0 < kf)
    below0 = live0 & (c_ge0 < kf)
    lo0 = jnp.where(at0, 0.0, mn)
    hi0 = jnp.where(below0, 0.0, mx)
    st = (lo0, hi0, jnp.where(at0, c_ge0, n_causal), jnp.where(below0, c_ge0, -1.0),
          jnp.where(below0, 0.5 * lo0 + 0.5 * hi0, mx), jnp.where(done0 | tie0, 1.0, 0.0))
    st = lax.fori_loop(0, UNCHECKED_BISECT_STEPS, lambda i, s: bisect(s, True), st)
    cond, body = checked(True, VALUE_BISECT_STEPS - UNCHECKED_BISECT_STEPS)
    st, active, _ = lax.while_loop(cond, body, (st, unresolved(st), jnp.int32(0)))
    cond, body = checked(False, CODE_BISECT_STEPS)
    st, _, _ = lax.while_loop(cond, body, (st, active, jnp.int32(0)))
    tmp_ref[1:2, :] = st[0]
    tmp_ref[2:3, :] = st[2]
    from_hi = (st[5] == 0.0) & (st[3] == kf - 1.0)

    @pl.when(jnp.max(jnp.where(from_hi, 1.0, 0.0)) > 0.0)
    def _():
        hi8 = jnp.broadcast_to(st[1], (8, tq))

        def below_hi(kb, acc):
            parts = [None] * COUNT_CHAINS
            for r in range(tk // 8):
                v = sc_ref[kb, r * 8:(r + 1) * 8, :]
                v = jnp.where(v < hi8, v, -jnp.inf)
                c = r % COUNT_CHAINS
                parts[c] = v if parts[c] is None else jnp.maximum(parts[c], v)
            return jnp.maximum(acc, jnp.maximum(jnp.maximum(parts[0], parts[1]),
                                                jnp.maximum(parts[2], parts[3])))

        nxt = jnp.max(lax.fori_loop(0, nkb, below_hi, jnp.full((8, tq), -jnp.inf, F32)),
                      axis=0, keepdims=True)
        t_new = jnp.where(from_hi, nxt, st[0])
        t8 = jnp.broadcast_to(t_new, (8, tq))
        tmp_ref[1:2, :] = t_new
        tmp_ref[2:3, :] = jnp.where(from_hi, count(lambda v: v >= t8), st[2])

    tau, c_ge = tmp_ref[1:2, :], tmp_ref[2:3, :]

    over = c_ge > kf

    @pl.when(jnp.max(c_ge) > kf)
    def _():
        tau8 = jnp.broadcast_to(tau, (8, tq))
        tmp_ref[0:1, :] = c_gt0

        @pl.when(jnp.max(jnp.where(over & jnp.logical_not(tie0), 1.0, 0.0)) > 0.0)
        def _():
            tmp_ref[0:1, :] = jnp.where(tie0, c_gt0, count(lambda v: v > tau8))

        quota = kf - tmp_ref[0:1, :]

        def tie_body(kb, seen):
            cnt = jnp.sum(block_count(kb, lambda v: v == tau8), axis=0, keepdims=True)
            inside = over & (seen < quota) & (seen + cnt > quota)
            gone = over & (seen >= quota) & (cnt > 0.0)
            flag = jnp.max(jnp.where(inside, 2.0, jnp.where(gone, 1.0, 0.0)))

            @pl.when(flag > 1.5)
            def _():
                s = sc_ref[kb]
                eq = s == tau
                rank = _dot(lstrict_ref[...], jnp.where(eq, 1.0, 0.0).astype(BF16)) + seen
                sc_ref[kb] = jnp.where(eq & over & (rank >= quota), jnp.nan, s)

            @pl.when(flag == 1.0)
            def _():
                s = sc_ref[kb]
                sc_ref[kb] = jnp.where((s == tau) & gone, jnp.nan, s)

            return seen + cnt

        lax.fori_loop(0, nkb, tie_body, jnp.zeros((1, tq), F32))

    tau8 = jnp.broadcast_to(tau, (8, tq))
    sub = lax.broadcasted_iota(jnp.int32, (8, tq), 0).astype(F32)

    def nearest_body(kb, acc):
        parts = [None] * COUNT_CHAINS
        for r in range(tk // 8):
            idx = sub + (kb * tk + r * 8).astype(F32)
            v = jnp.where(sc_ref[kb, r * 8:(r + 1) * 8, :] >= tau8, idx, -1.0)
            c = r % COUNT_CHAINS
            parts[c] = v if parts[c] is None else jnp.maximum(parts[c], v)
        return jnp.maximum(acc, jnp.maximum(jnp.maximum(parts[0], parts[1]),
                                            jnp.maximum(parts[2], parts[3])))

    last_sel = jnp.max(lax.fori_loop(0, nkb, nearest_body, jnp.full((8, tq), -1.0, F32)),
                       axis=0, keepdims=True)
    d_min = q_pos.astype(F32) - last_sel
    kmax2 = _tile_max(kmax_ref, nkb - 1, KMAX_LANE_CKV)
    for hh in range(DSA_HEADS):
        qh = qabsT_ref[hh * DSA_LATENT:(hh + 1) * DSA_LATENT, :]
        qf = qh.astype(F32)
        shift = _shift_bound(jnp.sum(qf * qf, axis=0, keepdims=True), kmax2, -slopes[hh] * d_min)
        qt_ref[0:DSA_LATENT, hh * tq:(hh + 1) * tq] = qh
        qt_ref[DSA_LATENT:VT_ROWS, hh * tq:(hh + 1) * tq] = _query_feature_rows(
            q_pos, BF16_ROWS, slopes[hh], shift).astype(BF16)
    qt_ref[VT_ROWS:2 * DSA_LATENT, :] = jnp.zeros((2 * DSA_LATENT - VT_ROWS, DSA_HEADS * tq), BF16)
    acc_ref[...] = jnp.zeros_like(acc_ref)

    def fast_pv(kb):
        kx = ckv_ref[pl.ds(pl.multiple_of(kb * tk, tk), tk), :]
        s = _dot(kx, qt_ref[...])
        keep = sc_ref[kb] >= tau
        ps = [jnp.exp(jnp.where(keep, s[:, hh * tq:(hh + 1) * tq], NEG_BIG)).astype(BF16)
              for hh in range(DSA_HEADS)]
        return _dot(ckvT_ref[kb], jnp.concatenate(ps, axis=1))

    def fast_pair(j, carry):
        acc_ref[...] += fast_pv(2 * j) + fast_pv(2 * j + 1)
        return carry

    lax.fori_loop(0, nkb // 2, fast_pair, 0)

    @pl.when(nkb % 2 == 1)
    def _():
        acc_ref[...] += fast_pv(nkb - 1)

    def attn_body(kb, carry):
        kx = ckv_ref[pl.ds(pl.multiple_of(kb * tk, tk), tk), :]
        vT = ckvT_ref[kb]
        keep = sc_ref[kb] >= tau
        s = _dot(kx, qt_ref[...])
        m_all = m_ref[...]
        m_out, alphas, ps = [], [], []
        for hh in range(DSA_HEADS):
            c0 = hh * tq
            sh = jnp.where(keep, s[:, c0:c0 + tq], NEG_BIG)
            m_old = m_all[:, c0:c0 + tq]
            m_new = jnp.maximum(m_old, jnp.max(sh, axis=0, keepdims=True))
            ps.append(jnp.exp(sh - m_new).astype(BF16))
            alphas.append(jnp.exp(m_old - m_new))
            m_out.append(m_new)
        m_ref[...] = jnp.concatenate(m_out, axis=1)
        acc_ref[...] = (jnp.concatenate(alphas, axis=1) * acc_ref[...]
                        + _dot(vT, jnp.concatenate(ps, axis=1)))
        return carry

    denom_ok = jnp.min(acc_ref[ONES_ROW:ONES_ROW + 1, :]) > L_MIN

    @pl.when(jnp.logical_not(denom_ok))
    def _():
        acc_ref[...] = jnp.zeros_like(acc_ref)
        m_ref[...] = jnp.full_like(m_ref, NEG_BIG)
        lax.fori_loop(0, nkb, attn_body, 0)

    acc = acc_ref[...]
    lat = acc[0:DSA_LATENT, :] / acc[ONES_ROW:ONES_ROW + 1, :]
    for p in range(DSA_HEADS // 2):
        pair = lat[:, 2 * p * tq:(2 * p + 2) * tq]
        pair = jnp.concatenate([pair[:, 0:tq], pair[:, tq:2 * tq]], axis=0).astype(BF16)
        o_ref[:, p * LANES:(p + 1) * LANES] = _dot(wuvT_ref[p], pair).T.astype(BF16)


def _dsa_call(qabsT, iqT, iwT, ik4, ckv, ckvT, kmax, lstrict, wuvT_pair, tq, tk, topk):
    B, _, S = qabsT.shape
    nkb = S // tk
    kern = functools.partial(_dsa_kernel, tq=tq, tk=tk, topk=topk)
    blkT = lambda r: pl.BlockSpec((None, r, tq), lambda b, i: (b, 0, i))
    per_b = lambda w: pl.BlockSpec((None, S, w), lambda b, i: (b, 0, 0))
    full = lambda a: pl.BlockSpec(a.shape, lambda b, i: (0,) * a.ndim)
    return pl.pallas_call(
        kern,
        out_shape=jax.ShapeDtypeStruct((B, S, W_DQ), BF16),
        grid=(B, S // tq),
        in_specs=[blkT(W_QABS), blkT(LANES), blkT(8), per_b(LANES), per_b(W_CKV),
                  pl.BlockSpec((None, nkb, VT_ROWS, tk), lambda b, i: (b, 0, 0, 0)),
                  pl.BlockSpec((None, nkb, 8, LANES), lambda b, i: (b, 0, 0, 0)),
                  full(lstrict), full(wuvT_pair)],
        out_specs=pl.BlockSpec((None, tq, W_DQ), lambda b, i: (b, i, 0)),
        scratch_shapes=[pltpu.VMEM((nkb, tk, tq), F32),
                        pltpu.VMEM((LANES, IDX_HEADS * tq), BF16),
                        pltpu.VMEM((2 * DSA_LATENT, DSA_HEADS * tq), BF16),
                        pltpu.VMEM((VT_ROWS, DSA_HEADS * tq), F32),
                        pltpu.VMEM((1, DSA_HEADS * tq), F32),
                        pltpu.VMEM((8, tq), F32)],
        compiler_params=pltpu.CompilerParams(dimension_semantics=("arbitrary", "arbitrary"),
                                             vmem_limit_bytes=VMEM_LIMIT),
        name="dsa",
    )(qabsT, iqT, iwT, ik4, ckv, ckvT, kmax, lstrict, wuvT_pair)


def _diff_kernel(fqT_ref, fk_ref, fvT_ref, kmax_ref, lam_ref, gsub_ref, o_ref, qt_ref, acc_ref, m_ref,
                 *, tq, tk, lam0):
    hd = pl.program_id(1)
    qi = pl.program_id(2)
    q_pos = qi * tq + lax.broadcasted_iota(jnp.int32, (1, tq), 1)
    n_full = (qi * tq) // tk
    n_diag = (tq + tk - 1) // tk
    slopes = _alibi_slopes(DIFF_HEADS)
    slope = jnp.float32(slopes[DIFF_HEADS - 1])
    for i in range(DIFF_HEADS - 1):
        slope = jnp.where(hd == i, slopes[i], slope)

    for m in range(2):
        r0 = m * LANES
        qh = fqT_ref[r0:r0 + DIFF_QK_DIM, :]
        qf = qh.astype(F32)
        shift = _shift_bound(jnp.sum(qf * qf, axis=0, keepdims=True),
                             _tile_max(kmax_ref, n_full + n_diag - 1, 2 * hd + m), 0.0)
        qt_ref[r0:r0 + DIFF_QK_DIM, :] = qh
        qt_ref[r0 + DIFF_QK_DIM:r0 + DIFF_QK_DIM + BF16_ROWS, :] = _query_feature_rows(
            q_pos, BF16_ROWS, slope, shift).astype(BF16)
        qt_ref[r0 + DIFF_QK_DIM + BF16_ROWS:r0 + LANES, :] = jnp.zeros(
            (LANES - DIFF_QK_DIM - BF16_ROWS, tq), BF16)
    acc_ref[...] = jnp.zeros_like(acc_ref)

    def fast_pv(kb, masked):
        kx = fk_ref[pl.ds(pl.multiple_of(kb * tk, tk), tk), :]
        vT = fvT_ref[kb]
        out = []
        for m in range(2):
            s = _dot(kx[:, m * LANES:(m + 1) * LANES], qt_ref[m * LANES:(m + 1) * LANES, :])
            if masked:
                causal = lax.broadcasted_iota(jnp.int32, (tk, tq), 0) + kb * tk <= q_pos
                s = jnp.where(causal, s, NEG_BIG)
            out.append(_dot(vT, jnp.exp(s).astype(BF16)))
        return out

    def accumulate(*pvs):
        for m in range(2):
            acc_ref[m] += functools.reduce(lambda a, b: a + b, [pv[m] for pv in pvs])

    def fast_pair(j, carry):
        accumulate(fast_pv(2 * j, False), fast_pv(2 * j + 1, False))
        return carry

    lax.fori_loop(0, n_full // 2, fast_pair, 0)
    assert n_diag in (1, 2)

    @pl.when(n_full % 2 == 1)
    def _():
        accumulate(fast_pv(n_full - 1, False), fast_pv(n_full, True))
        if n_diag == 2:
            accumulate(fast_pv(n_full + 1, True))

    @pl.when(n_full % 2 == 0)
    def _():
        accumulate(*[fast_pv(n_full + d, True) for d in range(n_diag)])

    def block(kb, masked):
        kx = fk_ref[pl.ds(pl.multiple_of(kb * tk, tk), tk), :]
        vT = fvT_ref[kb]
        qT = qt_ref[...]
        m_all = m_ref[...]
        new = []
        for m in range(2):
            s = _dot(kx[:, m * LANES:(m + 1) * LANES], qT[m * LANES:(m + 1) * LANES, :])
            if masked:
                causal = lax.broadcasted_iota(jnp.int32, (tk, tq), 0) + kb * tk <= q_pos
                s = jnp.where(causal, s, NEG_BIG)
            m_old = m_all[m]
            m_new = jnp.maximum(m_old, jnp.max(s, axis=0, keepdims=True))
            p = jnp.exp(s - m_new).astype(BF16)
            new.append((m_new, jnp.exp(m_old - m_new), _dot(vT, p)))
        for m in range(2):
            m_ref[m] = new[m][0]
            acc_ref[m] = new[m][1] * acc_ref[m] + new[m][2]

    def body(kb, carry):
        block(kb, False)
        return carry

    denom_ok = jnp.min(acc_ref[:, ONES_ROW:ONES_ROW + 1, :]) > L_MIN

    @pl.when(jnp.logical_not(denom_ok))
    def _():
        acc_ref[...] = jnp.zeros_like(acc_ref)
        m_ref[...] = jnp.full_like(m_ref, NEG_BIG)
        lax.fori_loop(0, n_full, body, 0)
        for d in range(n_diag):
            block(n_full + d, True)

    lv = lam_ref[...]
    lam = (jnp.exp(jnp.sum(lv[0:1] * lv[1:2], axis=-1, keepdims=True))
           - jnp.exp(jnp.sum(lv[2:3] * lv[3:4], axis=-1, keepdims=True)) + lam0)
    a1 = acc_ref[0]
    a2 = acc_ref[1]
    o = (a1[0:DIFF_V_DIM, :] / a1[ONES_ROW:ONES_ROW + 1, :]
         - lam * (a2[0:DIFF_V_DIM, :] / a2[ONES_ROW:ONES_ROW + 1, :]))
    o = o * lax.rsqrt(jnp.mean(o * o, axis=0, keepdims=True) + EPS) * gsub_ref[...] * (1.0 - lam0)
    o_ref[...] = o.T.astype(BF16)


def _diff_call(fqT, fk, fvT, kmax, lam_vecs, g_sub_col, tq, tk, lam0):
    B, S, _ = fk.shape
    nkb = S // tk
    kern = functools.partial(_diff_kernel, tq=tq, tk=tk, lam0=lam0)
    full = lambda a: pl.BlockSpec(a.shape, lambda b, h, i: (0,) * a.ndim)
    return pl.pallas_call(
        kern,
        out_shape=jax.ShapeDtypeStruct((B, S, DIFF_HEADS * DIFF_V_DIM), BF16),
        grid=(B, DIFF_HEADS, S // tq),
        in_specs=[pl.BlockSpec((None, 2 * LANES, tq), lambda b, h, i: (b, h, i)),
                  pl.BlockSpec((None, S, 2 * LANES), lambda b, h, i: (b, 0, h)),
                  pl.BlockSpec((None, None, nkb, VT_ROWS, tk), lambda b, h, i: (b, h, 0, 0, 0)),
                  pl.BlockSpec((None, nkb, 8, LANES), lambda b, h, i: (b, 0, 0, 0)),
                  full(lam_vecs), full(g_sub_col)],
        out_specs=pl.BlockSpec((None, tq, DIFF_V_DIM), lambda b, h, i: (b, i, h)),
        scratch_shapes=[pltpu.VMEM((2 * LANES, tq), BF16),
                        pltpu.VMEM((2, VT_ROWS, tq), F32),
                        pltpu.VMEM((2, 1, tq), F32)],
        compiler_params=pltpu.CompilerParams(
            dimension_semantics=("arbitrary", "arbitrary", "arbitrary"),
            vmem_limit_bytes=VMEM_LIMIT),
        name="diff",
    )(fqT, fk, fvT, kmax, lam_vecs, g_sub_col)


def _route_kernel(x_ref, od_ref, of_ref, mod_ref, wout_ref, g_ref, wr_hi_ref, wr_lo_ref, br_ref,
                  x1_ref, h_ref, gates_ref, *, tm):
    attn = (_dot(od_ref[...], wout_ref[0:W_DQ, :]) + _dot(of_ref[...], wout_ref[W_DQ:, :]))
    x1 = x_ref[...] + mod_ref[2:3, :] * attn
    x1_ref[...] = x1
    h = _rms(x1, g_ref[...]) * (1.0 + mod_ref[4:5, :]) + mod_ref[3:4, :]
    h_hi = h.astype(BF16)
    h_ref[...] = h_hi
    h_lo = (h - h_hi.astype(F32)).astype(BF16)
    w_hi = wr_hi_ref[...]
    logits = (_dot(h_hi, w_hi) + _dot(h_lo, w_hi) + _dot(h_hi, wr_lo_ref[...])) + br_ref[...]

    lane = lax.broadcasted_iota(jnp.int32, (tm, LANES), 1)
    big = jnp.int32(4 * LANES)
    neg = -jnp.inf
    is_g = (lane >= N_EXPERTS) & (lane < N_EXPERTS + N_GROUPS)
    gl = jnp.where(is_g, logits, neg)
    gmax = jnp.max(gl, axis=-1, keepdims=True)
    g_lane = jnp.min(jnp.where(gl == gmax, lane, big), axis=-1, keepdims=True)
    g_sel = g_lane - N_EXPERTS
    p_g = 1.0 / jnp.sum(jnp.where(is_g, jnp.exp(gl - gmax), 0.0), axis=-1, keepdims=True)
    in_grp = (lane < N_EXPERTS) & ((lane // EXPERTS_PER_GROUP) == g_sel)
    el = jnp.where(in_grp, logits, neg)
    v1 = jnp.max(el, axis=-1, keepdims=True)
    i1 = jnp.min(jnp.where(el == v1, lane, big), axis=-1, keepdims=True)
    el2 = jnp.where(lane == i1, neg, el)
    v2 = jnp.max(el2, axis=-1, keepdims=True)
    i2 = jnp.min(jnp.where(el2 == v2, lane, big), axis=-1, keepdims=True)
    e = jnp.exp(v2 - v1)
    w1 = 1.0 / (1.0 + e)
    w2 = e * w1
    gates_ref[...] = jnp.where(lane == i1, w1 * p_g, jnp.where(lane == i2, w2 * p_g, 0.0))


def _route_call(x, o_dsa, o_diff, mod3, w_out, g_moe, wr_hi, wr_lo, b_r, tm):
    B, S, D = x.shape
    kern = functools.partial(_route_kernel, tm=tm)
    tok = lambda w: pl.BlockSpec((None, tm, w), lambda b, i: (b, i, 0))
    full = lambda a: pl.BlockSpec(a.shape, lambda b, i: (0,) * a.ndim)
    return pl.pallas_call(
        kern,
        out_shape=[jax.ShapeDtypeStruct((B, S, D), F32), jax.ShapeDtypeStruct((B, S, D), BF16),
                   jax.ShapeDtypeStruct((B, S, LANES), F32)],
        grid=(B, S // tm),
        in_specs=[tok(D), tok(W_DQ), tok(DIFF_HEADS * DIFF_V_DIM),
                  pl.BlockSpec((None, 6, D), lambda b, i: (b, 0, 0)),
                  full(w_out), full(g_moe), full(wr_hi), full(wr_lo), full(b_r)],
        out_specs=[tok(D), tok(D), tok(LANES)],
        compiler_params=pltpu.CompilerParams(dimension_semantics=("arbitrary", "arbitrary"),
                                             vmem_limit_bytes=VMEM_LIMIT),
        name="route",
    )(x, o_dsa, o_diff, mod3, w_out, g_moe, wr_hi, wr_lo, b_r)


def _moe_kernel(h_ref, gates_ref, x1_ref, mod_ref, wg_ref, wu_ref, wd_ref, gf_ref, o_ref, acc_ref,
                *, tm, ec):
    j = pl.program_id(2)
    nc = pl.num_programs(2)

    @pl.when(j == 0)
    def _():
        acc_ref[...] = jnp.zeros_like(acc_ref)

    h = h_ref[...]
    hg = _dot(h, wg_ref[...])
    hu = _dot(h, wu_ref[...])
    hid = hg * jax.nn.sigmoid(hg) * hu
    gates = gates_ref[...]
    lane = lax.broadcasted_iota(jnp.int32, (tm, LANES), 1)
    parts = []
    for e in range(ec):
        gcol = jnp.sum(jnp.where(lane == j * ec + e, gates, 0.0), axis=-1, keepdims=True)
        parts.append((hid[:, e * D_EXPERT:(e + 1) * D_EXPERT] * gcol).astype(BF16))
    hs = jnp.concatenate(parts, axis=1)
    acc_ref[...] += _dot(hs, wd_ref[...])

    @pl.when(j == nc - 1)
    def _():
        x2 = x1_ref[...] + mod_ref[5:6, :] * acc_ref[...]
        o_ref[...] = _rms(x2, gf_ref[...])


def _moe_call(h, gates, x1, mod3, wg, wu, wd, g_final, tm, ec):
    B, S, D = x1.shape
    kern = functools.partial(_moe_kernel, tm=tm, ec=ec)
    tok = lambda w: pl.BlockSpec((None, tm, w), lambda b, i, j: (b, i, 0))
    return pl.pallas_call(
        kern,
        out_shape=jax.ShapeDtypeStruct((B, S, D), F32),
        grid=(B, S // tm, N_EXPERTS // ec),
        in_specs=[tok(D), tok(LANES), tok(D),
                  pl.BlockSpec((None, 6, D), lambda b, i, j: (b, 0, 0)),
                  pl.BlockSpec((D, ec * D_EXPERT), lambda b, i, j: (0, j)),
                  pl.BlockSpec((D, ec * D_EXPERT), lambda b, i, j: (0, j)),
                  pl.BlockSpec((ec * D_EXPERT, D), lambda b, i, j: (j, 0)),
                  pl.BlockSpec((1, D), lambda b, i, j: (0, 0))],
        out_specs=tok(D),
        scratch_shapes=[pltpu.VMEM((tm, D), F32)],
        compiler_params=pltpu.CompilerParams(
            dimension_semantics=("arbitrary", "arbitrary", "arbitrary"),
            vmem_limit_bytes=VMEM_LIMIT),
        name="moe",
    )(h, gates, x1, mod3, wg, wu, wd, g_final)


def _tile(n, pref):
    t = min(n, pref)
    assert n % t == 0, (n, t)
    return t


def kernel(x, c, w_ada, b_ada, g_attn, w_in, g_kv, w_uk, w_uv, lam_q1, lam_k1, lam_q2, lam_k2,
           g_sub, w_out, g_moe, w_group, b_group, w_router, b_router, w_gate, w_up, w_down, g_final):
    B, S, D = x.shape
    assert D == D_MODEL and w_ada.shape[0] == 1
    assert S % KEY_TILE == 0 and S <= POS_SPLIT * 256
    topk = min(TOPK_MAX, S // 4)
    l = 0
    lam0 = 0.8 - 0.6 * math.exp(-0.3 * l)

    mod3 = _mod_call(c, w_ada[l], b_ada[l].reshape(1, -1)).reshape(B, 6, D)

    wn, wt, w_pair = _proj_weights(w_in[l], w_uk[l])
    qabsT, iqT, iwT, ik4, ckv, ckvT, fqT, fk, fvT, kmax = _proj_call(
        x, mod3, g_attn[l].reshape(1, D), wn, wt, w_pair, g_kv[l].reshape(1, -1), KEY_TILE)

    lstrict = jnp.asarray(np.tril(np.ones((KEY_TILE, KEY_TILE), np.float32), -1), BF16)
    uvT = jnp.swapaxes(w_uv[l], 1, 2)
    z = jnp.zeros_like(uvT[0])
    wuvT_pair = jnp.stack([
        jnp.concatenate([jnp.concatenate([uvT[2 * p], z], axis=1),
                         jnp.concatenate([z, uvT[2 * p + 1]], axis=1)], axis=0)
        for p in range(DSA_HEADS // 2)]).astype(BF16)
    o_dsa = _dsa_call(qabsT, iqT, iwT, ik4, ckv, ckvT, kmax, lstrict, wuvT_pair, DSA_QUERY_TILE, KEY_TILE,
                      topk)

    lam_vecs = jnp.concatenate([lam_q1[l][None], lam_k1[l][None], lam_q2[l][None], lam_k2[l][None]],
                               axis=0).astype(F32)
    o_diff = _diff_call(fqT, fk, fvT, kmax, lam_vecs, g_sub[l].reshape(-1, 1),
                        _tile(S, DIFF_QUERY_TILE), KEY_TILE, lam0)

    wr = jnp.pad(jnp.concatenate([w_router[l], w_group[l]], axis=1),
                 ((0, 0), (0, LANES - N_EXPERTS - N_GROUPS)))
    wr_hi = wr.astype(BF16)
    wr_lo = (wr - wr_hi.astype(F32)).astype(BF16)
    b_r = jnp.pad(jnp.concatenate([b_router[l], b_group[l]]), (0, LANES - N_EXPERTS - N_GROUPS))
    x1, h2, gates = _route_call(x, o_dsa, o_diff, mod3, w_out[l].astype(BF16),
                                g_moe[l].reshape(1, D), wr_hi, wr_lo, b_r.reshape(1, LANES),
                                _tile(S, TOKEN_TILE))

    wg = jnp.swapaxes(w_gate[l], 0, 1).reshape(D, N_EXPERTS * D_EXPERT).astype(BF16)
    wu = jnp.swapaxes(w_up[l], 0, 1).reshape(D, N_EXPERTS * D_EXPERT).astype(BF16)
    wd = w_down[l].reshape(N_EXPERTS * D_EXPERT, D).astype(BF16)
    return _moe_call(h2, gates, x1, mod3, wg, wu, wd, g_final.reshape(1, D), _tile(S, MOE_TOKEN_TILE),
                     MOE_EXPERTS_PER_STEP)
```

```python
import functools
import math

import jax
import jax.numpy as jnp
import numpy as np
from jax import lax
from jax.experimental import pallas as pl
from jax.experimental.pallas import tpu as pltpu

F32 = jnp.float32
BF16 = jnp.bfloat16

D_MODEL = 1024
DSA_HEADS = 8
DSA_HEAD_DIM = 64
DSA_LATENT = 128
IDX_HEADS = 4
IDX_DIM = 32
TOPK_MAX = 256
DIFF_HEADS = 4
DIFF_QK_DIM = 64
DIFF_V_DIM = 128
N_GROUPS = 4
EXPERTS_PER_GROUP = 8
N_EXPERTS = 32
D_EXPERT = 256
EPS = 1e-6

LANES = 128
BF16_ROWS = 16
POS_SPLIT = 64
NEG_BIG = -1e30
VMEM_LIMIT = 56 * 1024 * 1024
KEY_TILE = 512
VALUE_BISECT_STEPS = 24
UNCHECKED_BISECT_STEPS = 16
COUNT_CHAINS = 4

W_DQ = DSA_HEADS * DSA_HEAD_DIM
W_QABS = DSA_HEADS * DSA_LATENT
W_CKV = 2 * LANES
VT_ROWS = DSA_LATENT + BF16_ROWS
ONES_ROW = DSA_LATENT + 2
N_FEATS = 5
KMAX_LANE_CKV = 8
L_MIN = 1e-30
BF16_ROUND_UP = 1.0 + 2.0 ** -7
BOUND_SLACK = 2.0 ** -6
CODE_BISECT_STEPS = 34
TOKEN_TILE = 512
DSA_QUERY_TILE = 256
DIFF_QUERY_TILE = 1024
MOE_EXPERTS_PER_STEP = 8
MOE_TOKEN_TILE = 512
W_FQ = DIFF_HEADS * 2 * LANES
W_DIFF_QK = DIFF_HEADS * 2 * DIFF_QK_DIM

T_DQ = 0
T_DLAT = T_DQ + W_DQ
T_IQ = T_DLAT + DSA_LATENT
T_IW = T_IQ + IDX_HEADS * IDX_DIM
T_FQ = T_IW + BF16_ROWS
T_FV = T_FQ + W_DIFF_QK
T_ROWS = T_FV + DIFF_HEADS * DIFF_V_DIM
N_IK = 0
N_DLAT = N_IK + LANES
N_FK = N_DLAT + DSA_LATENT
N_COLS = N_FK + W_FQ


def _alibi_slopes(n):
    return [2.0 ** (-8.0 * (i + 1) / n) for i in range(n)]


def _rms(x, g):
    return x * lax.rsqrt(jnp.mean(x * x, axis=-1, keepdims=True) + EPS) * g


def _dot(a, b):
    return jnp.dot(a, b, preferred_element_type=F32)


def _mod_kernel(c_ref, w_ref, b_ref, o_ref):
    c = c_ref[...]
    act = c * jax.nn.sigmoid(c)
    o_ref[...] = jnp.dot(act, w_ref[...], preferred_element_type=F32,
                         precision=lax.Precision.HIGHEST) + b_ref[...]


def _mod_call(c, w_ada, b_ada):
    B, D = c.shape
    n = w_ada.shape[1] // D
    return pl.pallas_call(
        _mod_kernel,
        out_shape=jax.ShapeDtypeStruct((B, n * D), F32),
        grid=(n,),
        in_specs=[pl.BlockSpec((B, D), lambda j: (0, 0)),
                  pl.BlockSpec((D, D), lambda j: (0, j)),
                  pl.BlockSpec((1, D), lambda j: (0, j))],
        out_specs=pl.BlockSpec((B, D), lambda j: (0, j)),
        compiler_params=pltpu.CompilerParams(dimension_semantics=("arbitrary",),
                                             vmem_limit_bytes=VMEM_LIMIT),
        name="mod",
    )(c, w_ada, b_ada)


def _key_feature_rows(pos_row, n_rows):
    r = lax.broadcasted_iota(jnp.int32, (n_rows, pos_row.shape[1]), 0)
    pa = (pos_row // POS_SPLIT).astype(F32)
    pb = (pos_row % POS_SPLIT).astype(F32)
    return jnp.where(r == 0, pa, jnp.where(r == 1, pb, jnp.where(r < N_FEATS, 1.0, 0.0)))


def _query_feature_rows(pos_row, n_rows, slope, shift):
    r = lax.broadcasted_iota(jnp.int32, (n_rows, pos_row.shape[1]), 0)
    pa = (pos_row // POS_SPLIT).astype(F32)
    pb = (pos_row % POS_SPLIT).astype(F32)
    return jnp.where(r == 0, POS_SPLIT * slope,
                     jnp.where(r == 1, slope,
                               jnp.where(r == 2, -POS_SPLIT * slope * pa,
                                         jnp.where(r == 3, -slope * pb,
                                                   jnp.where(r == 4, -shift, 0.0)))))


def _shift_bound(qn2, kmax2, extra):
    b = jnp.sqrt(qn2 * kmax2) + extra
    return b + jnp.abs(b) * BOUND_SLACK + BOUND_SLACK


def _tile_max(kmax_ref, last_tile, lane_idx):
    x = kmax_ref[...]
    t = lax.broadcasted_iota(jnp.int32, x.shape, 0)
    ln = lax.broadcasted_iota(jnp.int32, x.shape, 2)
    x = jnp.where((t <= last_tile) & (ln == lane_idx), x, 0.0)
    return jnp.max(jnp.max(x, axis=0), axis=1, keepdims=True)[0:1, :]


def _proj_kernel(x_ref, mod_ref, g_ref, wn_ref, wt_ref, wpair_ref, gkv_ref, gkvc_ref, fc_ref, sel_ref,
                 qabsT_ref, iqT_ref, iwT_ref, ik_ref, ckv_ref, ckvT_ref, fqT_ref, fk_ref, fvT_ref,
                 kmax_ref, *, tm):
    x = x_ref[...]
    h = _rms(x, g_ref[...]) * (1.0 + mod_ref[1:2, :]) + mod_ref[0:1, :]
    hb = h.astype(BF16)
    hT = h.T.astype(BF16)

    base = pl.program_id(1) * tm
    pos_c = base + lax.broadcasted_iota(jnp.int32, (tm, 1), 0)
    pos_r = base + lax.broadcasted_iota(jnp.int32, (1, tm), 1)
    pa = (pos_c // POS_SPLIT).astype(F32)
    pb = (pos_c % POS_SPLIT).astype(F32)

    def feats(row, width):
        return (fc_ref[row:row + 1, 0:width] + fc_ref[row + 1:row + 2, 0:width] * pa
                + fc_ref[row + 2:row + 3, 0:width] * pb)

    ik_ref[...] = _dot(hb, wn_ref[:, N_IK:N_IK + LANES]).astype(BF16)
    dlat = _dot(hb, wn_ref[:, N_DLAT:N_DLAT + DSA_LATENT])
    ckv_b = _rms(dlat, gkv_ref[...]).astype(BF16)
    ckv_ref[:, 0:LANES] = ckv_b
    ckv_ref[:, LANES:2 * LANES] = feats(0, LANES).astype(BF16)
    fk = _dot(hb, wn_ref[:, N_FK:N_FK + W_FQ])
    fk_ref[...] = (fk + feats(3, W_FQ)).astype(BF16)
    sq = jnp.concatenate([fk.astype(BF16).astype(F32), ckv_b.astype(F32)], axis=1)
    sq_up = (sq * sq * BF16_ROUND_UP).astype(BF16)
    kmax_ref[...] = jnp.broadcast_to(jnp.max(_dot(sq_up, sel_ref[...]), axis=0, keepdims=True),
                                     (8, LANES))

    dqT = _dot(wt_ref[T_DQ:T_DQ + W_DQ, :], hT).astype(BF16)
    for p in range(DSA_HEADS // 2):
        qa = _dot(wpair_ref[p], dqT[p * LANES:(p + 1) * LANES, :])
        qabsT_ref[p * 2 * LANES:(p + 1) * 2 * LANES, :] = qa.astype(BF16)
    dlatT = _dot(wt_ref[T_DLAT:T_DLAT + DSA_LATENT, :], hT)
    inv = lax.rsqrt(jnp.mean(dlatT * dlatT, axis=0, keepdims=True) + EPS)
    kfeat = _key_feature_rows(pos_r, BF16_ROWS).astype(BF16)
    ckvT_ref[0:DSA_LATENT, :] = (dlatT * inv * gkvc_ref[...]).astype(BF16)
    ckvT_ref[DSA_LATENT:VT_ROWS, :] = kfeat
    iqT_ref[...] = _dot(wt_ref[T_IQ:T_IQ + IDX_HEADS * IDX_DIM, :], hT).astype(BF16)
    iwT = _dot(wt_ref[T_IW:T_IW + BF16_ROWS, :], hT)
    iwT_ref[...] = iwT[0:8, :] * (IDX_HEADS ** -0.5 * IDX_DIM ** -0.5)
    fqT = _dot(wt_ref[T_FQ:T_FQ + W_DIFF_QK, :], hT)
    zpad = jnp.zeros((LANES - DIFF_QK_DIM, tm), BF16)
    for s in range(DIFF_HEADS * 2):
        r0 = s * LANES
        fqT_ref[r0:r0 + DIFF_QK_DIM, :] = fqT[s * DIFF_QK_DIM:(s + 1) * DIFF_QK_DIM, :].astype(BF16)
        fqT_ref[r0 + DIFF_QK_DIM:r0 + LANES, :] = zpad
    fvT = _dot(wt_ref[T_FV:T_FV + DIFF_HEADS * DIFF_V_DIM, :], hT)
    for hh in range(DIFF_HEADS):
        fvT_ref[hh, 0:DIFF_V_DIM, :] = fvT[hh * DIFF_V_DIM:(hh + 1) * DIFF_V_DIM, :].astype(BF16)
        fvT_ref[hh, DIFF_V_DIM:VT_ROWS, :] = kfeat


def _feature_consts():
    fc = np.zeros((8, W_FQ), np.float32)
    fc[1, 0] = 1.0
    fc[2, 1] = 1.0
    fc[0, 2:N_FEATS] = 1.0
    for s in range(DIFF_HEADS * 2):
        base = s * LANES + DIFF_QK_DIM
        fc[4, base + 0] = 1.0
        fc[5, base + 1] = 1.0
        fc[3, base + 2:base + N_FEATS] = 1.0
    return jnp.asarray(fc)


def _norm_selector():
    sel = np.zeros((W_FQ + DSA_LATENT, LANES), np.float32)
    for s in range(DIFF_HEADS * 2):
        sel[s * LANES:s * LANES + DIFF_QK_DIM, s] = 1.0
    sel[W_FQ:, KMAX_LANE_CKV] = 1.0
    return jnp.asarray(sel, BF16)


def _proj_weights(w_in, w_uk):
    D = w_in.shape[0]
    pts = np.cumsum([W_DQ, DSA_LATENT, IDX_HEADS * IDX_DIM, IDX_DIM, IDX_HEADS,
                     W_DIFF_QK, W_DIFF_QK])
    dq, dlat, iq, ik, iw, fq, fk, fv = jnp.split(w_in, list(pts), axis=1)
    ik4 = jnp.tile(ik, (1, IDX_HEADS))
    fke = jnp.pad(fk.reshape(D, DIFF_HEADS * 2, DIFF_QK_DIM),
                  ((0, 0), (0, 0), (0, LANES - DIFF_QK_DIM))).reshape(D, W_FQ)
    wn = jnp.concatenate([ik4, dlat, fke], axis=1).astype(BF16)
    iwp = jnp.pad(iw, ((0, 0), (0, BF16_ROWS - IDX_HEADS)))
    wt = jnp.concatenate([dq, dlat, iq, iwp, fq * (DIFF_QK_DIM ** -0.5), fv], axis=1).T.astype(BF16)
    uk = w_uk * (DSA_HEAD_DIM ** -0.5)
    z = jnp.zeros_like(uk[0])
    pairs = [jnp.concatenate([jnp.concatenate([uk[2 * p], z], axis=1),
                              jnp.concatenate([z, uk[2 * p + 1]], axis=1)], axis=0)
             for p in range(DSA_HEADS // 2)]
    return wn, wt, jnp.stack(pairs).astype(BF16)


def _proj_call(x, mod3, g_attn, wn, wt, w_pair, g_kv, tm):
    B, S, D = x.shape
    nt = S // tm
    fc = _feature_consts()
    sel = _norm_selector()
    kern = functools.partial(_proj_kernel, tm=tm)
    tok = lambda w: pl.BlockSpec((None, tm, w), lambda b, i: (b, i, 0))
    tokT = lambda r: pl.BlockSpec((None, r, tm), lambda b, i: (b, 0, i))
    full = lambda a: pl.BlockSpec(a.shape, lambda b, i: (0,) * a.ndim)
    g_kv_col = g_kv.reshape(-1, 1)
    out_shape = [jax.ShapeDtypeStruct((B, W_QABS, S), BF16),
                 jax.ShapeDtypeStruct((B, LANES, S), BF16),
                 jax.ShapeDtypeStruct((B, 8, S), F32),
                 jax.ShapeDtypeStruct((B, S, LANES), BF16),
                 jax.ShapeDtypeStruct((B, S, W_CKV), BF16),
                 jax.ShapeDtypeStruct((B, nt, VT_ROWS, tm), BF16),
                 jax.ShapeDtypeStruct((B, W_FQ, S), BF16),
                 jax.ShapeDtypeStruct((B, S, W_FQ), BF16),
                 jax.ShapeDtypeStruct((B, DIFF_HEADS, nt, VT_ROWS, tm), BF16),
                 jax.ShapeDtypeStruct((B, nt, 8, LANES), F32)]
    out_specs = [tokT(W_QABS), tokT(LANES), tokT(8), tok(LANES), tok(W_CKV),
                 pl.BlockSpec((None, None, VT_ROWS, tm), lambda b, i: (b, i, 0, 0)),
                 tokT(W_FQ), tok(W_FQ),
                 pl.BlockSpec((None, DIFF_HEADS, None, VT_ROWS, tm), lambda b, i: (b, 0, i, 0, 0)),
                 pl.BlockSpec((None, None, 8, LANES), lambda b, i: (b, i, 0, 0))]
    return pl.pallas_call(
        kern,
        out_shape=out_shape,
        grid=(B, nt),
        in_specs=[tok(D),
                  pl.BlockSpec((None, 6, D), lambda b, i: (b, 0, 0)),
                  full(g_attn), full(wn), full(wt), full(w_pair), full(g_kv), full(g_kv_col),
                  full(fc), full(sel)],
        out_specs=out_specs,
        compiler_params=pltpu.CompilerParams(dimension_semantics=("arbitrary", "arbitrary"),
                                             vmem_limit_bytes=VMEM_LIMIT),
        name="proj",
    )(x, mod3, g_attn, wn, wt, w_pair, g_kv, g_kv_col, fc, sel)


def _float_code(x):
    b = lax.bitcast_convert_type(x, jnp.int32)
    return b ^ (lax.shift_right_arithmetic(b, 31) & jnp.int32(0x7FFFFFFF))


def _float_decode(c):
    b = c ^ (lax.shift_right_arithmetic(c, 31) & jnp.int32(0x7FFFFFFF))
    return lax.bitcast_convert_type(b, F32)


def _dsa_kernel(qabsT_ref, iqT_ref, iwT_ref, ik_ref, ckv_ref, ckvT_ref, kmax_ref, lstrict_ref, wuvT_ref,
                o_ref, sc_ref, qst_ref, qt_ref, acc_ref, m_ref, tmp_ref, *, tq, tk, topk):
    qi = pl.program_id(1)
    nkb = (qi * tq) // tk + 1
    kf = float(topk)
    slopes = _alibi_slopes(DSA_HEADS)
    q_pos = qi * tq + lax.broadcasted_iota(jnp.int32, (1, tq), 1)

    iqT = iqT_ref[...]
    rowi = lax.broadcasted_iota(jnp.int32, (LANES, tq), 0)
    for hh in range(IDX_HEADS):
        qst_ref[:, hh * tq:(hh + 1) * tq] = jnp.where((rowi // IDX_DIM) == hh, iqT, jnp.zeros_like(iqT))
    iw = iwT_ref[...]
    wrow = [iw[hh:hh + 1, :] for hh in range(IDX_HEADS)]

    def scores(kb):
        kblk = ik_ref[pl.ds(pl.multiple_of(kb * tk, tk), tk), :]
        a = _dot(kblk, qst_ref[...])
        sc = jnp.maximum(a[:, 0:tq], 0.0) * wrow[0]
        for hh in range(1, IDX_HEADS):
            sc = sc + jnp.maximum(a[:, hh * tq:(hh + 1) * tq], 0.0) * wrow[hh]
        return sc

    def score_body(kb, carry):
        mn, mx = carry
        sc = scores(kb)
        sc_ref[kb] = sc
        return (jnp.minimum(mn, jnp.min(sc, axis=0, keepdims=True)),
                jnp.maximum(mx, jnp.max(sc, axis=0, keepdims=True)))

    last = nkb - 1
    mn, mx = lax.fori_loop(0, last // 2, lambda j, c: score_body(2 * j + 1, score_body(2 * j, c)),
                           (jnp.full((1, tq), jnp.inf, F32), jnp.full((1, tq), -jnp.inf, F32)))
    mn, mx = lax.cond(last % 2 == 1, lambda c: score_body(last - 1, c), lambda c: c, (mn, mx))
    sc = scores(last)
    causal = lax.broadcasted_iota(jnp.int32, (tk, tq), 0) + last * tk <= q_pos
    sc_ref[last] = jnp.where(causal, sc, jnp.nan)
    mn = jnp.minimum(mn, jnp.min(jnp.where(causal, sc, jnp.inf), axis=0, keepdims=True))
    mx = jnp.maximum(mx, jnp.max(jnp.where(causal, sc, -jnp.inf), axis=0, keepdims=True))

    def block_count(kb, pred):
        parts = [None] * COUNT_CHAINS
        for r in range(tk // 8):
            v = jnp.where(pred(sc_ref[kb, r * 8:(r + 1) * 8, :]), 1.0, 0.0)
            c = r % COUNT_CHAINS
            parts[c] = v if parts[c] is None else parts[c] + v
        return (parts[0] + parts[1]) + (parts[2] + parts[3])

    def count(pred):
        acc = lax.fori_loop(0, nkb, lambda kb, acc: acc + block_count(kb, pred),
                            jnp.zeros((8, tq), F32))
        return jnp.sum(acc, axis=0, keepdims=True)

    n_causal = (q_pos + 1).astype(F32)
    done0 = n_causal <= kf

    def bisect(st, value_mid):
        lo, hi, c_lo, c_hi, theta, done = st
        th8 = jnp.broadcast_to(theta, (8, tq))
        c = count(lambda v: v >= th8)
        ge = c >= kf
        live = done == 0.0
        lo = jnp.where(live & ge, theta, lo)
        c_lo = jnp.where(live & ge, c, c_lo)
        hi = jnp.where(live & jnp.logical_not(ge), theta, hi)
        c_hi = jnp.where(live & jnp.logical_not(ge), c, c_hi)
        if value_mid:
            nxt = 0.5 * lo + 0.5 * hi
        else:
            cl, ch = _float_code(lo), _float_code(hi)
            nxt = _float_decode((cl & ch) + lax.shift_right_arithmetic(cl ^ ch, 1))
        inside = (nxt > lo) & (nxt < hi)
        done = jnp.where((c_lo == kf) | jnp.logical_not(inside), 1.0, done)
        return lo, hi, c_lo, c_hi, nxt, done

    def unresolved(st):
        return jnp.sum(jnp.where((st[5] == 0.0) & (st[3] != kf - 1.0), 1.0, 0.0))

    def checked(value_mid, max_steps):
        def cond(c):
            return jnp.logical_and(c[1] > 0.0, c[2] < max_steps)

        def body(c):
            st = bisect(bisect(c[0], value_mid), value_mid)
            return st, unresolved(st), c[2] + 2
        return cond, body

    zero8 = jnp.zeros((8, tq), F32)
    c_ge0 = count(lambda v: v >= zero8)
    c_gt0 = count(lambda v: v > zero8)
    live0 = jnp.logical_not(done0)
    at0 = live0 & (c_ge0 >= kf)
    tie0 = at0 & (c_gt0 < kf)
    below0 = live0 & (c_ge0 < kf)
    lo0 = jnp.where(at0, 0.0, mn)
    hi0 = jnp.where(below0, 0.0, mx)
    st = (lo0, hi0, jnp.where(at0, c_ge0, n_causal), jnp.where(below0, c_ge0, -1.0),
          jnp.where(below0, 0.5 * lo0 + 0.5 * hi0, mx), jnp.where(done0 | tie0, 1.0, 0.0))
    st = lax.fori_loop(0, UNCHECKED_BISECT_STEPS, lambda i, s: bisect(s, True), st)
    cond, body = checked(True, VALUE_BISECT_STEPS - UNCHECKED_BISECT_STEPS)
    st, active, _ = lax.while_loop(cond, body, (st, unresolved(st), jnp.int32(0)))
    cond, body = checked(False, CODE_BISECT_STEPS)
    st, _, _ = lax.while_loop(cond, body, (st, active, jnp.int32(0)))
    tmp_ref[1:2, :] = st[0]
    tmp_ref[2:3, :] = st[2]
    from_hi = (st[5] == 0.0) & (st[3] == kf - 1.0)

    @pl.when(jnp.max(jnp.where(from_hi, 1.0, 0.0)) > 0.0)
    def _():
        hi8 = jnp.broadcast_to(st[1], (8, tq))

        def below_hi(kb, acc):
            parts = [None] * COUNT_CHAINS
            for r in range(tk // 8):
                v = sc_ref[kb, r * 8:(r + 1) * 8, :]
                v = jnp.where(v < hi8, v, -jnp.inf)
                c = r % COUNT_CHAINS
                parts[c] = v if parts[c] is None else jnp.maximum(parts[c], v)
            return jnp.maximum(acc, jnp.maximum(jnp.maximum(parts[0], parts[1]),
                                                jnp.maximum(parts[2], parts[3])))

        nxt = jnp.max(lax.fori_loop(0, nkb, below_hi, jnp.full((8, tq), -jnp.inf, F32)),
                      axis=0, keepdims=True)
        t_new = jnp.where(from_hi, nxt, st[0])
        t8 = jnp.broadcast_to(t_new, (8, tq))
        tmp_ref[1:2, :] = t_new
        tmp_ref[2:3, :] = jnp.where(from_hi, count(lambda v: v >= t8), st[2])

    tau, c_ge = tmp_ref[1:2, :], tmp_ref[2:3, :]

    over = c_ge > kf

    @pl.when(jnp.max(c_ge) > kf)
    def _():
        tau8 = jnp.broadcast_to(tau, (8, tq))
        tmp_ref[0:1, :] = c_gt0

        @pl.when(jnp.max(jnp.where(over & jnp.logical_not(tie0), 1.0, 0.0)) > 0.0)
        def _():
            tmp_ref[0:1, :] = jnp.where(tie0, c_gt0, count(lambda v: v > tau8))

        quota = kf - tmp_ref[0:1, :]

        def tie_body(kb, seen):
            cnt = jnp.sum(block_count(kb, lambda v: v == tau8), axis=0, keepdims=True)
            inside = over & (seen < quota) & (seen + cnt > quota)
            gone = over & (seen >= quota) & (cnt > 0.0)
            flag = jnp.max(jnp.where(inside, 2.0, jnp.where(gone, 1.0, 0.0)))

            @pl.when(flag > 1.5)
            def _():
                s = sc_ref[kb]
                eq = s == tau
                rank = _dot(lstrict_ref[...], jnp.where(eq, 1.0, 0.0).astype(BF16)) + seen
                sc_ref[kb] = jnp.where(eq & over & (rank >= quota), jnp.nan, s)

            @pl.when(flag == 1.0)
            def _():
                s = sc_ref[kb]
                sc_ref[kb] = jnp.where((s == tau) & gone, jnp.nan, s)

            return seen + cnt

        lax.fori_loop(0, nkb, tie_body, jnp.zeros((1, tq), F32))

    tau8 = jnp.broadcast_to(tau, (8, tq))
    sub = lax.broadcasted_iota(jnp.int32, (8, tq), 0).astype(F32)

    def nearest_body(kb, acc):
        parts = [None] * COUNT_CHAINS
        for r in range(tk // 8):
            idx = sub + (kb * tk + r * 8).astype(F32)
            v = jnp.where(sc_ref[kb, r * 8:(r + 1) * 8, :] >= tau8, idx, -1.0)
            c = r % COUNT_CHAINS
            parts[c] = v if parts[c] is None else jnp.maximum(parts[c], v)
        return jnp.maximum(acc, jnp.maximum(jnp.maximum(parts[0], parts[1]),
                                            jnp.maximum(parts[2], parts[3])))

    last_sel = jnp.max(lax.fori_loop(0, nkb, nearest_body, jnp.full((8, tq), -1.0, F32)),
                       axis=0, keepdims=True)
    d_min = q_pos.astype(F32) - last_sel
    kmax2 = _tile_max(kmax_ref, nkb - 1, KMAX_LANE_CKV)
    for hh in range(DSA_HEADS):
        qh = qabsT_ref[hh * DSA_LATENT:(hh + 1) * DSA_LATENT, :]
        qf = qh.astype(F32)
        shift = _shift_bound(jnp.sum(qf * qf, axis=0, keepdims=True), kmax2, -slopes[hh] * d_min)
        qt_ref[0:DSA_LATENT, hh * tq:(hh + 1) * tq] = qh
        qt_ref[DSA_LATENT:VT_ROWS, hh * tq:(hh + 1) * tq] = _query_feature_rows(
            q_pos, BF16_ROWS, slopes[hh], shift).astype(BF16)
    qt_ref[VT_ROWS:2 * DSA_LATENT, :] = jnp.zeros((2 * DSA_LATENT - VT_ROWS, DSA_HEADS * tq), BF16)
    acc_ref[...] = jnp.zeros_like(acc_ref)

    def fast_pv(kb):
        kx = ckv_ref[pl.ds(pl.multiple_of(kb * tk, tk), tk), :]
        s = _dot(kx, qt_ref[...])
        keep = sc_ref[kb] >= tau
        ps = [jnp.exp(jnp.where(keep, s[:, hh * tq:(hh + 1) * tq], NEG_BIG)).astype(BF16)
              for hh in range(DSA_HEADS)]
        return _dot(ckvT_ref[kb], jnp.concatenate(ps, axis=1))

    def fast_pair(j, carry):
        acc_ref[...] += fast_pv(2 * j) + fast_pv(2 * j + 1)
        return carry

    lax.fori_loop(0, nkb // 2, fast_pair, 0)

    @pl.when(nkb % 2 == 1)
    def _():
        acc_ref[...] += fast_pv(nkb - 1)

    def attn_body(kb, carry):
        kx = ckv_ref[pl.ds(pl.multiple_of(kb * tk, tk), tk), :]
        vT = ckvT_ref[kb]
        keep = sc_ref[kb] >= tau
        s = _dot(kx, qt_ref[...])
        m_all = m_ref[...]
        m_out, alphas, ps = [], [], []
        for hh in range(DSA_HEADS):
            c0 = hh * tq
            sh = jnp.where(keep, s[:, c0:c0 + tq], NEG_BIG)
            m_old = m_all[:, c0:c0 + tq]
            m_new = jnp.maximum(m_old, jnp.max(sh, axis=0, keepdims=True))
            ps.append(jnp.exp(sh - m_new).astype(BF16))
            alphas.append(jnp.exp(m_old - m_new))
            m_out.append(m_new)
        m_ref[...] = jnp.concatenate(m_out, axis=1)
        acc_ref[...] = (jnp.concatenate(alphas, axis=1) * acc_ref[...]
                        + _dot(vT, jnp.concatenate(ps, axis=1)))
        return carry

    denom_ok = jnp.min(acc_ref[ONES_ROW:ONES_ROW + 1, :]) > L_MIN

    @pl.when(jnp.logical_not(denom_ok))
    def _():
        acc_ref[...] = jnp.zeros_like(acc_ref)
        m_ref[...] = jnp.full_like(m_ref, NEG_BIG)
        lax.fori_loop(0, nkb, attn_body, 0)

    acc = acc_ref[...]
    lat = acc[0:DSA_LATENT, :] / acc[ONES_ROW:ONES_ROW + 1, :]
    for p in range(DSA_HEADS // 2):
        pair = lat[:, 2 * p * tq:(2 * p + 2) * tq]
        pair = jnp.concatenate([pair[:, 0:tq], pair[:, tq:2 * tq]], axis=0).astype(BF16)
        o_ref[:, p * LANES:(p + 1) * LANES] = _dot(wuvT_ref[p], pair).T.astype(BF16)


def _dsa_call(qabsT, iqT, iwT, ik4, ckv, ckvT, kmax, lstrict, wuvT_pair, tq, tk, topk):
    B, _, S = qabsT.shape
    nkb = S // tk
    kern = functools.partial(_dsa_kernel, tq=tq, tk=tk, topk=topk)
    blkT = lambda r: pl.BlockSpec((None, r, tq), lambda b, i: (b, 0, i))
    per_b = lambda w: pl.BlockSpec((None, S, w), lambda b, i: (b, 0, 0))
    full = lambda a: pl.BlockSpec(a.shape, lambda b, i: (0,) * a.ndim)
    return pl.pallas_call(
        kern,
        out_shape=jax.ShapeDtypeStruct((B, S, W_DQ), BF16),
        grid=(B, S // tq),
        in_specs=[blkT(W_QABS), blkT(LANES), blkT(8), per_b(LANES), per_b(W_CKV),
                  pl.BlockSpec((None, nkb, VT_ROWS, tk), lambda b, i: (b, 0, 0, 0)),
                  pl.BlockSpec((None, nkb, 8, LANES), lambda b, i: (b, 0, 0, 0)),
                  full(lstrict), full(wuvT_pair)],
        out_specs=pl.BlockSpec((None, tq, W_DQ), lambda b, i: (b, i, 0)),
        scratch_shapes=[pltpu.VMEM((nkb, tk, tq), F32),
                        pltpu.VMEM((LANES, IDX_HEADS * tq), BF16),
                        pltpu.VMEM((2 * DSA_LATENT, DSA_HEADS * tq), BF16),
                        pltpu.VMEM((VT_ROWS, DSA_HEADS * tq), F32),
                        pltpu.VMEM((1, DSA_HEADS * tq), F32),
                        pltpu.VMEM((8, tq), F32)],
        compiler_params=pltpu.CompilerParams(dimension_semantics=("arbitrary", "arbitrary"),
                                             vmem_limit_bytes=VMEM_LIMIT),
        name="dsa",
    )(qabsT, iqT, iwT, ik4, ckv, ckvT, kmax, lstrict, wuvT_pair)


def _diff_kernel(fqT_ref, fk_ref, fvT_ref, kmax_ref, lam_ref, gsub_ref, o_ref, qt_ref, acc_ref, m_ref,
                 *, tq, tk, lam0):
    hd = pl.program_id(1)
    qi = pl.program_id(2)
    q_pos = qi * tq + lax.broadcasted_iota(jnp.int32, (1, tq), 1)
    n_full = (qi * tq) // tk
    assert tq % tk == 0
    n_diag = tq // tk
    slopes = _alibi_slopes(DIFF_HEADS)
    slope = jnp.float32(slopes[DIFF_HEADS - 1])
    for i in range(DIFF_HEADS - 1):
        slope = jnp.where(hd == i, slopes[i], slope)

    for m in range(2):
        r0 = m * LANES
        qh = fqT_ref[r0:r0 + DIFF_QK_DIM, :]
        qf = qh.astype(F32)
        shift = _shift_bound(jnp.sum(qf * qf, axis=0, keepdims=True),
                             _tile_max(kmax_ref, n_full + n_diag - 1, 2 * hd + m), 0.0)
        qt_ref[r0:r0 + DIFF_QK_DIM, :] = qh
        qt_ref[r0 + DIFF_QK_DIM:r0 + DIFF_QK_DIM + BF16_ROWS, :] = _query_feature_rows(
            q_pos, BF16_ROWS, slope, shift).astype(BF16)
        qt_ref[r0 + DIFF_QK_DIM + BF16_ROWS:r0 + LANES, :] = jnp.zeros(
            (LANES - DIFF_QK_DIM - BF16_ROWS, tq), BF16)
    acc_ref[...] = jnp.zeros_like(acc_ref)

    def fast_pv(kb, masked, q0=0):
        kx = fk_ref[pl.ds(pl.multiple_of(kb * tk, tk), tk), :]
        vT = fvT_ref[kb]
        out = []
        for m in range(2):
            s = _dot(kx[:, m * LANES:(m + 1) * LANES], qt_ref[m * LANES:(m + 1) * LANES, q0:])
            if masked:
                causal = (lax.broadcasted_iota(jnp.int32, (tk, tq - q0), 0) + kb * tk
                          <= q_pos[:, q0:])
                s = jnp.where(causal, s, NEG_BIG)
            out.append(_dot(vT, jnp.exp(s).astype(BF16)))
        return out

    def accumulate(*pvs, q0=0):
        for m in range(2):
            acc_ref[m, :, q0:] += functools.reduce(lambda a, b: a + b, [pv[m] for pv in pvs])

    def diagonal_tail(first):
        for d in range(first, n_diag):
            accumulate(fast_pv(n_full + d, True, d * tk), q0=d * tk)

    def fast_pair(j, carry):
        accumulate(fast_pv(2 * j, False), fast_pv(2 * j + 1, False))
        return carry

    lax.fori_loop(0, n_full // 2, fast_pair, 0)

    @pl.when(n_full % 2 == 1)
    def _():
        accumulate(fast_pv(n_full - 1, False), fast_pv(n_full, True))
        diagonal_tail(1)

    @pl.when(n_full % 2 == 0)
    def _():
        diagonal_tail(0)

    def block(kb, masked):
        kx = fk_ref[pl.ds(pl.multiple_of(kb * tk, tk), tk), :]
        vT = fvT_ref[kb]
        qT = qt_ref[...]
        m_all = m_ref[...]
        new = []
        for m in range(2):
            s = _dot(kx[:, m * LANES:(m + 1) * LANES], qT[m * LANES:(m + 1) * LANES, :])
            if masked:
                causal = lax.broadcasted_iota(jnp.int32, (tk, tq), 0) + kb * tk <= q_pos
                s = jnp.where(causal, s, NEG_BIG)
            m_old = m_all[m]
            m_new = jnp.maximum(m_old, jnp.max(s, axis=0, keepdims=True))
            p = jnp.exp(s - m_new).astype(BF16)
            new.append((m_new, jnp.exp(m_old - m_new), _dot(vT, p)))
        for m in range(2):
            m_ref[m] = new[m][0]
            acc_ref[m] = new[m][1] * acc_ref[m] + new[m][2]

    def body(kb, carry):
        block(kb, False)
        return carry

    denom_ok = jnp.min(acc_ref[:, ONES_ROW:ONES_ROW + 1, :]) > L_MIN

    @pl.when(jnp.logical_not(denom_ok))
    def _():
        acc_ref[...] = jnp.zeros_like(acc_ref)
        m_ref[...] = jnp.full_like(m_ref, NEG_BIG)
        lax.fori_loop(0, n_full, body, 0)
        for d in range(n_diag):
            block(n_full + d, True)

    lv = lam_ref[...]
    lam = (jnp.exp(jnp.sum(lv[0:1] * lv[1:2], axis=-1, keepdims=True))
           - jnp.exp(jnp.sum(lv[2:3] * lv[3:4], axis=-1, keepdims=True)) + lam0)
    a1 = acc_ref[0]
    a2 = acc_ref[1]
    o = (a1[0:DIFF_V_DIM, :] / a1[ONES_ROW:ONES_ROW + 1, :]
         - lam * (a2[0:DIFF_V_DIM, :] / a2[ONES_ROW:ONES_ROW + 1, :]))
    o = o * lax.rsqrt(jnp.mean(o * o, axis=0, keepdims=True) + EPS) * gsub_ref[...] * (1.0 - lam0)
    o_ref[...] = o.T.astype(BF16)


def _diff_call(fqT, fk, fvT, kmax, lam_vecs, g_sub_col, tq, tk, lam0):
    B, S, _ = fk.shape
    nkb = S // tk
    kern = functools.partial(_diff_kernel, tq=tq, tk=tk, lam0=lam0)
    full = lambda a: pl.BlockSpec(a.shape, lambda b, h, i: (0,) * a.ndim)
    return pl.pallas_call(
        kern,
        out_shape=jax.ShapeDtypeStruct((B, S, DIFF_HEADS * DIFF_V_DIM), BF16),
        grid=(B, DIFF_HEADS, S // tq),
        in_specs=[pl.BlockSpec((None, 2 * LANES, tq), lambda b, h, i: (b, h, i)),
                  pl.BlockSpec((None, S, 2 * LANES), lambda b, h, i: (b, 0, h)),
                  pl.BlockSpec((None, None, nkb, VT_ROWS, tk), lambda b, h, i: (b, h, 0, 0, 0)),
                  pl.BlockSpec((None, nkb, 8, LANES), lambda b, h, i: (b, 0, 0, 0)),
                  full(lam_vecs), full(g_sub_col)],
        out_specs=pl.BlockSpec((None, tq, DIFF_V_DIM), lambda b, h, i: (b, i, h)),
        scratch_shapes=[pltpu.VMEM((2 * LANES, tq), BF16),
                        pltpu.VMEM((2, VT_ROWS, tq), F32),
                        pltpu.VMEM((2, 1, tq), F32)],
        compiler_params=pltpu.CompilerParams(
            dimension_semantics=("arbitrary", "arbitrary", "arbitrary"),
            vmem_limit_bytes=VMEM_LIMIT),
        name="diff",
    )(fqT, fk, fvT, kmax, lam_vecs, g_sub_col)


def _route_kernel(x_ref, od_ref, of_ref, mod_ref, wout_ref, g_ref, wr_hi_ref, wr_lo_ref, br_ref,
                  x1_ref, h_ref, gates_ref, *, tm):
    attn = (_dot(od_ref[...], wout_ref[0:W_DQ, :]) + _dot(of_ref[...], wout_ref[W_DQ:, :]))
    x1 = x_ref[...] + mod_ref[2:3, :] * attn
    x1_ref[...] = x1
    h = _rms(x1, g_ref[...]) * (1.0 + mod_ref[4:5, :]) + mod_ref[3:4, :]
    h_hi = h.astype(BF16)
    h_ref[...] = h_hi
    h_lo = (h - h_hi.astype(F32)).astype(BF16)
    w_hi = wr_hi_ref[...]
    logits = (_dot(h_hi, w_hi) + _dot(h_lo, w_hi) + _dot(h_hi, wr_lo_ref[...])) + br_ref[...]

    lane = lax.broadcasted_iota(jnp.int32, (tm, LANES), 1)
    big = jnp.int32(4 * LANES)
    neg = -jnp.inf
    is_g = (lane >= N_EXPERTS) & (lane < N_EXPERTS + N_GROUPS)
    gl = jnp.where(is_g, logits, neg)
    gmax = jnp.max(gl, axis=-1, keepdims=True)
    g_lane = jnp.min(jnp.where(gl == gmax, lane, big), axis=-1, keepdims=True)
    g_sel = g_lane - N_EXPERTS
    p_g = 1.0 / jnp.sum(jnp.where(is_g, jnp.exp(gl - gmax), 0.0), axis=-1, keepdims=True)
    in_grp = (lane < N_EXPERTS) & ((lane // EXPERTS_PER_GROUP) == g_sel)
    el = jnp.where(in_grp, logits, neg)
    v1 = jnp.max(el, axis=-1, keepdims=True)
    i1 = jnp.min(jnp.where(el == v1, lane, big), axis=-1, keepdims=True)
    el2 = jnp.where(lane == i1, neg, el)
    v2 = jnp.max(el2, axis=-1, keepdims=True)
    i2 = jnp.min(jnp.where(el2 == v2, lane, big), axis=-1, keepdims=True)
    e = jnp.exp(v2 - v1)
    w1 = 1.0 / (1.0 + e)
    w2 = e * w1
    gates_ref[...] = jnp.where(lane == i1, w1 * p_g, jnp.where(lane == i2, w2 * p_g, 0.0))


def _route_call(x, o_dsa, o_diff, mod3, w_out, g_moe, wr_hi, wr_lo, b_r, tm):
    B, S, D = x.shape
    kern = functools.partial(_route_kernel, tm=tm)
    tok = lambda w: pl.BlockSpec((None, tm, w), lambda b, i: (b, i, 0))
    full = lambda a: pl.BlockSpec(a.shape, lambda b, i: (0,) * a.ndim)
    return pl.pallas_call(
        kern,
        out_shape=[jax.ShapeDtypeStruct((B, S, D), F32), jax.ShapeDtypeStruct((B, S, D), BF16),
                   jax.ShapeDtypeStruct((B, S, LANES), F32)],
        grid=(B, S // tm),
        in_specs=[tok(D), tok(W_DQ), tok(DIFF_HEADS * DIFF_V_DIM),
                  pl.BlockSpec((None, 6, D), lambda b, i: (b, 0, 0)),
                  full(w_out), full(g_moe), full(wr_hi), full(wr_lo), full(b_r)],
        out_specs=[tok(D), tok(D), tok(LANES)],
        compiler_params=pltpu.CompilerParams(dimension_semantics=("arbitrary", "arbitrary"),
                                             vmem_limit_bytes=VMEM_LIMIT),
        name="route",
    )(x, o_dsa, o_diff, mod3, w_out, g_moe, wr_hi, wr_lo, b_r)


def _moe_kernel(h_ref, gates_ref, x1_ref, mod_ref, wg_ref, wu_ref, wd_ref, gf_ref, o_ref, acc_ref,
                *, tm, ec):
    j = pl.program_id(2)
    nc = pl.num_programs(2)

    @pl.when(j == 0)
    def _():
        acc_ref[...] = jnp.zeros_like(acc_ref)

    h = h_ref[...]
    hg = _dot(h, wg_ref[...])
    hu = _dot(h, wu_ref[...])
    hid = hg * jax.nn.sigmoid(hg) * hu
    gates = gates_ref[...]
    lane = lax.broadcasted_iota(jnp.int32, (tm, LANES), 1)
    parts = []
    for e in range(ec):
        gcol = jnp.sum(jnp.where(lane == j * ec + e, gates, 0.0), axis=-1, keepdims=True)
        parts.append((hid[:, e * D_EXPERT:(e + 1) * D_EXPERT] * gcol).astype(BF16))
    hs = jnp.concatenate(parts, axis=1)
    acc_ref[...] += _dot(hs, wd_ref[...])

    @pl.when(j == nc - 1)
    def _():
        x2 = x1_ref[...] + mod_ref[5:6, :] * acc_ref[...]
        o_ref[...] = _rms(x2, gf_ref[...])


def _moe_call(h, gates, x1, mod3, wg, wu, wd, g_final, tm, ec):
    B, S, D = x1.shape
    kern = functools.partial(_moe_kernel, tm=tm, ec=ec)
    tok = lambda w: pl.BlockSpec((None, tm, w), lambda b, i, j: (b, i, 0))
    return pl.pallas_call(
        kern,
        out_shape=jax.ShapeDtypeStruct((B, S, D), F32),
        grid=(B, S // tm, N_EXPERTS // ec),
        in_specs=[tok(D), tok(LANES), tok(D),
                  pl.BlockSpec((None, 6, D), lambda b, i, j: (b, 0, 0)),
                  pl.BlockSpec((D, ec * D_EXPERT), lambda b, i, j: (0, j)),
                  pl.BlockSpec((D, ec * D_EXPERT), lambda b, i, j: (0, j)),
                  pl.BlockSpec((ec * D_EXPERT, D), lambda b, i, j: (j, 0)),
                  pl.BlockSpec((1, D), lambda b, i, j: (0, 0))],
        out_specs=tok(D),
        scratch_shapes=[pltpu.VMEM((tm, D), F32)],
        compiler_params=pltpu.CompilerParams(
            dimension_semantics=("arbitrary", "arbitrary", "arbitrary"),
            vmem_limit_bytes=VMEM_LIMIT),
        name="moe",
    )(h, gates, x1, mod3, wg, wu, wd, g_final)


def _tile(n, pref):
    t = min(n, pref)
    assert n % t == 0, (n, t)
    return t


def kernel(x, c, w_ada, b_ada, g_attn, w_in, g_kv, w_uk, w_uv, lam_q1, lam_k1, lam_q2, lam_k2,
           g_sub, w_out, g_moe, w_group, b_group, w_router, b_router, w_gate, w_up, w_down, g_final):
    B, S, D = x.shape
    assert D == D_MODEL and w_ada.shape[0] == 1
    assert S % KEY_TILE == 0 and S <= POS_SPLIT * 256
    topk = min(TOPK_MAX, S // 4)
    l = 0
    lam0 = 0.8 - 0.6 * math.exp(-0.3 * l)

    mod3 = _mod_call(c, w_ada[l], b_ada[l].reshape(1, -1)).reshape(B, 6, D)

    wn, wt, w_pair = _proj_weights(w_in[l], w_uk[l])
    qabsT, iqT, iwT, ik4, ckv, ckvT, fqT, fk, fvT, kmax = _proj_call(
        x, mod3, g_attn[l].reshape(1, D), wn, wt, w_pair, g_kv[l].reshape(1, -1), KEY_TILE)

    lstrict = jnp.asarray(np.tril(np.ones((KEY_TILE, KEY_TILE), np.float32), -1), BF16)
    uvT = jnp.swapaxes(w_uv[l], 1, 2)
    z = jnp.zeros_like(uvT[0])
    wuvT_pair = jnp.stack([
        jnp.concatenate([jnp.concatenate([uvT[2 * p], z], axis=1),
                         jnp.concatenate([z, uvT[2 * p + 1]], axis=1)], axis=0)
        for p in range(DSA_HEADS // 2)]).astype(BF16)
    o_dsa = _dsa_call(qabsT, iqT, iwT, ik4, ckv, ckvT, kmax, lstrict, wuvT_pair, DSA_QUERY_TILE, KEY_TILE,
                      topk)

    lam_vecs = jnp.concatenate([lam_q1[l][None], lam_k1[l][None], lam_q2[l][None], lam_k2[l][None]],
                               axis=0).astype(F32)
    o_diff = _diff_call(fqT, fk, fvT, kmax, lam_vecs, g_sub[l].reshape(-1, 1),
                        _tile(S, DIFF_QUERY_TILE), KEY_TILE, lam0)

    wr = jnp.pad(jnp.concatenate([w_router[l], w_group[l]], axis=1),
                 ((0, 0), (0, LANES - N_EXPERTS - N_GROUPS)))
    wr_hi = wr.astype(BF16)
    wr_lo = (wr - wr_hi.astype(F32)).astype(BF16)
    b_r = jnp.pad(jnp.concatenate([b_router[l], b_group[l]]), (0, LANES - N_EXPERTS - N_GROUPS))
    x1, h2, gates = _route_call(x, o_dsa, o_diff, mod3, w_out[l].astype(BF16),
                                g_moe[l].reshape(1, D), wr_hi, wr_lo, b_r.reshape(1, LANES),
                                _tile(S, TOKEN_TILE))

    wg = jnp.swapaxes(w_gate[l], 0, 1).reshape(D, N_EXPERTS * D_EXPERT).astype(BF16)
    wu = jnp.swapaxes(w_up[l], 0, 1).reshape(D, N_EXPERTS * D_EXPERT).astype(BF16)
    wd = w_down[l].reshape(N_EXPERTS * D_EXPERT, D).astype(BF16)
    return _moe_call(h2, gates, x1, mod3, wg, wu, wd, g_final.reshape(1, D), _tile(S, MOE_TOKEN_TILE),
                     MOE_EXPERTS_PER_STEP)
```

```python
import functools
import math

import jax
import jax.numpy as jnp
import numpy as np
from jax import lax
from jax.experimental import pallas as pl
from jax.experimental.pallas import tpu as pltpu

F32 = jnp.float32
BF16 = jnp.bfloat16

D_MODEL = 1024
DSA_HEADS = 8
DSA_HEAD_DIM = 64
DSA_LATENT = 128
IDX_HEADS = 4
IDX_DIM = 32
TOPK_MAX = 256
DIFF_HEADS = 4
DIFF_QK_DIM = 64
DIFF_V_DIM = 128
N_GROUPS = 4
EXPERTS_PER_GROUP = 8
N_EXPERTS = 32
D_EXPERT = 256
EPS = 1e-6

LANES = 128
BF16_ROWS = 16
POS_SPLIT = 64
NEG_BIG = -1e30
VMEM_LIMIT = 56 * 1024 * 1024
KEY_TILE = 512
VALUE_BISECT_STEPS = 24
UNCHECKED_BISECT_STEPS = 16
COUNT_CHAINS = 4

W_DQ = DSA_HEADS * DSA_HEAD_DIM
W_QABS = DSA_HEADS * DSA_LATENT
W_CKV = 2 * LANES
VT_ROWS = DSA_LATENT + BF16_ROWS
ONES_ROW = DSA_LATENT + 2
N_FEATS = 5
KMAX_LANE_CKV = 8
L_MIN = 1e-30
BF16_ROUND_UP = 1.0 + 2.0 ** -7
BOUND_SLACK = 2.0 ** -6
CODE_BISECT_STEPS = 34
TOKEN_TILE = 512
DSA_QUERY_TILE = 256
DIFF_QUERY_TILE = 2048
MOE_EXPERTS_PER_STEP = 8
MOE_TOKEN_TILE = 512
W_FQ = DIFF_HEADS * 2 * LANES
W_DIFF_QK = DIFF_HEADS * 2 * DIFF_QK_DIM

T_DQ = 0
T_DLAT = T_DQ + W_DQ
T_IQ = T_DLAT + DSA_LATENT
T_IW = T_IQ + IDX_HEADS * IDX_DIM
T_FQ = T_IW + BF16_ROWS
T_FV = T_FQ + W_DIFF_QK
T_ROWS = T_FV + DIFF_HEADS * DIFF_V_DIM
N_IK = 0
N_DLAT = N_IK + LANES
N_FK = N_DLAT + DSA_LATENT
N_COLS = N_FK + W_FQ


def _alibi_slopes(n):
    return [2.0 ** (-8.0 * (i + 1) / n) for i in range(n)]


def _rms(x, g):
    return x * lax.rsqrt(jnp.mean(x * x, axis=-1, keepdims=True) + EPS) * g


def _dot(a, b):
    return jnp.dot(a, b, preferred_element_type=F32)


def _mod_kernel(c_ref, w_ref, b_ref, o_ref):
    c = c_ref[...]
    act = c * jax.nn.sigmoid(c)
    o_ref[...] = jnp.dot(act, w_ref[...], preferred_element_type=F32,
                         precision=lax.Precision.HIGHEST) + b_ref[...]


def _mod_call(c, w_ada, b_ada):
    B, D = c.shape
    n = w_ada.shape[1] // D
    return pl.pallas_call(
        _mod_kernel,
        out_shape=jax.ShapeDtypeStruct((B, n * D), F32),
        grid=(n,),
        in_specs=[pl.BlockSpec((B, D), lambda j: (0, 0)),
                  pl.BlockSpec((D, D), lambda j: (0, j)),
                  pl.BlockSpec((1, D), lambda j: (0, j))],
        out_specs=pl.BlockSpec((B, D), lambda j: (0, j)),
        compiler_params=pltpu.CompilerParams(dimension_semantics=("arbitrary",),
                                             vmem_limit_bytes=VMEM_LIMIT),
        name="mod",
    )(c, w_ada, b_ada)


def _key_feature_rows(pos_row, n_rows):
    r = lax.broadcasted_iota(jnp.int32, (n_rows, pos_row.shape[1]), 0)
    pa = (pos_row // POS_SPLIT).astype(F32)
    pb = (pos_row % POS_SPLIT).astype(F32)
    return jnp.where(r == 0, pa, jnp.where(r == 1, pb, jnp.where(r < N_FEATS, 1.0, 0.0)))


def _query_feature_rows(pos_row, n_rows, slope, shift):
    r = lax.broadcasted_iota(jnp.int32, (n_rows, pos_row.shape[1]), 0)
    pa = (pos_row // POS_SPLIT).astype(F32)
    pb = (pos_row % POS_SPLIT).astype(F32)
    return jnp.where(r == 0, POS_SPLIT * slope,
                     jnp.where(r == 1, slope,
                               jnp.where(r == 2, -POS_SPLIT * slope * pa,
                                         jnp.where(r == 3, -slope * pb,
                                                   jnp.where(r == 4, -shift, 0.0)))))


def _shift_bound(qn2, kmax2, extra):
    b = jnp.sqrt(qn2 * kmax2) + extra
    return b + jnp.abs(b) * BOUND_SLACK + BOUND_SLACK


def _tile_max(kmax_ref, last_tile, lane_idx):
    x = kmax_ref[...]
    t = lax.broadcasted_iota(jnp.int32, x.shape, 0)
    ln = lax.broadcasted_iota(jnp.int32, x.shape, 2)
    x = jnp.where((t <= last_tile) & (ln == lane_idx), x, 0.0)
    return jnp.max(jnp.max(x, axis=0), axis=1, keepdims=True)[0:1, :]


def _proj_kernel(x_ref, mod_ref, g_ref, wn_ref, wt_ref, wpair_ref, gkv_ref, gkvc_ref, fc_ref, sel_ref,
                 qabsT_ref, iqT_ref, iwT_ref, ik_ref, ckv_ref, ckvT_ref, fqT_ref, fk_ref, fvT_ref,
                 kmax_ref, *, tm):
    x = x_ref[...]
    h = _rms(x, g_ref[...]) * (1.0 + mod_ref[1:2, :]) + mod_ref[0:1, :]
    hb = h.astype(BF16)
    hT = h.T.astype(BF16)

    base = pl.program_id(1) * tm
    pos_c = base + lax.broadcasted_iota(jnp.int32, (tm, 1), 0)
    pos_r = base + lax.broadcasted_iota(jnp.int32, (1, tm), 1)
    pa = (pos_c // POS_SPLIT).astype(F32)
    pb = (pos_c % POS_SPLIT).astype(F32)

    def feats(row, width):
        return (fc_ref[row:row + 1, 0:width] + fc_ref[row + 1:row + 2, 0:width] * pa
                + fc_ref[row + 2:row + 3, 0:width] * pb)

    ik_ref[...] = _dot(hb, wn_ref[:, N_IK:N_IK + LANES]).astype(BF16)
    dlat = _dot(hb, wn_ref[:, N_DLAT:N_DLAT + DSA_LATENT])
    ckv_b = _rms(dlat, gkv_ref[...]).astype(BF16)
    ckv_ref[:, 0:LANES] = ckv_b
    ckv_ref[:, LANES:2 * LANES] = feats(0, LANES).astype(BF16)
    fk = _dot(hb, wn_ref[:, N_FK:N_FK + W_FQ])
    fk_ref[...] = (fk + feats(3, W_FQ)).astype(BF16)
    sq = jnp.concatenate([fk.astype(BF16).astype(F32), ckv_b.astype(F32)], axis=1)
    sq_up = (sq * sq * BF16_ROUND_UP).astype(BF16)
    kmax_ref[...] = jnp.broadcast_to(jnp.max(_dot(sq_up, sel_ref[...]), axis=0, keepdims=True),
                                     (8, LANES))

    dqT = _dot(wt_ref[T_DQ:T_DQ + W_DQ, :], hT).astype(BF16)
    for p in range(DSA_HEADS // 2):
        qa = _dot(wpair_ref[p], dqT[p * LANES:(p + 1) * LANES, :])
        qabsT_ref[p * 2 * LANES:(p + 1) * 2 * LANES, :] = qa.astype(BF16)
    dlatT = _dot(wt_ref[T_DLAT:T_DLAT + DSA_LATENT, :], hT)
    inv = lax.rsqrt(jnp.mean(dlatT * dlatT, axis=0, keepdims=True) + EPS)
    kfeat = _key_feature_rows(pos_r, BF16_ROWS).astype(BF16)
    ckvT_ref[0:DSA_LATENT, :] = (dlatT * inv * gkvc_ref[...]).astype(BF16)
    ckvT_ref[DSA_LATENT:VT_ROWS, :] = kfeat
    iqT_ref[...] = _dot(wt_ref[T_IQ:T_IQ + IDX_HEADS * IDX_DIM, :], hT).astype(BF16)
    iwT = _dot(wt_ref[T_IW:T_IW + BF16_ROWS, :], hT)
    iwT_ref[...] = iwT[0:8, :] * (IDX_HEADS ** -0.5 * IDX_DIM ** -0.5)
    fqT = _dot(wt_ref[T_FQ:T_FQ + W_DIFF_QK, :], hT)
    zpad = jnp.zeros((LANES - DIFF_QK_DIM, tm), BF16)
    for s in range(DIFF_HEADS * 2):
        r0 = s * LANES
        fqT_ref[r0:r0 + DIFF_QK_DIM, :] = fqT[s * DIFF_QK_DIM:(s + 1) * DIFF_QK_DIM, :].astype(BF16)
        fqT_ref[r0 + DIFF_QK_DIM:r0 + LANES, :] = zpad
    fvT = _dot(wt_ref[T_FV:T_FV + DIFF_HEADS * DIFF_V_DIM, :], hT)
    for hh in range(DIFF_HEADS):
        fvT_ref[hh, 0:DIFF_V_DIM, :] = fvT[hh * DIFF_V_DIM:(hh + 1) * DIFF_V_DIM, :].astype(BF16)
        fvT_ref[hh, DIFF_V_DIM:VT_ROWS, :] = kfeat


def _feature_consts():
    fc = np.zeros((8, W_FQ), np.float32)
    fc[1, 0] = 1.0
    fc[2, 1] = 1.0
    fc[0, 2:N_FEATS] = 1.0
    for s in range(DIFF_HEADS * 2):
        base = s * LANES + DIFF_QK_DIM
        fc[4, base + 0] = 1.0
        fc[5, base + 1] = 1.0
        fc[3, base + 2:base + N_FEATS] = 1.0
    return jnp.asarray(fc)


def _norm_selector():
    sel = np.zeros((W_FQ + DSA_LATENT, LANES), np.float32)
    for s in range(DIFF_HEADS * 2):
        sel[s * LANES:s * LANES + DIFF_QK_DIM, s] = 1.0
    sel[W_FQ:, KMAX_LANE_CKV] = 1.0
    return jnp.asarray(sel, BF16)


def _proj_weights(w_in, w_uk):
    D = w_in.shape[0]
    pts = np.cumsum([W_DQ, DSA_LATENT, IDX_HEADS * IDX_DIM, IDX_DIM, IDX_HEADS,
                     W_DIFF_QK, W_DIFF_QK])
    dq, dlat, iq, ik, iw, fq, fk, fv = jnp.split(w_in, list(pts), axis=1)
    ik4 = jnp.tile(ik, (1, IDX_HEADS))
    fke = jnp.pad(fk.reshape(D, DIFF_HEADS * 2, DIFF_QK_DIM),
                  ((0, 0), (0, 0), (0, LANES - DIFF_QK_DIM))).reshape(D, W_FQ)
    wn = jnp.concatenate([ik4, dlat, fke], axis=1).astype(BF16)
    iwp = jnp.pad(iw, ((0, 0), (0, BF16_ROWS - IDX_HEADS)))
    wt = jnp.concatenate([dq, dlat, iq, iwp, fq * (DIFF_QK_DIM ** -0.5), fv], axis=1).T.astype(BF16)
    uk = w_uk * (DSA_HEAD_DIM ** -0.5)
    z = jnp.zeros_like(uk[0])
    pairs = [jnp.concatenate([jnp.concatenate([uk[2 * p], z], axis=1),
                              jnp.concatenate([z, uk[2 * p + 1]], axis=1)], axis=0)
             for p in range(DSA_HEADS // 2)]
    return wn, wt, jnp.stack(pairs).astype(BF16)


def _proj_call(x, mod3, g_attn, wn, wt, w_pair, g_kv, tm):
    B, S, D = x.shape
    nt = S // tm
    fc = _feature_consts()
    sel = _norm_selector()
    kern = functools.partial(_proj_kernel, tm=tm)
    tok = lambda w: pl.BlockSpec((None, tm, w), lambda b, i: (b, i, 0))
    tokT = lambda r: pl.BlockSpec((None, r, tm), lambda b, i: (b, 0, i))
    full = lambda a: pl.BlockSpec(a.shape, lambda b, i: (0,) * a.ndim)
    g_kv_col = g_kv.reshape(-1, 1)
    out_shape = [jax.ShapeDtypeStruct((B, W_QABS, S), BF16),
                 jax.ShapeDtypeStruct((B, LANES, S), BF16),
                 jax.ShapeDtypeStruct((B, 8, S), F32),
                 jax.ShapeDtypeStruct((B, S, LANES), BF16),
                 jax.ShapeDtypeStruct((B, S, W_CKV), BF16),
                 jax.ShapeDtypeStruct((B, nt, VT_ROWS, tm), BF16),
                 jax.ShapeDtypeStruct((B, W_FQ, S), BF16),
                 jax.ShapeDtypeStruct((B, S, W_FQ), BF16),
                 jax.ShapeDtypeStruct((B, DIFF_HEADS, nt, VT_ROWS, tm), BF16),
                 jax.ShapeDtypeStruct((B, nt, 8, LANES), F32)]
    out_specs = [tokT(W_QABS), tokT(LANES), tokT(8), tok(LANES), tok(W_CKV),
                 pl.BlockSpec((None, None, VT_ROWS, tm), lambda b, i: (b, i, 0, 0)),
                 tokT(W_FQ), tok(W_FQ),
                 pl.BlockSpec((None, DIFF_HEADS, None, VT_ROWS, tm), lambda b, i: (b, 0, i, 0, 0)),
                 pl.BlockSpec((None, None, 8, LANES), lambda b, i: (b, i, 0, 0))]
    return pl.pallas_call(
        kern,
        out_shape=out_shape,
        grid=(B, nt),
        in_specs=[tok(D),
                  pl.BlockSpec((None, 6, D), lambda b, i: (b, 0, 0)),
                  full(g_attn), full(wn), full(wt), full(w_pair), full(g_kv), full(g_kv_col),
                  full(fc), full(sel)],
        out_specs=out_specs,
        compiler_params=pltpu.CompilerParams(dimension_semantics=("arbitrary", "arbitrary"),
                                             vmem_limit_bytes=VMEM_LIMIT),
        name="proj",
    )(x, mod3, g_attn, wn, wt, w_pair, g_kv, g_kv_col, fc, sel)


def _float_code(x):
    b = lax.bitcast_convert_type(x, jnp.int32)
    return b ^ (lax.shift_right_arithmetic(b, 31) & jnp.int32(0x7FFFFFFF))


def _float_decode(c):
    b = c ^ (lax.shift_right_arithmetic(c, 31) & jnp.int32(0x7FFFFFFF))
    return lax.bitcast_convert_type(b, F32)


def _dsa_kernel(qabsT_ref, iqT_ref, iwT_ref, ik_ref, ckv_ref, ckvT_ref, kmax_ref, lstrict_ref, wuvT_ref,
                o_ref, sc_ref, qst_ref, qt_ref, acc_ref, m_ref, tmp_ref, *, tq, tk, topk):
    qi = pl.program_id(1)
    nkb = (qi * tq) // tk + 1
    kf = float(topk)
    slopes = _alibi_slopes(DSA_HEADS)
    q_pos = qi * tq + lax.broadcasted_iota(jnp.int32, (1, tq), 1)

    iqT = iqT_ref[...]
    rowi = lax.broadcasted_iota(jnp.int32, (LANES, tq), 0)
    for hh in range(IDX_HEADS):
        qst_ref[:, hh * tq:(hh + 1) * tq] = jnp.where((rowi // IDX_DIM) == hh, iqT, jnp.zeros_like(iqT))
    iw = iwT_ref[...]
    wrow = [iw[hh:hh + 1, :] for hh in range(IDX_HEADS)]

    def scores(kb):
        kblk = ik_ref[pl.ds(pl.multiple_of(kb * tk, tk), tk), :]
        a = _dot(kblk, qst_ref[...])
        sc = jnp.maximum(a[:, 0:tq], 0.0) * wrow[0]
        for hh in range(1, IDX_HEADS):
            sc = sc + jnp.maximum(a[:, hh * tq:(hh + 1) * tq], 0.0) * wrow[hh]
        return sc

    def score_body(kb, carry):
        mn, mx = carry
        sc = scores(kb)
        sc_ref[kb] = sc
        return (jnp.minimum(mn, jnp.min(sc, axis=0, keepdims=True)),
                jnp.maximum(mx, jnp.max(sc, axis=0, keepdims=True)))

    last = nkb - 1
    mn, mx = lax.fori_loop(0, last // 2, lambda j, c: score_body(2 * j + 1, score_body(2 * j, c)),
                           (jnp.full((1, tq), jnp.inf, F32), jnp.full((1, tq), -jnp.inf, F32)))
    mn, mx = lax.cond(last % 2 == 1, lambda c: score_body(last - 1, c), lambda c: c, (mn, mx))
    sc = scores(last)
    causal = lax.broadcasted_iota(jnp.int32, (tk, tq), 0) + last * tk <= q_pos
    sc_ref[last] = jnp.where(causal, sc, jnp.nan)
    mn = jnp.minimum(mn, jnp.min(jnp.where(causal, sc, jnp.inf), axis=0, keepdims=True))
    mx = jnp.maximum(mx, jnp.max(jnp.where(causal, sc, -jnp.inf), axis=0, keepdims=True))

    def block_count(kb, pred):
        parts = [None] * COUNT_CHAINS
        for r in range(tk // 8):
            v = jnp.where(pred(sc_ref[kb, r * 8:(r + 1) * 8, :]), 1.0, 0.0)
            c = r % COUNT_CHAINS
            parts[c] = v if parts[c] is None else parts[c] + v
        return (parts[0] + parts[1]) + (parts[2] + parts[3])

    def count(pred):
        acc = lax.fori_loop(0, nkb, lambda kb, acc: acc + block_count(kb, pred),
                            jnp.zeros((8, tq), F32))
        return jnp.sum(acc, axis=0, keepdims=True)

    n_causal = (q_pos + 1).astype(F32)
    done0 = n_causal <= kf

    def bisect(st, value_mid):
        lo, hi, c_lo, c_hi, theta, done = st
        th8 = jnp.broadcast_to(theta, (8, tq))
        c = count(lambda v: v >= th8)
        ge = c >= kf
        live = done == 0.0
        lo = jnp.where(live & ge, theta, lo)
        c_lo = jnp.where(live & ge, c, c_lo)
        hi = jnp.where(live & jnp.logical_not(ge), theta, hi)
        c_hi = jnp.where(live & jnp.logical_not(ge), c, c_hi)
        if value_mid:
            nxt = 0.5 * lo + 0.5 * hi
        else:
            cl, ch = _float_code(lo), _float_code(hi)
            nxt = _float_decode((cl & ch) + lax.shift_right_arithmetic(cl ^ ch, 1))
        inside = (nxt > lo) & (nxt < hi)
        done = jnp.where((c_lo == kf) | jnp.logical_not(inside), 1.0, done)
        return lo, hi, c_lo, c_hi, nxt, done

    def unresolved(st):
        return jnp.sum(jnp.where((st[5] == 0.0) & (st[3] != kf - 1.0), 1.0, 0.0))

    def checked(value_mid, max_steps):
        def cond(c):
            return jnp.logical_and(c[1] > 0.0, c[2] < max_steps)

        def body(c):
            st = bisect(bisect(c[0], value_mid), value_mid)
            return st, unresolved(st), c[2] + 2
        return cond, body

    zero8 = jnp.zeros((8, tq), F32)
    c_ge0 = count(lambda v: v >= zero8)
    c_gt0 = count(lambda v: v > zero8)
    live0 = jnp.logical_not(done0)
    at0 = live0 & (c_ge0 >= kf)
    tie0 = at0 & (c_gt0 < kf)
    below0 = live0 & (c_ge0 < kf)
    lo0 = jnp.where(at0, 0.0, mn)
    hi0 = jnp.where(below0, 0.0, mx)
    st = (lo0, hi0, jnp.where(at0, c_ge0, n_causal), jnp.where(below0, c_ge0, -1.0),
          jnp.where(below0, 0.5 * lo0 + 0.5 * hi0, mx), jnp.where(done0 | tie0, 1.0, 0.0))
    st = lax.fori_loop(0, UNCHECKED_BISECT_STEPS, lambda i, s: bisect(s, True), st)
    cond, body = checked(True, VALUE_BISECT_STEPS - UNCHECKED_BISECT_STEPS)
    st, active, _ = lax.while_loop(cond, body, (st, unresolved(st), jnp.int32(0)))
    cond, body = checked(False, CODE_BISECT_STEPS)
    st, _, _ = lax.while_loop(cond, body, (st, active, jnp.int32(0)))
    tmp_ref[1:2, :] = st[0]
    tmp_ref[2:3, :] = st[2]
    from_hi = (st[5] == 0.0) & (st[3] == kf - 1.0)

    @pl.when(jnp.max(jnp.where(from_hi, 1.0, 0.0)) > 0.0)
    def _():
        hi8 = jnp.broadcast_to(st[1], (8, tq))

        def below_hi(kb, acc):
            parts = [None] * COUNT_CHAINS
            for r in range(tk // 8):
                v = sc_ref[kb, r * 8:(r + 1) * 8, :]
                v = jnp.where(v < hi8, v, -jnp.inf)
                c = r % COUNT_CHAINS
                parts[c] = v if parts[c] is None else jnp.maximum(parts[c], v)
            return jnp.maximum(acc, jnp.maximum(jnp.maximum(parts[0], parts[1]),
                                                jnp.maximum(parts[2], parts[3])))

        nxt = jnp.max(lax.fori_loop(0, nkb, below_hi, jnp.full((8, tq), -jnp.inf, F32)),
                      axis=0, keepdims=True)
        t_new = jnp.where(from_hi, nxt, st[0])
        t8 = jnp.broadcast_to(t_new, (8, tq))
        tmp_ref[1:2, :] = t_new
        tmp_ref[2:3, :] = jnp.where(from_hi, count(lambda v: v >= t8), st[2])

    tau, c_ge = tmp_ref[1:2, :], tmp_ref[2:3, :]

    over = c_ge > kf

    @pl.when(jnp.max(c_ge) > kf)
    def _():
        tau8 = jnp.broadcast_to(tau, (8, tq))
        tmp_ref[0:1, :] = c_gt0

        @pl.when(jnp.max(jnp.where(over & jnp.logical_not(tie0), 1.0, 0.0)) > 0.0)
        def _():
            tmp_ref[0:1, :] = jnp.where(tie0, c_gt0, count(lambda v: v > tau8))

        quota = kf - tmp_ref[0:1, :]

        def tie_body(kb, seen):
            cnt = jnp.sum(block_count(kb, lambda v: v == tau8), axis=0, keepdims=True)
            inside = over & (seen < quota) & (seen + cnt > quota)
            gone = over & (seen >= quota) & (cnt > 0.0)
            flag = jnp.max(jnp.where(inside, 2.0, jnp.where(gone, 1.0, 0.0)))

            @pl.when(flag > 1.5)
            def _():
                s = sc_ref[kb]
                eq = s == tau
                rank = _dot(lstrict_ref[...], jnp.where(eq, 1.0, 0.0).astype(BF16)) + seen
                sc_ref[kb] = jnp.where(eq & over & (rank >= quota), jnp.nan, s)

            @pl.when(flag == 1.0)
            def _():
                s = sc_ref[kb]
                sc_ref[kb] = jnp.where((s == tau) & gone, jnp.nan, s)

            return seen + cnt

        lax.fori_loop(0, nkb, tie_body, jnp.zeros((1, tq), F32))

    tau8 = jnp.broadcast_to(tau, (8, tq))
    sub = lax.broadcasted_iota(jnp.int32, (8, tq), 0).astype(F32)

    def nearest_body(kb, acc):
        parts = [None] * COUNT_CHAINS
        for r in range(tk // 8):
            idx = sub + (kb * tk + r * 8).astype(F32)
            v = jnp.where(sc_ref[kb, r * 8:(r + 1) * 8, :] >= tau8, idx, -1.0)
            c = r % COUNT_CHAINS
            parts[c] = v if parts[c] is None else jnp.maximum(parts[c], v)
        return jnp.maximum(acc, jnp.maximum(jnp.maximum(parts[0], parts[1]),
                                            jnp.maximum(parts[2], parts[3])))

    last_sel = jnp.max(lax.fori_loop(0, nkb, nearest_body, jnp.full((8, tq), -1.0, F32)),
                       axis=0, keepdims=True)
    d_min = q_pos.astype(F32) - last_sel
    kmax2 = _tile_max(kmax_ref, nkb - 1, KMAX_LANE_CKV)
    for hh in range(DSA_HEADS):
        qh = qabsT_ref[hh * DSA_LATENT:(hh + 1) * DSA_LATENT, :]
        qf = qh.astype(F32)
        shift = _shift_bound(jnp.sum(qf * qf, axis=0, keepdims=True), kmax2, -slopes[hh] * d_min)
        qt_ref[0:DSA_LATENT, hh * tq:(hh + 1) * tq] = qh
        qt_ref[DSA_LATENT:VT_ROWS, hh * tq:(hh + 1) * tq] = _query_feature_rows(
            q_pos, BF16_ROWS, slopes[hh], shift).astype(BF16)
    qt_ref[VT_ROWS:2 * DSA_LATENT, :] = jnp.zeros((2 * DSA_LATENT - VT_ROWS, DSA_HEADS * tq), BF16)
    acc_ref[...] = jnp.zeros_like(acc_ref)

    def fast_pv(kb):
        kx = ckv_ref[pl.ds(pl.multiple_of(kb * tk, tk), tk), :]
        s = _dot(kx, qt_ref[...])
        keep = sc_ref[kb] >= tau
        ps = [jnp.exp(jnp.where(keep, s[:, hh * tq:(hh + 1) * tq], NEG_BIG)).astype(BF16)
              for hh in range(DSA_HEADS)]
        return _dot(ckvT_ref[kb], jnp.concatenate(ps, axis=1))

    def fast_pair(j, carry):
        acc_ref[...] += fast_pv(2 * j) + fast_pv(2 * j + 1)
        return carry

    lax.fori_loop(0, nkb // 2, fast_pair, 0)

    @pl.when(nkb % 2 == 1)
    def _():
        acc_ref[...] += fast_pv(nkb - 1)

    def attn_body(kb, carry):
        kx = ckv_ref[pl.ds(pl.multiple_of(kb * tk, tk), tk), :]
        vT = ckvT_ref[kb]
        keep = sc_ref[kb] >= tau
        s = _dot(kx, qt_ref[...])
        m_all = m_ref[...]
        m_out, alphas, ps = [], [], []
        for hh in range(DSA_HEADS):
            c0 = hh * tq
            sh = jnp.where(keep, s[:, c0:c0 + tq], NEG_BIG)
            m_old = m_all[:, c0:c0 + tq]
            m_new = jnp.maximum(m_old, jnp.max(sh, axis=0, keepdims=True))
            ps.append(jnp.exp(sh - m_new).astype(BF16))
            alphas.append(jnp.exp(m_old - m_new))
            m_out.append(m_new)
        m_ref[...] = jnp.concatenate(m_out, axis=1)
        acc_ref[...] = (jnp.concatenate(alphas, axis=1) * acc_ref[...]
                        + _dot(vT, jnp.concatenate(ps, axis=1)))
        return carry

    denom_ok = jnp.min(acc_ref[ONES_ROW:ONES_ROW + 1, :]) > L_MIN

    @pl.when(jnp.logical_not(denom_ok))
    def _():
        acc_ref[...] = jnp.zeros_like(acc_ref)
        m_ref[...] = jnp.full_like(m_ref, NEG_BIG)
        lax.fori_loop(0, nkb, attn_body, 0)

    acc = acc_ref[...]
    lat = acc[0:DSA_LATENT, :] / acc[ONES_ROW:ONES_ROW + 1, :]
    for p in range(DSA_HEADS // 2):
        pair = lat[:, 2 * p * tq:(2 * p + 2) * tq]
        pair = jnp.concatenate([pair[:, 0:tq], pair[:, tq:2 * tq]], axis=0).astype(BF16)
        o_ref[:, p * LANES:(p + 1) * LANES] = _dot(wuvT_ref[p], pair).T.astype(BF16)


def _dsa_call(qabsT, iqT, iwT, ik4, ckv, ckvT, kmax, lstrict, wuvT_pair, tq, tk, topk):
    B, _, S = qabsT.shape
    nkb = S // tk
    kern = functools.partial(_dsa_kernel, tq=tq, tk=tk, topk=topk)
    blkT = lambda r: pl.BlockSpec((None, r, tq), lambda b, i: (b, 0, i))
    per_b = lambda w: pl.BlockSpec((None, S, w), lambda b, i: (b, 0, 0))
    full = lambda a: pl.BlockSpec(a.shape, lambda b, i: (0,) * a.ndim)
    return pl.pallas_call(
        kern,
        out_shape=jax.ShapeDtypeStruct((B, S, W_DQ), BF16),
        grid=(B, S // tq),
        in_specs=[blkT(W_QABS), blkT(LANES), blkT(8), per_b(LANES), per_b(W_CKV),
                  pl.BlockSpec((None, nkb, VT_ROWS, tk), lambda b, i: (b, 0, 0, 0)),
                  pl.BlockSpec((None, nkb, 8, LANES), lambda b, i: (b, 0, 0, 0)),
                  full(lstrict), full(wuvT_pair)],
        out_specs=pl.BlockSpec((None, tq, W_DQ), lambda b, i: (b, i, 0)),
        scratch_shapes=[pltpu.VMEM((nkb, tk, tq), F32),
                        pltpu.VMEM((LANES, IDX_HEADS * tq), BF16),
                        pltpu.VMEM((2 * DSA_LATENT, DSA_HEADS * tq), BF16),
                        pltpu.VMEM((VT_ROWS, DSA_HEADS * tq), F32),
                        pltpu.VMEM((1, DSA_HEADS * tq), F32),
                        pltpu.VMEM((8, tq), F32)],
        compiler_params=pltpu.CompilerParams(dimension_semantics=("arbitrary", "arbitrary"),
                                             vmem_limit_bytes=VMEM_LIMIT),
        name="dsa",
    )(qabsT, iqT, iwT, ik4, ckv, ckvT, kmax, lstrict, wuvT_pair)


def _diff_kernel(fqT_ref, fk_ref, fvT_ref, kmax_ref, lam_ref, gsub_ref, o_ref, qt_ref, acc_ref, m_ref,
                 *, tq, tk, lam0):
    hd = pl.program_id(1)
    qi = pl.program_id(2)
    q_pos = qi * tq + lax.broadcasted_iota(jnp.int32, (1, tq), 1)
    n_full = (qi * tq) // tk
    assert tq % tk == 0
    n_diag = tq // tk
    slopes = _alibi_slopes(DIFF_HEADS)
    slope = jnp.float32(slopes[DIFF_HEADS - 1])
    for i in range(DIFF_HEADS - 1):
        slope = jnp.where(hd == i, slopes[i], slope)

    for m in range(2):
        r0 = m * LANES
        qh = fqT_ref[r0:r0 + DIFF_QK_DIM, :]
        qf = qh.astype(F32)
        shift = _shift_bound(jnp.sum(qf * qf, axis=0, keepdims=True),
                             _tile_max(kmax_ref, n_full + n_diag - 1, 2 * hd + m), 0.0)
        qt_ref[r0:r0 + DIFF_QK_DIM, :] = qh
        qt_ref[r0 + DIFF_QK_DIM:r0 + DIFF_QK_DIM + BF16_ROWS, :] = _query_feature_rows(
            q_pos, BF16_ROWS, slope, shift).astype(BF16)
        qt_ref[r0 + DIFF_QK_DIM + BF16_ROWS:r0 + LANES, :] = jnp.zeros(
            (LANES - DIFF_QK_DIM - BF16_ROWS, tq), BF16)
    acc_ref[...] = jnp.zeros_like(acc_ref)

    def fast_pv(kb, masked, q0=0):
        kx = fk_ref[pl.ds(pl.multiple_of(kb * tk, tk), tk), :]
        vT = fvT_ref[kb]
        out = []
        for m in range(2):
            s = _dot(kx[:, m * LANES:(m + 1) * LANES], qt_ref[m * LANES:(m + 1) * LANES, q0:])
            if masked:
                causal = (lax.broadcasted_iota(jnp.int32, (tk, tq - q0), 0) + kb * tk
                          <= q_pos[:, q0:])
                s = jnp.where(causal, s, NEG_BIG)
            out.append(_dot(vT, jnp.exp(s).astype(BF16)))
        return out

    def accumulate(*pvs, q0=0):
        for m in range(2):
            acc_ref[m, :, q0:] += functools.reduce(lambda a, b: a + b, [pv[m] for pv in pvs])

    def diagonal_tail(first):
        for d in range(first, n_diag):
            accumulate(fast_pv(n_full + d, True, d * tk), q0=d * tk)

    def fast_pair(j, carry):
        accumulate(fast_pv(2 * j, False), fast_pv(2 * j + 1, False))
        return carry

    lax.fori_loop(0, n_full // 2, fast_pair, 0)

    @pl.when(n_full % 2 == 1)
    def _():
        accumulate(fast_pv(n_full - 1, False), fast_pv(n_full, True))
        diagonal_tail(1)

    @pl.when(n_full % 2 == 0)
    def _():
        diagonal_tail(0)

    def block(kb, masked):
        kx = fk_ref[pl.ds(pl.multiple_of(kb * tk, tk), tk), :]
        vT = fvT_ref[kb]
        qT = qt_ref[...]
        m_all = m_ref[...]
        new = []
        for m in range(2):
            s = _dot(kx[:, m * LANES:(m + 1) * LANES], qT[m * LANES:(m + 1) * LANES, :])
            if masked:
                causal = lax.broadcasted_iota(jnp.int32, (tk, tq), 0) + kb * tk <= q_pos
                s = jnp.where(causal, s, NEG_BIG)
            m_old = m_all[m]
            m_new = jnp.maximum(m_old, jnp.max(s, axis=0, keepdims=True))
            p = jnp.exp(s - m_new).astype(BF16)
            new.append((m_new, jnp.exp(m_old - m_new), _dot(vT, p)))
        for m in range(2):
            m_ref[m] = new[m][0]
            acc_ref[m] = new[m][1] * acc_ref[m] + new[m][2]

    def body(kb, carry):
        block(kb, False)
        return carry

    denom_ok = jnp.min(acc_ref[:, ONES_ROW:ONES_ROW + 1, :]) > L_MIN

    @pl.when(jnp.logical_not(denom_ok))
    def _():
        acc_ref[...] = jnp.zeros_like(acc_ref)
        m_ref[...] = jnp.full_like(m_ref, NEG_BIG)
        lax.fori_loop(0, n_full, body, 0)
        for d in range(n_diag):
            block(n_full + d, True)

    lv = lam_ref[...]
    lam = (jnp.exp(jnp.sum(lv[0:1] * lv[1:2], axis=-1, keepdims=True))
           - jnp.exp(jnp.sum(lv[2:3] * lv[3:4], axis=-1, keepdims=True)) + lam0)
    a1 = acc_ref[0]
    a2 = acc_ref[1]
    o = (a1[0:DIFF_V_DIM, :] / a1[ONES_ROW:ONES_ROW + 1, :]
         - lam * (a2[0:DIFF_V_DIM, :] / a2[ONES_ROW:ONES_ROW + 1, :]))
    o = o * lax.rsqrt(jnp.mean(o * o, axis=0, keepdims=True) + EPS) * gsub_ref[...] * (1.0 - lam0)
    o_ref[...] = o.T.astype(BF16)


def _diff_call(fqT, fk, fvT, kmax, lam_vecs, g_sub_col, tq, tk, lam0):
    B, S, _ = fk.shape
    nkb = S // tk
    kern = functools.partial(_diff_kernel, tq=tq, tk=tk, lam0=lam0)
    full = lambda a: pl.BlockSpec(a.shape, lambda b, h, i: (0,) * a.ndim)
    return pl.pallas_call(
        kern,
        out_shape=jax.ShapeDtypeStruct((B, S, DIFF_HEADS * DIFF_V_DIM), BF16),
        grid=(B, DIFF_HEADS, S // tq),
        in_specs=[pl.BlockSpec((None, 2 * LANES, tq), lambda b, h, i: (b, h, i)),
                  pl.BlockSpec((None, S, 2 * LANES), lambda b, h, i: (b, 0, h)),
                  pl.BlockSpec((None, None, nkb, VT_ROWS, tk), lambda b, h, i: (b, h, 0, 0, 0)),
                  pl.BlockSpec((None, nkb, 8, LANES), lambda b, h, i: (b, 0, 0, 0)),
                  full(lam_vecs), full(g_sub_col)],
        out_specs=pl.BlockSpec((None, tq, DIFF_V_DIM), lambda b, h, i: (b, i, h)),
        scratch_shapes=[pltpu.VMEM((2 * LANES, tq), BF16),
                        pltpu.VMEM((2, VT_ROWS, tq), F32),
                        pltpu.VMEM((2, 1, tq), F32)],
        compiler_params=pltpu.CompilerParams(
            dimension_semantics=("arbitrary", "arbitrary", "arbitrary"),
            vmem_limit_bytes=VMEM_LIMIT),
        name="diff",
    )(fqT, fk, fvT, kmax, lam_vecs, g_sub_col)


def _route_kernel(x_ref, od_ref, of_ref, mod_ref, wout_ref, g_ref, wr_hi_ref, wr_lo_ref, br_ref,
                  x1_ref, h_ref, gates_ref, *, tm):
    attn = (_dot(od_ref[...], wout_ref[0:W_DQ, :]) + _dot(of_ref[...], wout_ref[W_DQ:, :]))
    x1 = x_ref[...] + mod_ref[2:3, :] * attn
    x1_ref[...] = x1
    h = _rms(x1, g_ref[...]) * (1.0 + mod_ref[4:5, :]) + mod_ref[3:4, :]
    h_hi = h.astype(BF16)
    h_ref[...] = h_hi
    h_lo = (h - h_hi.astype(F32)).astype(BF16)
    w_hi = wr_hi_ref[...]
    logits = (_dot(h_hi, w_hi) + _dot(h_lo, w_hi) + _dot(h_hi, wr_lo_ref[...])) + br_ref[...]

    lane = lax.broadcasted_iota(jnp.int32, (tm, LANES), 1)
    big = jnp.int32(4 * LANES)
    neg = -jnp.inf
    is_g = (lane >= N_EXPERTS) & (lane < N_EXPERTS + N_GROUPS)
    gl = jnp.where(is_g, logits, neg)
    gmax = jnp.max(gl, axis=-1, keepdims=True)
    g_lane = jnp.min(jnp.where(gl == gmax, lane, big), axis=-1, keepdims=True)
    g_sel = g_lane - N_EXPERTS
    p_g = 1.0 / jnp.sum(jnp.where(is_g, jnp.exp(gl - gmax), 0.0), axis=-1, keepdims=True)
    in_grp = (lane < N_EXPERTS) & ((lane // EXPERTS_PER_GROUP) == g_sel)
    el = jnp.where(in_grp, logits, neg)
    v1 = jnp.max(el, axis=-1, keepdims=True)
    i1 = jnp.min(jnp.where(el == v1, lane, big), axis=-1, keepdims=True)
    el2 = jnp.where(lane == i1, neg, el)
    v2 = jnp.max(el2, axis=-1, keepdims=True)
    i2 = jnp.min(jnp.where(el2 == v2, lane, big), axis=-1, keepdims=True)
    e = jnp.exp(v2 - v1)
    w1 = 1.0 / (1.0 + e)
    w2 = e * w1
    gates_ref[...] = jnp.where(lane == i1, w1 * p_g, jnp.where(lane == i2, w2 * p_g, 0.0))


def _route_call(x, o_dsa, o_diff, mod3, w_out, g_moe, wr_hi, wr_lo, b_r, tm):
    B, S, D = x.shape
    kern = functools.partial(_route_kernel, tm=tm)
    tok = lambda w: pl.BlockSpec((None, tm, w), lambda b, i: (b, i, 0))
    full = lambda a: pl.BlockSpec(a.shape, lambda b, i: (0,) * a.ndim)
    return pl.pallas_call(
        kern,
        out_shape=[jax.ShapeDtypeStruct((B, S, D), F32), jax.ShapeDtypeStruct((B, S, D), BF16),
                   jax.ShapeDtypeStruct((B, S, LANES), F32)],
        grid=(B, S // tm),
        in_specs=[tok(D), tok(W_DQ), tok(DIFF_HEADS * DIFF_V_DIM),
                  pl.BlockSpec((None, 6, D), lambda b, i: (b, 0, 0)),
                  full(w_out), full(g_moe), full(wr_hi), full(wr_lo), full(b_r)],
        out_specs=[tok(D), tok(D), tok(LANES)],
        compiler_params=pltpu.CompilerParams(dimension_semantics=("arbitrary", "arbitrary"),
                                             vmem_limit_bytes=VMEM_LIMIT),
        name="route",
    )(x, o_dsa, o_diff, mod3, w_out, g_moe, wr_hi, wr_lo, b_r)


def _moe_kernel(h_ref, gates_ref, x1_ref, mod_ref, wg_ref, wu_ref, wd_ref, gf_ref, o_ref, acc_ref,
                *, tm, ec):
    j = pl.program_id(2)
    nc = pl.num_programs(2)

    @pl.when(j == 0)
    def _():
        acc_ref[...] = jnp.zeros_like(acc_ref)

    h = h_ref[...]
    hg = _dot(h, wg_ref[...])
    hu = _dot(h, wu_ref[...])
    hid = hg * jax.nn.sigmoid(hg) * hu
    gates = gates_ref[...]
    lane = lax.broadcasted_iota(jnp.int32, (tm, LANES), 1)
    parts = []
    for e in range(ec):
        gcol = jnp.sum(jnp.where(lane == j * ec + e, gates, 0.0), axis=-1, keepdims=True)
        parts.append((hid[:, e * D_EXPERT:(e + 1) * D_EXPERT] * gcol).astype(BF16))
    hs = jnp.concatenate(parts, axis=1)
    acc_ref[...] += _dot(hs, wd_ref[...])

    @pl.when(j == nc - 1)
    def _():
        x2 = x1_ref[...] + mod_ref[5:6, :] * acc_ref[...]
        o_ref[...] = _rms(x2, gf_ref[...])


def _moe_call(h, gates, x1, mod3, wg, wu, wd, g_final, tm, ec):
    B, S, D = x1.shape
    kern = functools.partial(_moe_kernel, tm=tm, ec=ec)
    tok = lambda w: pl.BlockSpec((None, tm, w), lambda b, i, j: (b, i, 0))
    return pl.pallas_call(
        kern,
        out_shape=jax.ShapeDtypeStruct((B, S, D), F32),
        grid=(B, S // tm, N_EXPERTS // ec),
        in_specs=[tok(D), tok(LANES), tok(D),
                  pl.BlockSpec((None, 6, D), lambda b, i, j: (b, 0, 0)),
                  pl.BlockSpec((D, ec * D_EXPERT), lambda b, i, j: (0, j)),
                  pl.BlockSpec((D, ec * D_EXPERT), lambda b, i, j: (0, j)),
                  pl.BlockSpec((ec * D_EXPERT, D), lambda b, i, j: (j, 0)),
                  pl.BlockSpec((1, D), lambda b, i, j: (0, 0))],
        out_specs=tok(D),
        scratch_shapes=[pltpu.VMEM((tm, D), F32)],
        compiler_params=pltpu.CompilerParams(
            dimension_semantics=("arbitrary", "arbitrary", "arbitrary"),
            vmem_limit_bytes=VMEM_LIMIT),
        name="moe",
    )(h, gates, x1, mod3, wg, wu, wd, g_final)


def _tile(n, pref):
    t = min(n, pref)
    assert n % t == 0, (n, t)
    return t


def kernel(x, c, w_ada, b_ada, g_attn, w_in, g_kv, w_uk, w_uv, lam_q1, lam_k1, lam_q2, lam_k2,
           g_sub, w_out, g_moe, w_group, b_group, w_router, b_router, w_gate, w_up, w_down, g_final):
    B, S, D = x.shape
    assert D == D_MODEL and w_ada.shape[0] == 1
    assert S % KEY_TILE == 0 and S <= POS_SPLIT * 256
    topk = min(TOPK_MAX, S // 4)
    l = 0
    lam0 = 0.8 - 0.6 * math.exp(-0.3 * l)

    mod3 = _mod_call(c, w_ada[l], b_ada[l].reshape(1, -1)).reshape(B, 6, D)

    wn, wt, w_pair = _proj_weights(w_in[l], w_uk[l])
    qabsT, iqT, iwT, ik4, ckv, ckvT, fqT, fk, fvT, kmax = _proj_call(
        x, mod3, g_attn[l].reshape(1, D), wn, wt, w_pair, g_kv[l].reshape(1, -1), KEY_TILE)

    lstrict = jnp.asarray(np.tril(np.ones((KEY_TILE, KEY_TILE), np.float32), -1), BF16)
    uvT = jnp.swapaxes(w_uv[l], 1, 2)
    z = jnp.zeros_like(uvT[0])
    wuvT_pair = jnp.stack([
        jnp.concatenate([jnp.concatenate([uvT[2 * p], z], axis=1),
                         jnp.concatenate([z, uvT[2 * p + 1]], axis=1)], axis=0)
        for p in range(DSA_HEADS // 2)]).astype(BF16)
    o_dsa = _dsa_call(qabsT, iqT, iwT, ik4, ckv, ckvT, kmax, lstrict, wuvT_pair, DSA_QUERY_TILE, KEY_TILE,
                      topk)

    lam_vecs = jnp.concatenate([lam_q1[l][None], lam_k1[l][None], lam_q2[l][None], lam_k2[l][None]],
                               axis=0).astype(F32)
    o_diff = _diff_call(fqT, fk, fvT, kmax, lam_vecs, g_sub[l].reshape(-1, 1),
                        _tile(S, DIFF_QUERY_TILE), KEY_TILE, lam0)

    wr = jnp.pad(jnp.concatenate([w_router[l], w_group[l]], axis=1),
                 ((0, 0), (0, LANES - N_EXPERTS - N_GROUPS)))
    wr_hi = wr.astype(BF16)
    wr_lo = (wr - wr_hi.astype(F32)).astype(BF16)
    b_r = jnp.pad(jnp.concatenate([b_router[l], b_group[l]]), (0, LANES - N_EXPERTS - N_GROUPS))
    x1, h2, gates = _route_call(x, o_dsa, o_diff, mod3, w_out[l].astype(BF16),
                                g_moe[l].reshape(1, D), wr_hi, wr_lo, b_r.reshape(1, LANES),
                                _tile(S, TOKEN_TILE))

    wg = jnp.swapaxes(w_gate[l], 0, 1).reshape(D, N_EXPERTS * D_EXPERT).astype(BF16)
    wu = jnp.swapaxes(w_up[l], 0, 1).reshape(D, N_EXPERTS * D_EXPERT).astype(BF16)
    wd = w_down[l].reshape(N_EXPERTS * D_EXPERT, D).astype(BF16)
    return _moe_call(h2, gates, x1, mod3, wg, wu, wd, g_final.reshape(1, D), _tile(S, MOE_TOKEN_TILE),
                     MOE_EXPERTS_PER_STEP)
```

```python
import functools
import math

import jax
import jax.numpy as jnp
import numpy as np
from jax import lax
from jax.experimental import pallas as pl
from jax.experimental.pallas import tpu as pltpu

F32 = jnp.float32
BF16 = jnp.bfloat16

D_MODEL = 1024
DSA_HEADS = 8
DSA_HEAD_DIM = 64
DSA_LATENT = 128
IDX_HEADS = 4
IDX_DIM = 32
TOPK_MAX = 256
DIFF_HEADS = 4
DIFF_QK_DIM = 64
DIFF_V_DIM = 128
N_GROUPS = 4
EXPERTS_PER_GROUP = 8
N_EXPERTS = 32
D_EXPERT = 256
EPS = 1e-6

LANES = 128
BF16_ROWS = 16
POS_SPLIT = 64
NEG_BIG = -1e30
VMEM_LIMIT = 56 * 1024 * 1024
KEY_TILE = 512
VALUE_BISECT_STEPS = 24
UNCHECKED_BISECT_STEPS = 16
COUNT_CHAINS = 4

W_DQ = DSA_HEADS * DSA_HEAD_DIM
W_QABS = DSA_HEADS * DSA_LATENT
W_CKV = 2 * LANES
VT_ROWS = DSA_LATENT + BF16_ROWS
ONES_ROW = DSA_LATENT + 2
N_FEATS = 5
KMAX_LANE_CKV = 8
L_MIN = 1e-30
BF16_ROUND_UP = 1.0 + 2.0 ** -7
BOUND_SLACK = 2.0 ** -6
CODE_BISECT_STEPS = 34
TOKEN_TILE = 1024
DSA_QUERY_TILE = 256
DIFF_QUERY_TILE = 2048
MOE_EXPERTS_PER_STEP = 8
MOE_TOKEN_TILE = 512
W_FQ = DIFF_HEADS * 2 * LANES
W_DIFF_QK = DIFF_HEADS * 2 * DIFF_QK_DIM

T_DQ = 0
T_DLAT = T_DQ + W_DQ
T_IQ = T_DLAT + DSA_LATENT
T_IW = T_IQ + IDX_HEADS * IDX_DIM
T_FQ = T_IW + BF16_ROWS
T_FV = T_FQ + W_DIFF_QK
T_ROWS = T_FV + DIFF_HEADS * DIFF_V_DIM
N_IK = 0
N_DLAT = N_IK + LANES
N_FK = N_DLAT + DSA_LATENT
N_COLS = N_FK + W_FQ


def _alibi_slopes(n):
    return [2.0 ** (-8.0 * (i + 1) / n) for i in range(n)]


def _rms(x, g):
    return x * lax.rsqrt(jnp.mean(x * x, axis=-1, keepdims=True) + EPS) * g


def _dot(a, b):
    return jnp.dot(a, b, preferred_element_type=F32)


def _mod_kernel(c_ref, w_ref, b_ref, o_ref):
    c = c_ref[...]
    act = c * jax.nn.sigmoid(c)
    o_ref[...] = jnp.dot(act, w_ref[...], preferred_element_type=F32,
                         precision=lax.Precision.HIGHEST) + b_ref[...]


def _mod_call(c, w_ada, b_ada):
    B, D = c.shape
    n = w_ada.shape[1] // D
    return pl.pallas_call(
        _mod_kernel,
        out_shape=jax.ShapeDtypeStruct((B, n * D), F32),
        grid=(n,),
        in_specs=[pl.BlockSpec((B, D), lambda j: (0, 0)),
                  pl.BlockSpec((D, D), lambda j: (0, j)),
                  pl.BlockSpec((1, D), lambda j: (0, j))],
        out_specs=pl.BlockSpec((B, D), lambda j: (0, j)),
        compiler_params=pltpu.CompilerParams(dimension_semantics=("arbitrary",),
                                             vmem_limit_bytes=VMEM_LIMIT),
        name="mod",
    )(c, w_ada, b_ada)


def _key_feature_rows(pos_row, n_rows):
    r = lax.broadcasted_iota(jnp.int32, (n_rows, pos_row.shape[1]), 0)
    pa = (pos_row // POS_SPLIT).astype(F32)
    pb = (pos_row % POS_SPLIT).astype(F32)
    return jnp.where(r == 0, pa, jnp.where(r == 1, pb, jnp.where(r < N_FEATS, 1.0, 0.0)))


def _query_feature_rows(pos_row, n_rows, slope, shift):
    r = lax.broadcasted_iota(jnp.int32, (n_rows, pos_row.shape[1]), 0)
    pa = (pos_row // POS_SPLIT).astype(F32)
    pb = (pos_row % POS_SPLIT).astype(F32)
    return jnp.where(r == 0, POS_SPLIT * slope,
                     jnp.where(r == 1, slope,
                               jnp.where(r == 2, -POS_SPLIT * slope * pa,
                                         jnp.where(r == 3, -slope * pb,
                                                   jnp.where(r == 4, -shift, 0.0)))))


def _shift_bound(qn2, kmax2, extra):
    b = jnp.sqrt(qn2 * kmax2) + extra
    return b + jnp.abs(b) * BOUND_SLACK + BOUND_SLACK


def _tile_max(kmax_ref, last_tile, lane_idx):
    x = kmax_ref[...]
    t = lax.broadcasted_iota(jnp.int32, x.shape, 0)
    ln = lax.broadcasted_iota(jnp.int32, x.shape, 2)
    x = jnp.where((t <= last_tile) & (ln == lane_idx), x, 0.0)
    return jnp.max(jnp.max(x, axis=0), axis=1, keepdims=True)[0:1, :]


def _proj_kernel(x_ref, mod_ref, g_ref, wn_ref, wt_ref, wpair_ref, gkv_ref, gkvc_ref, fc_ref, sel_ref,
                 qabsT_ref, iqT_ref, iwT_ref, ik_ref, ckv_ref, ckvT_ref, fqT_ref, fk_ref, fvT_ref,
                 kmax_ref, *, tm):
    x = x_ref[...]
    h = _rms(x, g_ref[...]) * (1.0 + mod_ref[1:2, :]) + mod_ref[0:1, :]
    hb = h.astype(BF16)
    hT = h.T.astype(BF16)

    base = pl.program_id(1) * tm
    pos_c = base + lax.broadcasted_iota(jnp.int32, (tm, 1), 0)
    pos_r = base + lax.broadcasted_iota(jnp.int32, (1, tm), 1)
    pa = (pos_c // POS_SPLIT).astype(F32)
    pb = (pos_c % POS_SPLIT).astype(F32)

    def feats(row, width):
        return (fc_ref[row:row + 1, 0:width] + fc_ref[row + 1:row + 2, 0:width] * pa
                + fc_ref[row + 2:row + 3, 0:width] * pb)

    ik_ref[...] = _dot(hb, wn_ref[:, N_IK:N_IK + LANES]).astype(BF16)
    dlat = _dot(hb, wn_ref[:, N_DLAT:N_DLAT + DSA_LATENT])
    ckv_b = _rms(dlat, gkv_ref[...]).astype(BF16)
    ckv_ref[:, 0:LANES] = ckv_b
    ckv_ref[:, LANES:2 * LANES] = feats(0, LANES).astype(BF16)
    fk = _dot(hb, wn_ref[:, N_FK:N_FK + W_FQ])
    fk_ref[...] = (fk + feats(3, W_FQ)).astype(BF16)
    sq = jnp.concatenate([fk.astype(BF16).astype(F32), ckv_b.astype(F32)], axis=1)
    sq_up = (sq * sq * BF16_ROUND_UP).astype(BF16)
    kmax_ref[...] = jnp.broadcast_to(jnp.max(_dot(sq_up, sel_ref[...]), axis=0, keepdims=True),
                                     (8, LANES))

    dqT = _dot(wt_ref[T_DQ:T_DQ + W_DQ, :], hT).astype(BF16)
    for p in range(DSA_HEADS // 2):
        qa = _dot(wpair_ref[p], dqT[p * LANES:(p + 1) * LANES, :])
        qabsT_ref[p * 2 * LANES:(p + 1) * 2 * LANES, :] = qa.astype(BF16)
    dlatT = _dot(wt_ref[T_DLAT:T_DLAT + DSA_LATENT, :], hT)
    inv = lax.rsqrt(jnp.mean(dlatT * dlatT, axis=0, keepdims=True) + EPS)
    kfeat = _key_feature_rows(pos_r, BF16_ROWS).astype(BF16)
    ckvT_ref[0:DSA_LATENT, :] = (dlatT * inv * gkvc_ref[...]).astype(BF16)
    ckvT_ref[DSA_LATENT:VT_ROWS, :] = kfeat
    iqT_ref[...] = _dot(wt_ref[T_IQ:T_IQ + IDX_HEADS * IDX_DIM, :], hT).astype(BF16)
    iwT = _dot(wt_ref[T_IW:T_IW + BF16_ROWS, :], hT)
    iwT_ref[...] = iwT[0:8, :] * (IDX_HEADS ** -0.5 * IDX_DIM ** -0.5)
    fqT = _dot(wt_ref[T_FQ:T_FQ + W_DIFF_QK, :], hT)
    zpad = jnp.zeros((LANES - DIFF_QK_DIM, tm), BF16)
    for s in range(DIFF_HEADS * 2):
        r0 = s * LANES
        fqT_ref[r0:r0 + DIFF_QK_DIM, :] = fqT[s * DIFF_QK_DIM:(s + 1) * DIFF_QK_DIM, :].astype(BF16)
        fqT_ref[r0 + DIFF_QK_DIM:r0 + LANES, :] = zpad
    fvT = _dot(wt_ref[T_FV:T_FV + DIFF_HEADS * DIFF_V_DIM, :], hT)
    for hh in range(DIFF_HEADS):
        fvT_ref[hh, 0:DIFF_V_DIM, :] = fvT[hh * DIFF_V_DIM:(hh + 1) * DIFF_V_DIM, :].astype(BF16)
        fvT_ref[hh, DIFF_V_DIM:VT_ROWS, :] = kfeat


def _feature_consts():
    fc = np.zeros((8, W_FQ), np.float32)
    fc[1, 0] = 1.0
    fc[2, 1] = 1.0
    fc[0, 2:N_FEATS] = 1.0
    for s in range(DIFF_HEADS * 2):
        base = s * LANES + DIFF_QK_DIM
        fc[4, base + 0] = 1.0
        fc[5, base + 1] = 1.0
        fc[3, base + 2:base + N_FEATS] = 1.0
    return jnp.asarray(fc)


def _norm_selector():
    sel = np.zeros((W_FQ + DSA_LATENT, LANES), np.float32)
    for s in range(DIFF_HEADS * 2):
        sel[s * LANES:s * LANES + DIFF_QK_DIM, s] = 1.0
    sel[W_FQ:, KMAX_LANE_CKV] = 1.0
    return jnp.asarray(sel, BF16)


def _proj_weights(w_in, w_uk):
    D = w_in.shape[0]
    pts = np.cumsum([W_DQ, DSA_LATENT, IDX_HEADS * IDX_DIM, IDX_DIM, IDX_HEADS,
                     W_DIFF_QK, W_DIFF_QK])
    dq, dlat, iq, ik, iw, fq, fk, fv = jnp.split(w_in, list(pts), axis=1)
    ik4 = jnp.tile(ik, (1, IDX_HEADS))
    fke = jnp.pad(fk.reshape(D, DIFF_HEADS * 2, DIFF_QK_DIM),
                  ((0, 0), (0, 0), (0, LANES - DIFF_QK_DIM))).reshape(D, W_FQ)
    wn = jnp.concatenate([ik4, dlat, fke], axis=1).astype(BF16)
    iwp = jnp.pad(iw, ((0, 0), (0, BF16_ROWS - IDX_HEADS)))
    wt = jnp.concatenate([dq, dlat, iq, iwp, fq * (DIFF_QK_DIM ** -0.5), fv], axis=1).T.astype(BF16)
    uk = w_uk * (DSA_HEAD_DIM ** -0.5)
    z = jnp.zeros_like(uk[0])
    pairs = [jnp.concatenate([jnp.concatenate([uk[2 * p], z], axis=1),
                              jnp.concatenate([z, uk[2 * p + 1]], axis=1)], axis=0)
             for p in range(DSA_HEADS // 2)]
    return wn, wt, jnp.stack(pairs).astype(BF16)


def _proj_call(x, mod3, g_attn, wn, wt, w_pair, g_kv, tm):
    B, S, D = x.shape
    nt = S // tm
    fc = _feature_consts()
    sel = _norm_selector()
    kern = functools.partial(_proj_kernel, tm=tm)
    tok = lambda w: pl.BlockSpec((None, tm, w), lambda b, i: (b, i, 0))
    tokT = lambda r: pl.BlockSpec((None, r, tm), lambda b, i: (b, 0, i))
    full = lambda a: pl.BlockSpec(a.shape, lambda b, i: (0,) * a.ndim)
    g_kv_col = g_kv.reshape(-1, 1)
    out_shape = [jax.ShapeDtypeStruct((B, W_QABS, S), BF16),
                 jax.ShapeDtypeStruct((B, LANES, S), BF16),
                 jax.ShapeDtypeStruct((B, 8, S), F32),
                 jax.ShapeDtypeStruct((B, S, LANES), BF16),
                 jax.ShapeDtypeStruct((B, S, W_CKV), BF16),
                 jax.ShapeDtypeStruct((B, nt, VT_ROWS, tm), BF16),
                 jax.ShapeDtypeStruct((B, W_FQ, S), BF16),
                 jax.ShapeDtypeStruct((B, S, W_FQ), BF16),
                 jax.ShapeDtypeStruct((B, DIFF_HEADS, nt, VT_ROWS, tm), BF16),
                 jax.ShapeDtypeStruct((B, nt, 8, LANES), F32)]
    out_specs = [tokT(W_QABS), tokT(LANES), tokT(8), tok(LANES), tok(W_CKV),
                 pl.BlockSpec((None, None, VT_ROWS, tm), lambda b, i: (b, i, 0, 0)),
                 tokT(W_FQ), tok(W_FQ),
                 pl.BlockSpec((None, DIFF_HEADS, None, VT_ROWS, tm), lambda b, i: (b, 0, i, 0, 0)),
                 pl.BlockSpec((None, None, 8, LANES), lambda b, i: (b, i, 0, 0))]
    return pl.pallas_call(
        kern,
        out_shape=out_shape,
        grid=(B, nt),
        in_specs=[tok(D),
                  pl.BlockSpec((None, 6, D), lambda b, i: (b, 0, 0)),
                  full(g_attn), full(wn), full(wt), full(w_pair), full(g_kv), full(g_kv_col),
                  full(fc), full(sel)],
        out_specs=out_specs,
        compiler_params=pltpu.CompilerParams(dimension_semantics=("arbitrary", "arbitrary"),
                                             vmem_limit_bytes=VMEM_LIMIT),
        name="proj",
    )(x, mod3, g_attn, wn, wt, w_pair, g_kv, g_kv_col, fc, sel)


def _float_code(x):
    b = lax.bitcast_convert_type(x, jnp.int32)
    return b ^ (lax.shift_right_arithmetic(b, 31) & jnp.int32(0x7FFFFFFF))


def _float_decode(c):
    b = c ^ (lax.shift_right_arithmetic(c, 31) & jnp.int32(0x7FFFFFFF))
    return lax.bitcast_convert_type(b, F32)


def _dsa_kernel(qabsT_ref, iqT_ref, iwT_ref, ik_ref, ckv_ref, ckvT_ref, kmax_ref, lstrict_ref, wuvT_ref,
                o_ref, sc_ref, qst_ref, qt_ref, acc_ref, m_ref, tmp_ref, *, tq, tk, topk):
    qi = pl.program_id(1)
    nkb = (qi * tq) // tk + 1
    kf = float(topk)
    slopes = _alibi_slopes(DSA_HEADS)
    q_pos = qi * tq + lax.broadcasted_iota(jnp.int32, (1, tq), 1)

    iqT = iqT_ref[...]
    rowi = lax.broadcasted_iota(jnp.int32, (LANES, tq), 0)
    for hh in range(IDX_HEADS):
        qst_ref[:, hh * tq:(hh + 1) * tq] = jnp.where((rowi // IDX_DIM) == hh, iqT, jnp.zeros_like(iqT))
    iw = iwT_ref[...]
    wrow = [iw[hh:hh + 1, :] for hh in range(IDX_HEADS)]

    def scores(kb):
        kblk = ik_ref[pl.ds(pl.multiple_of(kb * tk, tk), tk), :]
        a = _dot(kblk, qst_ref[...])
        sc = jnp.maximum(a[:, 0:tq], 0.0) * wrow[0]
        for hh in range(1, IDX_HEADS):
            sc = sc + jnp.maximum(a[:, hh * tq:(hh + 1) * tq], 0.0) * wrow[hh]
        return sc

    def score_body(kb, carry):
        mn, mx = carry
        sc = scores(kb)
        sc_ref[kb] = sc
        return (jnp.minimum(mn, jnp.min(sc, axis=0, keepdims=True)),
                jnp.maximum(mx, jnp.max(sc, axis=0, keepdims=True)))

    last = nkb - 1
    mn, mx = lax.fori_loop(0, last // 2, lambda j, c: score_body(2 * j + 1, score_body(2 * j, c)),
                           (jnp.full((1, tq), jnp.inf, F32), jnp.full((1, tq), -jnp.inf, F32)))
    mn, mx = lax.cond(last % 2 == 1, lambda c: score_body(last - 1, c), lambda c: c, (mn, mx))
    sc = scores(last)
    causal = lax.broadcasted_iota(jnp.int32, (tk, tq), 0) + last * tk <= q_pos
    sc_ref[last] = jnp.where(causal, sc, jnp.nan)
    mn = jnp.minimum(mn, jnp.min(jnp.where(causal, sc, jnp.inf), axis=0, keepdims=True))
    mx = jnp.maximum(mx, jnp.max(jnp.where(causal, sc, -jnp.inf), axis=0, keepdims=True))

    def block_count(kb, pred):
        parts = [None] * COUNT_CHAINS
        for r in range(tk // 8):
            v = jnp.where(pred(sc_ref[kb, r * 8:(r + 1) * 8, :]), 1.0, 0.0)
            c = r % COUNT_CHAINS
            parts[c] = v if parts[c] is None else parts[c] + v
        return (parts[0] + parts[1]) + (parts[2] + parts[3])

    def count(pred):
        acc = lax.fori_loop(0, nkb, lambda kb, acc: acc + block_count(kb, pred),
                            jnp.zeros((8, tq), F32))
        return jnp.sum(acc, axis=0, keepdims=True)

    n_causal = (q_pos + 1).astype(F32)
    done0 = n_causal <= kf

    def bisect(st, value_mid):
        lo, hi, c_lo, c_hi, theta, done = st
        th8 = jnp.broadcast_to(theta, (8, tq))
        c = count(lambda v: v >= th8)
        ge = c >= kf
        live = done == 0.0
        lo = jnp.where(live & ge, theta, lo)
        c_lo = jnp.where(live & ge, c, c_lo)
        hi = jnp.where(live & jnp.logical_not(ge), theta, hi)
        c_hi = jnp.where(live & jnp.logical_not(ge), c, c_hi)
        if value_mid:
            nxt = 0.5 * lo + 0.5 * hi
        else:
            cl, ch = _float_code(lo), _float_code(hi)
            nxt = _float_decode((cl & ch) + lax.shift_right_arithmetic(cl ^ ch, 1))
        inside = (nxt > lo) & (nxt < hi)
        done = jnp.where((c_lo == kf) | jnp.logical_not(inside), 1.0, done)
        return lo, hi, c_lo, c_hi, nxt, done

    def unresolved(st):
        return jnp.sum(jnp.where((st[5] == 0.0) & (st[3] != kf - 1.0), 1.0, 0.0))

    def checked(value_mid, max_steps):
        def cond(c):
            return jnp.logical_and(c[1] > 0.0, c[2] < max_steps)

        def body(c):
            st = bisect(bisect(c[0], value_mid), value_mid)
            return st, unresolved(st), c[2] + 2
        return cond, body

    zero8 = jnp.zeros((8, tq), F32)
    c_ge0 = count(lambda v: v >= zero8)
    c_gt0 = count(lambda v: v > zero8)
    live0 = jnp.logical_not(done0)
    at0 = live0 & (c_ge0 >= kf)
    tie0 = at0 & (c_gt0 < kf)
    below0 = live0 & (c_ge0 < kf)
    lo0 = jnp.where(at0, 0.0, mn)
    hi0 = jnp.where(below0, 0.0, mx)
    st = (lo0, hi0, jnp.where(at0, c_ge0, n_causal), jnp.where(below0, c_ge0, -1.0),
          jnp.where(below0, 0.5 * lo0 + 0.5 * hi0, mx), jnp.where(done0 | tie0, 1.0, 0.0))
    st = lax.fori_loop(0, UNCHECKED_BISECT_STEPS, lambda i, s: bisect(s, True), st)
    cond, body = checked(True, VALUE_BISECT_STEPS - UNCHECKED_BISECT_STEPS)
    st, active, _ = lax.while_loop(cond, body, (st, unresolved(st), jnp.int32(0)))
    cond, body = checked(False, CODE_BISECT_STEPS)
    st, _, _ = lax.while_loop(cond, body, (st, active, jnp.int32(0)))
    tmp_ref[1:2, :] = st[0]
    tmp_ref[2:3, :] = st[2]
    from_hi = (st[5] == 0.0) & (st[3] == kf - 1.0)

    @pl.when(jnp.max(jnp.where(from_hi, 1.0, 0.0)) > 0.0)
    def _():
        hi8 = jnp.broadcast_to(st[1], (8, tq))

        def below_hi(kb, acc):
            parts = [None] * COUNT_CHAINS
            for r in range(tk // 8):
                v = sc_ref[kb, r * 8:(r + 1) * 8, :]
                v = jnp.where(v < hi8, v, -jnp.inf)
                c = r % COUNT_CHAINS
                parts[c] = v if parts[c] is None else jnp.maximum(parts[c], v)
            return jnp.maximum(acc, jnp.maximum(jnp.maximum(parts[0], parts[1]),
                                                jnp.maximum(parts[2], parts[3])))

        nxt = jnp.max(lax.fori_loop(0, nkb, below_hi, jnp.full((8, tq), -jnp.inf, F32)),
                      axis=0, keepdims=True)
        t_new = jnp.where(from_hi, nxt, st[0])
        t8 = jnp.broadcast_to(t_new, (8, tq))
        tmp_ref[1:2, :] = t_new
        tmp_ref[2:3, :] = jnp.where(from_hi, count(lambda v: v >= t8), st[2])

    tau, c_ge = tmp_ref[1:2, :], tmp_ref[2:3, :]

    over = c_ge > kf

    @pl.when(jnp.max(c_ge) > kf)
    def _():
        tau8 = jnp.broadcast_to(tau, (8, tq))
        tmp_ref[0:1, :] = c_gt0

        @pl.when(jnp.max(jnp.where(over & jnp.logical_not(tie0), 1.0, 0.0)) > 0.0)
        def _():
            tmp_ref[0:1, :] = jnp.where(tie0, c_gt0, count(lambda v: v > tau8))

        quota = kf - tmp_ref[0:1, :]

        def tie_body(kb, seen):
            cnt = jnp.sum(block_count(kb, lambda v: v == tau8), axis=0, keepdims=True)
            inside = over & (seen < quota) & (seen + cnt > quota)
            gone = over & (seen >= quota) & (cnt > 0.0)
            flag = jnp.max(jnp.where(inside, 2.0, jnp.where(gone, 1.0, 0.0)))

            @pl.when(flag > 1.5)
            def _():
                s = sc_ref[kb]
                eq = s == tau
                rank = _dot(lstrict_ref[...], jnp.where(eq, 1.0, 0.0).astype(BF16)) + seen
                sc_ref[kb] = jnp.where(eq & over & (rank >= quota), jnp.nan, s)

            @pl.when(flag == 1.0)
            def _():
                s = sc_ref[kb]
                sc_ref[kb] = jnp.where((s == tau) & gone, jnp.nan, s)

            return seen + cnt

        lax.fori_loop(0, nkb, tie_body, jnp.zeros((1, tq), F32))

    tau8 = jnp.broadcast_to(tau, (8, tq))
    sub = lax.broadcasted_iota(jnp.int32, (8, tq), 0).astype(F32)

    def nearest_body(kb, acc):
        parts = [None] * COUNT_CHAINS
        for r in range(tk // 8):
            idx = sub + (kb * tk + r * 8).astype(F32)
            v = jnp.where(sc_ref[kb, r * 8:(r + 1) * 8, :] >= tau8, idx, -1.0)
            c = r % COUNT_CHAINS
            parts[c] = v if parts[c] is None else jnp.maximum(parts[c], v)
        return jnp.maximum(acc, jnp.maximum(jnp.maximum(parts[0], parts[1]),
                                            jnp.maximum(parts[2], parts[3])))

    last_sel = jnp.max(lax.fori_loop(0, nkb, nearest_body, jnp.full((8, tq), -1.0, F32)),
                       axis=0, keepdims=True)
    d_min = q_pos.astype(F32) - last_sel
    kmax2 = _tile_max(kmax_ref, nkb - 1, KMAX_LANE_CKV)
    for hh in range(DSA_HEADS):
        qh = qabsT_ref[hh * DSA_LATENT:(hh + 1) * DSA_LATENT, :]
        qf = qh.astype(F32)
        shift = _shift_bound(jnp.sum(qf * qf, axis=0, keepdims=True), kmax2, -slopes[hh] * d_min)
        qt_ref[0:DSA_LATENT, hh * tq:(hh + 1) * tq] = qh
        qt_ref[DSA_LATENT:VT_ROWS, hh * tq:(hh + 1) * tq] = _query_feature_rows(
            q_pos, BF16_ROWS, slopes[hh], shift).astype(BF16)
    qt_ref[VT_ROWS:2 * DSA_LATENT, :] = jnp.zeros((2 * DSA_LATENT - VT_ROWS, DSA_HEADS * tq), BF16)
    acc_ref[...] = jnp.zeros_like(acc_ref)

    def fast_pv(kb):
        kx = ckv_ref[pl.ds(pl.multiple_of(kb * tk, tk), tk), :]
        s = _dot(kx, qt_ref[...])
        keep = sc_ref[kb] >= tau
        ps = [jnp.exp(jnp.where(keep, s[:, hh * tq:(hh + 1) * tq], NEG_BIG)).astype(BF16)
              for hh in range(DSA_HEADS)]
        return _dot(ckvT_ref[kb], jnp.concatenate(ps, axis=1))

    def fast_pair(j, carry):
        acc_ref[...] += fast_pv(2 * j) + fast_pv(2 * j + 1)
        return carry

    lax.fori_loop(0, nkb // 2, fast_pair, 0)

    @pl.when(nkb % 2 == 1)
    def _():
        acc_ref[...] += fast_pv(nkb - 1)

    def attn_body(kb, carry):
        kx = ckv_ref[pl.ds(pl.multiple_of(kb * tk, tk), tk), :]
        vT = ckvT_ref[kb]
        keep = sc_ref[kb] >= tau
        s = _dot(kx, qt_ref[...])
        m_all = m_ref[...]
        m_out, alphas, ps = [], [], []
        for hh in range(DSA_HEADS):
            c0 = hh * tq
            sh = jnp.where(keep, s[:, c0:c0 + tq], NEG_BIG)
            m_old = m_all[:, c0:c0 + tq]
            m_new = jnp.maximum(m_old, jnp.max(sh, axis=0, keepdims=True))
            ps.append(jnp.exp(sh - m_new).astype(BF16))
            alphas.append(jnp.exp(m_old - m_new))
            m_out.append(m_new)
        m_ref[...] = jnp.concatenate(m_out, axis=1)
        acc_ref[...] = (jnp.concatenate(alphas, axis=1) * acc_ref[...]
                        + _dot(vT, jnp.concatenate(ps, axis=1)))
        return carry

    denom_ok = jnp.min(acc_ref[ONES_ROW:ONES_ROW + 1, :]) > L_MIN

    @pl.when(jnp.logical_not(denom_ok))
    def _():
        acc_ref[...] = jnp.zeros_like(acc_ref)
        m_ref[...] = jnp.full_like(m_ref, NEG_BIG)
        lax.fori_loop(0, nkb, attn_body, 0)

    acc = acc_ref[...]
    lat = acc[0:DSA_LATENT, :] / acc[ONES_ROW:ONES_ROW + 1, :]
    for p in range(DSA_HEADS // 2):
        pair = lat[:, 2 * p * tq:(2 * p + 2) * tq]
        pair = jnp.concatenate([pair[:, 0:tq], pair[:, tq:2 * tq]], axis=0).astype(BF16)
        o_ref[:, p * LANES:(p + 1) * LANES] = _dot(wuvT_ref[p], pair).T.astype(BF16)


def _dsa_call(qabsT, iqT, iwT, ik4, ckv, ckvT, kmax, lstrict, wuvT_pair, tq, tk, topk):
    B, _, S = qabsT.shape
    nkb = S // tk
    kern = functools.partial(_dsa_kernel, tq=tq, tk=tk, topk=topk)
    blkT = lambda r: pl.BlockSpec((None, r, tq), lambda b, i: (b, 0, i))
    per_b = lambda w: pl.BlockSpec((None, S, w), lambda b, i: (b, 0, 0))
    full = lambda a: pl.BlockSpec(a.shape, lambda b, i: (0,) * a.ndim)
    return pl.pallas_call(
        kern,
        out_shape=jax.ShapeDtypeStruct((B, S, W_DQ), BF16),
        grid=(B, S // tq),
        in_specs=[blkT(W_QABS), blkT(LANES), blkT(8), per_b(LANES), per_b(W_CKV),
                  pl.BlockSpec((None, nkb, VT_ROWS, tk), lambda b, i: (b, 0, 0, 0)),
                  pl.BlockSpec((None, nkb, 8, LANES), lambda b, i: (b, 0, 0, 0)),
                  full(lstrict), full(wuvT_pair)],
        out_specs=pl.BlockSpec((None, tq, W_DQ), lambda b, i: (b, i, 0)),
        scratch_shapes=[pltpu.VMEM((nkb, tk, tq), F32),
                        pltpu.VMEM((LANES, IDX_HEADS * tq), BF16),
                        pltpu.VMEM((2 * DSA_LATENT, DSA_HEADS * tq), BF16),
                        pltpu.VMEM((VT_ROWS, DSA_HEADS * tq), F32),
                        pltpu.VMEM((1, DSA_HEADS * tq), F32),
                        pltpu.VMEM((8, tq), F32)],
        compiler_params=pltpu.CompilerParams(dimension_semantics=("arbitrary", "arbitrary"),
                                             vmem_limit_bytes=VMEM_LIMIT),
        name="dsa",
    )(qabsT, iqT, iwT, ik4, ckv, ckvT, kmax, lstrict, wuvT_pair)


def _diff_kernel(fqT_ref, fk_ref, fvT_ref, kmax_ref, lam_ref, gsub_ref, o_ref, qt_ref, acc_ref, m_ref,
                 *, tq, tk, lam0):
    hd = pl.program_id(1)
    qi = pl.program_id(2)
    q_pos = qi * tq + lax.broadcasted_iota(jnp.int32, (1, tq), 1)
    n_full = (qi * tq) // tk
    assert tq % tk == 0
    n_diag = tq // tk
    slopes = _alibi_slopes(DIFF_HEADS)
    slope = jnp.float32(slopes[DIFF_HEADS - 1])
    for i in range(DIFF_HEADS - 1):
        slope = jnp.where(hd == i, slopes[i], slope)

    for m in range(2):
        r0 = m * LANES
        qh = fqT_ref[r0:r0 + DIFF_QK_DIM, :]
        qf = qh.astype(F32)
        shift = _shift_bound(jnp.sum(qf * qf, axis=0, keepdims=True),
                             _tile_max(kmax_ref, n_full + n_diag - 1, 2 * hd + m), 0.0)
        qt_ref[r0:r0 + DIFF_QK_DIM, :] = qh
        qt_ref[r0 + DIFF_QK_DIM:r0 + DIFF_QK_DIM + BF16_ROWS, :] = _query_feature_rows(
            q_pos, BF16_ROWS, slope, shift).astype(BF16)
        qt_ref[r0 + DIFF_QK_DIM + BF16_ROWS:r0 + LANES, :] = jnp.zeros(
            (LANES - DIFF_QK_DIM - BF16_ROWS, tq), BF16)
    acc_ref[...] = jnp.zeros_like(acc_ref)

    def fast_pv(kb, masked, q0=0):
        kx = fk_ref[pl.ds(pl.multiple_of(kb * tk, tk), tk), :]
        vT = fvT_ref[kb]
        out = []
        for m in range(2):
            s = _dot(kx[:, m * LANES:(m + 1) * LANES], qt_ref[m * LANES:(m + 1) * LANES, q0:])
            if masked:
                causal = (lax.broadcasted_iota(jnp.int32, (tk, tq - q0), 0) + kb * tk
                          <= q_pos[:, q0:])
                s = jnp.where(causal, s, NEG_BIG)
            out.append(_dot(vT, jnp.exp(s).astype(BF16)))
        return out

    def accumulate(*pvs, q0=0):
        for m in range(2):
            acc_ref[m, :, q0:] += functools.reduce(lambda a, b: a + b, [pv[m] for pv in pvs])

    def diagonal_tail(first):
        for d in range(first, n_diag):
            accumulate(fast_pv(n_full + d, True, d * tk), q0=d * tk)

    def fast_pair(j, carry):
        accumulate(fast_pv(2 * j, False), fast_pv(2 * j + 1, False))
        return carry

    lax.fori_loop(0, n_full // 2, fast_pair, 0)

    @pl.when(n_full % 2 == 1)
    def _():
        accumulate(fast_pv(n_full - 1, False), fast_pv(n_full, True))
        diagonal_tail(1)

    @pl.when(n_full % 2 == 0)
    def _():
        diagonal_tail(0)

    def block(kb, masked):
        kx = fk_ref[pl.ds(pl.multiple_of(kb * tk, tk), tk), :]
        vT = fvT_ref[kb]
        qT = qt_ref[...]
        m_all = m_ref[...]
        new = []
        for m in range(2):
            s = _dot(kx[:, m * LANES:(m + 1) * LANES], qT[m * LANES:(m + 1) * LANES, :])
            if masked:
                causal = lax.broadcasted_iota(jnp.int32, (tk, tq), 0) + kb * tk <= q_pos
                s = jnp.where(causal, s, NEG_BIG)
            m_old = m_all[m]
            m_new = jnp.maximum(m_old, jnp.max(s, axis=0, keepdims=True))
            p = jnp.exp(s - m_new).astype(BF16)
            new.append((m_new, jnp.exp(m_old - m_new), _dot(vT, p)))
        for m in range(2):
            m_ref[m] = new[m][0]
            acc_ref[m] = new[m][1] * acc_ref[m] + new[m][2]

    def body(kb, carry):
        block(kb, False)
        return carry

    denom_ok = jnp.min(acc_ref[:, ONES_ROW:ONES_ROW + 1, :]) > L_MIN

    @pl.when(jnp.logical_not(denom_ok))
    def _():
        acc_ref[...] = jnp.zeros_like(acc_ref)
        m_ref[...] = jnp.full_like(m_ref, NEG_BIG)
        lax.fori_loop(0, n_full, body, 0)
        for d in range(n_diag):
            block(n_full + d, True)

    lv = lam_ref[...]
    lam = (jnp.exp(jnp.sum(lv[0:1] * lv[1:2], axis=-1, keepdims=True))
           - jnp.exp(jnp.sum(lv[2:3] * lv[3:4], axis=-1, keepdims=True)) + lam0)
    a1 = acc_ref[0]
    a2 = acc_ref[1]
    o = (a1[0:DIFF_V_DIM, :] / a1[ONES_ROW:ONES_ROW + 1, :]
         - lam * (a2[0:DIFF_V_DIM, :] / a2[ONES_ROW:ONES_ROW + 1, :]))
    o = o * lax.rsqrt(jnp.mean(o * o, axis=0, keepdims=True) + EPS) * gsub_ref[...] * (1.0 - lam0)
    o_ref[...] = o.T.astype(BF16)


def _diff_call(fqT, fk, fvT, kmax, lam_vecs, g_sub_col, tq, tk, lam0):
    B, S, _ = fk.shape
    nkb = S // tk
    kern = functools.partial(_diff_kernel, tq=tq, tk=tk, lam0=lam0)
    full = lambda a: pl.BlockSpec(a.shape, lambda b, h, i: (0,) * a.ndim)
    return pl.pallas_call(
        kern,
        out_shape=jax.ShapeDtypeStruct((B, S, DIFF_HEADS * DIFF_V_DIM), BF16),
        grid=(B, DIFF_HEADS, S // tq),
        in_specs=[pl.BlockSpec((None, 2 * LANES, tq), lambda b, h, i: (b, h, i)),
                  pl.BlockSpec((None, S, 2 * LANES), lambda b, h, i: (b, 0, h)),
                  pl.BlockSpec((None, None, nkb, VT_ROWS, tk), lambda b, h, i: (b, h, 0, 0, 0)),
                  pl.BlockSpec((None, nkb, 8, LANES), lambda b, h, i: (b, 0, 0, 0)),
                  full(lam_vecs), full(g_sub_col)],
        out_specs=pl.BlockSpec((None, tq, DIFF_V_DIM), lambda b, h, i: (b, i, h)),
        scratch_shapes=[pltpu.VMEM((2 * LANES, tq), BF16),
                        pltpu.VMEM((2, VT_ROWS, tq), F32),
                        pltpu.VMEM((2, 1, tq), F32)],
        compiler_params=pltpu.CompilerParams(
            dimension_semantics=("arbitrary", "arbitrary", "arbitrary"),
            vmem_limit_bytes=VMEM_LIMIT),
        name="diff",
    )(fqT, fk, fvT, kmax, lam_vecs, g_sub_col)


def _route_kernel(x_ref, od_ref, of_ref, mod_ref, wout_ref, g_ref, wr_hi_ref, wr_lo_ref, br_ref,
                  x1_ref, h_ref, gates_ref, *, tm):
    attn = (_dot(od_ref[...], wout_ref[0:W_DQ, :]) + _dot(of_ref[...], wout_ref[W_DQ:, :]))
    x1 = x_ref[...] + mod_ref[2:3, :] * attn
    x1_ref[...] = x1
    h = _rms(x1, g_ref[...]) * (1.0 + mod_ref[4:5, :]) + mod_ref[3:4, :]
    h_hi = h.astype(BF16)
    h_ref[...] = h_hi
    h_lo = (h - h_hi.astype(F32)).astype(BF16)
    w_hi = wr_hi_ref[...]
    logits = (_dot(h_hi, w_hi) + _dot(h_lo, w_hi) + _dot(h_hi, wr_lo_ref[...])) + br_ref[...]

    lane = lax.broadcasted_iota(jnp.int32, (tm, LANES), 1)
    big = jnp.int32(4 * LANES)
    neg = -jnp.inf
    is_g = (lane >= N_EXPERTS) & (lane < N_EXPERTS + N_GROUPS)
    gl = jnp.where(is_g, logits, neg)
    gmax = jnp.max(gl, axis=-1, keepdims=True)
    g_lane = jnp.min(jnp.where(gl == gmax, lane, big), axis=-1, keepdims=True)
    g_sel = g_lane - N_EXPERTS
    p_g = 1.0 / jnp.sum(jnp.where(is_g, jnp.exp(gl - gmax), 0.0), axis=-1, keepdims=True)
    in_grp = (lane < N_EXPERTS) & ((lane // EXPERTS_PER_GROUP) == g_sel)
    el = jnp.where(in_grp, logits, neg)
    v1 = jnp.max(el, axis=-1, keepdims=True)
    i1 = jnp.min(jnp.where(el == v1, lane, big), axis=-1, keepdims=True)
    el2 = jnp.where(lane == i1, neg, el)
    v2 = jnp.max(el2, axis=-1, keepdims=True)
    i2 = jnp.min(jnp.where(el2 == v2, lane, big), axis=-1, keepdims=True)
    e = jnp.exp(v2 - v1)
    w1 = 1.0 / (1.0 + e)
    w2 = e * w1
    gates_ref[...] = jnp.where(lane == i1, w1 * p_g, jnp.where(lane == i2, w2 * p_g, 0.0))


def _route_call(x, o_dsa, o_diff, mod3, w_out, g_moe, wr_hi, wr_lo, b_r, tm):
    B, S, D = x.shape
    kern = functools.partial(_route_kernel, tm=tm)
    tok = lambda w: pl.BlockSpec((None, tm, w), lambda b, i: (b, i, 0))
    full = lambda a: pl.BlockSpec(a.shape, lambda b, i: (0,) * a.ndim)
    return pl.pallas_call(
        kern,
        out_shape=[jax.ShapeDtypeStruct((B, S, D), F32), jax.ShapeDtypeStruct((B, S, D), BF16),
                   jax.ShapeDtypeStruct((B, S, LANES), F32)],
        grid=(B, S // tm),
        in_specs=[tok(D), tok(W_DQ), tok(DIFF_HEADS * DIFF_V_DIM),
                  pl.BlockSpec((None, 6, D), lambda b, i: (b, 0, 0)),
                  full(w_out), full(g_moe), full(wr_hi), full(wr_lo), full(b_r)],
        out_specs=[tok(D), tok(D), tok(LANES)],
        compiler_params=pltpu.CompilerParams(dimension_semantics=("arbitrary", "arbitrary"),
                                             vmem_limit_bytes=VMEM_LIMIT),
        name="route",
    )(x, o_dsa, o_diff, mod3, w_out, g_moe, wr_hi, wr_lo, b_r)


def _moe_kernel(h_ref, gates_ref, x1_ref, mod_ref, wg_ref, wu_ref, wd_ref, gf_ref, o_ref, acc_ref,
                *, tm, ec):
    j = pl.program_id(2)
    nc = pl.num_programs(2)

    @pl.when(j == 0)
    def _():
        acc_ref[...] = jnp.zeros_like(acc_ref)

    h = h_ref[...]
    hg = _dot(h, wg_ref[...])
    hu = _dot(h, wu_ref[...])
    hid = hg * jax.nn.sigmoid(hg) * hu
    gates = gates_ref[...]
    lane = lax.broadcasted_iota(jnp.int32, (tm, LANES), 1)
    parts = []
    for e in range(ec):
        gcol = jnp.sum(jnp.where(lane == j * ec + e, gates, 0.0), axis=-1, keepdims=True)
        parts.append((hid[:, e * D_EXPERT:(e + 1) * D_EXPERT] * gcol).astype(BF16))
    hs = jnp.concatenate(parts, axis=1)
    acc_ref[...] += _dot(hs, wd_ref[...])

    @pl.when(j == nc - 1)
    def _():
        x2 = x1_ref[...] + mod_ref[5:6, :] * acc_ref[...]
        o_ref[...] = _rms(x2, gf_ref[...])


def _moe_call(h, gates, x1, mod3, wg, wu, wd, g_final, tm, ec):
    B, S, D = x1.shape
    kern = functools.partial(_moe_kernel, tm=tm, ec=ec)
    tok = lambda w: pl.BlockSpec((None, tm, w), lambda b, i, j: (b, i, 0))
    return pl.pallas_call(
        kern,
        out_shape=jax.ShapeDtypeStruct((B, S, D), F32),
        grid=(B, S // tm, N_EXPERTS // ec),
        in_specs=[tok(D), tok(LANES), tok(D),
                  pl.BlockSpec((None, 6, D), lambda b, i, j: (b, 0, 0)),
                  pl.BlockSpec((D, ec * D_EXPERT), lambda b, i, j: (0, j)),
                  pl.BlockSpec((D, ec * D_EXPERT), lambda b, i, j: (0, j)),
                  pl.BlockSpec((ec * D_EXPERT, D), lambda b, i, j: (j, 0)),
                  pl.BlockSpec((1, D), lambda b, i, j: (0, 0))],
        out_specs=tok(D),
        scratch_shapes=[pltpu.VMEM((tm, D), F32)],
        compiler_params=pltpu.CompilerParams(
            dimension_semantics=("arbitrary", "arbitrary", "arbitrary"),
            vmem_limit_bytes=VMEM_LIMIT),
        name="moe",
    )(h, gates, x1, mod3, wg, wu, wd, g_final)


def _tile(n, pref):
    t = min(n, pref)
    assert n % t == 0, (n, t)
    return t


def kernel(x, c, w_ada, b_ada, g_attn, w_in, g_kv, w_uk, w_uv, lam_q1, lam_k1, lam_q2, lam_k2,
           g_sub, w_out, g_moe, w_group, b_group, w_router, b_router, w_gate, w_up, w_down, g_final):
    B, S, D = x.shape
    assert D == D_MODEL and w_ada.shape[0] == 1
    assert S % KEY_TILE == 0 and S <= POS_SPLIT * 256
    topk = min(TOPK_MAX, S // 4)
    l = 0
    lam0 = 0.8 - 0.6 * math.exp(-0.3 * l)

    mod3 = _mod_call(c, w_ada[l], b_ada[l].reshape(1, -1)).reshape(B, 6, D)

    wn, wt, w_pair = _proj_weights(w_in[l], w_uk[l])
    qabsT, iqT, iwT, ik4, ckv, ckvT, fqT, fk, fvT, kmax = _proj_call(
        x, mod3, g_attn[l].reshape(1, D), wn, wt, w_pair, g_kv[l].reshape(1, -1), KEY_TILE)

    lstrict = jnp.asarray(np.tril(np.ones((KEY_TILE, KEY_TILE), np.float32), -1), BF16)
    uvT = jnp.swapaxes(w_uv[l], 1, 2)
    z = jnp.zeros_like(uvT[0])
    wuvT_pair = jnp.stack([
        jnp.concatenate([jnp.concatenate([uvT[2 * p], z], axis=1),
                         jnp.concatenate([z, uvT[2 * p + 1]], axis=1)], axis=0)
        for p in range(DSA_HEADS // 2)]).astype(BF16)
    o_dsa = _dsa_call(qabsT, iqT, iwT, ik4, ckv, ckvT, kmax, lstrict, wuvT_pair, DSA_QUERY_TILE, KEY_TILE,
                      topk)

    lam_vecs = jnp.concatenate([lam_q1[l][None], lam_k1[l][None], lam_q2[l][None], lam_k2[l][None]],
                               axis=0).astype(F32)
    o_diff = _diff_call(fqT, fk, fvT, kmax, lam_vecs, g_sub[l].reshape(-1, 1),
                        _tile(S, DIFF_QUERY_TILE), KEY_TILE, lam0)

    wr = jnp.pad(jnp.concatenate([w_router[l], w_group[l]], axis=1),
                 ((0, 0), (0, LANES - N_EXPERTS - N_GROUPS)))
    wr_hi = wr.astype(BF16)
    wr_lo = (wr - wr_hi.astype(F32)).astype(BF16)
    b_r = jnp.pad(jnp.concatenate([b_router[l], b_group[l]]), (0, LANES - N_EXPERTS - N_GROUPS))
    x1, h2, gates = _route_call(x, o_dsa, o_diff, mod3, w_out[l].astype(BF16),
                                g_moe[l].reshape(1, D), wr_hi, wr_lo, b_r.reshape(1, LANES),
                                _tile(S, TOKEN_TILE))

    wg = jnp.swapaxes(w_gate[l], 0, 1).reshape(D, N_EXPERTS * D_EXPERT).astype(BF16)
    wu = jnp.swapaxes(w_up[l], 0, 1).reshape(D, N_EXPERTS * D_EXPERT).astype(BF16)
    wd = w_down[l].reshape(N_EXPERTS * D_EXPERT, D).astype(BF16)
    return _moe_call(h2, gates, x1, mod3, wg, wu, wd, g_final.reshape(1, D), _tile(S, MOE_TOKEN_TILE),
                     MOE_EXPERTS_PER_STEP)
```
